```python
import jax, jax.numpy as jnp
from jax import lax
import numpy as np

D_MODEL = 2048
BATCH = 8
SEQ = 4096
DEPTH = 4


NORM_EPS = 1e-6
NEG_BIG = -1e30

GLA_HEADS = 4
GLA_DK = 128
GLA_DV = 256
GLA_GATE_RANK = 16
GLA_GATE_NORMALIZER = 16.0
GLA_CHUNK = 64

RWKV_HEADS = 16
RWKV_HEAD_DIM = 64
RWKV_DIM = RWKV_HEADS * RWKV_HEAD_DIM
RWKV_W_RANK = 64
RWKV_A_RANK = 64
RWKV_G_RANK = 160
RWKV_LN_EPS = 64e-5

NSA_HEADS = 16
NSA_GROUPS = 4
NSA_HPG = NSA_HEADS // NSA_GROUPS
NSA_HEAD_DIM = 64
NSA_CMP_LEN = 32
NSA_CMP_STRIDE = 16
NSA_CMP_HIDDEN = 128
NSA_SLC_LEN = 64
NSA_N_SEL = 16
NSA_WINDOW = 512
NSA_Q_BLOCK = 128
NSA_SEL_Q_BLOCK = 32

BRANCH_DIM = 1024
N_BRANCHES = 3
D_FF = 5632
N_EXPERTS = 8
TOP_K = 2
D_FF_EXPERT = 5632
N_DENSE = (DEPTH + 1) // 2
N_MOE = DEPTH // 2

GLA_WIDTHS = (GLA_HEADS * GLA_DK, GLA_HEADS * GLA_DK, GLA_HEADS * GLA_DV, GLA_HEADS * GLA_DV, GLA_GATE_RANK)
RWKV_WIDTHS = (RWKV_DIM, RWKV_DIM, RWKV_DIM, RWKV_W_RANK, RWKV_A_RANK, RWKV_G_RANK)
NSA_KV = NSA_GROUPS * NSA_HEAD_DIM
NSA_WIDTHS = (NSA_HEADS * NSA_HEAD_DIM, NSA_KV, NSA_KV, NSA_KV, NSA_KV, NSA_KV, NSA_KV, NSA_HEADS * 3)
SECTION_WIDTHS = (sum(GLA_WIDTHS), sum(RWKV_WIDTHS), sum(NSA_WIDTHS), N_BRANCHES * D_MODEL)
D_IN = sum(SECTION_WIDTHS)

kernel_name = 'hybrid_gla_rwkv7_nsa_moe_block'


def _split(u, widths):
    offs = np.cumsum(widths)[:-1].tolist()
    return jnp.split(u, offs, axis=-1)


def _rmsnorm(x, g):
    xf = x.astype(jnp.float32)
    y = xf * lax.rsqrt(jnp.mean(xf * xf, axis=-1, keepdims=True) + NORM_EPS)
    return (y * g.astype(jnp.float32)).astype(x.dtype)


def _modulate(x, g, shift, scale):
    return _rmsnorm(x, g) * (1.0 + scale[:, None, :]) + shift[:, None, :]


def _token_shift(u):
    return jnp.pad(u, ((0, 0), (1, 0), (0, 0)))[:, :-1]


def _gla(q, k, v, g, a_lr, a2, a_b, norm_g):
    B, T, _ = q.shape
    H, C = GLA_HEADS, GLA_CHUNK
    N = T // C
    f32 = jnp.float32
    log_a = jax.nn.log_sigmoid((a_lr @ a2 + a_b).astype(f32)) / GLA_GATE_NORMALIZER

    def chunks(u, d):
        return u.astype(f32).reshape(B, N, C, H, d).transpose(0, 3, 1, 2, 4)

    qc = chunks(q, GLA_DK) * (GLA_DK ** -0.5)
    kc = chunks(k, GLA_DK)
    vc = chunks(v, GLA_DV)
    bc = jnp.cumsum(chunks(log_a, GLA_DK), axis=3)
    b_last = bc[:, :, :, -1:, :]
    q_dec = qc * jnp.exp(bc)
    k_dec = kc * jnp.exp(-bc)
    causal = jnp.tril(jnp.ones((C, C), dtype=bool))
    att = jnp.where(causal, jnp.einsum('bhnid,bhnjd->bhnij', q_dec, k_dec), 0.0)
    o_intra = jnp.einsum('bhnij,bhnjv->bhniv', att, vc)
    kv_chunk = jnp.einsum('bhnjd,bhnjv->bhndv', kc * jnp.exp(b_last - bc), vc)
    decay_chunk = jnp.exp(b_last[:, :, :, 0, :])

    def step(S, inp):
        dec, kv = inp
        return dec[..., None] * S + kv, S

    S0 = jnp.zeros((B, H, GLA_DK, GLA_DV), f32)
    _, S_prev = lax.scan(step, S0, (jnp.moveaxis(decay_chunk, 2, 0), jnp.moveaxis(kv_chunk, 2, 0)))
    S_prev = jnp.moveaxis(S_prev, 0, 2)
    o = o_intra + jnp.einsum('bhnid,bhndv->bhniv', q_dec, S_prev)
    o = o.transpose(0, 2, 3, 1, 4).reshape(B, T, H, GLA_DV)
    o = o * lax.rsqrt(jnp.mean(o * o, axis=-1, keepdims=True) + NORM_EPS) * norm_g.astype(f32)
    o = o.reshape(B, T, H * GLA_DV) * jax.nn.silu(g.astype(f32))
    return o.astype(q.dtype)


def _rwkv7(r, k, v, w_lr, a_lr, g_lr, w0, w2, a0, a2, g2, k_k, k_a, r_k, ln_w, ln_b):
    B, T, _ = r.shape
    H, N = RWKV_HEADS, RWKV_HEAD_DIM
    f32 = jnp.float32
    r = r.astype(f32)
    k = k.astype(f32)
    v = v.astype(f32)
    w_log = -jax.nn.softplus(-(w0 + jnp.tanh(w_lr) @ w2).astype(f32)) - 0.5
    decay = jnp.exp(-jnp.exp(w_log))
    a = jax.nn.sigmoid((a0 + a_lr @ a2).astype(f32))
    gate = (jax.nn.sigmoid(g_lr) @ g2).astype(f32)
    kk = (k * k_k).reshape(B, T, H, N)
    kk = kk / jnp.maximum(jnp.sqrt(jnp.sum(kk * kk, axis=-1, keepdims=True)), 1e-12)
    k = k * (1.0 + (a - 1.0) * k_a)

    def heads(u):
        return jnp.moveaxis(u.reshape(B, T, H, N), 1, 0)

    def step(S, inp):
        r_t, w_t, k_t, v_t, kk_t, a_t = inp
        sa = jnp.einsum('bhij,bhj->bhi', S, -kk_t)
        S = S * w_t[:, :, None, :] + sa[..., None] * (kk_t * a_t)[:, :, None, :] + v_t[..., None] * k_t[:, :, None, :]
        return S, jnp.einsum('bhij,bhj->bhi', S, r_t)

    S0 = jnp.zeros((B, H, N, N), f32)
    _, y = lax.scan(step, S0, (heads(r), heads(decay), heads(k), heads(v), jnp.moveaxis(kk, 1, 0), heads(a)))
    y = jnp.moveaxis(y, 0, 1)
    mu = jnp.mean(y, axis=-1, keepdims=True)
    var = jnp.mean(jnp.square(y - mu), axis=-1, keepdims=True)
    y = ((y - mu) * lax.rsqrt(var + RWKV_LN_EPS)).reshape(B, T, RWKV_DIM) * ln_w + ln_b
    bonus = jnp.sum(r.reshape(B, T, H, N) * k.reshape(B, T, H, N) * r_k, axis=-1, keepdims=True) * v.reshape(B, T, H, N)
    y = (y + bonus.reshape(B, T, RWKV_DIM)) * gate
    return y.astype(r_lr_dtype(w_lr))


def r_lr_dtype(u):
    return u.dtype


def _nsa(q, k_cmp, v_cmp, k_slc, v_slc, k_win, v_win, gate, pos_k, w1_k, w2_k, pos_v, w1_v, w2_v):
    B, T, _ = q.shape
    G, HPG, dh = NSA_GROUPS, NSA_HPG, NSA_HEAD_DIM
    L, S, LS, W = NSA_CMP_LEN, NSA_CMP_STRIDE, NSA_SLC_LEN, NSA_WINDOW
    f32 = jnp.float32

    def kv_heads(u):
        return u.astype(f32).reshape(B, T, G, dh).transpose(0, 2, 1, 3)

    qh = q.astype(f32).reshape(B, T, G, HPG, dh).transpose(0, 2, 3, 1, 4) * (dh ** -0.5)
    kc_tok, vc_tok = kv_heads(k_cmp), kv_heads(v_cmp)
    ks, vs = kv_heads(k_slc), kv_heads(v_slc)
    kw, vw = kv_heads(k_win), kv_heads(v_win)

    n_cmp = (T - L) // S + 1
    blk_start = jnp.arange(n_cmp) * S
    blk_idx = blk_start[:, None] + jnp.arange(L)[None, :]

    def compress(u, pos, w1, w2):
        blocks = u[:, :, blk_idx] + pos.astype(f32)
        return jax.nn.gelu(blocks.reshape(B, G, n_cmp, L * dh) @ w1) @ w2

    kc = compress(kc_tok, pos_k, w1_k, w2_k).astype(f32)
    vc = compress(vc_tok, pos_v, w1_v, w2_v).astype(f32)
    cmp_end = blk_start + (L - 1)
    cmp_center = blk_start.astype(f32) + (L - 1) / 2.0

    n_slc = T // LS
    n_sel = min(NSA_N_SEL, n_slc)
    slc_blk = jnp.arange(n_slc)
    overlap = ((cmp_end[None, :] >= slc_blk[:, None] * LS) & (blk_start[None, :] <= slc_blk[:, None] * LS + LS - 1)).astype(f32)
    slopes = jnp.exp2(-8.0 * jnp.arange(1, NSA_HEADS + 1, dtype=f32) / NSA_HEADS).reshape(G, HPG)

    def q_blocks(u, qb):
        return jnp.moveaxis(u.reshape(B, G, HPG, T // qb, qb, dh), 3, 0)

    def unblock(o):
        return jnp.moveaxis(o, 0, 3).reshape(B, G, HPG, T, dh)

    QB = NSA_Q_BLOCK
    kw_pad = jnp.pad(kw, ((0, 0), (0, 0), (W, 0), (0, 0)))
    vw_pad = jnp.pad(vw, ((0, 0), (0, 0), (W, 0), (0, 0)))

    def sweep_cmp_win(inp):
        qb_, start = inp
        t = start + jnp.arange(QB)
        dist_c = t[:, None].astype(f32) - cmp_center[None, :]
        valid_c = cmp_end[None, :] <= t[:, None]
        s = jnp.einsum('bghqd,bgnd->bghqn', qb_, kc) - slopes[:, :, None, None] * jnp.abs(dist_c)
        p = jnp.where(valid_c, jax.nn.softmax(jnp.where(valid_c, s, NEG_BIG), axis=-1), 0.0)
        o_c = jnp.einsum('bghqn,bgnd->bghqd', p, vc)
        imp = jnp.einsum('bghqn,jn->bgqj', p, overlap)
        cur = t // LS
        forced = (slc_blk[None, :] == 0) | (slc_blk[None, :] == cur[:, None]) | (slc_blk[None, :] == cur[:, None] - 1)
        future = slc_blk[None, :] * LS > t[:, None]
        score = jnp.where(forced, jnp.inf, jnp.where(future, -jnp.inf, imp))
        _, idx = lax.top_k(score, n_sel)
        kslab = lax.dynamic_slice_in_dim(kw_pad, start, W + QB, axis=2)
        vslab = lax.dynamic_slice_in_dim(vw_pad, start, W + QB, axis=2)
        s_pos = start - W + jnp.arange(W + QB)
        dist_w = t[:, None] - s_pos[None, :]
        valid_w = (dist_w >= 0) & (dist_w < W) & (s_pos[None, :] >= 0)
        sw = jnp.einsum('bghqd,bgkd->bghqk', qb_, kslab) - slopes[:, :, None, None] * jnp.abs(dist_w).astype(f32)
        pw = jax.nn.softmax(jnp.where(valid_w, sw, NEG_BIG), axis=-1)
        o_w = jnp.einsum('bghqk,bgkd->bghqd', pw, vslab)
        return o_c, o_w, idx

    o_c, o_w, idx = lax.map(sweep_cmp_win, (q_blocks(qh, QB), jnp.arange(T // QB) * QB))
    o_c, o_w = unblock(o_c), unblock(o_w)
    idx = jnp.moveaxis(idx, 0, 2).reshape(B, G, T, n_sel)

    QS = NSA_SEL_Q_BLOCK
    ks_blk = ks.reshape(B, G, n_slc, LS, dh)
    vs_blk = vs.reshape(B, G, n_slc, LS, dh)
    b_ix = jnp.arange(B)[:, None, None, None]
    g_ix = jnp.arange(G)[None, :, None, None]

    def sweep_slc(inp):
        qb_, idx_b, start = inp
        t = start + jnp.arange(QS)
        k_sel = ks_blk[b_ix, g_ix, idx_b]
        v_sel = vs_blk[b_ix, g_ix, idx_b]
        pos = idx_b[..., None] * LS + jnp.arange(LS)
        dist = t[None, None, :, None, None] - pos
        s = jnp.einsum('bghqd,bgqnkd->bghqnk', qb_, k_sel) - slopes[None, :, :, None, None, None] * jnp.abs(dist).astype(f32)[:, :, None]
        s = jnp.where((dist >= 0)[:, :, None], s, NEG_BIG)
        p = jax.nn.softmax(s.reshape(B, G, HPG, QS, n_sel * LS), axis=-1).reshape(s.shape)
        return jnp.einsum('bghqnk,bgqnkd->bghqd', p, v_sel)

    idx_blocks = jnp.moveaxis(idx.reshape(B, G, T // QS, QS, n_sel), 2, 0)
    o_s = unblock(lax.map(sweep_slc, (q_blocks(qh, QS), idx_blocks, jnp.arange(T // QS) * QS)))

    gt = jax.nn.sigmoid(gate.astype(f32)).reshape(B, T, G, HPG, 3).transpose(0, 2, 3, 1, 4)
    o = gt[..., 0:1] * o_c + gt[..., 1:2] * o_s + gt[..., 2:3] * o_w
    return o.transpose(0, 3, 1, 2, 4).reshape(B, T, NSA_HEADS * dh).astype(q.dtype)


def _mixer(h, w_in, gla_a2, gla_a_b, gla_norm, rwkv_mu, rwkv_w0, rwkv_w2, rwkv_a0, rwkv_a2, rwkv_g2,
           rwkv_k_k, rwkv_k_a, rwkv_r_k, rwkv_ln_w, rwkv_ln_b,
           nsa_pos_k, nsa_w1_k, nsa_w2_k, nsa_pos_v, nsa_w1_v, nsa_w2_v, p_merge, w_out):
    B, T, _ = h.shape
    u_gla, u_rwkv, u_nsa, u_gate = _split(h @ w_in, SECTION_WIDTHS)
    y_a = _gla(*_split(u_gla, GLA_WIDTHS), gla_a2, gla_a_b, gla_norm)
    u_rwkv = u_rwkv + (_token_shift(u_rwkv) - u_rwkv) * rwkv_mu
    y_b = _rwkv7(*_split(u_rwkv, RWKV_WIDTHS), rwkv_w0, rwkv_w2, rwkv_a0, rwkv_a2, rwkv_g2,
                 rwkv_k_k, rwkv_k_a, rwkv_r_k, rwkv_ln_w, rwkv_ln_b)
    y_c = _nsa(*_split(u_nsa, NSA_WIDTHS), nsa_pos_k, nsa_w1_k, nsa_w2_k, nsa_pos_v, nsa_w1_v, nsa_w2_v)
    gates = jax.nn.sigmoid(u_gate).reshape(B, T, N_BRANCHES, D_MODEL)
    merged = (gates[:, :, 0] * (y_a @ p_merge[0]) + gates[:, :, 1] * (y_b @ p_merge[1])
              + gates[:, :, 2] * (y_c @ p_merge[2]))
    return merged @ w_out


def _swiglu(h, w1, w3, w2):
    return (jax.nn.silu(h @ w1) * (h @ w3)) @ w2


def _moe(h, router, w1, w3, w2):
    logits = (h @ router).astype(jnp.float32)
    top_v, top_i = lax.top_k(logits, TOP_K)
    wts = jax.nn.softmax(top_v, axis=-1)
    combine = jnp.sum(jax.nn.one_hot(top_i, N_EXPERTS, dtype=jnp.float32) * wts[..., None], axis=-2).astype(h.dtype)
    out = jnp.zeros_like(h)
    for e in range(N_EXPERTS):
        out = out + combine[..., e:e + 1] * _swiglu(h, w1[e], w3[e], w2[e])
    return out


def setup_inputs(seed: int = 0) -> dict:
    key = jax.random.key(seed)
    keys = jax.random.split(key, 64)
    ctr = [0]
    f32 = jnp.float32

    def take():
        ctr[0] += 1
        return keys[ctr[0] - 1]

    def nrm(shape, scale):
        return scale * jax.random.normal(take(), shape, f32)

    def gain(shape):
        return 1.0 + nrm(shape, 0.1)

    L = DEPTH
    return {
        'x': nrm((BATCH, SEQ, D_MODEL), 1.0),
        'c': nrm((BATCH, D_MODEL), 1.0),
        'norm_mix': gain((L, D_MODEL)),
        'norm_ffn': gain((L, D_MODEL)),
        'ada_w': nrm((L, D_MODEL, 6 * D_MODEL), 0.01),
        'ada_b': nrm((L, 6 * D_MODEL), 0.01),
        'w_in': nrm((L, D_MODEL, D_IN), D_MODEL ** -0.5),
        'gla_a2': nrm((L, GLA_GATE_RANK, GLA_HEADS * GLA_DK), GLA_GATE_RANK ** -0.5),
        'gla_a_b': nrm((L, GLA_HEADS * GLA_DK), 0.1),
        'gla_norm': gain((L, GLA_DV)),
        'rwkv_mu': jax.random.uniform(take(), (L, sum(RWKV_WIDTHS)), f32, 0.0, 1.0),
        'rwkv_w0': jax.random.uniform(take(), (L, RWKV_DIM), f32, -6.0, 1.0),
        'rwkv_w2': nrm((L, RWKV_W_RANK, RWKV_DIM), 0.1 * RWKV_W_RANK ** -0.5),
        'rwkv_a0': nrm((L, RWKV_DIM), 0.1),
        'rwkv_a2': nrm((L, RWKV_A_RANK, RWKV_DIM), 0.1 * RWKV_A_RANK ** -0.5),
        'rwkv_g2': nrm((L, RWKV_G_RANK, RWKV_DIM), RWKV_G_RANK ** -0.5),
        'rwkv_k_k': 0.85 + nrm((L, RWKV_DIM), 0.1),
        'rwkv_k_a': gain((L, RWKV_DIM)),
        'rwkv_r_k': nrm((L, RWKV_HEADS, RWKV_HEAD_DIM), 0.1),
        'rwkv_ln_w': gain((L, RWKV_DIM)),
        'rwkv_ln_b': nrm((L, RWKV_DIM), 0.01),
        'nsa_pos_k': nrm((L, NSA_CMP_LEN, NSA_HEAD_DIM), 0.1),
        'nsa_w1_k': nrm((L, NSA_CMP_LEN * NSA_HEAD_DIM, NSA_CMP_HIDDEN), (NSA_CMP_LEN * NSA_HEAD_DIM) ** -0.5),
        'nsa_w2_k': nrm((L, NSA_CMP_HIDDEN, NSA_HEAD_DIM), NSA_CMP_HIDDEN ** -0.5),
        'nsa_pos_v': nrm((L, NSA_CMP_LEN, NSA_HEAD_DIM), 0.1),
        'nsa_w1_v': nrm((L, NSA_CMP_LEN * NSA_HEAD_DIM, NSA_CMP_HIDDEN), (NSA_CMP_LEN * NSA_HEAD_DIM) ** -0.5),
        'nsa_w2_v': nrm((L, NSA_CMP_HIDDEN, NSA_HEAD_DIM), NSA_CMP_HIDDEN ** -0.5),
        'p_merge': nrm((L, N_BRANCHES, BRANCH_DIM, D_MODEL), BRANCH_DIM ** -0.5),
        'w_out': nrm((L, D_MODEL, D_MODEL), D_MODEL ** -0.5),
        'ffn_w1': nrm((N_DENSE, D_MODEL, D_FF), D_MODEL ** -0.5),
        'ffn_w3': nrm((N_DENSE, D_MODEL, D_FF), D_MODEL ** -0.5),
        'ffn_w2': nrm((N_DENSE, D_FF, D_MODEL), D_FF ** -0.5),
        'moe_router': nrm((N_MOE, D_MODEL, N_EXPERTS), D_MODEL ** -0.5),
        'moe_w1': nrm((N_MOE, N_EXPERTS, D_MODEL, D_FF_EXPERT), D_MODEL ** -0.5),
        'moe_w3': nrm((N_MOE, N_EXPERTS, D_MODEL, D_FF_EXPERT), D_MODEL ** -0.5),
        'moe_w2': nrm((N_MOE, N_EXPERTS, D_FF_EXPERT, D_MODEL), D_FF_EXPERT ** -0.5),
        'final_norm': gain((D_MODEL,)),
    }


def reference(x, c, norm_mix, norm_ffn, ada_w, ada_b, w_in, gla_a2, gla_a_b, gla_norm,
              rwkv_mu, rwkv_w0, rwkv_w2, rwkv_a0, rwkv_a2, rwkv_g2, rwkv_k_k, rwkv_k_a, rwkv_r_k,
              rwkv_ln_w, rwkv_ln_b, nsa_pos_k, nsa_w1_k, nsa_w2_k, nsa_pos_v, nsa_w1_v, nsa_w2_v,
              p_merge, w_out, ffn_w1, ffn_w3, ffn_w2, moe_router, moe_w1, moe_w3, moe_w2, final_norm):
    c_act = jax.nn.silu(c)
    for l in range(DEPTH):
        ada = c_act @ ada_w[l] + ada_b[l]
        sh1, sc1, g1, sh2, sc2, g2 = jnp.split(ada, 6, axis=-1)
        h = _modulate(x, norm_mix[l], sh1, sc1)
        y = _mixer(h, w_in[l], gla_a2[l], gla_a_b[l], gla_norm[l], rwkv_mu[l], rwkv_w0[l], rwkv_w2[l],
                   rwkv_a0[l], rwkv_a2[l], rwkv_g2[l], rwkv_k_k[l], rwkv_k_a[l], rwkv_r_k[l],
                   rwkv_ln_w[l], rwkv_ln_b[l], nsa_pos_k[l], nsa_w1_k[l], nsa_w2_k[l],
                   nsa_pos_v[l], nsa_w1_v[l], nsa_w2_v[l], p_merge[l], w_out[l])
        x = x + g1[:, None, :] * y
        h = _modulate(x, norm_ffn[l], sh2, sc2)
        if l % 2 == 0:
            y = _swiglu(h, ffn_w1[l // 2], ffn_w3[l // 2], ffn_w2[l // 2])
        else:
            y = _moe(h, moe_router[l // 2], moe_w1[l // 2], moe_w3[l // 2], moe_w2[l // 2])
        x = x + g2[:, None, :] * y
    return _rmsnorm(x, final_norm)
```

```python
import functools

import numpy as np
import jax
import jax.numpy as jnp
from jax import lax
from jax.experimental import pallas as pl
from jax.experimental.pallas import tpu as pltpu

F32 = jnp.float32
BF16 = jnp.bfloat16
HI = lax.Precision.HIGHEST

V7X_VMEM_LIMIT_BYTES = 56 * 1024 * 1024

D_MODEL = 2048
NORM_EPS = 1e-6
NEG_BIG = -1e30

GLA_HEADS = 4
GLA_DK = 128
GLA_DV = 256
GLA_RANK = 16
GLA_NORMALIZER = 16.0
CHUNK = 64

RWKV_HEADS = 16
RWKV_N = 64
RWKV_DIM = RWKV_HEADS * RWKV_N
RWKV_W_RANK = 64
RWKV_A_RANK = 64
RWKV_G_RANK = 160
RWKV_LN_EPS = 64e-5
RWKV_HB = 4

NSA_HEADS = 16
NSA_GROUPS = 4
NSA_HPG = 4
NSA_DH = 64
NSA_CMP_LEN = 32
NSA_CMP_STRIDE = 16
NSA_CMP_HIDDEN = 128
NSA_SLC_LEN = 64
NSA_N_SEL = 16
NSA_WINDOW = 512
NSA_QB = 256
SEL_MASK_BIAS = 131072.0

BRANCH_DIM = 1024
D_FF = 5632
N_EXPERTS = 8

U_GLA_Q, U_GLA_K, U_GLA_V, U_GLA_G = 0, 512, 1024, 2048
U_RWKV_R, U_RWKV_K, U_RWKV_V = 3072, 4096, 5120
U_NSA_Q = 6144
U_NSA_KV = 7168
U_GATE = 8704
U_COLS = U_GATE + 3 * D_MODEL
S_GLA_A = 0
S_RWKV_WA = 128
S_RWKV_G = 256
S_NSA_GATE = 512
S_COLS = 640


def _cparams(sem, vmem=V7X_VMEM_LIMIT_BYTES):
    return pltpu.CompilerParams(dimension_semantics=sem, vmem_limit_bytes=vmem)


def _sigmoid(x):
    return 1.0 / (1.0 + jnp.exp(-x))


def _softplus(x):
    return jnp.maximum(x, 0.0) + jnp.log(1.0 + jnp.exp(-jnp.abs(x)))


def _dot(a, b):
    return jnp.dot(a, b, preferred_element_type=F32)


def _dot_nt(a, b):
    return lax.dot_general(a, b, (((1,), (1,)), ((), ())), preferred_element_type=F32)


def _dot_tn(a, b):
    return lax.dot_general(a, b, (((0,), (0,)), ((), ())), preferred_element_type=F32)


def _dot_hi(a, b):
    return jnp.dot(a, b, preferred_element_type=F32, precision=HI)


def _dot_split(x, w):
    hi = x.astype(BF16)
    lo = (x - hi.astype(F32)).astype(BF16)
    return _dot(hi, w) + _dot(lo, w)


def _ada_kernel(c_ref, w_ref, b_ref, o_ref):
    c = c_ref[...]
    o_ref[0] = _dot_hi(c * _sigmoid(c), w_ref[0]) + b_ref[0]


def ada_all(c, ada_w, ada_b):
    L, D, N6 = ada_w.shape
    B = c.shape[0]
    tn = 1024
    return pl.pallas_call(
        _ada_kernel,
        out_shape=jax.ShapeDtypeStruct((L, B, N6), F32),
        grid=(L, N6 // tn),
        in_specs=[
            pl.BlockSpec((B, D), lambda l, j: (0, 0)),
            pl.BlockSpec((1, D, tn), lambda l, j: (l, 0, j)),
            pl.BlockSpec((1, 1, tn), lambda l, j: (l, 0, j)),
        ],
        out_specs=pl.BlockSpec((1, B, tn), lambda l, j: (l, 0, j)),
        compiler_params=_cparams(("parallel", "parallel")),
        name="ada",
    )(c, ada_w, ada_b.reshape(L, 1, N6))


def _resmod_kernel(*refs, has_res, final, router):
    it = iter(refs)
    x_ref = next(it)
    y_ref = next(it) if has_res else None
    g_ref = next(it) if has_res else None
    ng_ref = next(it)
    sh_ref = None if final else next(it)
    sc_ref = None if final else next(it)
    rt_ref = next(it) if router else None
    xo_ref = next(it) if (has_res and not final) else None
    h_ref = next(it)
    lg_ref = next(it) if router else None

    x = x_ref[...]
    if has_res:
        x = x + g_ref[0] * y_ref[...].astype(F32)
        if xo_ref is not None:
            xo_ref[...] = x
    ms = jnp.mean(x * x, axis=-1, keepdims=True)
    h = x * lax.rsqrt(ms + NORM_EPS) * ng_ref[...]
    if not final:
        h = h * (1.0 + sc_ref[0]) + sh_ref[0]
    h_ref[...] = h.astype(h_ref.dtype)
    if router:
        lg_ref[...] = _dot_hi(h, rt_ref[...])


def resmod(x, y, g, norm_g, shift, scale, *, seq, router=None, final=False):
    N, D = x.shape
    tm = 512
    spb = seq // tm
    has_res = y is not None
    row = lambda i: (i, 0)
    per_b = lambda i: (i // spb, 0, 0)
    ins, specs = [x], [pl.BlockSpec((tm, D), row)]
    if has_res:
        ins += [y, g.reshape(-1, 1, D)]
        specs += [pl.BlockSpec((tm, D), row), pl.BlockSpec((1, 1, D), per_b)]
    ins.append(norm_g.reshape(1, D))
    specs.append(pl.BlockSpec((1, D), lambda i: (0, 0)))
    if not final:
        ins += [shift.reshape(-1, 1, D), scale.reshape(-1, 1, D)]
        specs += [pl.BlockSpec((1, 1, D), per_b), pl.BlockSpec((1, 1, D), per_b)]
    if router is not None:
        ins.append(router)
        specs.append(pl.BlockSpec(router.shape, lambda i: (0, 0)))
    outs, ospecs = [], []
    if has_res and not final:
        outs.append(jax.ShapeDtypeStruct((N, D), F32))
        ospecs.append(pl.BlockSpec((tm, D), row))
    outs.append(jax.ShapeDtypeStruct((N, D), F32 if final else BF16))
    ospecs.append(pl.BlockSpec((tm, D), row))
    if router is not None:
        outs.append(jax.ShapeDtypeStruct((N, 128), F32))
        ospecs.append(pl.BlockSpec((tm, 128), row))
    res = pl.pallas_call(
        functools.partial(_resmod_kernel, has_res=has_res, final=final, router=router is not None),
        out_shape=tuple(outs),
        grid=(N // tm,),
        in_specs=specs,
        out_specs=tuple(ospecs),
        compiler_params=_cparams(("parallel",)),
        name="resmod",
    )(*ins)
    return res


def _mm_kernel(x_ref, w_ref, o_ref):
    o_ref[...] = _dot(x_ref[...], w_ref[...]).astype(o_ref.dtype)


def matmul(x, w, out_dtype, tm, tn):
    M, K = x.shape
    N = w.shape[1]
    return pl.pallas_call(
        _mm_kernel,
        out_shape=jax.ShapeDtypeStruct((M, N), out_dtype),
        grid=(M // tm, N // tn),
        in_specs=[pl.BlockSpec((tm, K), lambda i, j: (i, 0)), pl.BlockSpec((K, tn), lambda i, j: (0, j))],
        out_specs=pl.BlockSpec((tm, tn), lambda i, j: (i, j)),
        compiler_params=_cparams(("parallel", "parallel")),
        name="matmul",
    )(x, w)


GLA_TC = 512


def _gla_kernel(q_ref, k_ref, v_ref, g_ref, alr_ref, a2_ref, ab_ref, ng_ref, o_ref, st_ref):
    @pl.when(pl.program_id(2) == 0)
    def _():
        st_ref[...] = jnp.zeros_like(st_ref)

    C = CHUNK
    la = _dot_hi(alr_ref[...], a2_ref[...]) + ab_ref[...]
    la = -_softplus(-la) / GLA_NORMALIZER
    ri = lax.broadcasted_iota(jnp.int32, (C, C), 0)
    ci = lax.broadcasted_iota(jnp.int32, (C, C), 1)
    causal = ri >= ci
    tril = causal.astype(F32)
    for c in range(GLA_TC // C):
        sl = slice(c * C, (c + 1) * C)
        bc = _dot_hi(tril, la[sl])
        bl = bc[C - 1:C, :]
        q = q_ref[sl, :].astype(F32) * (GLA_DK ** -0.5)
        k = k_ref[sl, :].astype(F32)
        v = v_ref[sl, :]
        qd = (q * jnp.exp(bc)).astype(BF16)
        kd = (k * jnp.exp(-bc)).astype(BF16)
        kl = (k * jnp.exp(bl - bc)).astype(BF16)
        att = jnp.where(causal, _dot_nt(qd, kd), 0.0).astype(BF16)
        st = st_ref[...]
        o = _dot(att, v) + _dot_nt(qd, st.astype(BF16))
        st_ref[...] = st * jnp.exp(bl) + _dot_tn(v, kl)
        o = o * lax.rsqrt(jnp.mean(o * o, axis=-1, keepdims=True) + NORM_EPS) * ng_ref[...]
        gg = g_ref[sl, :].astype(F32)
        o_ref[sl, :] = (o * (gg * _sigmoid(gg))).astype(o_ref.dtype)


def gla(U, Us, a2, a_b, norm_g, *, batch, seq):
    N = U.shape[0]
    nt = seq // GLA_TC
    a2p = jnp.zeros((128, GLA_HEADS * GLA_DK), F32).at[:GLA_RANK].set(a2)
    row = lambda b, h, i: b * nt + i
    return pl.pallas_call(
        _gla_kernel,
        out_shape=jax.ShapeDtypeStruct((N, GLA_HEADS * GLA_DV), BF16),
        grid=(batch, GLA_HEADS, nt),
        in_specs=[
            pl.BlockSpec((GLA_TC, GLA_DK), lambda b, h, i: (row(b, h, i), U_GLA_Q // GLA_DK + h)),
            pl.BlockSpec((GLA_TC, GLA_DK), lambda b, h, i: (row(b, h, i), U_GLA_K // GLA_DK + h)),
            pl.BlockSpec((GLA_TC, GLA_DV), lambda b, h, i: (row(b, h, i), U_GLA_V // GLA_DV + h)),
            pl.BlockSpec((GLA_TC, GLA_DV), lambda b, h, i: (row(b, h, i), U_GLA_G // GLA_DV + h)),
            pl.BlockSpec((GLA_TC, 128), lambda b, h, i: (row(b, h, i), S_GLA_A // 128)),
            pl.BlockSpec((128, GLA_DK), lambda b, h, i: (0, h)),
            pl.BlockSpec((1, GLA_DK), lambda b, h, i: (0, h)),
            pl.BlockSpec((1, GLA_DV), lambda b, h, i: (0, 0)),
        ],
        out_specs=pl.BlockSpec((GLA_TC, GLA_DV), lambda b, h, i: (row(b, h, i), h)),
        scratch_shapes=[pltpu.VMEM((GLA_DV, GLA_DK), F32)],
        compiler_params=_cparams(("parallel", "parallel", "arbitrary")),
        name="gla",
    )(U, U, U, U, Us, a2p, a_b.reshape(1, -1), norm_g.reshape(1, -1))


RWKV_TM = 256


def _seg_ones(n=256, seg=RWKV_N):
    i = np.arange(n)
    return jnp.asarray((i[:, None] // seg == i[None, :] // seg).astype(np.float32), BF16)


def _seg_sum(x, bd):
    outs = [_dot_split(x[:, s:s + 256], bd) for s in range(0, x.shape[1], 256)]
    return outs[0] if len(outs) == 1 else jnp.concatenate(outs, axis=1)


def _shift_lerp(u_ref, p_ref, mu, first):
    u = u_ref[...].astype(F32)
    prev_last = jnp.where(first, 0.0, p_ref[7:8, :].astype(F32))
    rolled = pltpu.roll(u, 1, 0)
    is_row0 = lax.broadcasted_iota(jnp.int32, u.shape, 0) == 0
    sh = jnp.where(is_row0, prev_last, rolled)
    return u + (sh - u) * mu


def _rwkv_prep_kernel(r_ref, k_ref, v_ref, wa_ref, gl_ref, rp_ref, kp_ref, vp_ref, wap_ref, glp_ref,
                      mur_ref, muk_ref, muv_ref, muwa_ref, mug_ref, w0_ref, w2_ref, a0_ref, a2_ref, g2_ref,
                      kk_ref, ka_ref, bd_ref,
                      ro_ref, ko_ref, vo_ref, kap_ref, bo_ref, lw_ref, go_ref, *, spb):
    first = (pl.program_id(0) % spb) == 0
    r = _shift_lerp(r_ref, rp_ref, mur_ref[...], first)
    k = _shift_lerp(k_ref, kp_ref, muk_ref[...], first)
    v = _shift_lerp(v_ref, vp_ref, muv_ref[...], first)
    wa = _shift_lerp(wa_ref, wap_ref, muwa_ref[...], first)
    gl = _shift_lerp(gl_ref, glp_ref, mug_ref[...], first)
    w_log = -_softplus(-(w0_ref[...] + _dot_hi(jnp.tanh(wa), w2_ref[...]))) - 0.5
    lw_ref[...] = -jnp.exp(w_log)
    a = _sigmoid(a0_ref[...] + _dot_hi(wa, a2_ref[...]))
    go_ref[...] = _dot(_sigmoid(gl).astype(BF16), g2_ref[...]).astype(go_ref.dtype)
    kk = k * kk_ref[...]
    nrm = jnp.sqrt(_seg_sum(kk * kk, bd_ref[...]))
    kk = kk / jnp.maximum(nrm, 1e-12)
    ro_ref[...] = r.astype(ro_ref.dtype)
    ko_ref[...] = (k * (1.0 + (a - 1.0) * ka_ref[...])).astype(ko_ref.dtype)
    vo_ref[...] = v.astype(vo_ref.dtype)
    kap_ref[...] = kk.astype(kap_ref.dtype)
    bo_ref[...] = (kk * a).astype(bo_ref.dtype)


def rwkv_prep(U, Us, mu, w0, w2, a0, a2, g2, k_k, k_a, *, seq):
    N = U.shape[0]
    tm = RWKV_TM
    spb = seq // tm
    R = RWKV_DIM
    cur = lambda cb: (lambda i: (i, cb))
    prv = lambda cb: (lambda i: (jnp.maximum(i * (tm // 8) - 1, 0), cb))
    mu_r, mu_k, mu_v = mu[:R], mu[R:2 * R], mu[2 * R:3 * R]
    mu_wa = mu[3 * R:3 * R + 128]
    mu_g = jnp.zeros((256,), F32).at[:RWKV_G_RANK].set(mu[3 * R + 128:])
    w2p = jnp.zeros((128, R), F32).at[:RWKV_W_RANK].set(w2)
    a2p = jnp.zeros((128, R), F32).at[RWKV_W_RANK:].set(a2)
    g2p = jnp.zeros((256, R), BF16).at[:RWKV_G_RANK].set(g2.astype(BF16))
    vec = lambda a: a.reshape(1, -1)
    full = lambda a: pl.BlockSpec(a.shape, lambda i: (0, 0))
    params = [vec(mu_r), vec(mu_k), vec(mu_v), vec(mu_wa), vec(mu_g), vec(w0), w2p, vec(a0), a2p, g2p,
              vec(k_k), vec(k_a), _seg_ones()]
    in_specs = [
        pl.BlockSpec((tm, R), cur(U_RWKV_R // R)), pl.BlockSpec((tm, R), cur(U_RWKV_K // R)),
        pl.BlockSpec((tm, R), cur(U_RWKV_V // R)),
        pl.BlockSpec((tm, 128), cur(S_RWKV_WA // 128)), pl.BlockSpec((tm, 256), cur(S_RWKV_G // 256)),
        pl.BlockSpec((8, R), prv(U_RWKV_R // R)), pl.BlockSpec((8, R), prv(U_RWKV_K // R)),
        pl.BlockSpec((8, R), prv(U_RWKV_V // R)),
        pl.BlockSpec((8, 128), prv(S_RWKV_WA // 128)), pl.BlockSpec((8, 256), prv(S_RWKV_G // 256)),
    ] + [full(p) for p in params]
    out = lambda dt: jax.ShapeDtypeStruct((N, R), dt)
    ospec = pl.BlockSpec((tm, R), lambda i: (i, 0))
    return pl.pallas_call(
        functools.partial(_rwkv_prep_kernel, spb=spb),
        out_shape=(out(BF16), out(BF16), out(BF16), out(BF16), out(BF16), out(F32), out(BF16)),
        grid=(N // tm,),
        in_specs=in_specs,
        out_specs=(ospec,) * 7,
        compiler_params=_cparams(("parallel",)),
        name="rwkv_prep",
    )(U, U, U, Us, Us, U, U, U, Us, Us, *params)


RWKV_TC = 512
RWKV_W = RWKV_HB * RWKV_N


def _rwkv_masks():
    W, C = RWKV_W, CHUNK
    i = np.arange(W)
    same = i[:, None] // C == i[None, :] // C
    m_bd = same.astype(np.float32)
    low_s = (same & (i[:, None] % C > i[None, :] % C)).astype(np.float32)
    low_i = (same & (i[:, None] % C >= i[None, :] % C)).astype(np.float32)
    tril = np.tril(np.ones((C, C), np.float32))
    return (jnp.asarray(m_bd), jnp.asarray(low_s), jnp.asarray(low_i), jnp.asarray(tril),
            jnp.asarray(np.eye(W, dtype=np.float32)))


def _tile4(x):
    return jnp.concatenate([x] * RWKV_HB, axis=0)


def _rwkv_chunk_kernel(r_ref, k_ref, v_ref, kap_ref, b_ref, lw_ref, g_ref,
                       mbd_ref, lows_ref, lowi_ref, tril_ref, eye_ref, bd_ref,
                       rk_ref, lnw_ref, lnb_ref, o_ref, st_ref):
    @pl.when(pl.program_id(2) == 0)
    def _():
        st_ref[...] = jnp.zeros_like(st_ref)

    C, W = CHUNK, RWKV_W
    m_bd = mbd_ref[...]
    low_s = lows_ref[...]
    low_i = lowi_ref[...]
    tril = tril_ref[...]
    eye = eye_ref[...]
    bd = bd_ref[...]

    def expand(x):
        return (_tile4(x) * m_bd).astype(BF16)

    def body(c, carry):
        sl = pl.ds(pl.multiple_of(c * C, C), C)
        r = r_ref[sl, :].astype(F32)
        k = k_ref[sl, :].astype(F32)
        v = v_ref[sl, :].astype(F32)
        kap = kap_ref[sl, :].astype(F32)
        b = b_ref[sl, :].astype(F32)
        lw = lw_ref[sl, :]
        cum = _dot_hi(tril, lw)
        p_in = jnp.exp(cum)
        p_ex = jnp.exp(cum - lw)
        p_inv = jnp.exp(-cum)
        clast = cum[C - 1:C, :]
        p_tail = jnp.exp(clast - cum)
        xe = jnp.concatenate([expand(kap * p_ex), expand(r * p_in)], axis=0)
        he = jnp.concatenate([expand(b * p_inv), expand(k * p_inv)], axis=0)
        sc = _dot_nt(xe, he)
        a_m = sc[:W, :W] * low_s
        bm = (sc[:W, W:] * low_s).astype(BF16)
        qb = (sc[W:, :W] * low_i).astype(BF16)
        qk = (sc[W:, W:] * low_i).astype(BF16)
        tinv = eye - a_m
        pw = a_m
        for _ in range(int(np.log2(C)) - 1):
            pwb = pw.astype(BF16)
            pw = _dot(pwb, pwb)
            tinv = tinv + _dot(tinv.astype(BF16), pw.astype(BF16))
        ve = expand(v)
        st = st_ref[...]
        xs = _dot_nt(xe, st.astype(BF16))
        u = _dot(tinv.astype(BF16), (xs[:W] + _dot(bm, ve)).astype(BF16))
        ub = u.astype(BF16)
        y_e = xs[W:] + _dot(qk, ve) - _dot(qb, ub)
        vu = jnp.concatenate([ve, (-u).astype(BF16)], axis=0)
        kb = jnp.concatenate([expand(k * p_tail), expand(b * p_tail)], axis=0)
        st_ref[...] = st * jnp.exp(clast) + _dot_tn(vu, kb) * m_bd
        y = y_e[0:C] + y_e[C:2 * C] + y_e[2 * C:3 * C] + y_e[3 * C:4 * C]
        mu = _seg_sum(y, bd) * (1.0 / RWKV_N)
        yc = y - mu
        var = _seg_sum(yc * yc, bd) * (1.0 / RWKV_N)
        yn = yc * lax.rsqrt(var + RWKV_LN_EPS) * lnw_ref[...] + lnb_ref[...]
        bonus = _seg_sum(r * k * rk_ref[...], bd) * v
        o_ref[sl, :] = ((yn + bonus) * g_ref[sl, :].astype(F32)).astype(o_ref.dtype)
        return carry

    lax.fori_loop(0, RWKV_TC // C, body, 0)


def rwkv_chunk(r, k, v, kap, b, lw, gate, r_k, ln_w, ln_b, *, batch, seq):
    N, R = r.shape
    nt = seq // RWKV_TC
    W = RWKV_W
    blk = pl.BlockSpec((RWKV_TC, W), lambda bb, h, i: (bb * nt + i, h))
    masks = _rwkv_masks() + (_seg_ones(),)
    full = lambda a: pl.BlockSpec(a.shape, lambda bb, h, i: (0, 0))
    pvec = pl.BlockSpec((1, W), lambda bb, h, i: (0, h))
    return pl.pallas_call(
        _rwkv_chunk_kernel,
        out_shape=jax.ShapeDtypeStruct((N, R), BF16),
        grid=(batch, R // W, nt),
        in_specs=[blk] * 7 + [full(m) for m in masks] + [pvec] * 3,
        out_specs=blk,
        scratch_shapes=[pltpu.VMEM((W, W), F32)],
        compiler_params=_cparams(("parallel", "parallel", "arbitrary")),
        name="rwkv_chunk",
    )(r, k, v, kap, b, lw, gate, *masks, r_k.reshape(1, R), ln_w.reshape(1, R), ln_b.reshape(1, R))


def _gelu_tanh(x):
    return 0.5 * x * (1.0 + jnp.tanh(np.sqrt(2.0 / np.pi) * (x + 0.044715 * (x * x * x))))


def _nsa_compress_kernel(x_ref, pos_ref, w1_ref, w2_ref, o_ref):
    x = x_ref[0, 0]
    w1 = w1_ref[0]
    half = w1.shape[0] // 2
    nrow = x.shape[0]
    ha = _dot(x, w1[:half])
    hb = _dot(x, w1[half:])
    h = ha + pltpu.roll(hb, nrow - 1, 0)
    pb = _dot(pos_ref[0], w1)
    h = _gelu_tanh(h + pb[0:1, :])
    o_ref[0, 0, 0] = _dot(h.astype(BF16), w2_ref[0]).astype(o_ref.dtype)


def nsa_compress(kv_cmp, pos, w1, w2):
    two, B, G, T, dh = kv_cmp.shape
    nr = T // NSA_CMP_STRIDE
    x = kv_cmp.reshape(two * B, G, nr, NSA_CMP_STRIDE * dh)
    posf = jnp.broadcast_to(pos.reshape(two, 1, NSA_CMP_LEN * dh), (two, 8, NSA_CMP_LEN * dh)).astype(BF16)
    return pl.pallas_call(
        _nsa_compress_kernel,
        out_shape=jax.ShapeDtypeStruct((two, B, G, nr, dh), BF16),
        grid=(two, B, G),
        in_specs=[
            pl.BlockSpec((1, 1, nr, NSA_CMP_STRIDE * dh), lambda s, b, g: (s * B + b, g, 0, 0)),
            pl.BlockSpec((1, 8, NSA_CMP_LEN * dh), lambda s, b, g: (s, 0, 0)),
            pl.BlockSpec((1, NSA_CMP_LEN * dh, NSA_CMP_HIDDEN), lambda s, b, g: (s, 0, 0)),
            pl.BlockSpec((1, NSA_CMP_HIDDEN, dh), lambda s, b, g: (s, 0, 0)),
        ],
        out_specs=pl.BlockSpec((1, 1, 1, nr, dh), lambda s, b, g: (s, b, g, 0, 0)),
        compiler_params=_cparams(("parallel", "parallel", "parallel")),
        name="nsa_compress",
    )(x, posf, w1.astype(BF16), w2.astype(BF16))


def _nsa_cmp_kernel(slope_ref, q_ref, kc_ref, vc_ref, ov_ref, oc_ref, sb_ref, *, n_slc, n_sel):
    g = pl.program_id(1)
    qi = pl.program_id(2)
    QB = NSA_QB
    ncp = kc_ref.shape[3]
    kc = kc_ref[0, 0, 0]
    vc = vc_ref[0, 0, 0]
    t = qi * QB + lax.broadcasted_iota(jnp.int32, (QB, ncp), 0)
    n = lax.broadcasted_iota(jnp.int32, (QB, ncp), 1)
    valid = (n * NSA_CMP_STRIDE + (NSA_CMP_LEN - 1)) <= t
    adist = jnp.abs(t.astype(F32) - (n.astype(F32) * NSA_CMP_STRIDE + (NSA_CMP_LEN - 1) / 2.0))
    psum = jnp.zeros((QB, ncp), F32)
    for h in range(NSA_HPG):
        qh = q_ref[:, h * NSA_DH:(h + 1) * NSA_DH] * (NSA_DH ** -0.5)
        s = _dot_nt(qh, kc) - slope_ref[g * NSA_HPG + h] * adist
        s = jnp.where(valid, s, NEG_BIG)
        m = jnp.max(s, axis=-1, keepdims=True)
        e = jnp.where(valid, jnp.exp(s - m), 0.0)
        l = jnp.sum(e, axis=-1, keepdims=True)
        p = e / jnp.maximum(l, 1e-30)
        oc_ref[:, h * NSA_DH:(h + 1) * NSA_DH] = _dot(p.astype(BF16), vc).astype(oc_ref.dtype)
        psum = psum + p
    imp = lax.dot_general(ov_ref[...], psum, (((1,), (1,)), ((), ())), preferred_element_type=F32, precision=HI)
    j = lax.broadcasted_iota(jnp.int32, (n_slc, QB), 0)
    tt = qi * QB + lax.broadcasted_iota(jnp.int32, (n_slc, QB), 1)
    cur = jnp.right_shift(tt, 6)
    forced = (j == 0) | (j == cur) | (j == cur - 1)
    cand = (j >= 1) & (j <= cur - 2)
    cur_row = cur[0:1, :]
    rank = jnp.zeros((n_slc, QB), jnp.int32)
    for jp in range(1, n_slc):
        row = imp[jp:jp + 1, :]
        ahead = (row > imp) | ((row == imp) & (jp < j))
        rank = rank + jnp.where(ahead & (jp <= cur_row - 2), 1, 0)
    sel = forced | (cand & (rank < n_sel - 3))
    sb = jnp.where(sel, 0.0, -SEL_MASK_BIAS)
    sb_ref[0, 0] = sb.T.astype(sb_ref.dtype)


def nsa_cmp(U, kvc, slopes, *, batch, seq):
    N = U.shape[0]
    QB = NSA_QB
    nq = seq // QB
    ncp = kvc.shape[3]
    n_slc = seq // NSA_SLC_LEN
    n_sel = min(NSA_N_SEL, n_slc)
    nn = np.arange(ncp)
    jj = np.arange(n_slc)
    ov = ((nn[None, :] * NSA_CMP_STRIDE + NSA_CMP_LEN - 1 >= jj[:, None] * NSA_SLC_LEN)
          & (nn[None, :] * NSA_CMP_STRIDE <= jj[:, None] * NSA_SLC_LEN + NSA_SLC_LEN - 1)
          & (nn[None, :] < ncp - 1)).astype(np.float32)
    G = NSA_GROUPS
    W = NSA_HPG * NSA_DH
    grid_spec = pltpu.PrefetchScalarGridSpec(
        num_scalar_prefetch=1,
        grid=(batch, G, nq),
        in_specs=[
            pl.BlockSpec((QB, W), lambda b, g, i, s: (b * nq + i, U_NSA_Q // W + g)),
            pl.BlockSpec((1, 1, 1, ncp, NSA_DH), lambda b, g, i, s: (0, b, g, 0, 0)),
            pl.BlockSpec((1, 1, 1, ncp, NSA_DH), lambda b, g, i, s: (1, b, g, 0, 0)),
            pl.BlockSpec((n_slc, ncp), lambda b, g, i, s: (0, 0)),
        ],
        out_specs=(
            pl.BlockSpec((QB, W), lambda b, g, i, s: (b * nq + i, g)),
            pl.BlockSpec((1, 1, QB, n_slc), lambda b, g, i, s: (b, g, i, 0)),
        ),
    )
    return pl.pallas_call(
        functools.partial(_nsa_cmp_kernel, n_slc=n_slc, n_sel=n_sel),
        out_shape=(jax.ShapeDtypeStruct((N, NSA_HEADS * NSA_DH), BF16),
                   jax.ShapeDtypeStruct((batch, G, seq, n_slc), BF16)),
        grid_spec=grid_spec,
        compiler_params=_cparams(("parallel", "parallel", "parallel")),
        name="nsa_cmp",
    )(slopes, U, kvc, kvc, jnp.asarray(ov))


def _nsa_win_kernel(slope_ref, q_ref, k0_ref, k1_ref, k2_ref, v0_ref, v1_ref, v2_ref, o_ref):
    g = pl.program_id(1)
    qi = pl.program_id(2)
    QB = NSA_QB
    KW = NSA_WINDOW + QB
    k = jnp.concatenate([k0_ref[0, 0], k1_ref[0, 0], k2_ref[0, 0]], axis=0)
    v = jnp.concatenate([v0_ref[0, 0], v1_ref[0, 0], v2_ref[0, 0]], axis=0)
    r = lax.broadcasted_iota(jnp.int32, (QB, KW), 0)
    c = lax.broadcasted_iota(jnp.int32, (QB, KW), 1)
    dist = r - c + NSA_WINDOW
    pos = qi * QB - NSA_WINDOW + c
    valid = (dist >= 0) & (dist < NSA_WINDOW) & (pos >= 0)
    adist = jnp.abs(dist).astype(F32)
    for h in range(NSA_HPG):
        qh = q_ref[:, h * NSA_DH:(h + 1) * NSA_DH] * (NSA_DH ** -0.5)
        s = _dot_nt(qh, k) - slope_ref[g * NSA_HPG + h] * adist
        s = jnp.where(valid, s, NEG_BIG)
        m = jnp.max(s, axis=-1, keepdims=True)
        e = jnp.exp(s - m)
        l = jnp.sum(e, axis=-1, keepdims=True)
        o = _dot(e.astype(BF16), v) / l
        o_ref[:, h * NSA_DH:(h + 1) * NSA_DH] = o.astype(o_ref.dtype)


def nsa_win(U, kw_pad, vw_pad, slopes, *, batch, seq):
    N = U.shape[0]
    QB = NSA_QB
    nq = seq // QB
    G = NSA_GROUPS
    W = NSA_HPG * NSA_DH
    kv = lambda off: pl.BlockSpec((1, 1, QB, NSA_DH), lambda b, g, i, s: (b, g, i + off, 0))
    grid_spec = pltpu.PrefetchScalarGridSpec(
        num_scalar_prefetch=1,
        grid=(batch, G, nq),
        in_specs=[pl.BlockSpec((QB, W), lambda b, g, i, s: (b * nq + i, U_NSA_Q // W + g)),
                  kv(0), kv(1), kv(2), kv(0), kv(1), kv(2)],
        out_specs=pl.BlockSpec((QB, W), lambda b, g, i, s: (b * nq + i, g)),
    )
    return pl.pallas_call(
        _nsa_win_kernel,
        out_shape=jax.ShapeDtypeStruct((N, NSA_HEADS * NSA_DH), BF16),
        grid_spec=grid_spec,
        compiler_params=_cparams(("parallel", "parallel", "parallel")),
        name="nsa_win",
    )(slopes, U, kw_pad, kw_pad, kw_pad, vw_pad, vw_pad, vw_pad)


def _nsa_sel_kernel(slope_ref, q_ref, sb_ref, ka_ref, v_ref, o_ref, qa_ref, m_ref, l_ref, acc_ref):
    g = pl.program_id(1)
    qi = pl.program_id(2)
    kj = pl.program_id(3)
    QB = NSA_QB
    nblk = sb_ref.shape[3]

    @pl.when(kj == 0)
    def _():
        for h in range(NSA_HPG):
            qa_ref[h * QB:(h + 1) * QB, 0:NSA_DH] = q_ref[:, h * NSA_DH:(h + 1) * NSA_DH] * (NSA_DH ** -0.5)
            qa_ref[h * QB:(h + 1) * QB, NSA_DH:NSA_DH + nblk] = sb_ref[0, 0]
        m_ref[...] = jnp.full_like(m_ref, NEG_BIG)
        l_ref[...] = jnp.zeros_like(l_ref)
        acc_ref[...] = jnp.zeros_like(acc_ref)

    @pl.when(kj <= qi)
    def _():
        s = _dot_nt(qa_ref[...], ka_ref[0, 0])
        t = qi * QB + lax.broadcasted_iota(jnp.int32, (QB, QB), 0)
        pos = kj * QB + lax.broadcasted_iota(jnp.int32, (QB, QB), 1)
        dist = t - pos
        causal = dist >= 0
        adist = jnp.abs(dist).astype(F32)
        v = v_ref[0, 0]
        for h in range(NSA_HPG):
            rows = slice(h * QB, (h + 1) * QB)
            sh = jnp.where(causal, s[rows] - slope_ref[g * NSA_HPG + h] * adist, NEG_BIG)
            m_old = m_ref[rows]
            m_new = jnp.maximum(m_old, jnp.max(sh, axis=-1, keepdims=True))
            alpha = jnp.exp(m_old - m_new)
            e = jnp.exp(sh - m_new)
            l_ref[rows] = alpha * l_ref[rows] + jnp.sum(e, axis=-1, keepdims=True)
            acc_ref[rows] = alpha * acc_ref[rows] + _dot(e.astype(BF16), v)
            m_ref[rows] = m_new

    @pl.when(kj == qi)
    def _():
        for h in range(NSA_HPG):
            rows = slice(h * QB, (h + 1) * QB)
            o_ref[:, h * NSA_DH:(h + 1) * NSA_DH] = (acc_ref[rows] / l_ref[rows]).astype(o_ref.dtype)


def nsa_sel(U, selbias, k_aug, v_slc, slopes, *, batch, seq):
    N = U.shape[0]
    QB = NSA_QB
    nq = seq // QB
    G = NSA_GROUPS
    W = NSA_HPG * NSA_DH
    n_slc = selbias.shape[3]
    ka_w = k_aug.shape[3]
    grid_spec = pltpu.PrefetchScalarGridSpec(
        num_scalar_prefetch=1,
        grid=(batch, G, nq, nq),
        in_specs=[
            pl.BlockSpec((QB, W), lambda b, g, i, j, s: (b * nq + i, U_NSA_Q // W + g)),
            pl.BlockSpec((1, 1, QB, n_slc), lambda b, g, i, j, s: (b, g, i, 0)),
            pl.BlockSpec((1, 1, QB, ka_w), lambda b, g, i, j, s: (b, g, jnp.minimum(j, i), 0)),
            pl.BlockSpec((1, 1, QB, NSA_DH), lambda b, g, i, j, s: (b, g, jnp.minimum(j, i), 0)),
        ],
        out_specs=pl.BlockSpec((QB, W), lambda b, g, i, j, s: (b * nq + i, g)),
        scratch_shapes=[
            pltpu.VMEM((NSA_HPG * QB, ka_w), BF16),
            pltpu.VMEM((NSA_HPG * QB, 1), F32),
            pltpu.VMEM((NSA_HPG * QB, 1), F32),
            pltpu.VMEM((NSA_HPG * QB, NSA_DH), F32),
        ],
    )
    return pl.pallas_call(
        _nsa_sel_kernel,
        out_shape=jax.ShapeDtypeStruct((N, NSA_HEADS * NSA_DH), BF16),
        grid_spec=grid_spec,
        compiler_params=_cparams(("parallel", "parallel", "parallel", "arbitrary")),
        name="nsa_sel",
    )(slopes, U, selbias, k_aug, v_slc)


def _nsa_combine_kernel(oc_ref, os_ref, ow_ref, gate_ref, e_ref, o_ref):
    ge = _dot_split(_sigmoid(gate_ref[...]), e_ref[...])
    Wd = NSA_HEADS * NSA_DH
    o = (ge[:, :Wd] * oc_ref[...].astype(F32) + ge[:, Wd:2 * Wd] * os_ref[...].astype(F32)
         + ge[:, 2 * Wd:] * ow_ref[...].astype(F32))
    o_ref[...] = o.astype(o_ref.dtype)


def nsa_combine(o_c, o_s, o_w, Us):
    N, Wd = o_c.shape
    tm = 512
    e = np.zeros((128, 3 * Wd), np.float32)
    for h in range(NSA_HEADS):
        for j in range(3):
            e[h * 3 + j, j * Wd + h * NSA_DH:j * Wd + (h + 1) * NSA_DH] = 1.0
    blk = pl.BlockSpec((tm, Wd), lambda i: (i, 0))
    return pl.pallas_call(
        _nsa_combine_kernel,
        out_shape=jax.ShapeDtypeStruct((N, Wd), BF16),
        grid=(N // tm,),
        in_specs=[blk, blk, blk, pl.BlockSpec((tm, 128), lambda i: (i, S_NSA_GATE // 128)),
                  pl.BlockSpec((128, 3 * Wd), lambda i: (0, 0))],
        out_specs=blk,
        compiler_params=_cparams(("parallel",)),
        name="nsa_combine",
    )(o_c, o_s, o_w, Us, jnp.asarray(e, BF16))


def _merge_kernel(ya_ref, yb_ref, yc_ref, ga_ref, gb_ref, gc_ref, p_ref, o_ref):
    m = (_sigmoid(ga_ref[...].astype(F32)) * _dot(ya_ref[...], p_ref[0])
         + _sigmoid(gb_ref[...].astype(F32)) * _dot(yb_ref[...], p_ref[1])
         + _sigmoid(gc_ref[...].astype(F32)) * _dot(yc_ref[...], p_ref[2]))
    o_ref[...] = m.astype(o_ref.dtype)


def merge(y_a, y_b, y_c, U, p_merge):
    N = y_a.shape[0]
    tm, tn = 1024, 512
    yb = pl.BlockSpec((tm, BRANCH_DIM), lambda i, j: (i, 0))
    gate = lambda br: pl.BlockSpec((tm, tn), lambda i, j: (i, (U_GATE + br * D_MODEL) // tn + j))
    return pl.pallas_call(
        _merge_kernel,
        out_shape=jax.ShapeDtypeStruct((N, D_MODEL), BF16),
        grid=(N // tm, D_MODEL // tn),
        in_specs=[yb, yb, yb, gate(0), gate(1), gate(2),
                  pl.BlockSpec((3, BRANCH_DIM, tn), lambda i, j: (0, 0, j))],
        out_specs=pl.BlockSpec((tm, tn), lambda i, j: (i, j)),
        compiler_params=_cparams(("parallel", "parallel")),
        name="merge",
    )(y_a, y_b, y_c, U, U, U, p_merge)


def _ffn_kernel(h_ref, w1_ref, w3_ref, w2_ref, o_ref, acc_ref):
    f = pl.program_id(1)

    @pl.when(f == 0)
    def _():
        acc_ref[...] = jnp.zeros_like(acc_ref)

    h = h_ref[...]
    a = _dot(h, w1_ref[...])
    z = (a * _sigmoid(a) * _dot(h, w3_ref[...])).astype(BF16)
    acc_ref[...] += _dot(z, w2_ref[...])

    @pl.when(f == pl.num_programs(1) - 1)
    def _():
        o_ref[...] = acc_ref[...].astype(o_ref.dtype)


def ffn(h, w1, w3, w2):
    N, D = h.shape
    F = w1.shape[1]
    tm, tf = 1024, 512
    return pl.pallas_call(
        _ffn_kernel,
        out_shape=jax.ShapeDtypeStruct((N, D), BF16),
        grid=(N // tm, F // tf),
        in_specs=[pl.BlockSpec((tm, D), lambda i, f: (i, 0)),
                  pl.BlockSpec((D, tf), lambda i, f: (0, f)),
                  pl.BlockSpec((D, tf), lambda i, f: (0, f)),
                  pl.BlockSpec((tf, D), lambda i, f: (f, 0))],
        out_specs=pl.BlockSpec((tm, D), lambda i, f: (i, 0)),
        scratch_shapes=[pltpu.VMEM((tm, D), F32)],
        compiler_params=_cparams(("parallel", "arbitrary")),
        name="ffn",
    )(h, w1, w3, w2)


def _route_kernel(lg_ref, cb_ref):
    lg = lg_ref[...]
    lane = lax.broadcasted_iota(jnp.int32, lg.shape, 1)
    x = jnp.where(lane < N_EXPERTS, lg, NEG_BIG)
    v1 = jnp.max(x, axis=-1, keepdims=True)
    i1 = jnp.min(jnp.where(x == v1, lane, 1024), axis=-1, keepdims=True)
    x2 = jnp.where(lane == i1, NEG_BIG, x)
    v2 = jnp.max(x2, axis=-1, keepdims=True)
    i2 = jnp.min(jnp.where(x2 == v2, lane, 1024), axis=-1, keepdims=True)
    e2 = jnp.exp(v2 - v1)
    w1 = 1.0 / (1.0 + e2)
    w2 = e2 / (1.0 + e2)
    cb_ref[...] = jnp.where(lane == i1, w1, 0.0) + jnp.where(lane == i2, w2, 0.0)


def route(logits):
    N = logits.shape[0]
    tm = 1024
    blk = pl.BlockSpec((tm, 128), lambda i: (i, 0))
    return pl.pallas_call(
        _route_kernel,
        out_shape=jax.ShapeDtypeStruct((N, 128), F32),
        grid=(N // tm,),
        in_specs=[blk],
        out_specs=blk,
        compiler_params=_cparams(("parallel",)),
        name="route",
    )(logits)


def _moe_kernel(h_ref, cb_ref, w1_ref, w3_ref, w2_ref, o_ref, acc_ref):
    e = pl.program_id(1)
    f = pl.program_id(2)

    @pl.when((e == 0) & (f == 0))
    def _():
        acc_ref[...] = jnp.zeros_like(acc_ref)

    cb = cb_ref[...]
    lane = lax.broadcasted_iota(jnp.int32, cb.shape, 1)
    w = jnp.sum(jnp.where(lane == e, cb, 0.0), axis=-1, keepdims=True)
    h = h_ref[...]
    a = _dot(h, w1_ref[0])
    z = (a * _sigmoid(a) * _dot(h, w3_ref[0]) * w).astype(BF16)
    acc_ref[...] += _dot(z, w2_ref[0])

    @pl.when((e == pl.num_programs(1) - 1) & (f == pl.num_programs(2) - 1))
    def _():
        o_ref[...] = acc_ref[...].astype(o_ref.dtype)


def moe(h, combine, w1, w3, w2):
    N, D = h.shape
    E, _, F = w1.shape
    tm, tf = 1024, 512
    return pl.pallas_call(
        _moe_kernel,
        out_shape=jax.ShapeDtypeStruct((N, D), BF16),
        grid=(N // tm, E, F // tf),
        in_specs=[pl.BlockSpec((tm, D), lambda i, e, f: (i, 0)),
                  pl.BlockSpec((tm, 128), lambda i, e, f: (i, 0)),
                  pl.BlockSpec((1, D, tf), lambda i, e, f: (e, 0, f)),
                  pl.BlockSpec((1, D, tf), lambda i, e, f: (e, 0, f)),
                  pl.BlockSpec((1, tf, D), lambda i, e, f: (e, f, 0))],
        out_specs=pl.BlockSpec((tm, D), lambda i, e, f: (i, 0)),
        scratch_shapes=[pltpu.VMEM((tm, D), F32)],
        compiler_params=_cparams(("parallel", "arbitrary", "arbitrary")),
        name="moe",
    )(h, combine, w1, w3, w2)


def _project_weights(w_in):
    gla_w = 2 * GLA_HEADS * GLA_DK + 2 * GLA_HEADS * GLA_DV + GLA_RANK
    rw_w = 3 * RWKV_DIM + RWKV_W_RANK + RWKV_A_RANK + RWKV_G_RANK
    kvw = NSA_GROUPS * NSA_DH
    nsa_w = NSA_HEADS * NSA_DH + 6 * kvw + NSA_HEADS * 3
    o_rw = gla_w
    o_nsa = gla_w + rw_w
    o_gate = o_nsa + nsa_w
    big = jnp.concatenate([
        w_in[:, 0:gla_w - GLA_RANK],
        w_in[:, o_rw:o_rw + 3 * RWKV_DIM],
        w_in[:, o_nsa:o_nsa + NSA_HEADS * NSA_DH + 6 * kvw],
        w_in[:, o_gate:],
    ], axis=1).astype(BF16)
    D = w_in.shape[0]
    z = lambda n: jnp.zeros((D, n), w_in.dtype)
    small = jnp.concatenate([
        w_in[:, gla_w - GLA_RANK:gla_w], z(128 - GLA_RANK),
        w_in[:, o_rw + 3 * RWKV_DIM:o_rw + rw_w], z(256 - RWKV_G_RANK),
        w_in[:, o_gate - NSA_HEADS * 3:o_gate], z(128 - NSA_HEADS * 3),
    ], axis=1).astype(BF16)
    return big, small


def _mixer(h, batch, seq, w_in, gla_a2, gla_a_b, gla_norm, rwkv_mu, rwkv_w0, rwkv_w2, rwkv_a0, rwkv_a2, rwkv_g2,
           rwkv_k_k, rwkv_k_a, rwkv_r_k, rwkv_ln_w, rwkv_ln_b,
           nsa_pos_k, nsa_w1_k, nsa_w2_k, nsa_pos_v, nsa_w1_v, nsa_w2_v, p_merge, w_out, slopes):
    N = h.shape[0]
    w_big, w_small = _project_weights(w_in)
    tm_u = 2048 if N % 2048 == 0 else N
    U = matmul(h, w_big, BF16, tm_u, 512)
    Us = matmul(h, w_small, F32, tm_u, S_COLS)

    y_a = gla(U, Us, gla_a2, gla_a_b, gla_norm, batch=batch, seq=seq)

    r, k, v, kap, b, lw, gate = rwkv_prep(U, Us, rwkv_mu, rwkv_w0, rwkv_w2, rwkv_a0, rwkv_a2, rwkv_g2,
                                          rwkv_k_k, rwkv_k_a, seq=seq)
    y_b = rwkv_chunk(r, k, v, kap, b, lw, gate, rwkv_r_k.reshape(-1), rwkv_ln_w, rwkv_ln_b, batch=batch, seq=seq)

    kv6 = U[:, U_NSA_KV:U_NSA_KV + 6 * NSA_GROUPS * NSA_DH]
    kv6 = kv6.reshape(batch, seq, 6, NSA_GROUPS, NSA_DH).transpose(2, 0, 3, 1, 4)
    kvc = nsa_compress(kv6[0:2], jnp.stack([nsa_pos_k, nsa_pos_v]), jnp.stack([nsa_w1_k, nsa_w1_v]),
                       jnp.stack([nsa_w2_k, nsa_w2_v]))
    o_c, selbias = nsa_cmp(U, kvc, slopes, batch=batch, seq=seq)
    n_slc = seq // NSA_SLC_LEN
    blk_id = jnp.arange(seq) // NSA_SLC_LEN
    onehot = (blk_id[:, None] == jnp.arange(n_slc)[None, :]).astype(BF16)
    k_aug = jnp.concatenate([kv6[2], jnp.broadcast_to(onehot, (batch, NSA_GROUPS, seq, n_slc))], axis=-1)
    o_s = nsa_sel(U, selbias, k_aug, kv6[3], slopes, batch=batch, seq=seq)
    pad = ((0, 0), (0, 0), (NSA_WINDOW, 0), (0, 0))
    o_w = nsa_win(U, jnp.pad(kv6[4], pad), jnp.pad(kv6[5], pad), slopes, batch=batch, seq=seq)
    y_c = nsa_combine(o_c, o_s, o_w, Us)

    merged = merge(y_a, y_b, y_c, U, p_merge.astype(BF16))
    return matmul(merged, w_out.astype(BF16), BF16, 1024 if N % 1024 == 0 else N, 1024)


def kernel(x, c, norm_mix, norm_ffn, ada_w, ada_b, w_in, gla_a2, gla_a_b, gla_norm, rwkv_mu, rwkv_w0, rwkv_w2, rwkv_a0, rwkv_a2, rwkv_g2, rwkv_k_k, rwkv_k_a, rwkv_r_k, rwkv_ln_w, rwkv_ln_b, nsa_pos_k, nsa_w1_k, nsa_w2_k, nsa_pos_v, nsa_w1_v, nsa_w2_v, p_merge, w_out, ffn_w1, ffn_w3, ffn_w2, moe_router, moe_w1, moe_w3, moe_w2, final_norm):
    B, T, D = x.shape
    depth = w_in.shape[0]
    N = B * T
    xs = x.reshape(N, D)
    ada = ada_all(c, ada_w, ada_b)
    slopes = jnp.exp2(-8.0 * jnp.arange(1, NSA_HEADS + 1, dtype=F32) / NSA_HEADS)
    y = None
    g_prev = None
    for l in range(depth):
        sh1, sc1, g1, sh2, sc2, g2 = jnp.split(ada[l], 6, axis=-1)
        if y is None:
            (h,) = resmod(xs, None, None, norm_mix[l], sh1, sc1, seq=T)
        else:
            xs, h = resmod(xs, y, g_prev, norm_mix[l], sh1, sc1, seq=T)
        y = _mixer(h, B, T, w_in[l], gla_a2[l], gla_a_b[l], gla_norm[l], rwkv_mu[l], rwkv_w0[l], rwkv_w2[l],
                   rwkv_a0[l], rwkv_a2[l], rwkv_g2[l], rwkv_k_k[l], rwkv_k_a[l], rwkv_r_k[l],
                   rwkv_ln_w[l], rwkv_ln_b[l], nsa_pos_k[l], nsa_w1_k[l], nsa_w2_k[l],
                   nsa_pos_v[l], nsa_w1_v[l], nsa_w2_v[l], p_merge[l], w_out[l], slopes)
        if l % 2 == 0:
            xs, h = resmod(xs, y, g1, norm_ffn[l], sh2, sc2, seq=T)
            y = ffn(h, ffn_w1[l // 2].astype(BF16), ffn_w3[l // 2].astype(BF16), ffn_w2[l // 2].astype(BF16))
        else:
            rt = jnp.zeros((D, 128), F32).at[:, :N_EXPERTS].set(moe_router[l // 2])
            xs, h, logits = resmod(xs, y, g1, norm_ffn[l], sh2, sc2, seq=T, router=rt)
            y = moe(h, route(logits), moe_w1[l // 2].astype(BF16), moe_w3[l // 2].astype(BF16),
                    moe_w2[l // 2].astype(BF16))
        g_prev = g2
    (out,) = resmod(xs, y, g_prev, final_norm, None, None, seq=T, final=True)
    return out.reshape(B, T, D)
```

```python
import functools

import numpy as np
import jax
import jax.numpy as jnp
from jax import lax
from jax.experimental import pallas as pl
from jax.experimental.pallas import tpu as pltpu

F32 = jnp.float32
BF16 = jnp.bfloat16
HI = lax.Precision.HIGHEST

V7X_VMEM_LIMIT_BYTES = 56 * 1024 * 1024

D_MODEL = 2048
NORM_EPS = 1e-6
NEG_BIG = -1e30

GLA_HEADS = 4
GLA_DK = 128
GLA_DV = 256
GLA_RANK = 16
GLA_NORMALIZER = 16.0
CHUNK = 64

RWKV_HEADS = 16
RWKV_N = 64
RWKV_DIM = RWKV_HEADS * RWKV_N
RWKV_W_RANK = 64
RWKV_A_RANK = 64
RWKV_G_RANK = 160
RWKV_LN_EPS = 64e-5
RWKV_HB = 4

NSA_HEADS = 16
NSA_GROUPS = 4
NSA_HPG = 4
NSA_DH = 64
NSA_CMP_LEN = 32
NSA_CMP_STRIDE = 16
NSA_CMP_HIDDEN = 128
NSA_SLC_LEN = 64
NSA_N_SEL = 16
NSA_WINDOW = 512
NSA_QB = 256
SEL_MASK_BIAS = 131072.0

BRANCH_DIM = 1024
D_FF = 5632
N_EXPERTS = 8

U_GLA_Q, U_GLA_K, U_GLA_V, U_GLA_G = 0, 512, 1024, 2048
U_RWKV_R, U_RWKV_K, U_RWKV_V = 3072, 4096, 5120
U_NSA_Q = 6144
U_NSA_KV = 7168
U_GATE = 8704
U_COLS = U_GATE + 3 * D_MODEL
S_GLA_A = 0
S_RWKV_WA = 128
S_RWKV_G = 256
S_NSA_GATE = 512
S_COLS = 640


def _cparams(sem, vmem=V7X_VMEM_LIMIT_BYTES):
    return pltpu.CompilerParams(dimension_semantics=sem, vmem_limit_bytes=vmem)


def _sigmoid(x):
    return 1.0 / (1.0 + jnp.exp(-x))


def _softplus(x):
    return jnp.maximum(x, 0.0) + jnp.log(1.0 + jnp.exp(-jnp.abs(x)))


def _dot(a, b):
    return jnp.dot(a, b, preferred_element_type=F32)


def _dot_nt(a, b):
    return lax.dot_general(a, b, (((1,), (1,)), ((), ())), preferred_element_type=F32)


def _dot_tn(a, b):
    return lax.dot_general(a, b, (((0,), (0,)), ((), ())), preferred_element_type=F32)


def _dot_hi(a, b):
    return jnp.dot(a, b, preferred_element_type=F32, precision=HI)


def _dot_split(x, w):
    hi = x.astype(BF16)
    lo = (x - hi.astype(F32)).astype(BF16)
    return _dot(hi, w) + _dot(lo, w)


def _ada_kernel(c_ref, w_ref, b_ref, o_ref):
    c = c_ref[...]
    o_ref[0] = _dot_hi(c * _sigmoid(c), w_ref[0]) + b_ref[0]


def ada_all(c, ada_w, ada_b):
    L, D, N6 = ada_w.shape
    B = c.shape[0]
    tn = 1024
    return pl.pallas_call(
        _ada_kernel,
        out_shape=jax.ShapeDtypeStruct((L, B, N6), F32),
        grid=(L, N6 // tn),
        in_specs=[
            pl.BlockSpec((B, D), lambda l, j: (0, 0)),
            pl.BlockSpec((1, D, tn), lambda l, j: (l, 0, j)),
            pl.BlockSpec((1, 1, tn), lambda l, j: (l, 0, j)),
        ],
        out_specs=pl.BlockSpec((1, B, tn), lambda l, j: (l, 0, j)),
        compiler_params=_cparams(("parallel", "parallel")),
        name="ada",
    )(c, ada_w, ada_b.reshape(L, 1, N6))


def _resmod_kernel(*refs, has_res, final, router):
    it = iter(refs)
    x_ref = next(it)
    y_ref = next(it) if has_res else None
    g_ref = next(it) if has_res else None
    ng_ref = next(it)
    sh_ref = None if final else next(it)
    sc_ref = None if final else next(it)
    rt_ref = next(it) if router else None
    xo_ref = next(it) if (has_res and not final) else None
    h_ref = next(it)
    lg_ref = next(it) if router else None

    x = x_ref[...]
    if has_res:
        x = x + g_ref[0] * y_ref[...].astype(F32)
        if xo_ref is not None:
            xo_ref[...] = x
    ms = jnp.mean(x * x, axis=-1, keepdims=True)
    h = x * lax.rsqrt(ms + NORM_EPS) * ng_ref[...]
    if not final:
        h = h * (1.0 + sc_ref[0]) + sh_ref[0]
    h_ref[...] = h.astype(h_ref.dtype)
    if router:
        lg_ref[...] = _dot_hi(h, rt_ref[...])


def resmod(x, y, g, norm_g, shift, scale, *, seq, router=None, final=False):
    N, D = x.shape
    tm = 512
    spb = seq // tm
    has_res = y is not None
    row = lambda i: (i, 0)
    per_b = lambda i: (i // spb, 0, 0)
    ins, specs = [x], [pl.BlockSpec((tm, D), row)]
    if has_res:
        ins += [y, g.reshape(-1, 1, D)]
        specs += [pl.BlockSpec((tm, D), row), pl.BlockSpec((1, 1, D), per_b)]
    ins.append(norm_g.reshape(1, D))
    specs.append(pl.BlockSpec((1, D), lambda i: (0, 0)))
    if not final:
        ins += [shift.reshape(-1, 1, D), scale.reshape(-1, 1, D)]
        specs += [pl.BlockSpec((1, 1, D), per_b), pl.BlockSpec((1, 1, D), per_b)]
    if router is not None:
        ins.append(router)
        specs.append(pl.BlockSpec(router.shape, lambda i: (0, 0)))
    outs, ospecs = [], []
    if has_res and not final:
        outs.append(jax.ShapeDtypeStruct((N, D), F32))
        ospecs.append(pl.BlockSpec((tm, D), row))
    outs.append(jax.ShapeDtypeStruct((N, D), F32 if final else BF16))
    ospecs.append(pl.BlockSpec((tm, D), row))
    if router is not None:
        outs.append(jax.ShapeDtypeStruct((N, 128), F32))
        ospecs.append(pl.BlockSpec((tm, 128), row))
    res = pl.pallas_call(
        functools.partial(_resmod_kernel, has_res=has_res, final=final, router=router is not None),
        out_shape=tuple(outs),
        grid=(N // tm,),
        in_specs=specs,
        out_specs=tuple(ospecs),
        compiler_params=_cparams(("parallel",)),
        name="resmod",
    )(*ins)
    return res


def _mm_kernel(x_ref, w_ref, o_ref):
    o_ref[...] = _dot(x_ref[...], w_ref[...]).astype(o_ref.dtype)


def matmul(x, w, out_dtype, tm, tn):
    M, K = x.shape
    N = w.shape[1]
    return pl.pallas_call(
        _mm_kernel,
        out_shape=jax.ShapeDtypeStruct((M, N), out_dtype),
        grid=(M // tm, N // tn),
        in_specs=[pl.BlockSpec((tm, K), lambda i, j: (i, 0)), pl.BlockSpec((K, tn), lambda i, j: (0, j))],
        out_specs=pl.BlockSpec((tm, tn), lambda i, j: (i, j)),
        compiler_params=_cparams(("parallel", "parallel")),
        name="matmul",
    )(x, w)


GLA_TC = 512


def _gla_kernel(q_ref, k_ref, v_ref, g_ref, alr_ref, a2_ref, ab_ref, ng_ref, o_ref, st_ref):
    @pl.when(pl.program_id(2) == 0)
    def _():
        st_ref[...] = jnp.zeros_like(st_ref)

    C = CHUNK
    la = _dot_hi(alr_ref[...], a2_ref[...]) + ab_ref[...]
    la = -_softplus(-la) / GLA_NORMALIZER
    ri = lax.broadcasted_iota(jnp.int32, (C, C), 0)
    ci = lax.broadcasted_iota(jnp.int32, (C, C), 1)
    causal = ri >= ci
    tril = causal.astype(F32)
    for c in range(GLA_TC // C):
        sl = slice(c * C, (c + 1) * C)
        bc = _dot_hi(tril, la[sl])
        bl = bc[C - 1:C, :]
        q = q_ref[sl, :].astype(F32) * (GLA_DK ** -0.5)
        k = k_ref[sl, :].astype(F32)
        v = v_ref[sl, :]
        qd = (q * jnp.exp(bc)).astype(BF16)
        kd = (k * jnp.exp(-bc)).astype(BF16)
        kl = (k * jnp.exp(bl - bc)).astype(BF16)
        att = jnp.where(causal, _dot_nt(qd, kd), 0.0).astype(BF16)
        st = st_ref[...]
        o = _dot(att, v) + _dot_nt(qd, st.astype(BF16))
        st_ref[...] = st * jnp.exp(bl) + _dot_tn(v, kl)
        o = o * lax.rsqrt(jnp.mean(o * o, axis=-1, keepdims=True) + NORM_EPS) * ng_ref[...]
        gg = g_ref[sl, :].astype(F32)
        o_ref[sl, :] = (o * (gg * _sigmoid(gg))).astype(o_ref.dtype)


def gla(U, Us, a2, a_b, norm_g, *, batch, seq):
    N = U.shape[0]
    nt = seq // GLA_TC
    a2p = jnp.zeros((128, GLA_HEADS * GLA_DK), F32).at[:GLA_RANK].set(a2)
    row = lambda b, h, i: b * nt + i
    return pl.pallas_call(
        _gla_kernel,
        out_shape=jax.ShapeDtypeStruct((N, GLA_HEADS * GLA_DV), BF16),
        grid=(batch, GLA_HEADS, nt),
        in_specs=[
            pl.BlockSpec((GLA_TC, GLA_DK), lambda b, h, i: (row(b, h, i), U_GLA_Q // GLA_DK + h)),
            pl.BlockSpec((GLA_TC, GLA_DK), lambda b, h, i: (row(b, h, i), U_GLA_K // GLA_DK + h)),
            pl.BlockSpec((GLA_TC, GLA_DV), lambda b, h, i: (row(b, h, i), U_GLA_V // GLA_DV + h)),
            pl.BlockSpec((GLA_TC, GLA_DV), lambda b, h, i: (row(b, h, i), U_GLA_G // GLA_DV + h)),
            pl.BlockSpec((GLA_TC, 128), lambda b, h, i: (row(b, h, i), S_GLA_A // 128)),
            pl.BlockSpec((128, GLA_DK), lambda b, h, i: (0, h)),
            pl.BlockSpec((1, GLA_DK), lambda b, h, i: (0, h)),
            pl.BlockSpec((1, GLA_DV), lambda b, h, i: (0, 0)),
        ],
        out_specs=pl.BlockSpec((GLA_TC, GLA_DV), lambda b, h, i: (row(b, h, i), h)),
        scratch_shapes=[pltpu.VMEM((GLA_DV, GLA_DK), F32)],
        compiler_params=_cparams(("parallel", "parallel", "arbitrary")),
        name="gla",
    )(U, U, U, U, Us, a2p, a_b.reshape(1, -1), norm_g.reshape(1, -1))


RWKV_TM = 256


def _seg_ones(n=256, seg=RWKV_N):
    i = np.arange(n)
    return jnp.asarray((i[:, None] // seg == i[None, :] // seg).astype(np.float32), BF16)


def _seg_sum(x, bd):
    outs = [_dot_split(x[:, s:s + 256], bd) for s in range(0, x.shape[1], 256)]
    return outs[0] if len(outs) == 1 else jnp.concatenate(outs, axis=1)


def _shift_lerp(u_ref, p_ref, mu, first):
    u = u_ref[...].astype(F32)
    prev_last = jnp.where(first, 0.0, p_ref[7:8, :].astype(F32))
    rolled = pltpu.roll(u, 1, 0)
    is_row0 = lax.broadcasted_iota(jnp.int32, u.shape, 0) == 0
    sh = jnp.where(is_row0, prev_last, rolled)
    return u + (sh - u) * mu


def _rwkv_prep_kernel(r_ref, k_ref, v_ref, wa_ref, gl_ref, rp_ref, kp_ref, vp_ref, wap_ref, glp_ref,
                      mur_ref, muk_ref, muv_ref, muwa_ref, mug_ref, w0_ref, w2_ref, a0_ref, a2_ref, g2_ref,
                      kk_ref, ka_ref, bd_ref,
                      ro_ref, ko_ref, vo_ref, kap_ref, bo_ref, lw_ref, go_ref, *, spb):
    first = (pl.program_id(0) % spb) == 0
    r = _shift_lerp(r_ref, rp_ref, mur_ref[...], first)
    k = _shift_lerp(k_ref, kp_ref, muk_ref[...], first)
    v = _shift_lerp(v_ref, vp_ref, muv_ref[...], first)
    wa = _shift_lerp(wa_ref, wap_ref, muwa_ref[...], first)
    gl = _shift_lerp(gl_ref, glp_ref, mug_ref[...], first)
    w_log = -_softplus(-(w0_ref[...] + _dot_hi(jnp.tanh(wa), w2_ref[...]))) - 0.5
    lw_ref[...] = -jnp.exp(w_log)
    a = _sigmoid(a0_ref[...] + _dot_hi(wa, a2_ref[...]))
    go_ref[...] = _dot(_sigmoid(gl).astype(BF16), g2_ref[...]).astype(go_ref.dtype)
    kk = k * kk_ref[...]
    nrm = jnp.sqrt(_seg_sum(kk * kk, bd_ref[...]))
    kk = kk / jnp.maximum(nrm, 1e-12)
    ro_ref[...] = r.astype(ro_ref.dtype)
    ko_ref[...] = (k * (1.0 + (a - 1.0) * ka_ref[...])).astype(ko_ref.dtype)
    vo_ref[...] = v.astype(vo_ref.dtype)
    kap_ref[...] = kk.astype(kap_ref.dtype)
    bo_ref[...] = (kk * a).astype(bo_ref.dtype)


def rwkv_prep(U, Us, mu, w0, w2, a0, a2, g2, k_k, k_a, *, seq):
    N = U.shape[0]
    tm = RWKV_TM
    spb = seq // tm
    R = RWKV_DIM
    cur = lambda cb: (lambda i: (i, cb))
    prv = lambda cb: (lambda i: (jnp.maximum(i * (tm // 8) - 1, 0), cb))
    mu_r, mu_k, mu_v = mu[:R], mu[R:2 * R], mu[2 * R:3 * R]
    mu_wa = mu[3 * R:3 * R + 128]
    mu_g = jnp.zeros((256,), F32).at[:RWKV_G_RANK].set(mu[3 * R + 128:])
    w2p = jnp.zeros((128, R), F32).at[:RWKV_W_RANK].set(w2)
    a2p = jnp.zeros((128, R), F32).at[RWKV_W_RANK:].set(a2)
    g2p = jnp.zeros((256, R), BF16).at[:RWKV_G_RANK].set(g2.astype(BF16))
    vec = lambda a: a.reshape(1, -1)
    full = lambda a: pl.BlockSpec(a.shape, lambda i: (0, 0))
    params = [vec(mu_r), vec(mu_k), vec(mu_v), vec(mu_wa), vec(mu_g), vec(w0), w2p, vec(a0), a2p, g2p,
              vec(k_k), vec(k_a), _seg_ones()]
    in_specs = [
        pl.BlockSpec((tm, R), cur(U_RWKV_R // R)), pl.BlockSpec((tm, R), cur(U_RWKV_K // R)),
        pl.BlockSpec((tm, R), cur(U_RWKV_V // R)),
        pl.BlockSpec((tm, 128), cur(S_RWKV_WA // 128)), pl.BlockSpec((tm, 256), cur(S_RWKV_G // 256)),
        pl.BlockSpec((8, R), prv(U_RWKV_R // R)), pl.BlockSpec((8, R), prv(U_RWKV_K // R)),
        pl.BlockSpec((8, R), prv(U_RWKV_V // R)),
        pl.BlockSpec((8, 128), prv(S_RWKV_WA // 128)), pl.BlockSpec((8, 256), prv(S_RWKV_G // 256)),
    ] + [full(p) for p in params]
    out = lambda dt: jax.ShapeDtypeStruct((N, R), dt)
    ospec = pl.BlockSpec((tm, R), lambda i: (i, 0))
    return pl.pallas_call(
        functools.partial(_rwkv_prep_kernel, spb=spb),
        out_shape=(out(BF16), out(BF16), out(BF16), out(BF16), out(BF16), out(F32), out(BF16)),
        grid=(N // tm,),
        in_specs=in_specs,
        out_specs=(ospec,) * 7,
        compiler_params=_cparams(("parallel",)),
        name="rwkv_prep",
    )(U, U, U, Us, Us, U, U, U, Us, Us, *params)


RWKV_TC = 512
RWKV_W = RWKV_HB * RWKV_N


def _rwkv_masks():
    W, C = RWKV_W, CHUNK
    i = np.arange(W)
    same = i[:, None] // C == i[None, :] // C
    m_bd = same.astype(np.float32)
    low_s = (same & (i[:, None] % C > i[None, :] % C)).astype(np.float32)
    low_i = (same & (i[:, None] % C >= i[None, :] % C)).astype(np.float32)
    tril = np.tril(np.ones((C, C), np.float32))
    return (jnp.asarray(m_bd), jnp.asarray(low_s), jnp.asarray(low_i), jnp.asarray(tril),
            jnp.asarray(np.eye(W, dtype=np.float32)))


def _tile4(x):
    return jnp.concatenate([x] * RWKV_HB, axis=0)


def _rwkv_chunk_kernel(r_ref, k_ref, v_ref, kap_ref, b_ref, lw_ref, g_ref,
                       mbd_ref, lows_ref, lowi_ref, tril_ref, eye_ref, bd_ref,
                       rk_ref, lnw_ref, lnb_ref, o_ref, st_ref):
    @pl.when(pl.program_id(1) == 0)
    def _():
        st_ref[...] = jnp.zeros_like(st_ref)

    C, W = CHUNK, RWKV_W
    n_groups = r_ref.shape[1] // W
    m_bd = mbd_ref[...]
    low_s = lows_ref[...]
    low_i = lowi_ref[...]
    tril = tril_ref[...]
    eye = eye_ref[...]
    bd = bd_ref[...]

    def expand(x):
        return (_tile4(x) * m_bd).astype(BF16)

    def body(c, carry):
        sl = pl.ds(pl.multiple_of(c * C, C), C)
        lw_all = lw_ref[sl, :]
        cum_all = _dot_hi(tril, lw_all)
        G = range(n_groups)
        lns = [slice(gi * W, (gi + 1) * W) for gi in G]
        cums = [cum_all[:, ln] for ln in lns]
        clasts = [cum[C - 1:C, :] for cum in cums]
        rs = [r_ref[sl, ln].astype(F32) for ln in lns]
        ks = [k_ref[sl, ln].astype(F32) for ln in lns]
        vs = [v_ref[sl, ln].astype(F32) for ln in lns]
        bs = [b_ref[sl, ln].astype(F32) for ln in lns]
        p_invs = [jnp.exp(-cum) for cum in cums]
        xes = [jnp.concatenate([expand(kap_ref[sl, ln].astype(F32) * jnp.exp(cum - lw_all[:, ln])),
                                expand(r * jnp.exp(cum))], axis=0)
               for ln, cum, r in zip(lns, cums, rs)]
        hes = [jnp.concatenate([expand(b * pi), expand(k * pi)], axis=0) for b, k, pi in zip(bs, ks, p_invs)]
        scs = [_dot_nt(xe, he) for xe, he in zip(xes, hes)]
        ams = [sc[:W, :W] * low_s for sc in scs]
        tinvs = [eye - a_m for a_m in ams]
        pws = ams
        for _ in range(int(np.log2(C)) - 1):
            pwbs = [pw.astype(BF16) for pw in pws]
            pws = [_dot(pwb, pwb) for pwb in pwbs]
            tinvs = [tinv + _dot(tinv.astype(BF16), pw.astype(BF16)) for tinv, pw in zip(tinvs, pws)]
        ves = [expand(v) for v in vs]
        bmv = [_dot((sc[:W, W:] * low_s).astype(BF16), ve) for sc, ve in zip(scs, ves)]
        qkv = [_dot((sc[W:, W:] * low_i).astype(BF16), ve) for sc, ve in zip(scs, ves)]
        sts = [st_ref[gi] for gi in G]
        xss = [_dot_nt(xe, st.astype(BF16)) for xe, st in zip(xes, sts)]
        us = [_dot(tinv.astype(BF16), (xs[:W] + bv).astype(BF16)) for tinv, xs, bv in zip(tinvs, xss, bmv)]
        kbs = [jnp.concatenate([expand(k * jnp.exp(cl - cum)), expand(b * jnp.exp(cl - cum))], axis=0)
               for k, b, cl, cum in zip(ks, bs, clasts, cums)]
        for gi in G:
            vu = jnp.concatenate([ves[gi], (-us[gi]).astype(BF16)], axis=0)
            st_ref[gi] = sts[gi] * jnp.exp(clasts[gi]) + _dot_tn(vu, kbs[gi]) * m_bd
        for gi in G:
            ln = lns[gi]
            qb = (scs[gi][W:, :W] * low_i).astype(BF16)
            y_e = xss[gi][W:] + qkv[gi] - _dot(qb, us[gi].astype(BF16))
            y = y_e[0:C] + y_e[C:2 * C] + y_e[2 * C:3 * C] + y_e[3 * C:4 * C]
            mu = _seg_sum(y, bd) * (1.0 / RWKV_N)
            yc = y - mu
            var = _seg_sum(yc * yc, bd) * (1.0 / RWKV_N)
            yn = yc * lax.rsqrt(var + RWKV_LN_EPS) * lnw_ref[:, ln] + lnb_ref[:, ln]
            bonus = _seg_sum(rs[gi] * ks[gi] * rk_ref[:, ln], bd) * vs[gi]
            o_ref[sl, ln] = ((yn + bonus) * g_ref[sl, ln].astype(F32)).astype(o_ref.dtype)
        return carry

    lax.fori_loop(0, RWKV_TC // C, body, 0)


def rwkv_chunk(r, k, v, kap, b, lw, gate, r_k, ln_w, ln_b, *, batch, seq):
    N, R = r.shape
    nt = seq // RWKV_TC
    W = RWKV_W
    blk = pl.BlockSpec((RWKV_TC, R), lambda bb, i: (bb * nt + i, 0))
    masks = _rwkv_masks() + (_seg_ones(),)
    full = lambda a: pl.BlockSpec(a.shape, lambda bb, i: (0, 0))
    pvec = pl.BlockSpec((1, R), lambda bb, i: (0, 0))
    return pl.pallas_call(
        _rwkv_chunk_kernel,
        out_shape=jax.ShapeDtypeStruct((N, R), BF16),
        grid=(batch, nt),
        in_specs=[blk] * 7 + [full(m) for m in masks] + [pvec] * 3,
        out_specs=blk,
        scratch_shapes=[pltpu.VMEM((R // W, W, W), F32)],
        compiler_params=_cparams(("parallel", "arbitrary")),
        name="rwkv_chunk",
    )(r, k, v, kap, b, lw, gate, *masks, r_k.reshape(1, R), ln_w.reshape(1, R), ln_b.reshape(1, R))


def _gelu_tanh(x):
    return 0.5 * x * (1.0 + jnp.tanh(np.sqrt(2.0 / np.pi) * (x + 0.044715 * (x * x * x))))


def _nsa_compress_kernel(x_ref, pos_ref, w1_ref, w2_ref, o_ref):
    x = x_ref[0, 0]
    w1 = w1_ref[0]
    half = w1.shape[0] // 2
    nrow = x.shape[0]
    ha = _dot(x, w1[:half])
    hb = _dot(x, w1[half:])
    h = ha + pltpu.roll(hb, nrow - 1, 0)
    pb = _dot(pos_ref[0], w1)
    h = _gelu_tanh(h + pb[0:1, :])
    o_ref[0, 0, 0] = _dot(h.astype(BF16), w2_ref[0]).astype(o_ref.dtype)


def nsa_compress(kv_cmp, pos, w1, w2):
    two, B, G, T, dh = kv_cmp.shape
    nr = T // NSA_CMP_STRIDE
    x = kv_cmp.reshape(two * B, G, nr, NSA_CMP_STRIDE * dh)
    posf = jnp.broadcast_to(pos.reshape(two, 1, NSA_CMP_LEN * dh), (two, 8, NSA_CMP_LEN * dh)).astype(BF16)
    return pl.pallas_call(
        _nsa_compress_kernel,
        out_shape=jax.ShapeDtypeStruct((two, B, G, nr, dh), BF16),
        grid=(two, B, G),
        in_specs=[
            pl.BlockSpec((1, 1, nr, NSA_CMP_STRIDE * dh), lambda s, b, g: (s * B + b, g, 0, 0)),
            pl.BlockSpec((1, 8, NSA_CMP_LEN * dh), lambda s, b, g: (s, 0, 0)),
            pl.BlockSpec((1, NSA_CMP_LEN * dh, NSA_CMP_HIDDEN), lambda s, b, g: (s, 0, 0)),
            pl.BlockSpec((1, NSA_CMP_HIDDEN, dh), lambda s, b, g: (s, 0, 0)),
        ],
        out_specs=pl.BlockSpec((1, 1, 1, nr, dh), lambda s, b, g: (s, b, g, 0, 0)),
        compiler_params=_cparams(("parallel", "parallel", "parallel")),
        name="nsa_compress",
    )(x, posf, w1.astype(BF16), w2.astype(BF16))


def _nsa_cmp_kernel(slope_ref, q_ref, kc_ref, vc_ref, ov_ref, oc_ref, sb_ref, *, n_slc, n_sel):
    g = pl.program_id(1)
    qi = pl.program_id(2)
    QB = NSA_QB
    ncp = kc_ref.shape[3]
    kc = kc_ref[0, 0, 0]
    vc = vc_ref[0, 0, 0]
    t = qi * QB + lax.broadcasted_iota(jnp.int32, (QB, ncp), 0)
    n = lax.broadcasted_iota(jnp.int32, (QB, ncp), 1)
    valid = (n * NSA_CMP_STRIDE + (NSA_CMP_LEN - 1)) <= t
    adist = jnp.abs(t.astype(F32) - (n.astype(F32) * NSA_CMP_STRIDE + (NSA_CMP_LEN - 1) / 2.0))
    psum = jnp.zeros((QB, ncp), F32)
    for h in range(NSA_HPG):
        qh = q_ref[:, h * NSA_DH:(h + 1) * NSA_DH] * (NSA_DH ** -0.5)
        s = _dot_nt(qh, kc) - slope_ref[g * NSA_HPG + h] * adist
        s = jnp.where(valid, s, NEG_BIG)
        m = jnp.max(s, axis=-1, keepdims=True)
        e = jnp.where(valid, jnp.exp(s - m), 0.0)
        l = jnp.sum(e, axis=-1, keepdims=True)
        p = e / jnp.maximum(l, 1e-30)
        oc_ref[:, h * NSA_DH:(h + 1) * NSA_DH] = _dot(p.astype(BF16), vc).astype(oc_ref.dtype)
        psum = psum + p
    imp = lax.dot_general(ov_ref[...], psum, (((1,), (1,)), ((), ())), preferred_element_type=F32, precision=HI)
    j = lax.broadcasted_iota(jnp.int32, (n_slc, QB), 0)
    tt = qi * QB + lax.broadcasted_iota(jnp.int32, (n_slc, QB), 1)
    cur = jnp.right_shift(tt, 6)
    forced = (j == 0) | (j == cur) | (j == cur - 1)
    cand = (j >= 1) & (j <= cur - 2)
    cur_row = cur[0:1, :]
    rank = jnp.zeros((n_slc, QB), jnp.int32)
    for jp in range(1, n_slc):
        row = imp[jp:jp + 1, :]
        ahead = (row > imp) | ((row == imp) & (jp < j))
        rank = rank + jnp.where(ahead & (jp <= cur_row - 2), 1, 0)
    sel = forced | (cand & (rank < n_sel - 3))
    sb = jnp.where(sel, 0.0, -SEL_MASK_BIAS)
    sb_ref[0, 0] = sb.T.astype(sb_ref.dtype)


def nsa_cmp(U, kvc, slopes, *, batch, seq):
    N = U.shape[0]
    QB = NSA_QB
    nq = seq // QB
    ncp = kvc.shape[3]
    n_slc = seq // NSA_SLC_LEN
    n_sel = min(NSA_N_SEL, n_slc)
    nn = np.arange(ncp)
    jj = np.arange(n_slc)
    ov = ((nn[None, :] * NSA_CMP_STRIDE + NSA_CMP_LEN - 1 >= jj[:, None] * NSA_SLC_LEN)
          & (nn[None, :] * NSA_CMP_STRIDE <= jj[:, None] * NSA_SLC_LEN + NSA_SLC_LEN - 1)
          & (nn[None, :] < ncp - 1)).astype(np.float32)
    G = NSA_GROUPS
    W = NSA_HPG * NSA_DH
    grid_spec = pltpu.PrefetchScalarGridSpec(
        num_scalar_prefetch=1,
        grid=(batch, G, nq),
        in_specs=[
            pl.BlockSpec((QB, W), lambda b, g, i, s: (b * nq + i, U_NSA_Q // W + g)),
            pl.BlockSpec((1, 1, 1, ncp, NSA_DH), lambda b, g, i, s: (0, b, g, 0, 0)),
            pl.BlockSpec((1, 1, 1, ncp, NSA_DH), lambda b, g, i, s: (1, b, g, 0, 0)),
            pl.BlockSpec((n_slc, ncp), lambda b, g, i, s: (0, 0)),
        ],
        out_specs=(
            pl.BlockSpec((QB, W), lambda b, g, i, s: (b * nq + i, g)),
            pl.BlockSpec((1, 1, QB, n_slc), lambda b, g, i, s: (b, g, i, 0)),
        ),
    )
    return pl.pallas_call(
        functools.partial(_nsa_cmp_kernel, n_slc=n_slc, n_sel=n_sel),
        out_shape=(jax.ShapeDtypeStruct((N, NSA_HEADS * NSA_DH), BF16),
                   jax.ShapeDtypeStruct((batch, G, seq, n_slc), BF16)),
        grid_spec=grid_spec,
        compiler_params=_cparams(("parallel", "parallel", "parallel")),
        name="nsa_cmp",
    )(slopes, U, kvc, kvc, jnp.asarray(ov))


def _nsa_win_kernel(slope_ref, q_ref, k0_ref, k1_ref, k2_ref, v0_ref, v1_ref, v2_ref, o_ref):
    g = pl.program_id(1)
    qi = pl.program_id(2)
    QB = NSA_QB
    KW = NSA_WINDOW + QB
    k = jnp.concatenate([k0_ref[0, 0], k1_ref[0, 0], k2_ref[0, 0]], axis=0)
    v = jnp.concatenate([v0_ref[0, 0], v1_ref[0, 0], v2_ref[0, 0]], axis=0)
    r = lax.broadcasted_iota(jnp.int32, (QB, KW), 0)
    c = lax.broadcasted_iota(jnp.int32, (QB, KW), 1)
    dist = r - c + NSA_WINDOW
    pos = qi * QB - NSA_WINDOW + c
    valid = (dist >= 0) & (dist < NSA_WINDOW) & (pos >= 0)
    adist = jnp.abs(dist).astype(F32)
    for h in range(NSA_HPG):
        qh = q_ref[:, h * NSA_DH:(h + 1) * NSA_DH] * (NSA_DH ** -0.5)
        s = _dot_nt(qh, k) - slope_ref[g * NSA_HPG + h] * adist
        s = jnp.where(valid, s, NEG_BIG)
        m = jnp.max(s, axis=-1, keepdims=True)
        e = jnp.exp(s - m)
        l = jnp.sum(e, axis=-1, keepdims=True)
        o = _dot(e.astype(BF16), v) / l
        o_ref[:, h * NSA_DH:(h + 1) * NSA_DH] = o.astype(o_ref.dtype)


def nsa_win(U, kw_pad, vw_pad, slopes, *, batch, seq):
    N = U.shape[0]
    QB = NSA_QB
    nq = seq // QB
    G = NSA_GROUPS
    W = NSA_HPG * NSA_DH
    kv = lambda off: pl.BlockSpec((1, 1, QB, NSA_DH), lambda b, g, i, s: (b, g, i + off, 0))
    grid_spec = pltpu.PrefetchScalarGridSpec(
        num_scalar_prefetch=1,
        grid=(batch, G, nq),
        in_specs=[pl.BlockSpec((QB, W), lambda b, g, i, s: (b * nq + i, U_NSA_Q // W + g)),
                  kv(0), kv(1), kv(2), kv(0), kv(1), kv(2)],
        out_specs=pl.BlockSpec((QB, W), lambda b, g, i, s: (b * nq + i, g)),
    )
    return pl.pallas_call(
        _nsa_win_kernel,
        out_shape=jax.ShapeDtypeStruct((N, NSA_HEADS * NSA_DH), BF16),
        grid_spec=grid_spec,
        compiler_params=_cparams(("parallel", "parallel", "parallel")),
        name="nsa_win",
    )(slopes, U, kw_pad, kw_pad, kw_pad, vw_pad, vw_pad, vw_pad)


SEL_QB = 256
SEL_KV = 512


def _sel_pairs(seq):
    qs, ks = [], []
    for qi in range(seq // SEL_QB):
        for kj in range((qi * SEL_QB) // SEL_KV + 1):
            qs.append(qi)
            ks.append(kj)
    return np.asarray(qs, np.int32), np.asarray(ks, np.int32)


def _nsa_sel_kernel(qi_ref, kj_ref, slope_ref, q_ref, sb_ref, ka_ref, vt_ref, bias_ref, o_ref,
                    qa_ref, m_ref, l_ref, acc_ref):
    g = pl.program_id(1)
    p = pl.program_id(2)
    qi = qi_ref[p]
    kj = kj_ref[p]
    QB, KV = SEL_QB, SEL_KV
    nblk = sb_ref.shape[3]

    @pl.when(kj == 0)
    def _():
        for h in range(NSA_HPG):
            qa_ref[h * QB:(h + 1) * QB, 0:NSA_DH] = q_ref[:, h * NSA_DH:(h + 1) * NSA_DH] * (NSA_DH ** -0.5)
            qa_ref[h * QB:(h + 1) * QB, NSA_DH:NSA_DH + nblk] = sb_ref[0, 0]
        m_ref[...] = jnp.full_like(m_ref, NEG_BIG)
        l_ref[...] = jnp.zeros_like(l_ref)
        acc_ref[...] = jnp.zeros_like(acc_ref)

    off = qi * QB - kj * KV
    offf = off.astype(F32)

    def step(masked):
        s = _dot_nt(ka_ref[0, 0], qa_ref[...])
        vt = vt_ref[0, 0]
        if masked:
            r = lax.broadcasted_iota(jnp.int32, (KV, QB), 0)
            c = lax.broadcasted_iota(jnp.int32, (KV, QB), 1)
            causal = (c - r + off) >= 0
        for h in range(NSA_HPG):
            cols = slice(h * QB, (h + 1) * QB)
            sh = s[:, cols] + bias_ref[h]
            if masked:
                sh = jnp.where(causal, sh, NEG_BIG)
            delta = -slope_ref[g * NSA_HPG + h] * offf
            m_old = m_ref[h:h + 1, :]
            m_new = jnp.maximum(m_old, jnp.max(sh, axis=0, keepdims=True) + delta)
            e = jnp.exp(sh - (m_new - delta))
            alpha = jnp.exp(m_old - m_new)
            l_ref[h:h + 1, :] = alpha * l_ref[h:h + 1, :] + jnp.sum(e, axis=0, keepdims=True)
            acc_ref[:, cols] = alpha * acc_ref[:, cols] + _dot(vt, e.astype(BF16))
            m_ref[h:h + 1, :] = m_new

    last = (qi * QB) // KV

    @pl.when(kj < last)
    def _():
        step(False)

    @pl.when(kj == last)
    def _():
        step(True)
        for h in range(NSA_HPG):
            cols = slice(h * QB, (h + 1) * QB)
            o = acc_ref[:, cols] / l_ref[h:h + 1, :]
            o_ref[:, h * NSA_DH:(h + 1) * NSA_DH] = o.T.astype(o_ref.dtype)


def nsa_sel(U, selbias, k_aug, vt_slc, slopes, *, batch, seq):
    N = U.shape[0]
    QB, KV = SEL_QB, SEL_KV
    nq = seq // QB
    G = NSA_GROUPS
    W = NSA_HPG * NSA_DH
    n_slc = selbias.shape[3]
    ka_w = k_aug.shape[3]
    qs, ks = _sel_pairs(seq)
    rc = (np.arange(QB)[None, :] - np.arange(KV)[:, None]).astype(np.float32)
    bias = -slopes[:, None, None] * jnp.asarray(rc)[None]
    grid_spec = pltpu.PrefetchScalarGridSpec(
        num_scalar_prefetch=3,
        grid=(batch, G, len(qs)),
        in_specs=[
            pl.BlockSpec((QB, W), lambda b, g, p, qi, kj, s: (b * nq + qi[p], U_NSA_Q // W + g)),
            pl.BlockSpec((1, 1, QB, n_slc), lambda b, g, p, qi, kj, s: (b, g, qi[p], 0)),
            pl.BlockSpec((1, 1, KV, ka_w), lambda b, g, p, qi, kj, s: (b, g, kj[p], 0)),
            pl.BlockSpec((1, 1, NSA_DH, KV), lambda b, g, p, qi, kj, s: (b, g, 0, kj[p])),
            pl.BlockSpec((NSA_HPG, KV, QB), lambda b, g, p, qi, kj, s: (g, 0, 0)),
        ],
        out_specs=pl.BlockSpec((QB, W), lambda b, g, p, qi, kj, s: (b * nq + qi[p], g)),
        scratch_shapes=[
            pltpu.VMEM((NSA_HPG * QB, ka_w), BF16),
            pltpu.VMEM((NSA_HPG, QB), F32),
            pltpu.VMEM((NSA_HPG, QB), F32),
            pltpu.VMEM((NSA_DH, NSA_HPG * QB), F32),
        ],
    )
    return pl.pallas_call(
        _nsa_sel_kernel,
        out_shape=jax.ShapeDtypeStruct((N, NSA_HEADS * NSA_DH), BF16),
        grid_spec=grid_spec,
        compiler_params=_cparams(("parallel", "parallel", "arbitrary")),
        name="nsa_sel",
    )(jnp.asarray(qs), jnp.asarray(ks), slopes, U, selbias, k_aug, vt_slc, bias)


def _nsa_combine_kernel(oc_ref, os_ref, ow_ref, gate_ref, e_ref, o_ref):
    ge = _dot_split(_sigmoid(gate_ref[...]), e_ref[...])
    Wd = NSA_HEADS * NSA_DH
    o = (ge[:, :Wd] * oc_ref[...].astype(F32) + ge[:, Wd:2 * Wd] * os_ref[...].astype(F32)
         + ge[:, 2 * Wd:] * ow_ref[...].astype(F32))
    o_ref[...] = o.astype(o_ref.dtype)


def nsa_combine(o_c, o_s, o_w, Us):
    N, Wd = o_c.shape
    tm = 512
    e = np.zeros((128, 3 * Wd), np.float32)
    for h in range(NSA_HEADS):
        for j in range(3):
            e[h * 3 + j, j * Wd + h * NSA_DH:j * Wd + (h + 1) * NSA_DH] = 1.0
    blk = pl.BlockSpec((tm, Wd), lambda i: (i, 0))
    return pl.pallas_call(
        _nsa_combine_kernel,
        out_shape=jax.ShapeDtypeStruct((N, Wd), BF16),
        grid=(N // tm,),
        in_specs=[blk, blk, blk, pl.BlockSpec((tm, 128), lambda i: (i, S_NSA_GATE // 128)),
                  pl.BlockSpec((128, 3 * Wd), lambda i: (0, 0))],
        out_specs=blk,
        compiler_params=_cparams(("parallel",)),
        name="nsa_combine",
    )(o_c, o_s, o_w, Us, jnp.asarray(e, BF16))


def _merge_kernel(ya_ref, yb_ref, yc_ref, ga_ref, gb_ref, gc_ref, p_ref, o_ref):
    m = (_sigmoid(ga_ref[...].astype(F32)) * _dot(ya_ref[...], p_ref[0])
         + _sigmoid(gb_ref[...].astype(F32)) * _dot(yb_ref[...], p_ref[1])
         + _sigmoid(gc_ref[...].astype(F32)) * _dot(yc_ref[...], p_ref[2]))
    o_ref[...] = m.astype(o_ref.dtype)


def merge(y_a, y_b, y_c, U, p_merge):
    N = y_a.shape[0]
    tm, tn = 1024, 512
    yb = pl.BlockSpec((tm, BRANCH_DIM), lambda i, j: (i, 0))
    gate = lambda br: pl.BlockSpec((tm, tn), lambda i, j: (i, (U_GATE + br * D_MODEL) // tn + j))
    return pl.pallas_call(
        _merge_kernel,
        out_shape=jax.ShapeDtypeStruct((N, D_MODEL), BF16),
        grid=(N // tm, D_MODEL // tn),
        in_specs=[yb, yb, yb, gate(0), gate(1), gate(2),
                  pl.BlockSpec((3, BRANCH_DIM, tn), lambda i, j: (0, 0, j))],
        out_specs=pl.BlockSpec((tm, tn), lambda i, j: (i, j)),
        compiler_params=_cparams(("parallel", "parallel")),
        name="merge",
    )(y_a, y_b, y_c, U, U, U, p_merge)


def _ffn_kernel(h_ref, w1_ref, w3_ref, w2_ref, o_ref, acc_ref):
    f = pl.program_id(1)

    @pl.when(f == 0)
    def _():
        acc_ref[...] = jnp.zeros_like(acc_ref)

    h = h_ref[...]
    a = _dot(h, w1_ref[...])
    z = (a * _sigmoid(a) * _dot(h, w3_ref[...])).astype(BF16)
    acc_ref[...] += _dot(z, w2_ref[...])

    @pl.when(f == pl.num_programs(1) - 1)
    def _():
        o_ref[...] = acc_ref[...].astype(o_ref.dtype)


def ffn(h, w1, w3, w2):
    N, D = h.shape
    F = w1.shape[1]
    tm, tf = 1024, 512
    return pl.pallas_call(
        _ffn_kernel,
        out_shape=jax.ShapeDtypeStruct((N, D), BF16),
        grid=(N // tm, F // tf),
        in_specs=[pl.BlockSpec((tm, D), lambda i, f: (i, 0)),
                  pl.BlockSpec((D, tf), lambda i, f: (0, f)),
                  pl.BlockSpec((D, tf), lambda i, f: (0, f)),
                  pl.BlockSpec((tf, D), lambda i, f: (f, 0))],
        out_specs=pl.BlockSpec((tm, D), lambda i, f: (i, 0)),
        scratch_shapes=[pltpu.VMEM((tm, D), F32)],
        compiler_params=_cparams(("parallel", "arbitrary")),
        name="ffn",
    )(h, w1, w3, w2)


def _route_kernel(lg_ref, cb_ref):
    lg = lg_ref[...]
    lane = lax.broadcasted_iota(jnp.int32, lg.shape, 1)
    x = jnp.where(lane < N_EXPERTS, lg, NEG_BIG)
    v1 = jnp.max(x, axis=-1, keepdims=True)
    i1 = jnp.min(jnp.where(x == v1, lane, 1024), axis=-1, keepdims=True)
    x2 = jnp.where(lane == i1, NEG_BIG, x)
    v2 = jnp.max(x2, axis=-1, keepdims=True)
    i2 = jnp.min(jnp.where(x2 == v2, lane, 1024), axis=-1, keepdims=True)
    e2 = jnp.exp(v2 - v1)
    w1 = 1.0 / (1.0 + e2)
    w2 = e2 / (1.0 + e2)
    cb_ref[...] = jnp.where(lane == i1, w1, 0.0) + jnp.where(lane == i2, w2, 0.0)


def route(logits):
    N = logits.shape[0]
    tm = 1024
    blk = pl.BlockSpec((tm, 128), lambda i: (i, 0))
    return pl.pallas_call(
        _route_kernel,
        out_shape=jax.ShapeDtypeStruct((N, 128), F32),
        grid=(N // tm,),
        in_specs=[blk],
        out_specs=blk,
        compiler_params=_cparams(("parallel",)),
        name="route",
    )(logits)


def _moe_kernel(h_ref, cb_ref, w1_ref, w3_ref, w2_ref, o_ref, acc_ref):
    e = pl.program_id(1)
    f = pl.program_id(2)

    @pl.when((e == 0) & (f == 0))
    def _():
        acc_ref[...] = jnp.zeros_like(acc_ref)

    cb = cb_ref[...]
    lane = lax.broadcasted_iota(jnp.int32, cb.shape, 1)
    w = jnp.sum(jnp.where(lane == e, cb, 0.0), axis=-1, keepdims=True)
    h = h_ref[...]
    a = _dot(h, w1_ref[0])
    z = (a * _sigmoid(a) * _dot(h, w3_ref[0]) * w).astype(BF16)
    acc_ref[...] += _dot(z, w2_ref[0])

    @pl.when((e == pl.num_programs(1) - 1) & (f == pl.num_programs(2) - 1))
    def _():
        o_ref[...] = acc_ref[...].astype(o_ref.dtype)


def moe(h, combine, w1, w3, w2):
    N, D = h.shape
    E, _, F = w1.shape
    tm, tf = 1024, 512
    return pl.pallas_call(
        _moe_kernel,
        out_shape=jax.ShapeDtypeStruct((N, D), BF16),
        grid=(N // tm, E, F // tf),
        in_specs=[pl.BlockSpec((tm, D), lambda i, e, f: (i, 0)),
                  pl.BlockSpec((tm, 128), lambda i, e, f: (i, 0)),
                  pl.BlockSpec((1, D, tf), lambda i, e, f: (e, 0, f)),
                  pl.BlockSpec((1, D, tf), lambda i, e, f: (e, 0, f)),
                  pl.BlockSpec((1, tf, D), lambda i, e, f: (e, f, 0))],
        out_specs=pl.BlockSpec((tm, D), lambda i, e, f: (i, 0)),
        scratch_shapes=[pltpu.VMEM((tm, D), F32)],
        compiler_params=_cparams(("parallel", "arbitrary", "arbitrary")),
        name="moe",
    )(h, combine, w1, w3, w2)


def _project_weights(w_in):
    gla_w = 2 * GLA_HEADS * GLA_DK + 2 * GLA_HEADS * GLA_DV + GLA_RANK
    rw_w = 3 * RWKV_DIM + RWKV_W_RANK + RWKV_A_RANK + RWKV_G_RANK
    kvw = NSA_GROUPS * NSA_DH
    nsa_w = NSA_HEADS * NSA_DH + 6 * kvw + NSA_HEADS * 3
    o_rw = gla_w
    o_nsa = gla_w + rw_w
    o_gate = o_nsa + nsa_w
    big = jnp.concatenate([
        w_in[:, 0:gla_w - GLA_RANK],
        w_in[:, o_rw:o_rw + 3 * RWKV_DIM],
        w_in[:, o_nsa:o_nsa + NSA_HEADS * NSA_DH + 6 * kvw],
        w_in[:, o_gate:],
    ], axis=1).astype(BF16)
    D = w_in.shape[0]
    z = lambda n: jnp.zeros((D, n), w_in.dtype)
    small = jnp.concatenate([
        w_in[:, gla_w - GLA_RANK:gla_w], z(128 - GLA_RANK),
        w_in[:, o_rw + 3 * RWKV_DIM:o_rw + rw_w], z(256 - RWKV_G_RANK),
        w_in[:, o_gate - NSA_HEADS * 3:o_gate], z(128 - NSA_HEADS * 3),
    ], axis=1).astype(BF16)
    return big, small


def _mixer(h, batch, seq, w_in, gla_a2, gla_a_b, gla_norm, rwkv_mu, rwkv_w0, rwkv_w2, rwkv_a0, rwkv_a2, rwkv_g2,
           rwkv_k_k, rwkv_k_a, rwkv_r_k, rwkv_ln_w, rwkv_ln_b,
           nsa_pos_k, nsa_w1_k, nsa_w2_k, nsa_pos_v, nsa_w1_v, nsa_w2_v, p_merge, w_out, slopes):
    N = h.shape[0]
    w_big, w_small = _project_weights(w_in)
    tm_u = 2048 if N % 2048 == 0 else N
    U = matmul(h, w_big, BF16, tm_u, 512)
    Us = matmul(h, w_small, F32, tm_u, S_COLS)

    y_a = gla(U, Us, gla_a2, gla_a_b, gla_norm, batch=batch, seq=seq)

    r, k, v, kap, b, lw, gate = rwkv_prep(U, Us, rwkv_mu, rwkv_w0, rwkv_w2, rwkv_a0, rwkv_a2, rwkv_g2,
                                          rwkv_k_k, rwkv_k_a, seq=seq)
    y_b = rwkv_chunk(r, k, v, kap, b, lw, gate, rwkv_r_k.reshape(-1), rwkv_ln_w, rwkv_ln_b, batch=batch, seq=seq)

    kv6 = U[:, U_NSA_KV:U_NSA_KV + 6 * NSA_GROUPS * NSA_DH]
    kv6 = kv6.reshape(batch, seq, 6, NSA_GROUPS, NSA_DH).transpose(2, 0, 3, 1, 4)
    kvc = nsa_compress(kv6[0:2], jnp.stack([nsa_pos_k, nsa_pos_v]), jnp.stack([nsa_w1_k, nsa_w1_v]),
                       jnp.stack([nsa_w2_k, nsa_w2_v]))
    o_c, selbias = nsa_cmp(U, kvc, slopes, batch=batch, seq=seq)
    n_slc = seq // NSA_SLC_LEN
    blk_id = jnp.arange(seq) // NSA_SLC_LEN
    onehot = (blk_id[:, None] == jnp.arange(n_slc)[None, :]).astype(BF16)
    k_aug = jnp.concatenate([kv6[2], jnp.broadcast_to(onehot, (batch, NSA_GROUPS, seq, n_slc))], axis=-1)
    o_s = nsa_sel(U, selbias, k_aug, kv6[3].transpose(0, 1, 3, 2), slopes, batch=batch, seq=seq)
    pad = ((0, 0), (0, 0), (NSA_WINDOW, 0), (0, 0))
    o_w = nsa_win(U, jnp.pad(kv6[4], pad), jnp.pad(kv6[5], pad), slopes, batch=batch, seq=seq)
    y_c = nsa_combine(o_c, o_s, o_w, Us)

    merged = merge(y_a, y_b, y_c, U, p_merge.astype(BF16))
    return matmul(merged, w_out.astype(BF16), BF16, 1024 if N % 1024 == 0 else N, 1024)


def kernel(x, c, norm_mix, norm_ffn, ada_w, ada_b, w_in, gla_a2, gla_a_b, gla_norm, rwkv_mu, rwkv_w0, rwkv_w2, rwkv_a0, rwkv_a2, rwkv_g2, rwkv_k_k, rwkv_k_a, rwkv_r_k, rwkv_ln_w, rwkv_ln_b, nsa_pos_k, nsa_w1_k, nsa_w2_k, nsa_pos_v, nsa_w1_v, nsa_w2_v, p_merge, w_out, ffn_w1, ffn_w3, ffn_w2, moe_router, moe_w1, moe_w3, moe_w2, final_norm):
    B, T, D = x.shape
    depth = w_in.shape[0]
    N = B * T
    xs = x.reshape(N, D)
    ada = ada_all(c, ada_w, ada_b)
    slopes = jnp.exp2(-8.0 * jnp.arange(1, NSA_HEADS + 1, dtype=F32) / NSA_HEADS)
    y = None
    g_prev = None
    for l in range(depth):
        sh1, sc1, g1, sh2, sc2, g2 = jnp.split(ada[l], 6, axis=-1)
        if y is None:
            (h,) = resmod(xs, None, None, norm_mix[l], sh1, sc1, seq=T)
        else:
            xs, h = resmod(xs, y, g_prev, norm_mix[l], sh1, sc1, seq=T)
        y = _mixer(h, B, T, w_in[l], gla_a2[l], gla_a_b[l], gla_norm[l], rwkv_mu[l], rwkv_w0[l], rwkv_w2[l],
                   rwkv_a0[l], rwkv_a2[l], rwkv_g2[l], rwkv_k_k[l], rwkv_k_a[l], rwkv_r_k[l],
                   rwkv_ln_w[l], rwkv_ln_b[l], nsa_pos_k[l], nsa_w1_k[l], nsa_w2_k[l],
                   nsa_pos_v[l], nsa_w1_v[l], nsa_w2_v[l], p_merge[l], w_out[l], slopes)
        if l % 2 == 0:
            xs, h = resmod(xs, y, g1, norm_ffn[l], sh2, sc2, seq=T)
            y = ffn(h, ffn_w1[l // 2].astype(BF16), ffn_w3[l // 2].astype(BF16), ffn_w2[l // 2].astype(BF16))
        else:
            rt = jnp.zeros((D, 128), F32).at[:, :N_EXPERTS].set(moe_router[l // 2])
            xs, h, logits = resmod(xs, y, g1, norm_ffn[l], sh2, sc2, seq=T, router=rt)
            y = moe(h, route(logits), moe_w1[l // 2].astype(BF16), moe_w3[l // 2].astype(BF16),
                    moe_w2[l // 2].astype(BF16))
        g_prev = g2
    (out,) = resmod(xs, y, g_prev, final_norm, None, None, seq=T, final=True)
    return out.reshape(B, T, D)
```

```python
import functools

import numpy as np
import jax
import jax.numpy as jnp
from jax import lax
from jax.experimental import pallas as pl
from jax.experimental.pallas import tpu as pltpu

F32 = jnp.float32
BF16 = jnp.bfloat16
HI = lax.Precision.HIGHEST

V7X_VMEM_LIMIT_BYTES = 56 * 1024 * 1024

D_MODEL = 2048
NORM_EPS = 1e-6
NEG_BIG = -1e30

GLA_HEADS = 4
GLA_DK = 128
GLA_DV = 256
GLA_RANK = 16
GLA_NORMALIZER = 16.0
CHUNK = 64

RWKV_HEADS = 16
RWKV_N = 64
RWKV_DIM = RWKV_HEADS * RWKV_N
RWKV_W_RANK = 64
RWKV_A_RANK = 64
RWKV_G_RANK = 160
RWKV_LN_EPS = 64e-5
RWKV_HB = 4

NSA_HEADS = 16
NSA_GROUPS = 4
NSA_HPG = 4
NSA_DH = 64
NSA_CMP_LEN = 32
NSA_CMP_STRIDE = 16
NSA_CMP_HIDDEN = 128
NSA_SLC_LEN = 64
NSA_N_SEL = 16
NSA_WINDOW = 512
NSA_QB = 256
SEL_MASK_BIAS = 131072.0

BRANCH_DIM = 1024
D_FF = 5632
N_EXPERTS = 8

U_GLA_Q, U_GLA_K, U_GLA_V, U_GLA_G = 0, 512, 1024, 2048
U_RWKV_R, U_RWKV_K, U_RWKV_V = 3072, 4096, 5120
U_NSA_Q = 6144
U_NSA_KV = 7168
U_GATE = 8704
U_COLS = U_GATE + 3 * D_MODEL
S_GLA_A = 0
S_RWKV_WA = 128
S_RWKV_G = 256
S_NSA_GATE = 512
S_COLS = 640


def _cparams(sem, vmem=V7X_VMEM_LIMIT_BYTES):
    return pltpu.CompilerParams(dimension_semantics=sem, vmem_limit_bytes=vmem)


def _sigmoid(x):
    return 1.0 / (1.0 + jnp.exp(-x))


def _softplus(x):
    return jnp.maximum(x, 0.0) + jnp.log(1.0 + jnp.exp(-jnp.abs(x)))


def _dot(a, b):
    return jnp.dot(a, b, preferred_element_type=F32)


def _dot_nt(a, b):
    return lax.dot_general(a, b, (((1,), (1,)), ((), ())), preferred_element_type=F32)


def _dot_tn(a, b):
    return lax.dot_general(a, b, (((0,), (0,)), ((), ())), preferred_element_type=F32)


def _dot_hi(a, b):
    return jnp.dot(a, b, preferred_element_type=F32, precision=HI)


def _dot_split(x, w):
    hi = x.astype(BF16)
    lo = (x - hi.astype(F32)).astype(BF16)
    return _dot(hi, w) + _dot(lo, w)


def _ada_kernel(c_ref, w_ref, b_ref, o_ref):
    c = c_ref[...]
    o_ref[0] = _dot_hi(c * _sigmoid(c), w_ref[0]) + b_ref[0]


def ada_all(c, ada_w, ada_b):
    L, D, N6 = ada_w.shape
    B = c.shape[0]
    tn = 1024
    return pl.pallas_call(
        _ada_kernel,
        out_shape=jax.ShapeDtypeStruct((L, B, N6), F32),
        grid=(L, N6 // tn),
        in_specs=[
            pl.BlockSpec((B, D), lambda l, j: (0, 0)),
            pl.BlockSpec((1, D, tn), lambda l, j: (l, 0, j)),
            pl.BlockSpec((1, 1, tn), lambda l, j: (l, 0, j)),
        ],
        out_specs=pl.BlockSpec((1, B, tn), lambda l, j: (l, 0, j)),
        compiler_params=_cparams(("parallel", "parallel")),
        name="ada",
    )(c, ada_w, ada_b.reshape(L, 1, N6))


def _resmod_kernel(*refs, has_res, final, router):
    it = iter(refs)
    x_ref = next(it)
    y_ref = next(it) if has_res else None
    g_ref = next(it) if has_res else None
    ng_ref = next(it)
    sh_ref = None if final else next(it)
    sc_ref = None if final else next(it)
    rt_ref = next(it) if router else None
    xo_ref = next(it) if (has_res and not final) else None
    h_ref = next(it)
    lg_ref = next(it) if router else None

    x = x_ref[...]
    if has_res:
        x = x + g_ref[0] * y_ref[...].astype(F32)
        if xo_ref is not None:
            xo_ref[...] = x
    ms = jnp.mean(x * x, axis=-1, keepdims=True)
    h = x * lax.rsqrt(ms + NORM_EPS) * ng_ref[...]
    if not final:
        h = h * (1.0 + sc_ref[0]) + sh_ref[0]
    h_ref[...] = h.astype(h_ref.dtype)
    if router:
        lg_ref[...] = _dot_hi(h, rt_ref[...])


def resmod(x, y, g, norm_g, shift, scale, *, seq, router=None, final=False):
    N, D = x.shape
    tm = 512
    spb = seq // tm
    has_res = y is not None
    row = lambda i: (i, 0)
    per_b = lambda i: (i // spb, 0, 0)
    ins, specs = [x], [pl.BlockSpec((tm, D), row)]
    if has_res:
        ins += [y, g.reshape(-1, 1, D)]
        specs += [pl.BlockSpec((tm, D), row), pl.BlockSpec((1, 1, D), per_b)]
    ins.append(norm_g.reshape(1, D))
    specs.append(pl.BlockSpec((1, D), lambda i: (0, 0)))
    if not final:
        ins += [shift.reshape(-1, 1, D), scale.reshape(-1, 1, D)]
        specs += [pl.BlockSpec((1, 1, D), per_b), pl.BlockSpec((1, 1, D), per_b)]
    if router is not None:
        ins.append(router)
        specs.append(pl.BlockSpec(router.shape, lambda i: (0, 0)))
    outs, ospecs = [], []
    if has_res and not final:
        outs.append(jax.ShapeDtypeStruct((N, D), F32))
        ospecs.append(pl.BlockSpec((tm, D), row))
    outs.append(jax.ShapeDtypeStruct((N, D), F32 if final else BF16))
    ospecs.append(pl.BlockSpec((tm, D), row))
    if router is not None:
        outs.append(jax.ShapeDtypeStruct((N, 128), F32))
        ospecs.append(pl.BlockSpec((tm, 128), row))
    res = pl.pallas_call(
        functools.partial(_resmod_kernel, has_res=has_res, final=final, router=router is not None),
        out_shape=tuple(outs),
        grid=(N // tm,),
        in_specs=specs,
        out_specs=tuple(ospecs),
        compiler_params=_cparams(("parallel",)),
        name="resmod",
    )(*ins)
    return res


def _mm_kernel(x_ref, w_ref, o_ref):
    o_ref[...] = _dot(x_ref[...], w_ref[...]).astype(o_ref.dtype)


def matmul(x, w, out_dtype, tm, tn):
    M, K = x.shape
    N = w.shape[1]
    return pl.pallas_call(
        _mm_kernel,
        out_shape=jax.ShapeDtypeStruct((M, N), out_dtype),
        grid=(M // tm, N // tn),
        in_specs=[pl.BlockSpec((tm, K), lambda i, j: (i, 0)), pl.BlockSpec((K, tn), lambda i, j: (0, j))],
        out_specs=pl.BlockSpec((tm, tn), lambda i, j: (i, j)),
        compiler_params=_cparams(("parallel", "parallel")),
        name="matmul",
    )(x, w)


GLA_TC = 512


def _gla_kernel(q_ref, k_ref, v_ref, g_ref, alr_ref, a2_ref, ab_ref, ng_ref, o_ref, st_ref):
    @pl.when(pl.program_id(2) == 0)
    def _():
        st_ref[...] = jnp.zeros_like(st_ref)

    C = CHUNK
    la = _dot_hi(alr_ref[...], a2_ref[...]) + ab_ref[...]
    la = -_softplus(-la) / GLA_NORMALIZER
    ri = lax.broadcasted_iota(jnp.int32, (C, C), 0)
    ci = lax.broadcasted_iota(jnp.int32, (C, C), 1)
    causal = ri >= ci
    tril = causal.astype(F32)
    for c in range(GLA_TC // C):
        sl = slice(c * C, (c + 1) * C)
        bc = _dot_hi(tril, la[sl])
        bl = bc[C - 1:C, :]
        q = q_ref[sl, :].astype(F32) * (GLA_DK ** -0.5)
        k = k_ref[sl, :].astype(F32)
        v = v_ref[sl, :]
        qd = (q * jnp.exp(bc)).astype(BF16)
        kd = (k * jnp.exp(-bc)).astype(BF16)
        kl = (k * jnp.exp(bl - bc)).astype(BF16)
        att = jnp.where(causal, _dot_nt(qd, kd), 0.0).astype(BF16)
        st = st_ref[...]
        o = _dot(att, v) + _dot_nt(qd, st.astype(BF16))
        st_ref[...] = st * jnp.exp(bl) + _dot_tn(v, kl)
        o = o * lax.rsqrt(jnp.mean(o * o, axis=-1, keepdims=True) + NORM_EPS) * ng_ref[...]
        gg = g_ref[sl, :].astype(F32)
        o_ref[sl, :] = (o * (gg * _sigmoid(gg))).astype(o_ref.dtype)


def gla(U, Us, a2, a_b, norm_g, *, batch, seq):
    N = U.shape[0]
    nt = seq // GLA_TC
    a2p = jnp.zeros((128, GLA_HEADS * GLA_DK), F32).at[:GLA_RANK].set(a2)
    row = lambda b, h, i: b * nt + i
    return pl.pallas_call(
        _gla_kernel,
        out_shape=jax.ShapeDtypeStruct((N, GLA_HEADS * GLA_DV), BF16),
        grid=(batch, GLA_HEADS, nt),
        in_specs=[
            pl.BlockSpec((GLA_TC, GLA_DK), lambda b, h, i: (row(b, h, i), U_GLA_Q // GLA_DK + h)),
            pl.BlockSpec((GLA_TC, GLA_DK), lambda b, h, i: (row(b, h, i), U_GLA_K // GLA_DK + h)),
            pl.BlockSpec((GLA_TC, GLA_DV), lambda b, h, i: (row(b, h, i), U_GLA_V // GLA_DV + h)),
            pl.BlockSpec((GLA_TC, GLA_DV), lambda b, h, i: (row(b, h, i), U_GLA_G // GLA_DV + h)),
            pl.BlockSpec((GLA_TC, 128), lambda b, h, i: (row(b, h, i), S_GLA_A // 128)),
            pl.BlockSpec((128, GLA_DK), lambda b, h, i: (0, h)),
            pl.BlockSpec((1, GLA_DK), lambda b, h, i: (0, h)),
            pl.BlockSpec((1, GLA_DV), lambda b, h, i: (0, 0)),
        ],
        out_specs=pl.BlockSpec((GLA_TC, GLA_DV), lambda b, h, i: (row(b, h, i), h)),
        scratch_shapes=[pltpu.VMEM((GLA_DV, GLA_DK), F32)],
        compiler_params=_cparams(("parallel", "parallel", "arbitrary")),
        name="gla",
    )(U, U, U, U, Us, a2p, a_b.reshape(1, -1), norm_g.reshape(1, -1))


RWKV_TM = 256


def _seg_ones(n=256, seg=RWKV_N):
    i = np.arange(n)
    return jnp.asarray((i[:, None] // seg == i[None, :] // seg).astype(np.float32), BF16)


def _seg_sum(x, bd):
    outs = [_dot_split(x[:, s:s + 256], bd) for s in range(0, x.shape[1], 256)]
    return outs[0] if len(outs) == 1 else jnp.concatenate(outs, axis=1)


def _shift_lerp(u_ref, p_ref, mu, first):
    u = u_ref[...].astype(F32)
    prev_last = jnp.where(first, 0.0, p_ref[7:8, :].astype(F32))
    rolled = pltpu.roll(u, 1, 0)
    is_row0 = lax.broadcasted_iota(jnp.int32, u.shape, 0) == 0
    sh = jnp.where(is_row0, prev_last, rolled)
    return u + (sh - u) * mu


def _rwkv_prep_kernel(r_ref, k_ref, v_ref, wa_ref, gl_ref, rp_ref, kp_ref, vp_ref, wap_ref, glp_ref,
                      mur_ref, muk_ref, muv_ref, muwa_ref, mug_ref, w0_ref, w2_ref, a0_ref, a2_ref, g2_ref,
                      kk_ref, ka_ref, bd_ref,
                      ro_ref, ko_ref, vo_ref, kap_ref, bo_ref, lw_ref, go_ref, *, spb):
    first = (pl.program_id(0) % spb) == 0
    r = _shift_lerp(r_ref, rp_ref, mur_ref[...], first)
    k = _shift_lerp(k_ref, kp_ref, muk_ref[...], first)
    v = _shift_lerp(v_ref, vp_ref, muv_ref[...], first)
    wa = _shift_lerp(wa_ref, wap_ref, muwa_ref[...], first)
    gl = _shift_lerp(gl_ref, glp_ref, mug_ref[...], first)
    w_log = -_softplus(-(w0_ref[...] + _dot_hi(jnp.tanh(wa), w2_ref[...]))) - 0.5
    lw_ref[...] = -jnp.exp(w_log)
    a = _sigmoid(a0_ref[...] + _dot_hi(wa, a2_ref[...]))
    go_ref[...] = _dot(_sigmoid(gl).astype(BF16), g2_ref[...]).astype(go_ref.dtype)
    kk = k * kk_ref[...]
    nrm = jnp.sqrt(_seg_sum(kk * kk, bd_ref[...]))
    kk = kk / jnp.maximum(nrm, 1e-12)
    ro_ref[...] = r.astype(ro_ref.dtype)
    ko_ref[...] = (k * (1.0 + (a - 1.0) * ka_ref[...])).astype(ko_ref.dtype)
    vo_ref[...] = v.astype(vo_ref.dtype)
    kap_ref[...] = kk.astype(kap_ref.dtype)
    bo_ref[...] = (kk * a).astype(bo_ref.dtype)


def rwkv_prep(U, Us, mu, w0, w2, a0, a2, g2, k_k, k_a, *, seq):
    N = U.shape[0]
    tm = RWKV_TM
    spb = seq // tm
    R = RWKV_DIM
    cur = lambda cb: (lambda i: (i, cb))
    prv = lambda cb: (lambda i: (jnp.maximum(i * (tm // 8) - 1, 0), cb))
    mu_r, mu_k, mu_v = mu[:R], mu[R:2 * R], mu[2 * R:3 * R]
    mu_wa = mu[3 * R:3 * R + 128]
    mu_g = jnp.zeros((256,), F32).at[:RWKV_G_RANK].set(mu[3 * R + 128:])
    w2p = jnp.zeros((128, R), F32).at[:RWKV_W_RANK].set(w2)
    a2p = jnp.zeros((128, R), F32).at[RWKV_W_RANK:].set(a2)
    g2p = jnp.zeros((256, R), BF16).at[:RWKV_G_RANK].set(g2.astype(BF16))
    vec = lambda a: a.reshape(1, -1)
    full = lambda a: pl.BlockSpec(a.shape, lambda i: (0, 0))
    params = [vec(mu_r), vec(mu_k), vec(mu_v), vec(mu_wa), vec(mu_g), vec(w0), w2p, vec(a0), a2p, g2p,
              vec(k_k), vec(k_a), _seg_ones()]
    in_specs = [
        pl.BlockSpec((tm, R), cur(U_RWKV_R // R)), pl.BlockSpec((tm, R), cur(U_RWKV_K // R)),
        pl.BlockSpec((tm, R), cur(U_RWKV_V // R)),
        pl.BlockSpec((tm, 128), cur(S_RWKV_WA // 128)), pl.BlockSpec((tm, 256), cur(S_RWKV_G // 256)),
        pl.BlockSpec((8, R), prv(U_RWKV_R // R)), pl.BlockSpec((8, R), prv(U_RWKV_K // R)),
        pl.BlockSpec((8, R), prv(U_RWKV_V // R)),
        pl.BlockSpec((8, 128), prv(S_RWKV_WA // 128)), pl.BlockSpec((8, 256), prv(S_RWKV_G // 256)),
    ] + [full(p) for p in params]
    out = lambda dt: jax.ShapeDtypeStruct((N, R), dt)
    ospec = pl.BlockSpec((tm, R), lambda i: (i, 0))
    return pl.pallas_call(
        functools.partial(_rwkv_prep_kernel, spb=spb),
        out_shape=(out(BF16), out(BF16), out(BF16), out(BF16), out(BF16), out(F32), out(BF16)),
        grid=(N // tm,),
        in_specs=in_specs,
        out_specs=(ospec,) * 7,
        compiler_params=_cparams(("parallel",)),
        name="rwkv_prep",
    )(U, U, U, Us, Us, U, U, U, Us, Us, *params)


RWKV_TC = 512
RWKV_W = RWKV_HB * RWKV_N


def _rwkv_masks():
    W, C = RWKV_W, CHUNK
    i = np.arange(W)
    same = i[:, None] // C == i[None, :] // C
    m_bd = same.astype(np.float32)
    low_s = (same & (i[:, None] % C > i[None, :] % C)).astype(np.float32)
    low_i = (same & (i[:, None] % C >= i[None, :] % C)).astype(np.float32)
    tril = np.tril(np.ones((C, C), np.float32))
    return (jnp.asarray(m_bd), jnp.asarray(low_s), jnp.asarray(low_i), jnp.asarray(tril),
            jnp.asarray(np.eye(W, dtype=np.float32)))


def _tile4(x):
    return jnp.concatenate([x] * RWKV_HB, axis=0)


def _rwkv_chunk_kernel(r_ref, k_ref, v_ref, kap_ref, b_ref, lw_ref, g_ref,
                       mbd_ref, lows_ref, lowi_ref, tril_ref, eye_ref, bd_ref,
                       rk_ref, lnw_ref, lnb_ref, o_ref, st_ref):
    @pl.when(pl.program_id(1) == 0)
    def _():
        st_ref[...] = jnp.zeros_like(st_ref)

    C, W = CHUNK, RWKV_W
    n_groups = r_ref.shape[1] // W
    m_bd = mbd_ref[...]
    low_s = lows_ref[...]
    low_i = lowi_ref[...]
    tril = tril_ref[...]
    eye = eye_ref[...]
    bd = bd_ref[...]

    def expand(x):
        return (_tile4(x) * m_bd).astype(BF16)

    def body(c, carry):
        sl = pl.ds(pl.multiple_of(c * C, C), C)
        lw_all = lw_ref[sl, :]
        cum_all = _dot_hi(tril, lw_all)
        G = range(n_groups)
        lns = [slice(gi * W, (gi + 1) * W) for gi in G]
        cums = [cum_all[:, ln] for ln in lns]
        clasts = [cum[C - 1:C, :] for cum in cums]
        rs = [r_ref[sl, ln].astype(F32) for ln in lns]
        ks = [k_ref[sl, ln].astype(F32) for ln in lns]
        vs = [v_ref[sl, ln].astype(F32) for ln in lns]
        bs = [b_ref[sl, ln].astype(F32) for ln in lns]
        p_invs = [jnp.exp(-cum) for cum in cums]
        xes = [jnp.concatenate([expand(kap_ref[sl, ln].astype(F32) * jnp.exp(cum - lw_all[:, ln])),
                                expand(r * jnp.exp(cum))], axis=0)
               for ln, cum, r in zip(lns, cums, rs)]
        hes = [jnp.concatenate([expand(b * pi), expand(k * pi)], axis=0) for b, k, pi in zip(bs, ks, p_invs)]
        scs = [_dot_nt(xe, he) for xe, he in zip(xes, hes)]
        ams = [sc[:W, :W] * low_s for sc in scs]
        tinvs = [eye - a_m for a_m in ams]
        pws = ams
        for _ in range(int(np.log2(C)) - 1):
            pwbs = [pw.astype(BF16) for pw in pws]
            pws = [_dot(pwb, pwb) for pwb in pwbs]
            tinvs = [tinv + _dot(tinv.astype(BF16), pw.astype(BF16)) for tinv, pw in zip(tinvs, pws)]
        ves = [expand(v) for v in vs]
        bmv = [_dot((sc[:W, W:] * low_s).astype(BF16), ve) for sc, ve in zip(scs, ves)]
        qkv = [_dot((sc[W:, W:] * low_i).astype(BF16), ve) for sc, ve in zip(scs, ves)]
        sts = [st_ref[gi] for gi in G]
        xss = [_dot_nt(xe, st.astype(BF16)) for xe, st in zip(xes, sts)]
        us = [_dot(tinv.astype(BF16), (xs[:W] + bv).astype(BF16)) for tinv, xs, bv in zip(tinvs, xss, bmv)]
        kbs = [jnp.concatenate([expand(k * jnp.exp(cl - cum)), expand(b * jnp.exp(cl - cum))], axis=0)
               for k, b, cl, cum in zip(ks, bs, clasts, cums)]
        for gi in G:
            vu = jnp.concatenate([ves[gi], (-us[gi]).astype(BF16)], axis=0)
            st_ref[gi] = sts[gi] * jnp.exp(clasts[gi]) + _dot_tn(vu, kbs[gi]) * m_bd
        for gi in G:
            ln = lns[gi]
            qb = (scs[gi][W:, :W] * low_i).astype(BF16)
            y_e = xss[gi][W:] + qkv[gi] - _dot(qb, us[gi].astype(BF16))
            y = y_e[0:C] + y_e[C:2 * C] + y_e[2 * C:3 * C] + y_e[3 * C:4 * C]
            mu = _seg_sum(y, bd) * (1.0 / RWKV_N)
            yc = y - mu
            var = _seg_sum(yc * yc, bd) * (1.0 / RWKV_N)
            yn = yc * lax.rsqrt(var + RWKV_LN_EPS) * lnw_ref[:, ln] + lnb_ref[:, ln]
            bonus = _seg_sum(rs[gi] * ks[gi] * rk_ref[:, ln], bd) * vs[gi]
            o_ref[sl, ln] = ((yn + bonus) * g_ref[sl, ln].astype(F32)).astype(o_ref.dtype)
        return carry

    lax.fori_loop(0, RWKV_TC // C, body, 0)


def rwkv_chunk(r, k, v, kap, b, lw, gate, r_k, ln_w, ln_b, *, batch, seq):
    N, R = r.shape
    nt = seq // RWKV_TC
    W = RWKV_W
    blk = pl.BlockSpec((RWKV_TC, R), lambda bb, i: (bb * nt + i, 0))
    masks = _rwkv_masks() + (_seg_ones(),)
    full = lambda a: pl.BlockSpec(a.shape, lambda bb, i: (0, 0))
    pvec = pl.BlockSpec((1, R), lambda bb, i: (0, 0))
    return pl.pallas_call(
        _rwkv_chunk_kernel,
        out_shape=jax.ShapeDtypeStruct((N, R), BF16),
        grid=(batch, nt),
        in_specs=[blk] * 7 + [full(m) for m in masks] + [pvec] * 3,
        out_specs=blk,
        scratch_shapes=[pltpu.VMEM((R // W, W, W), F32)],
        compiler_params=_cparams(("parallel", "arbitrary")),
        name="rwkv_chunk",
    )(r, k, v, kap, b, lw, gate, *masks, r_k.reshape(1, R), ln_w.reshape(1, R), ln_b.reshape(1, R))


def _gelu_tanh(x):
    return 0.5 * x * (1.0 + jnp.tanh(np.sqrt(2.0 / np.pi) * (x + 0.044715 * (x * x * x))))


def _nsa_compress_kernel(x_ref, pos_ref, w1_ref, w2_ref, o_ref):
    x = x_ref[0, 0]
    w1 = w1_ref[0]
    half = w1.shape[0] // 2
    nrow = x.shape[0]
    ha = _dot(x, w1[:half])
    hb = _dot(x, w1[half:])
    h = ha + pltpu.roll(hb, nrow - 1, 0)
    pb = _dot(pos_ref[0], w1)
    h = _gelu_tanh(h + pb[0:1, :])
    o_ref[0, 0, 0] = _dot(h.astype(BF16), w2_ref[0]).astype(o_ref.dtype)


def nsa_compress(kv_cmp, pos, w1, w2):
    two, B, G, T, dh = kv_cmp.shape
    nr = T // NSA_CMP_STRIDE
    x = kv_cmp.reshape(two * B, G, nr, NSA_CMP_STRIDE * dh)
    posf = jnp.broadcast_to(pos.reshape(two, 1, NSA_CMP_LEN * dh), (two, 8, NSA_CMP_LEN * dh)).astype(BF16)
    return pl.pallas_call(
        _nsa_compress_kernel,
        out_shape=jax.ShapeDtypeStruct((two, B, G, nr, dh), BF16),
        grid=(two, B, G),
        in_specs=[
            pl.BlockSpec((1, 1, nr, NSA_CMP_STRIDE * dh), lambda s, b, g: (s * B + b, g, 0, 0)),
            pl.BlockSpec((1, 8, NSA_CMP_LEN * dh), lambda s, b, g: (s, 0, 0)),
            pl.BlockSpec((1, NSA_CMP_LEN * dh, NSA_CMP_HIDDEN), lambda s, b, g: (s, 0, 0)),
            pl.BlockSpec((1, NSA_CMP_HIDDEN, dh), lambda s, b, g: (s, 0, 0)),
        ],
        out_specs=pl.BlockSpec((1, 1, 1, nr, dh), lambda s, b, g: (s, b, g, 0, 0)),
        compiler_params=_cparams(("parallel", "parallel", "parallel")),
        name="nsa_compress",
    )(x, posf, w1.astype(BF16), w2.astype(BF16))


def _nsa_cmp_kernel(slope_ref, q_ref, kc_ref, vc_ref, ov_ref, oc_ref, sb_ref, *, n_slc, n_sel):
    g = pl.program_id(1)
    qi = pl.program_id(2)
    QB = NSA_QB
    ncp = kc_ref.shape[3]
    kc = kc_ref[0, 0, 0]
    vc = vc_ref[0, 0, 0]
    t = qi * QB + lax.broadcasted_iota(jnp.int32, (QB, ncp), 0)
    n = lax.broadcasted_iota(jnp.int32, (QB, ncp), 1)
    valid = (n * NSA_CMP_STRIDE + (NSA_CMP_LEN - 1)) <= t
    adist = jnp.abs(t.astype(F32) - (n.astype(F32) * NSA_CMP_STRIDE + (NSA_CMP_LEN - 1) / 2.0))
    psum = jnp.zeros((QB, ncp), F32)
    for h in range(NSA_HPG):
        qh = q_ref[:, h * NSA_DH:(h + 1) * NSA_DH] * (NSA_DH ** -0.5)
        s = _dot_nt(qh, kc) - slope_ref[g * NSA_HPG + h] * adist
        s = jnp.where(valid, s, NEG_BIG)
        m = jnp.max(s, axis=-1, keepdims=True)
        e = jnp.where(valid, jnp.exp(s - m), 0.0)
        l = jnp.sum(e, axis=-1, keepdims=True)
        p = e / jnp.maximum(l, 1e-30)
        oc_ref[:, h * NSA_DH:(h + 1) * NSA_DH] = _dot(p.astype(BF16), vc).astype(oc_ref.dtype)
        psum = psum + p
    imp = lax.dot_general(ov_ref[...], psum, (((1,), (1,)), ((), ())), preferred_element_type=F32, precision=HI)
    j = lax.broadcasted_iota(jnp.int32, (n_slc, QB), 0)
    tt = qi * QB + lax.broadcasted_iota(jnp.int32, (n_slc, QB), 1)
    cur = jnp.right_shift(tt, 6)
    forced = (j == 0) | (j == cur) | (j == cur - 1)
    cand = (j >= 1) & (j <= cur - 2)
    cur_row = cur[0:1, :]
    rank = jnp.zeros((n_slc, QB), jnp.int32)
    for jp in range(1, n_slc):
        row = imp[jp:jp + 1, :]
        ahead = (row > imp) | ((row == imp) & (jp < j))
        rank = rank + jnp.where(ahead & (jp <= cur_row - 2), 1, 0)
    sel = forced | (cand & (rank < n_sel - 3))
    sb = jnp.where(sel, 0.0, -SEL_MASK_BIAS)
    sb_ref[0, 0] = sb.T.astype(sb_ref.dtype)


def nsa_cmp(U, kvc, slopes, *, batch, seq):
    N = U.shape[0]
    QB = NSA_QB
    nq = seq // QB
    ncp = kvc.shape[3]
    n_slc = seq // NSA_SLC_LEN
    n_sel = min(NSA_N_SEL, n_slc)
    nn = np.arange(ncp)
    jj = np.arange(n_slc)
    ov = ((nn[None, :] * NSA_CMP_STRIDE + NSA_CMP_LEN - 1 >= jj[:, None] * NSA_SLC_LEN)
          & (nn[None, :] * NSA_CMP_STRIDE <= jj[:, None] * NSA_SLC_LEN + NSA_SLC_LEN - 1)
          & (nn[None, :] < ncp - 1)).astype(np.float32)
    G = NSA_GROUPS
    W = NSA_HPG * NSA_DH
    grid_spec = pltpu.PrefetchScalarGridSpec(
        num_scalar_prefetch=1,
        grid=(batch, G, nq),
        in_specs=[
            pl.BlockSpec((QB, W), lambda b, g, i, s: (b * nq + i, U_NSA_Q // W + g)),
            pl.BlockSpec((1, 1, 1, ncp, NSA_DH), lambda b, g, i, s: (0, b, g, 0, 0)),
            pl.BlockSpec((1, 1, 1, ncp, NSA_DH), lambda b, g, i, s: (1, b, g, 0, 0)),
            pl.BlockSpec((n_slc, ncp), lambda b, g, i, s: (0, 0)),
        ],
        out_specs=(
            pl.BlockSpec((QB, W), lambda b, g, i, s: (b * nq + i, g)),
            pl.BlockSpec((1, 1, QB, n_slc), lambda b, g, i, s: (b, g, i, 0)),
        ),
    )
    return pl.pallas_call(
        functools.partial(_nsa_cmp_kernel, n_slc=n_slc, n_sel=n_sel),
        out_shape=(jax.ShapeDtypeStruct((N, NSA_HEADS * NSA_DH), BF16),
                   jax.ShapeDtypeStruct((batch, G, seq, n_slc), BF16)),
        grid_spec=grid_spec,
        compiler_params=_cparams(("parallel", "parallel", "parallel")),
        name="nsa_cmp",
    )(slopes, U, kvc, kvc, jnp.asarray(ov))


def _nsa_win_kernel(slope_ref, q_ref, k0_ref, k1_ref, k2_ref, v0_ref, v1_ref, v2_ref, o_ref):
    g = pl.program_id(1)
    qi = pl.program_id(2)
    QB = NSA_QB
    KW = NSA_WINDOW + QB
    k = jnp.concatenate([k0_ref[0, 0], k1_ref[0, 0], k2_ref[0, 0]], axis=0)
    v = jnp.concatenate([v0_ref[0, 0], v1_ref[0, 0], v2_ref[0, 0]], axis=0)
    r = lax.broadcasted_iota(jnp.int32, (QB, KW), 0)
    c = lax.broadcasted_iota(jnp.int32, (QB, KW), 1)
    dist = r - c + NSA_WINDOW
    pos = qi * QB - NSA_WINDOW + c
    valid = (dist >= 0) & (dist < NSA_WINDOW) & (pos >= 0)
    adist = jnp.abs(dist).astype(F32)
    for h in range(NSA_HPG):
        qh = q_ref[:, h * NSA_DH:(h + 1) * NSA_DH] * (NSA_DH ** -0.5)
        s = _dot_nt(qh, k) - slope_ref[g * NSA_HPG + h] * adist
        s = jnp.where(valid, s, NEG_BIG)
        m = jnp.max(s, axis=-1, keepdims=True)
        e = jnp.exp(s - m)
        l = jnp.sum(e, axis=-1, keepdims=True)
        o = _dot(e.astype(BF16), v) / l
        o_ref[:, h * NSA_DH:(h + 1) * NSA_DH] = o.astype(o_ref.dtype)


def nsa_win(U, kw_pad, vw_pad, slopes, *, batch, seq):
    N = U.shape[0]
    QB = NSA_QB
    nq = seq // QB
    G = NSA_GROUPS
    W = NSA_HPG * NSA_DH
    kv = lambda off: pl.BlockSpec((1, 1, QB, NSA_DH), lambda b, g, i, s: (b, g, i + off, 0))
    grid_spec = pltpu.PrefetchScalarGridSpec(
        num_scalar_prefetch=1,
        grid=(batch, G, nq),
        in_specs=[pl.BlockSpec((QB, W), lambda b, g, i, s: (b * nq + i, U_NSA_Q // W + g)),
                  kv(0), kv(1), kv(2), kv(0), kv(1), kv(2)],
        out_specs=pl.BlockSpec((QB, W), lambda b, g, i, s: (b * nq + i, g)),
    )
    return pl.pallas_call(
        _nsa_win_kernel,
        out_shape=jax.ShapeDtypeStruct((N, NSA_HEADS * NSA_DH), BF16),
        grid_spec=grid_spec,
        compiler_params=_cparams(("parallel", "parallel", "parallel")),
        name="nsa_win",
    )(slopes, U, kw_pad, kw_pad, kw_pad, vw_pad, vw_pad, vw_pad)


SEL_QB = 256
SEL_KV = 512


def _sel_pairs(seq):
    qs, ks = [], []
    for qi in range(seq // SEL_QB):
        for kj in range((qi * SEL_QB) // SEL_KV + 1):
            qs.append(qi)
            ks.append(kj)
    return np.asarray(qs, np.int32), np.asarray(ks, np.int32)


def _nsa_sel_kernel(qi_ref, kj_ref, slope_ref, q_ref, sb_ref, ka_ref, vt_ref, bias_ref, o_ref,
                    qa_ref, m_ref, l_ref, acc_ref):
    g = pl.program_id(1)
    p = pl.program_id(2)
    qi = qi_ref[p]
    kj = kj_ref[p]
    QB, KV = SEL_QB, SEL_KV
    nblk = sb_ref.shape[3]

    @pl.when(kj == 0)
    def _():
        for h in range(NSA_HPG):
            qa_ref[h * QB:(h + 1) * QB, 0:NSA_DH] = q_ref[:, h * NSA_DH:(h + 1) * NSA_DH] * (NSA_DH ** -0.5)
            qa_ref[h * QB:(h + 1) * QB, NSA_DH:NSA_DH + nblk] = sb_ref[0, 0]
        m_ref[...] = jnp.full_like(m_ref, NEG_BIG)
        l_ref[...] = jnp.zeros_like(l_ref)
        acc_ref[...] = jnp.zeros_like(acc_ref)

    off = qi * QB - kj * KV
    offf = off.astype(F32)

    def step(masked):
        s = _dot_nt(ka_ref[0, 0], qa_ref[...])
        vt = vt_ref[0, 0]
        if masked:
            r = lax.broadcasted_iota(jnp.int32, (KV, QB), 0)
            c = lax.broadcasted_iota(jnp.int32, (KV, QB), 1)
            causal = (c - r + off) >= 0
        for h in range(NSA_HPG):
            cols = slice(h * QB, (h + 1) * QB)
            sh = s[:, cols] + bias_ref[h]
            if masked:
                sh = jnp.where(causal, sh, NEG_BIG)
            delta = -slope_ref[g * NSA_HPG + h] * offf
            m_old = m_ref[h:h + 1, :]
            m_new = jnp.maximum(m_old, jnp.max(sh, axis=0, keepdims=True) + delta)
            e = jnp.exp(sh - (m_new - delta))
            alpha = jnp.exp(m_old - m_new)
            l_ref[h:h + 1, :] = alpha * l_ref[h:h + 1, :] + jnp.sum(e, axis=0, keepdims=True)
            acc_ref[:, cols] = alpha * acc_ref[:, cols] + _dot(vt, e.astype(BF16))
            m_ref[h:h + 1, :] = m_new

    last = (qi * QB) // KV

    @pl.when(kj < last)
    def _():
        step(False)

    @pl.when(kj == last)
    def _():
        step(True)
        for h in range(NSA_HPG):
            cols = slice(h * QB, (h + 1) * QB)
            o = acc_ref[:, cols] / l_ref[h:h + 1, :]
            o_ref[:, h * NSA_DH:(h + 1) * NSA_DH] = o.T.astype(o_ref.dtype)


def nsa_sel(U, selbias, k_aug, vt_slc, slopes, *, batch, seq):
    N = U.shape[0]
    QB, KV = SEL_QB, SEL_KV
    nq = seq // QB
    G = NSA_GROUPS
    W = NSA_HPG * NSA_DH
    n_slc = selbias.shape[3]
    ka_w = k_aug.shape[3]
    qs, ks = _sel_pairs(seq)
    rc = (np.arange(QB)[None, :] - np.arange(KV)[:, None]).astype(np.float32)
    bias = -slopes[:, None, None] * jnp.asarray(rc)[None]
    grid_spec = pltpu.PrefetchScalarGridSpec(
        num_scalar_prefetch=3,
        grid=(batch, G, len(qs)),
        in_specs=[
            pl.BlockSpec((QB, W), lambda b, g, p, qi, kj, s: (b * nq + qi[p], U_NSA_Q // W + g)),
            pl.BlockSpec((1, 1, QB, n_slc), lambda b, g, p, qi, kj, s: (b, g, qi[p], 0)),
            pl.BlockSpec((1, 1, KV, ka_w), lambda b, g, p, qi, kj, s: (b, g, kj[p], 0)),
            pl.BlockSpec((1, 1, NSA_DH, KV), lambda b, g, p, qi, kj, s: (b, g, 0, kj[p])),
            pl.BlockSpec((NSA_HPG, KV, QB), lambda b, g, p, qi, kj, s: (g, 0, 0)),
        ],
        out_specs=pl.BlockSpec((QB, W), lambda b, g, p, qi, kj, s: (b * nq + qi[p], g)),
        scratch_shapes=[
            pltpu.VMEM((NSA_HPG * QB, ka_w), BF16),
            pltpu.VMEM((NSA_HPG, QB), F32),
            pltpu.VMEM((NSA_HPG, QB), F32),
            pltpu.VMEM((NSA_DH, NSA_HPG * QB), F32),
        ],
    )
    return pl.pallas_call(
        _nsa_sel_kernel,
        out_shape=jax.ShapeDtypeStruct((N, NSA_HEADS * NSA_DH), BF16),
        grid_spec=grid_spec,
        compiler_params=_cparams(("parallel", "parallel", "arbitrary")),
        name="nsa_sel",
    )(jnp.asarray(qs), jnp.asarray(ks), slopes, U, selbias, k_aug, vt_slc, bias)


def _nsa_combine_kernel(oc_ref, os_ref, ow_ref, gate_ref, e_ref, o_ref):
    ge = _dot_split(_sigmoid(gate_ref[...]), e_ref[...])
    Wd = NSA_HEADS * NSA_DH
    o = (ge[:, :Wd] * oc_ref[...].astype(F32) + ge[:, Wd:2 * Wd] * os_ref[...].astype(F32)
         + ge[:, 2 * Wd:] * ow_ref[...].astype(F32))
    o_ref[...] = o.astype(o_ref.dtype)


def nsa_combine(o_c, o_s, o_w, Us):
    N, Wd = o_c.shape
    tm = 512
    e = np.zeros((128, 3 * Wd), np.float32)
    for h in range(NSA_HEADS):
        for j in range(3):
            e[h * 3 + j, j * Wd + h * NSA_DH:j * Wd + (h + 1) * NSA_DH] = 1.0
    blk = pl.BlockSpec((tm, Wd), lambda i: (i, 0))
    return pl.pallas_call(
        _nsa_combine_kernel,
        out_shape=jax.ShapeDtypeStruct((N, Wd), BF16),
        grid=(N // tm,),
        in_specs=[blk, blk, blk, pl.BlockSpec((tm, 128), lambda i: (i, S_NSA_GATE // 128)),
                  pl.BlockSpec((128, 3 * Wd), lambda i: (0, 0))],
        out_specs=blk,
        compiler_params=_cparams(("parallel",)),
        name="nsa_combine",
    )(o_c, o_s, o_w, Us, jnp.asarray(e, BF16))


def _merge_kernel(ya_ref, yb_ref, yc_ref, ga_ref, gb_ref, gc_ref, p_ref, o_ref):
    m = (_sigmoid(ga_ref[...].astype(F32)) * _dot(ya_ref[...], p_ref[0])
         + _sigmoid(gb_ref[...].astype(F32)) * _dot(yb_ref[...], p_ref[1])
         + _sigmoid(gc_ref[...].astype(F32)) * _dot(yc_ref[...], p_ref[2]))
    o_ref[...] = m.astype(o_ref.dtype)


def merge(y_a, y_b, y_c, U, p_merge):
    N = y_a.shape[0]
    tm, tn = 1024, 512
    yb = pl.BlockSpec((tm, BRANCH_DIM), lambda i, j: (i, 0))
    gate = lambda br: pl.BlockSpec((tm, tn), lambda i, j: (i, (U_GATE + br * D_MODEL) // tn + j))
    return pl.pallas_call(
        _merge_kernel,
        out_shape=jax.ShapeDtypeStruct((N, D_MODEL), BF16),
        grid=(N // tm, D_MODEL // tn),
        in_specs=[yb, yb, yb, gate(0), gate(1), gate(2),
                  pl.BlockSpec((3, BRANCH_DIM, tn), lambda i, j: (0, 0, j))],
        out_specs=pl.BlockSpec((tm, tn), lambda i, j: (i, j)),
        compiler_params=_cparams(("parallel", "parallel")),
        name="merge",
    )(y_a, y_b, y_c, U, U, U, p_merge)


def _ffn_kernel(h_ref, w1_ref, w3_ref, w2_ref, o_ref, acc_ref):
    f = pl.program_id(1)

    @pl.when(f == 0)
    def _():
        acc_ref[...] = jnp.zeros_like(acc_ref)

    h = h_ref[...]
    a = _dot(h, w1_ref[...])
    z = (a * _sigmoid(a) * _dot(h, w3_ref[...])).astype(BF16)
    acc_ref[...] += _dot(z, w2_ref[...])

    @pl.when(f == pl.num_programs(1) - 1)
    def _():
        o_ref[...] = acc_ref[...].astype(o_ref.dtype)


def ffn(h, w1, w3, w2):
    N, D = h.shape
    F = w1.shape[1]
    tm, tf = 1024, 512
    return pl.pallas_call(
        _ffn_kernel,
        out_shape=jax.ShapeDtypeStruct((N, D), BF16),
        grid=(N // tm, F // tf),
        in_specs=[pl.BlockSpec((tm, D), lambda i, f: (i, 0)),
                  pl.BlockSpec((D, tf), lambda i, f: (0, f)),
                  pl.BlockSpec((D, tf), lambda i, f: (0, f)),
                  pl.BlockSpec((tf, D), lambda i, f: (f, 0))],
        out_specs=pl.BlockSpec((tm, D), lambda i, f: (i, 0)),
        scratch_shapes=[pltpu.VMEM((tm, D), F32)],
        compiler_params=_cparams(("parallel", "arbitrary")),
        name="ffn",
    )(h, w1, w3, w2)


def _route_kernel(lg_ref, cb_ref, sel_ref):
    lg = lg_ref[...]
    lane = lax.broadcasted_iota(jnp.int32, lg.shape, 1)
    x = jnp.where(lane < N_EXPERTS, lg, NEG_BIG)
    v1 = jnp.max(x, axis=-1, keepdims=True)
    i1 = jnp.min(jnp.where(x == v1, lane, 1024), axis=-1, keepdims=True)
    x2 = jnp.where(lane == i1, NEG_BIG, x)
    v2 = jnp.max(x2, axis=-1, keepdims=True)
    i2 = jnp.min(jnp.where(x2 == v2, lane, 1024), axis=-1, keepdims=True)
    e2 = jnp.exp(v2 - v1)
    w1 = 1.0 / (1.0 + e2)
    w2 = e2 / (1.0 + e2)
    cb_ref[...] = jnp.where(lane == i1, w1, 0.0) + jnp.where(lane == i2, w2, 0.0)
    sel_ref[...] = jnp.where((lane == i1) | (lane == i2), 1.0, 0.0)


def route(logits):
    N = logits.shape[0]
    tm = 1024
    blk = pl.BlockSpec((tm, 128), lambda i: (i, 0))
    return pl.pallas_call(
        _route_kernel,
        out_shape=(jax.ShapeDtypeStruct((N, 128), F32), jax.ShapeDtypeStruct((N, 128), F32)),
        grid=(N // tm,),
        in_specs=[blk],
        out_specs=(blk, blk),
        compiler_params=_cparams(("parallel",)),
        name="route",
    )(logits)


MOE_TM = 512
MOE_WB = 512


def _moe_plan(sel, wte):
    N, E = sel.shape
    tm, wb = MOE_TM, MOE_WB
    NT = 2 * N // tm + E
    R = NT * tm
    P = NT + E * (N // wb)
    i32 = jnp.int32
    seli = sel.astype(i32)
    cnt = seli.sum(0)
    tiles_e = (cnt + tm - 1) // tm
    tile_end = jnp.cumsum(tiles_e)
    tile_start = tile_end - tiles_e
    total_tiles = tile_end[-1]
    pos = jnp.cumsum(seli, axis=0) - 1
    dest = jnp.where(sel, tile_start[None, :] * tm + pos, R).reshape(-1)
    tok = jnp.broadcast_to(jnp.arange(N, dtype=i32)[:, None], (N, E)).reshape(-1)
    src = jnp.full((R,), -1, i32).at[dest].set(tok, mode="drop")
    wrow = jnp.zeros((R,), F32).at[dest].set(wte.reshape(-1), mode="drop")
    ti = jnp.arange(NT, dtype=i32)
    tile_valid = ti < total_tiles
    tile_e = jnp.minimum(jnp.searchsorted(tile_end, ti, side="right").astype(i32), E - 1)
    nvalid = jnp.where(tile_valid, jnp.clip(cnt[tile_e] - (ti - tile_start[tile_e]) * tm, 1, tm), 1)
    blo = jnp.where(tile_valid, src[ti * tm] // wb, 0)
    bhi = jnp.where(tile_valid, src[ti * tm + nvalid - 1] // wb, -1)
    npair = bhi - blo + 1
    pend = jnp.cumsum(npair)
    pstart = pend - npair
    total_p = pend[-1]
    pi = jnp.arange(P, dtype=i32)
    pvalid = pi < total_p
    ptile = jnp.minimum(jnp.searchsorted(pend, pi, side="right").astype(i32), NT - 1)
    pblk = blo[ptile] + pi - pstart[ptile]
    pfirst = pvalid & (pi == pstart[ptile])
    ptile = jnp.where(pvalid, ptile, ptile[total_p - 1])
    pblk = jnp.where(pvalid, pblk, pblk[total_p - 1])
    order = jnp.argsort(jnp.where(pvalid, pblk * NT + ptile, jnp.iinfo(jnp.int32).max))
    s_valid = pvalid
    s_tile = jnp.where(s_valid, ptile[order], ptile[order][total_p - 1])
    s_blk = jnp.where(s_valid, pblk[order], pblk[order][total_p - 1])
    s_first = s_valid & ((pi == 0) | (s_blk != jnp.roll(s_blk, 1)))
    e_src = jnp.where(tile_valid, ti, 0)
    e_exp = jnp.where(tile_valid, tile_e, tile_e[jnp.maximum(total_tiles - 1, 0)])
    b2i = lambda x: x.astype(i32)
    return dict(src=src, wrow=wrow, g=(ptile, pblk, b2i(pvalid), b2i(pfirst)),
                e=(e_exp, b2i(tile_valid), e_src), s=(s_tile, s_blk, b2i(s_valid), b2i(s_first)), NT=NT, P=P)


def _moe_group_kernel(pt_ref, pb_ref, pv_ref, pf_ref, src_ref, h_ref, o_ref):
    p = pl.program_id(0)

    @pl.when(pv_ref[p] == 1)
    def _():
        rel = src_ref[...] - pb_ref[p] * MOE_WB
        col = lax.broadcasted_iota(jnp.int32, (MOE_TM, MOE_WB), 1)
        onehot = jnp.where(rel == col, 1.0, 0.0).astype(BF16)
        rows = _dot(onehot, h_ref[...]).astype(o_ref.dtype)

        @pl.when(pf_ref[p] == 1)
        def _():
            o_ref[...] = rows

        @pl.when(pf_ref[p] == 0)
        def _():
            o_ref[...] += rows


def _moe_expert_kernel(te_ref, tv_ref, ts_ref, h_ref, w_ref, w1_ref, w3_ref, w2_ref, o_ref, acc_ref):
    i = pl.program_id(0)
    f = pl.program_id(1)
    nf = pl.num_programs(1)

    @pl.when(tv_ref[i] == 1)
    def _():
        @pl.when(f == 0)
        def _():
            acc_ref[...] = jnp.zeros_like(acc_ref)

        h = h_ref[...]
        a = _dot(h, w1_ref[0])
        z = (a * _sigmoid(a) * _dot(h, w3_ref[0]) * w_ref[...]).astype(BF16)
        acc_ref[...] += _dot(z, w2_ref[0])

        @pl.when(f == nf - 1)
        def _():
            o_ref[...] = acc_ref[...].astype(o_ref.dtype)

    @pl.when((tv_ref[i] == 0) & (f == nf - 1))
    def _():
        o_ref[...] = jnp.zeros_like(o_ref)


def _moe_ungroup_kernel(st_ref, sb_ref, sv_ref, sf_ref, src_ref, y_ref, o_ref):
    p = pl.program_id(0)

    @pl.when(sv_ref[p] == 1)
    def _():
        rel = src_ref[...] - sb_ref[p] * MOE_WB
        row = lax.broadcasted_iota(jnp.int32, (MOE_WB, MOE_TM), 0)
        onehot_t = jnp.where(rel == row, 1.0, 0.0).astype(BF16)
        part = _dot(onehot_t, y_ref[...])

        @pl.when(sf_ref[p] == 1)
        def _():
            o_ref[...] = part

        @pl.when(sf_ref[p] == 0)
        def _():
            o_ref[...] += part


def moe(h, combine, selm, w1, w3, w2):
    N, D = h.shape
    E, _, F = w1.shape
    tm, wb, tf = MOE_TM, MOE_WB, 512
    plan = _moe_plan(selm[:, :E] > 0.5, combine[:, :E])
    NT, P = plan["NT"], plan["P"]
    R = NT * tm
    h_sorted = pl.pallas_call(
        _moe_group_kernel,
        out_shape=jax.ShapeDtypeStruct((R, D), BF16),
        grid_spec=pltpu.PrefetchScalarGridSpec(
            num_scalar_prefetch=4,
            grid=(P,),
            in_specs=[pl.BlockSpec((tm, 1), lambda p, pt, pb, pv, pf: (pt[p], 0)),
                      pl.BlockSpec((wb, D), lambda p, pt, pb, pv, pf: (pb[p], 0))],
            out_specs=pl.BlockSpec((tm, D), lambda p, pt, pb, pv, pf: (pt[p], 0)),
        ),
        compiler_params=_cparams(("arbitrary",)),
        name="moe_group",
    )(*plan["g"], plan["src"].reshape(R, 1), h)
    y_sorted = pl.pallas_call(
        _moe_expert_kernel,
        out_shape=jax.ShapeDtypeStruct((R, D), BF16),
        grid_spec=pltpu.PrefetchScalarGridSpec(
            num_scalar_prefetch=3,
            grid=(NT, F // tf),
            in_specs=[pl.BlockSpec((tm, D), lambda i, f, te, tv, ts: (ts[i], 0)),
                      pl.BlockSpec((tm, 1), lambda i, f, te, tv, ts: (ts[i], 0)),
                      pl.BlockSpec((1, D, tf), lambda i, f, te, tv, ts: (te[i], 0, f * tv[i])),
                      pl.BlockSpec((1, D, tf), lambda i, f, te, tv, ts: (te[i], 0, f * tv[i])),
                      pl.BlockSpec((1, tf, D), lambda i, f, te, tv, ts: (te[i], f * tv[i], 0))],
            out_specs=pl.BlockSpec((tm, D), lambda i, f, te, tv, ts: (i, 0)),
            scratch_shapes=[pltpu.VMEM((tm, D), F32)],
        ),
        compiler_params=_cparams(("arbitrary", "arbitrary")),
        name="moe_expert",
    )(*plan["e"], h_sorted, plan["wrow"].reshape(R, 1), w1, w3, w2)
    return pl.pallas_call(
        _moe_ungroup_kernel,
        out_shape=jax.ShapeDtypeStruct((N, D), F32),
        grid_spec=pltpu.PrefetchScalarGridSpec(
            num_scalar_prefetch=4,
            grid=(P,),
            in_specs=[pl.BlockSpec((1, tm), lambda p, st, sb, sv, sf: (0, st[p])),
                      pl.BlockSpec((tm, D), lambda p, st, sb, sv, sf: (st[p], 0))],
            out_specs=pl.BlockSpec((wb, D), lambda p, st, sb, sv, sf: (sb[p], 0)),
        ),
        compiler_params=_cparams(("arbitrary",)),
        name="moe_ungroup",
    )(*plan["s"], plan["src"].reshape(1, R), y_sorted)


def _project_weights(w_in):
    gla_w = 2 * GLA_HEADS * GLA_DK + 2 * GLA_HEADS * GLA_DV + GLA_RANK
    rw_w = 3 * RWKV_DIM + RWKV_W_RANK + RWKV_A_RANK + RWKV_G_RANK
    kvw = NSA_GROUPS * NSA_DH
    nsa_w = NSA_HEADS * NSA_DH + 6 * kvw + NSA_HEADS * 3
    o_rw = gla_w
    o_nsa = gla_w + rw_w
    o_gate = o_nsa + nsa_w
    big = jnp.concatenate([
        w_in[:, 0:gla_w - GLA_RANK],
        w_in[:, o_rw:o_rw + 3 * RWKV_DIM],
        w_in[:, o_nsa:o_nsa + NSA_HEADS * NSA_DH + 6 * kvw],
        w_in[:, o_gate:],
    ], axis=1).astype(BF16)
    D = w_in.shape[0]
    z = lambda n: jnp.zeros((D, n), w_in.dtype)
    small = jnp.concatenate([
        w_in[:, gla_w - GLA_RANK:gla_w], z(128 - GLA_RANK),
        w_in[:, o_rw + 3 * RWKV_DIM:o_rw + rw_w], z(256 - RWKV_G_RANK),
        w_in[:, o_gate - NSA_HEADS * 3:o_gate], z(128 - NSA_HEADS * 3),
    ], axis=1).astype(BF16)
    return big, small


def _mixer(h, batch, seq, w_in, gla_a2, gla_a_b, gla_norm, rwkv_mu, rwkv_w0, rwkv_w2, rwkv_a0, rwkv_a2, rwkv_g2,
           rwkv_k_k, rwkv_k_a, rwkv_r_k, rwkv_ln_w, rwkv_ln_b,
           nsa_pos_k, nsa_w1_k, nsa_w2_k, nsa_pos_v, nsa_w1_v, nsa_w2_v, p_merge, w_out, slopes):
    N = h.shape[0]
    w_big, w_small = _project_weights(w_in)
    tm_u = 2048 if N % 2048 == 0 else N
    U = matmul(h, w_big, BF16, tm_u, 512)
    Us = matmul(h, w_small, F32, tm_u, S_COLS)

    y_a = gla(U, Us, gla_a2, gla_a_b, gla_norm, batch=batch, seq=seq)

    r, k, v, kap, b, lw, gate = rwkv_prep(U, Us, rwkv_mu, rwkv_w0, rwkv_w2, rwkv_a0, rwkv_a2, rwkv_g2,
                                          rwkv_k_k, rwkv_k_a, seq=seq)
    y_b = rwkv_chunk(r, k, v, kap, b, lw, gate, rwkv_r_k.reshape(-1), rwkv_ln_w, rwkv_ln_b, batch=batch, seq=seq)

    kv6 = U[:, U_NSA_KV:U_NSA_KV + 6 * NSA_GROUPS * NSA_DH]
    kv6 = kv6.reshape(batch, seq, 6, NSA_GROUPS, NSA_DH).transpose(2, 0, 3, 1, 4)
    kvc = nsa_compress(kv6[0:2], jnp.stack([nsa_pos_k, nsa_pos_v]), jnp.stack([nsa_w1_k, nsa_w1_v]),
                       jnp.stack([nsa_w2_k, nsa_w2_v]))
    o_c, selbias = nsa_cmp(U, kvc, slopes, batch=batch, seq=seq)
    n_slc = seq // NSA_SLC_LEN
    blk_id = jnp.arange(seq) // NSA_SLC_LEN
    onehot = (blk_id[:, None] == jnp.arange(n_slc)[None, :]).astype(BF16)
    k_aug = jnp.concatenate([kv6[2], jnp.broadcast_to(onehot, (batch, NSA_GROUPS, seq, n_slc))], axis=-1)
    o_s = nsa_sel(U, selbias, k_aug, kv6[3].transpose(0, 1, 3, 2), slopes, batch=batch, seq=seq)
    pad = ((0, 0), (0, 0), (NSA_WINDOW, 0), (0, 0))
    o_w = nsa_win(U, jnp.pad(kv6[4], pad), jnp.pad(kv6[5], pad), slopes, batch=batch, seq=seq)
    y_c = nsa_combine(o_c, o_s, o_w, Us)

    merged = merge(y_a, y_b, y_c, U, p_merge.astype(BF16))
    return matmul(merged, w_out.astype(BF16), BF16, 1024 if N % 1024 == 0 else N, 1024)


def kernel(x, c, norm_mix, norm_ffn, ada_w, ada_b, w_in, gla_a2, gla_a_b, gla_norm, rwkv_mu, rwkv_w0, rwkv_w2, rwkv_a0, rwkv_a2, rwkv_g2, rwkv_k_k, rwkv_k_a, rwkv_r_k, rwkv_ln_w, rwkv_ln_b, nsa_pos_k, nsa_w1_k, nsa_w2_k, nsa_pos_v, nsa_w1_v, nsa_w2_v, p_merge, w_out, ffn_w1, ffn_w3, ffn_w2, moe_router, moe_w1, moe_w3, moe_w2, final_norm):
    B, T, D = x.shape
    depth = w_in.shape[0]
    N = B * T
    xs = x.reshape(N, D)
    ada = ada_all(c, ada_w, ada_b)
    slopes = jnp.exp2(-8.0 * jnp.arange(1, NSA_HEADS + 1, dtype=F32) / NSA_HEADS)
    y = None
    g_prev = None
    for l in range(depth):
        sh1, sc1, g1, sh2, sc2, g2 = jnp.split(ada[l], 6, axis=-1)
        if y is None:
            (h,) = resmod(xs, None, None, norm_mix[l], sh1, sc1, seq=T)
        else:
            xs, h = resmod(xs, y, g_prev, norm_mix[l], sh1, sc1, seq=T)
        y = _mixer(h, B, T, w_in[l], gla_a2[l], gla_a_b[l], gla_norm[l], rwkv_mu[l], rwkv_w0[l], rwkv_w2[l],
                   rwkv_a0[l], rwkv_a2[l], rwkv_g2[l], rwkv_k_k[l], rwkv_k_a[l], rwkv_r_k[l],
                   rwkv_ln_w[l], rwkv_ln_b[l], nsa_pos_k[l], nsa_w1_k[l], nsa_w2_k[l],
                   nsa_pos_v[l], nsa_w1_v[l], nsa_w2_v[l], p_merge[l], w_out[l], slopes)
        if l % 2 == 0:
            xs, h = resmod(xs, y, g1, norm_ffn[l], sh2, sc2, seq=T)
            y = ffn(h, ffn_w1[l // 2].astype(BF16), ffn_w3[l // 2].astype(BF16), ffn_w2[l // 2].astype(BF16))
        else:
            rt = jnp.zeros((D, 128), F32).at[:, :N_EXPERTS].set(moe_router[l // 2])
            xs, h, logits = resmod(xs, y, g1, norm_ffn[l], sh2, sc2, seq=T, router=rt)
            combine, selm = route(logits)
            y = moe(h, combine, selm, moe_w1[l // 2].astype(BF16), moe_w3[l // 2].astype(BF16),
                    moe_w2[l // 2].astype(BF16))
        g_prev = g2
    (out,) = resmod(xs, y, g_prev, final_norm, None, None, seq=T, final=True)
    return out.reshape(B, T, D)
```

```python
import functools

import numpy as np
import jax
import jax.numpy as jnp
from jax import lax
from jax.experimental import pallas as pl
from jax.experimental.pallas import tpu as pltpu

F32 = jnp.float32
BF16 = jnp.bfloat16
HI = lax.Precision.HIGHEST

V7X_VMEM_LIMIT_BYTES = 56 * 1024 * 1024

D_MODEL = 2048
NORM_EPS = 1e-6
NEG_BIG = -1e30

GLA_HEADS = 4
GLA_DK = 128
GLA_DV = 256
GLA_RANK = 16
GLA_NORMALIZER = 16.0
CHUNK = 64

RWKV_HEADS = 16
RWKV_N = 64
RWKV_DIM = RWKV_HEADS * RWKV_N
RWKV_W_RANK = 64
RWKV_A_RANK = 64
RWKV_G_RANK = 160
RWKV_LN_EPS = 64e-5
RWKV_HB = 4

NSA_HEADS = 16
NSA_GROUPS = 4
NSA_HPG = 4
NSA_DH = 64
NSA_CMP_LEN = 32
NSA_CMP_STRIDE = 16
NSA_CMP_HIDDEN = 128
NSA_SLC_LEN = 64
NSA_N_SEL = 16
NSA_WINDOW = 512
NSA_QB = 256
SEL_MASK_BIAS = 131072.0

BRANCH_DIM = 1024
D_FF = 5632
N_EXPERTS = 8

U_GLA_Q, U_GLA_K, U_GLA_V, U_GLA_G = 0, 512, 1024, 2048
U_RWKV_R, U_RWKV_K, U_RWKV_V = 3072, 4096, 5120
U_NSA_Q = 6144
U_NSA_KV = 7168
U_GATE = 8704
U_COLS = U_GATE + 3 * D_MODEL
S_GLA_A = 0
S_RWKV_WA = 128
S_RWKV_G = 256
S_NSA_GATE = 512
S_COLS = 640


def _cparams(sem, vmem=V7X_VMEM_LIMIT_BYTES):
    return pltpu.CompilerParams(dimension_semantics=sem, vmem_limit_bytes=vmem)


def _sigmoid(x):
    return 1.0 / (1.0 + jnp.exp(-x))


def _softplus(x):
    return jnp.maximum(x, 0.0) + jnp.log(1.0 + jnp.exp(-jnp.abs(x)))


def _dot(a, b):
    return jnp.dot(a, b, preferred_element_type=F32)


def _dot_nt(a, b):
    return lax.dot_general(a, b, (((1,), (1,)), ((), ())), preferred_element_type=F32)


def _dot_tn(a, b):
    return lax.dot_general(a, b, (((0,), (0,)), ((), ())), preferred_element_type=F32)


def _dot_hi(a, b):
    return jnp.dot(a, b, preferred_element_type=F32, precision=HI)


def _dot_split(x, w):
    hi = x.astype(BF16)
    lo = (x - hi.astype(F32)).astype(BF16)
    return _dot(hi, w) + _dot(lo, w)


def _ada_kernel(c_ref, w_ref, b_ref, o_ref):
    c = c_ref[...]
    o_ref[0] = _dot_hi(c * _sigmoid(c), w_ref[0]) + b_ref[0]


def ada_all(c, ada_w, ada_b):
    L, D, N6 = ada_w.shape
    B = c.shape[0]
    tn = 1024
    return pl.pallas_call(
        _ada_kernel,
        out_shape=jax.ShapeDtypeStruct((L, B, N6), F32),
        grid=(L, N6 // tn),
        in_specs=[
            pl.BlockSpec((B, D), lambda l, j: (0, 0)),
            pl.BlockSpec((1, D, tn), lambda l, j: (l, 0, j)),
            pl.BlockSpec((1, 1, tn), lambda l, j: (l, 0, j)),
        ],
        out_specs=pl.BlockSpec((1, B, tn), lambda l, j: (l, 0, j)),
        compiler_params=_cparams(("parallel", "parallel")),
        name="ada",
    )(c, ada_w, ada_b.reshape(L, 1, N6))


def _resmod_kernel(*refs, has_res, final, router):
    it = iter(refs)
    x_ref = next(it)
    y_ref = next(it) if has_res else None
    g_ref = next(it) if has_res else None
    ng_ref = next(it)
    sh_ref = None if final else next(it)
    sc_ref = None if final else next(it)
    rt_ref = next(it) if router else None
    xo_ref = next(it) if (has_res and not final) else None
    h_ref = next(it)
    lg_ref = next(it) if router else None

    x = x_ref[...]
    if has_res:
        x = x + g_ref[0] * y_ref[...].astype(F32)
        if xo_ref is not None:
            xo_ref[...] = x
    ms = jnp.mean(x * x, axis=-1, keepdims=True)
    h = x * lax.rsqrt(ms + NORM_EPS) * ng_ref[...]
    if not final:
        h = h * (1.0 + sc_ref[0]) + sh_ref[0]
    h_ref[...] = h.astype(h_ref.dtype)
    if router:
        lg_ref[...] = _dot_hi(h, rt_ref[...])


def resmod(x, y, g, norm_g, shift, scale, *, seq, router=None, final=False):
    N, D = x.shape
    tm = 512
    spb = seq // tm
    has_res = y is not None
    row = lambda i: (i, 0)
    per_b = lambda i: (i // spb, 0, 0)
    ins, specs = [x], [pl.BlockSpec((tm, D), row)]
    if has_res:
        ins += [y, g.reshape(-1, 1, D)]
        specs += [pl.BlockSpec((tm, D), row), pl.BlockSpec((1, 1, D), per_b)]
    ins.append(norm_g.reshape(1, D))
    specs.append(pl.BlockSpec((1, D), lambda i: (0, 0)))
    if not final:
        ins += [shift.reshape(-1, 1, D), scale.reshape(-1, 1, D)]
        specs += [pl.BlockSpec((1, 1, D), per_b), pl.BlockSpec((1, 1, D), per_b)]
    if router is not None:
        ins.append(router)
        specs.append(pl.BlockSpec(router.shape, lambda i: (0, 0)))
    outs, ospecs = [], []
    if has_res and not final:
        outs.append(jax.ShapeDtypeStruct((N, D), F32))
        ospecs.append(pl.BlockSpec((tm, D), row))
    outs.append(jax.ShapeDtypeStruct((N, D), F32 if final else BF16))
    ospecs.append(pl.BlockSpec((tm, D), row))
    if router is not None:
        outs.append(jax.ShapeDtypeStruct((N, 128), F32))
        ospecs.append(pl.BlockSpec((tm, 128), row))
    res = pl.pallas_call(
        functools.partial(_resmod_kernel, has_res=has_res, final=final, router=router is not None),
        out_shape=tuple(outs),
        grid=(N // tm,),
        in_specs=specs,
        out_specs=tuple(ospecs),
        compiler_params=_cparams(("parallel",)),
        name="resmod",
    )(*ins)
    return res


def _mm_kernel(x_ref, w_ref, o_ref):
    o_ref[...] = _dot(x_ref[...], w_ref[...]).astype(o_ref.dtype)


def matmul(x, w, out_dtype, tm, tn):
    M, K = x.shape
    N = w.shape[1]
    return pl.pallas_call(
        _mm_kernel,
        out_shape=jax.ShapeDtypeStruct((M, N), out_dtype),
        grid=(M // tm, N // tn),
        in_specs=[pl.BlockSpec((tm, K), lambda i, j: (i, 0)), pl.BlockSpec((K, tn), lambda i, j: (0, j))],
        out_specs=pl.BlockSpec((tm, tn), lambda i, j: (i, j)),
        compiler_params=_cparams(("parallel", "parallel")),
        name="matmul",
    )(x, w)


GLA_TC = 512


def _gla_kernel(q_ref, k_ref, v_ref, g_ref, alr_ref, a2_ref, ab_ref, ng_ref, o_ref, st_ref, la_ref):
    @pl.when(pl.program_id(1) == 0)
    def _():
        st_ref[...] = jnp.zeros_like(st_ref)

    C, DK, DV = CHUNK, GLA_DK, GLA_DV
    la = _dot_hi(alr_ref[...], a2_ref[...]) + ab_ref[...]
    la_ref[...] = -_softplus(-la) / GLA_NORMALIZER
    ri = lax.broadcasted_iota(jnp.int32, (C, C), 0)
    ci = lax.broadcasted_iota(jnp.int32, (C, C), 1)
    causal = ri >= ci
    tril = causal.astype(F32)
    H = range(GLA_HEADS)

    def body(c, carry):
        sl = pl.ds(pl.multiple_of(c * C, C), C)
        bc_all = _dot_hi(tril, la_ref[sl, :])
        bcs = [bc_all[:, h * DK:(h + 1) * DK] for h in H]
        bls = [bc[C - 1:C, :] for bc in bcs]
        ks = [k_ref[sl, h * DK:(h + 1) * DK].astype(F32) for h in H]
        vs = [v_ref[sl, h * DV:(h + 1) * DV] for h in H]
        qds = [(q_ref[sl, h * DK:(h + 1) * DK].astype(F32) * (DK ** -0.5) * jnp.exp(bc)).astype(BF16)
               for h, bc in zip(H, bcs)]
        kds = [(k * jnp.exp(-bc)).astype(BF16) for k, bc in zip(ks, bcs)]
        kls = [(k * jnp.exp(bl - bc)).astype(BF16) for k, bl, bc in zip(ks, bls, bcs)]
        atts = [jnp.where(causal, _dot_nt(qd, kd), 0.0).astype(BF16) for qd, kd in zip(qds, kds)]
        sts = [st_ref[h] for h in H]
        os_ = [_dot(att, v) + _dot_nt(qd, st.astype(BF16)) for att, v, qd, st in zip(atts, vs, qds, sts)]
        for h in H:
            st_ref[h] = sts[h] * jnp.exp(bls[h]) + _dot_tn(vs[h], kls[h])
        for h in H:
            o = os_[h]
            o = o * lax.rsqrt(jnp.mean(o * o, axis=-1, keepdims=True) + NORM_EPS) * ng_ref[...]
            gg = g_ref[sl, h * DV:(h + 1) * DV].astype(F32)
            o_ref[sl, h * DV:(h + 1) * DV] = (o * (gg * _sigmoid(gg))).astype(o_ref.dtype)
        return carry

    lax.fori_loop(0, GLA_TC // C, body, 0)


def gla(U, Us, a2, a_b, norm_g, *, batch, seq):
    N = U.shape[0]
    nt = seq // GLA_TC
    HK, HV = GLA_HEADS * GLA_DK, GLA_HEADS * GLA_DV
    a2p = jnp.zeros((128, HK), F32).at[:GLA_RANK].set(a2)
    row = lambda b, i: b * nt + i
    return pl.pallas_call(
        _gla_kernel,
        out_shape=jax.ShapeDtypeStruct((N, HV), BF16),
        grid=(batch, nt),
        in_specs=[
            pl.BlockSpec((GLA_TC, HK), lambda b, i: (row(b, i), U_GLA_Q // HK)),
            pl.BlockSpec((GLA_TC, HK), lambda b, i: (row(b, i), U_GLA_K // HK)),
            pl.BlockSpec((GLA_TC, HV), lambda b, i: (row(b, i), U_GLA_V // HV)),
            pl.BlockSpec((GLA_TC, HV), lambda b, i: (row(b, i), U_GLA_G // HV)),
            pl.BlockSpec((GLA_TC, 128), lambda b, i: (row(b, i), S_GLA_A // 128)),
            pl.BlockSpec((128, HK), lambda b, i: (0, 0)),
            pl.BlockSpec((1, HK), lambda b, i: (0, 0)),
            pl.BlockSpec((1, GLA_DV), lambda b, i: (0, 0)),
        ],
        out_specs=pl.BlockSpec((GLA_TC, HV), lambda b, i: (row(b, i), 0)),
        scratch_shapes=[pltpu.VMEM((GLA_HEADS, GLA_DV, GLA_DK), F32), pltpu.VMEM((GLA_TC, HK), F32)],
        compiler_params=_cparams(("parallel", "arbitrary")),
        name="gla",
    )(U, U, U, U, Us, a2p, a_b.reshape(1, -1), norm_g.reshape(1, -1))


RWKV_TM = 256


def _seg_ones(n=256, seg=RWKV_N):
    i = np.arange(n)
    return jnp.asarray((i[:, None] // seg == i[None, :] // seg).astype(np.float32), BF16)


def _seg_sum(x, bd):
    outs = [_dot_split(x[:, s:s + 256], bd) for s in range(0, x.shape[1], 256)]
    return outs[0] if len(outs) == 1 else jnp.concatenate(outs, axis=1)


def _shift_lerp(u_ref, p_ref, mu, first):
    u = u_ref[...].astype(F32)
    prev_last = jnp.where(first, 0.0, p_ref[7:8, :].astype(F32))
    rolled = pltpu.roll(u, 1, 0)
    is_row0 = lax.broadcasted_iota(jnp.int32, u.shape, 0) == 0
    sh = jnp.where(is_row0, prev_last, rolled)
    return u + (sh - u) * mu


def _rwkv_prep_kernel(r_ref, k_ref, v_ref, wa_ref, gl_ref, rp_ref, kp_ref, vp_ref, wap_ref, glp_ref,
                      mur_ref, muk_ref, muv_ref, muwa_ref, mug_ref, w0_ref, w2_ref, a0_ref, a2_ref, g2_ref,
                      kk_ref, ka_ref, bd_ref,
                      ro_ref, ko_ref, vo_ref, kap_ref, bo_ref, lw_ref, go_ref, *, spb):
    first = (pl.program_id(0) % spb) == 0
    r = _shift_lerp(r_ref, rp_ref, mur_ref[...], first)
    k = _shift_lerp(k_ref, kp_ref, muk_ref[...], first)
    v = _shift_lerp(v_ref, vp_ref, muv_ref[...], first)
    wa = _shift_lerp(wa_ref, wap_ref, muwa_ref[...], first)
    gl = _shift_lerp(gl_ref, glp_ref, mug_ref[...], first)
    w_log = -_softplus(-(w0_ref[...] + _dot_hi(jnp.tanh(wa), w2_ref[...]))) - 0.5
    lw_ref[...] = -jnp.exp(w_log)
    a = _sigmoid(a0_ref[...] + _dot_hi(wa, a2_ref[...]))
    go_ref[...] = _dot(_sigmoid(gl).astype(BF16), g2_ref[...]).astype(go_ref.dtype)
    kk = k * kk_ref[...]
    nrm = jnp.sqrt(_seg_sum(kk * kk, bd_ref[...]))
    kk = kk / jnp.maximum(nrm, 1e-12)
    ro_ref[...] = r.astype(ro_ref.dtype)
    ko_ref[...] = (k * (1.0 + (a - 1.0) * ka_ref[...])).astype(ko_ref.dtype)
    vo_ref[...] = v.astype(vo_ref.dtype)
    kap_ref[...] = kk.astype(kap_ref.dtype)
    bo_ref[...] = (kk * a).astype(bo_ref.dtype)


def rwkv_prep(U, Us, mu, w0, w2, a0, a2, g2, k_k, k_a, *, seq):
    N = U.shape[0]
    tm = RWKV_TM
    spb = seq // tm
    R = RWKV_DIM
    cur = lambda cb: (lambda i: (i, cb))
    prv = lambda cb: (lambda i: (jnp.maximum(i * (tm // 8) - 1, 0), cb))
    mu_r, mu_k, mu_v = mu[:R], mu[R:2 * R], mu[2 * R:3 * R]
    mu_wa = mu[3 * R:3 * R + 128]
    mu_g = jnp.zeros((256,), F32).at[:RWKV_G_RANK].set(mu[3 * R + 128:])
    w2p = jnp.zeros((128, R), F32).at[:RWKV_W_RANK].set(w2)
    a2p = jnp.zeros((128, R), F32).at[RWKV_W_RANK:].set(a2)
    g2p = jnp.zeros((256, R), BF16).at[:RWKV_G_RANK].set(g2.astype(BF16))
    vec = lambda a: a.reshape(1, -1)
    full = lambda a: pl.BlockSpec(a.shape, lambda i: (0, 0))
    params = [vec(mu_r), vec(mu_k), vec(mu_v), vec(mu_wa), vec(mu_g), vec(w0), w2p, vec(a0), a2p, g2p,
              vec(k_k), vec(k_a), _seg_ones()]
    in_specs = [
        pl.BlockSpec((tm, R), cur(U_RWKV_R // R)), pl.BlockSpec((tm, R), cur(U_RWKV_K // R)),
        pl.BlockSpec((tm, R), cur(U_RWKV_V // R)),
        pl.BlockSpec((tm, 128), cur(S_RWKV_WA // 128)), pl.BlockSpec((tm, 256), cur(S_RWKV_G // 256)),
        pl.BlockSpec((8, R), prv(U_RWKV_R // R)), pl.BlockSpec((8, R), prv(U_RWKV_K // R)),
        pl.BlockSpec((8, R), prv(U_RWKV_V // R)),
        pl.BlockSpec((8, 128), prv(S_RWKV_WA // 128)), pl.BlockSpec((8, 256), prv(S_RWKV_G // 256)),
    ] + [full(p) for p in params]
    out = lambda dt: jax.ShapeDtypeStruct((N, R), dt)
    ospec = pl.BlockSpec((tm, R), lambda i: (i, 0))
    return pl.pallas_call(
        functools.partial(_rwkv_prep_kernel, spb=spb),
        out_shape=(out(BF16), out(BF16), out(BF16), out(BF16), out(BF16), out(F32), out(BF16)),
        grid=(N // tm,),
        in_specs=in_specs,
        out_specs=(ospec,) * 7,
        compiler_params=_cparams(("parallel",)),
        name="rwkv_prep",
    )(U, U, U, Us, Us, U, U, U, Us, Us, *params)


RWKV_TC = 512
RWKV_W = RWKV_HB * RWKV_N


def _rwkv_masks():
    W, C = RWKV_W, CHUNK
    i = np.arange(W)
    same = i[:, None] // C == i[None, :] // C
    m_bd = same.astype(np.float32)
    low_s = (same & (i[:, None] % C > i[None, :] % C)).astype(np.float32)
    low_i = (same & (i[:, None] % C >= i[None, :] % C)).astype(np.float32)
    tril = np.tril(np.ones((C, C), np.float32))
    return (jnp.asarray(m_bd), jnp.asarray(low_s), jnp.asarray(low_i), jnp.asarray(tril),
            jnp.asarray(np.eye(W, dtype=np.float32)))


def _tile4(x):
    return jnp.concatenate([x] * RWKV_HB, axis=0)


def _rwkv_chunk_kernel(r_ref, k_ref, v_ref, kap_ref, b_ref, lw_ref, g_ref,
                       mbd_ref, lows_ref, lowi_ref, tril_ref, eye_ref, bd_ref,
                       rk_ref, lnw_ref, lnb_ref, o_ref, st_ref):
    @pl.when(pl.program_id(1) == 0)
    def _():
        st_ref[...] = jnp.zeros_like(st_ref)

    C, W = CHUNK, RWKV_W
    n_groups = r_ref.shape[1] // W
    m_bd = mbd_ref[...]
    low_s = lows_ref[...]
    low_i = lowi_ref[...]
    tril = tril_ref[...]
    eye = eye_ref[...]
    bd = bd_ref[...]

    def expand(x):
        return (_tile4(x) * m_bd).astype(BF16)

    def body(c, carry):
        sl = pl.ds(pl.multiple_of(c * C, C), C)
        lw_all = lw_ref[sl, :]
        cum_all = _dot_hi(tril, lw_all)
        G = range(n_groups)
        lns = [slice(gi * W, (gi + 1) * W) for gi in G]
        cums = [cum_all[:, ln] for ln in lns]
        clasts = [cum[C - 1:C, :] for cum in cums]
        rs = [r_ref[sl, ln].astype(F32) for ln in lns]
        ks = [k_ref[sl, ln].astype(F32) for ln in lns]
        vs = [v_ref[sl, ln].astype(F32) for ln in lns]
        bs = [b_ref[sl, ln].astype(F32) for ln in lns]
        p_invs = [jnp.exp(-cum) for cum in cums]
        xes = [jnp.concatenate([expand(kap_ref[sl, ln].astype(F32) * jnp.exp(cum - lw_all[:, ln])),
                                expand(r * jnp.exp(cum))], axis=0)
               for ln, cum, r in zip(lns, cums, rs)]
        hes = [jnp.concatenate([expand(b * pi), expand(k * pi)], axis=0) for b, k, pi in zip(bs, ks, p_invs)]
        scs = [_dot_nt(xe, he) for xe, he in zip(xes, hes)]
        ams = [sc[:W, :W] * low_s for sc in scs]
        tinvs = [eye - a_m for a_m in ams]
        ambs = [a_m.astype(BF16) for a_m in ams]
        pws = [_dot(ab, ab) for ab in ambs]
        n_lev = int(np.log2(C))
        for lev in range(1, n_lev):
            pwbs = [pw.astype(BF16) for pw in pws]
            if lev < n_lev - 1:
                outs = [_dot(jnp.concatenate([pwb, tinv.astype(BF16)], axis=0), pwb)
                        for pwb, tinv in zip(pwbs, tinvs)]
                pws = [o[:W] for o in outs]
                tinvs = [tinv + o[W:] for tinv, o in zip(tinvs, outs)]
            else:
                tinvs = [tinv + _dot(tinv.astype(BF16), pwb) for tinv, pwb in zip(tinvs, pwbs)]
        ves = [expand(v) for v in vs]
        bqv = [_dot(jnp.concatenate([(sc[:W, W:] * low_s).astype(BF16), (sc[W:, W:] * low_i).astype(BF16)], axis=0),
                    ve) for sc, ve in zip(scs, ves)]
        bmv = [o[:W] for o in bqv]
        qkv = [o[W:] for o in bqv]
        sts = [st_ref[gi] for gi in G]
        xss = [_dot_nt(xe, st.astype(BF16)) for xe, st in zip(xes, sts)]
        us = [_dot(tinv.astype(BF16), (xs[:W] + bv).astype(BF16)) for tinv, xs, bv in zip(tinvs, xss, bmv)]
        kbs = [jnp.concatenate([expand(k * jnp.exp(cl - cum)), expand(b * jnp.exp(cl - cum))], axis=0)
               for k, b, cl, cum in zip(ks, bs, clasts, cums)]
        for gi in G:
            vu = jnp.concatenate([ves[gi], (-us[gi]).astype(BF16)], axis=0)
            st_ref[gi] = sts[gi] * jnp.exp(clasts[gi]) + _dot_tn(vu, kbs[gi]) * m_bd
        for gi in G:
            ln = lns[gi]
            qb = (scs[gi][W:, :W] * low_i).astype(BF16)
            y_e = xss[gi][W:] + qkv[gi] - _dot(qb, us[gi].astype(BF16))
            y = y_e[0:C] + y_e[C:2 * C] + y_e[2 * C:3 * C] + y_e[3 * C:4 * C]
            mu = _seg_sum(y, bd) * (1.0 / RWKV_N)
            yc = y - mu
            var = _seg_sum(yc * yc, bd) * (1.0 / RWKV_N)
            yn = yc * lax.rsqrt(var + RWKV_LN_EPS) * lnw_ref[:, ln] + lnb_ref[:, ln]
            bonus = _seg_sum(rs[gi] * ks[gi] * rk_ref[:, ln], bd) * vs[gi]
            o_ref[sl, ln] = ((yn + bonus) * g_ref[sl, ln].astype(F32)).astype(o_ref.dtype)
        return carry

    lax.fori_loop(0, RWKV_TC // C, body, 0)


def rwkv_chunk(r, k, v, kap, b, lw, gate, r_k, ln_w, ln_b, *, batch, seq):
    N, R = r.shape
    nt = seq // RWKV_TC
    W = RWKV_W
    blk = pl.BlockSpec((RWKV_TC, R), lambda bb, i: (bb * nt + i, 0))
    masks = _rwkv_masks() + (_seg_ones(),)
    full = lambda a: pl.BlockSpec(a.shape, lambda bb, i: (0, 0))
    pvec = pl.BlockSpec((1, R), lambda bb, i: (0, 0))
    return pl.pallas_call(
        _rwkv_chunk_kernel,
        out_shape=jax.ShapeDtypeStruct((N, R), BF16),
        grid=(batch, nt),
        in_specs=[blk] * 7 + [full(m) for m in masks] + [pvec] * 3,
        out_specs=blk,
        scratch_shapes=[pltpu.VMEM((R // W, W, W), F32)],
        compiler_params=_cparams(("parallel", "arbitrary")),
        name="rwkv_chunk",
    )(r, k, v, kap, b, lw, gate, *masks, r_k.reshape(1, R), ln_w.reshape(1, R), ln_b.reshape(1, R))


def _gelu_tanh(x):
    return 0.5 * x * (1.0 + jnp.tanh(np.sqrt(2.0 / np.pi) * (x + 0.044715 * (x * x * x))))


def _nsa_compress_kernel(x_ref, pos_ref, w1_ref, w2_ref, o_ref):
    x = x_ref[0, 0]
    w1 = w1_ref[0]
    half = w1.shape[0] // 2
    nrow = x.shape[0]
    ha = _dot(x, w1[:half])
    hb = _dot(x, w1[half:])
    h = ha + pltpu.roll(hb, nrow - 1, 0)
    pb = _dot(pos_ref[0], w1)
    h = _gelu_tanh(h + pb[0:1, :])
    o_ref[0, 0, 0] = _dot(h.astype(BF16), w2_ref[0]).astype(o_ref.dtype)


def nsa_compress(kv_cmp, pos, w1, w2):
    two, B, G, T, dh = kv_cmp.shape
    nr = T // NSA_CMP_STRIDE
    x = kv_cmp.reshape(two * B, G, nr, NSA_CMP_STRIDE * dh)
    posf = jnp.broadcast_to(pos.reshape(two, 1, NSA_CMP_LEN * dh), (two, 8, NSA_CMP_LEN * dh)).astype(BF16)
    return pl.pallas_call(
        _nsa_compress_kernel,
        out_shape=jax.ShapeDtypeStruct((two, B, G, nr, dh), BF16),
        grid=(two, B, G),
        in_specs=[
            pl.BlockSpec((1, 1, nr, NSA_CMP_STRIDE * dh), lambda s, b, g: (s * B + b, g, 0, 0)),
            pl.BlockSpec((1, 8, NSA_CMP_LEN * dh), lambda s, b, g: (s, 0, 0)),
            pl.BlockSpec((1, NSA_CMP_LEN * dh, NSA_CMP_HIDDEN), lambda s, b, g: (s, 0, 0)),
            pl.BlockSpec((1, NSA_CMP_HIDDEN, dh), lambda s, b, g: (s, 0, 0)),
        ],
        out_specs=pl.BlockSpec((1, 1, 1, nr, dh), lambda s, b, g: (s, b, g, 0, 0)),
        compiler_params=_cparams(("parallel", "parallel", "parallel")),
        name="nsa_compress",
    )(x, posf, w1.astype(BF16), w2.astype(BF16))


def _nsa_cmp_kernel(slope_ref, q_ref, kc_ref, vc_ref, ov_ref, oc_ref, sb_ref, *, n_slc, n_sel):
    g = pl.program_id(1)
    qi = pl.program_id(2)
    QB = NSA_QB
    ncp = kc_ref.shape[3]
    kc = kc_ref[0, 0, 0]
    vc = vc_ref[0, 0, 0]
    t = qi * QB + lax.broadcasted_iota(jnp.int32, (QB, ncp), 0)
    n = lax.broadcasted_iota(jnp.int32, (QB, ncp), 1)
    valid = (n * NSA_CMP_STRIDE + (NSA_CMP_LEN - 1)) <= t
    adist = jnp.abs(t.astype(F32) - (n.astype(F32) * NSA_CMP_STRIDE + (NSA_CMP_LEN - 1) / 2.0))
    psum = jnp.zeros((QB, ncp), F32)
    for h in range(NSA_HPG):
        qh = q_ref[:, h * NSA_DH:(h + 1) * NSA_DH] * (NSA_DH ** -0.5)
        s = _dot_nt(qh, kc) - slope_ref[g * NSA_HPG + h] * adist
        s = jnp.where(valid, s, NEG_BIG)
        m = jnp.max(s, axis=-1, keepdims=True)
        e = jnp.where(valid, jnp.exp(s - m), 0.0)
        l = jnp.sum(e, axis=-1, keepdims=True)
        p = e / jnp.maximum(l, 1e-30)
        oc_ref[:, h * NSA_DH:(h + 1) * NSA_DH] = _dot(p.astype(BF16), vc).astype(oc_ref.dtype)
        psum = psum + p
    imp = lax.dot_general(ov_ref[...], psum, (((1,), (1,)), ((), ())), preferred_element_type=F32, precision=HI)
    j = lax.broadcasted_iota(jnp.int32, (n_slc, QB), 0)
    tt = qi * QB + lax.broadcasted_iota(jnp.int32, (n_slc, QB), 1)
    cur = jnp.right_shift(tt, 6)
    forced = (j == 0) | (j == cur) | (j == cur - 1)
    cand = (j >= 1) & (j <= cur - 2)
    cur_row = cur[0:1, :]
    rank = jnp.zeros((n_slc, QB), jnp.int32)
    for jp in range(1, n_slc):
        row = imp[jp:jp + 1, :]
        ahead = (row > imp) | ((row == imp) & (jp < j))
        rank = rank + jnp.where(ahead & (jp <= cur_row - 2), 1, 0)
    sel = forced | (cand & (rank < n_sel - 3))
    sb = jnp.where(sel, 0.0, -SEL_MASK_BIAS)
    sb_ref[0, 0] = sb.T.astype(sb_ref.dtype)


def nsa_cmp(U, kvc, slopes, *, batch, seq):
    N = U.shape[0]
    QB = NSA_QB
    nq = seq // QB
    ncp = kvc.shape[3]
    n_slc = seq // NSA_SLC_LEN
    n_sel = min(NSA_N_SEL, n_slc)
    nn = np.arange(ncp)
    jj = np.arange(n_slc)
    ov = ((nn[None, :] * NSA_CMP_STRIDE + NSA_CMP_LEN - 1 >= jj[:, None] * NSA_SLC_LEN)
          & (nn[None, :] * NSA_CMP_STRIDE <= jj[:, None] * NSA_SLC_LEN + NSA_SLC_LEN - 1)
          & (nn[None, :] < ncp - 1)).astype(np.float32)
    G = NSA_GROUPS
    W = NSA_HPG * NSA_DH
    grid_spec = pltpu.PrefetchScalarGridSpec(
        num_scalar_prefetch=1,
        grid=(batch, G, nq),
        in_specs=[
            pl.BlockSpec((QB, W), lambda b, g, i, s: (b * nq + i, U_NSA_Q // W + g)),
            pl.BlockSpec((1, 1, 1, ncp, NSA_DH), lambda b, g, i, s: (0, b, g, 0, 0)),
            pl.BlockSpec((1, 1, 1, ncp, NSA_DH), lambda b, g, i, s: (1, b, g, 0, 0)),
            pl.BlockSpec((n_slc, ncp), lambda b, g, i, s: (0, 0)),
        ],
        out_specs=(
            pl.BlockSpec((QB, W), lambda b, g, i, s: (b * nq + i, g)),
            pl.BlockSpec((1, 1, QB, n_slc), lambda b, g, i, s: (b, g, i, 0)),
        ),
    )
    return pl.pallas_call(
        functools.partial(_nsa_cmp_kernel, n_slc=n_slc, n_sel=n_sel),
        out_shape=(jax.ShapeDtypeStruct((N, NSA_HEADS * NSA_DH), BF16),
                   jax.ShapeDtypeStruct((batch, G, seq, n_slc), BF16)),
        grid_spec=grid_spec,
        compiler_params=_cparams(("parallel", "parallel", "parallel")),
        name="nsa_cmp",
    )(slopes, U, kvc, kvc, jnp.asarray(ov))


def _nsa_win_kernel(slope_ref, q_ref, k0_ref, k1_ref, k2_ref, v0_ref, v1_ref, v2_ref, o_ref):
    g = pl.program_id(1)
    qi = pl.program_id(2)
    QB = NSA_QB
    KW = NSA_WINDOW + QB
    k = jnp.concatenate([k0_ref[0, 0], k1_ref[0, 0], k2_ref[0, 0]], axis=0)
    v = jnp.concatenate([v0_ref[0, 0], v1_ref[0, 0], v2_ref[0, 0]], axis=0)
    r = lax.broadcasted_iota(jnp.int32, (QB, KW), 0)
    c = lax.broadcasted_iota(jnp.int32, (QB, KW), 1)
    dist = r - c + NSA_WINDOW
    pos = qi * QB - NSA_WINDOW + c
    valid = (dist >= 0) & (dist < NSA_WINDOW) & (pos >= 0)
    adist = jnp.abs(dist).astype(F32)
    for h in range(NSA_HPG):
        qh = q_ref[:, h * NSA_DH:(h + 1) * NSA_DH] * (NSA_DH ** -0.5)
        s = _dot_nt(qh, k) - slope_ref[g * NSA_HPG + h] * adist
        s = jnp.where(valid, s, NEG_BIG)
        m = jnp.max(s, axis=-1, keepdims=True)
        e = jnp.exp(s - m)
        l = jnp.sum(e, axis=-1, keepdims=True)
        o = _dot(e.astype(BF16), v) / l
        o_ref[:, h * NSA_DH:(h + 1) * NSA_DH] = o.astype(o_ref.dtype)


def nsa_win(U, kw_pad, vw_pad, slopes, *, batch, seq):
    N = U.shape[0]
    QB = NSA_QB
    nq = seq // QB
    G = NSA_GROUPS
    W = NSA_HPG * NSA_DH
    kv = lambda off: pl.BlockSpec((1, 1, QB, NSA_DH), lambda b, g, i, s: (b, g, i + off, 0))
    grid_spec = pltpu.PrefetchScalarGridSpec(
        num_scalar_prefetch=1,
        grid=(batch, G, nq),
        in_specs=[pl.BlockSpec((QB, W), lambda b, g, i, s: (b * nq + i, U_NSA_Q // W + g)),
                  kv(0), kv(1), kv(2), kv(0), kv(1), kv(2)],
        out_specs=pl.BlockSpec((QB, W), lambda b, g, i, s: (b * nq + i, g)),
    )
    return pl.pallas_call(
        _nsa_win_kernel,
        out_shape=jax.ShapeDtypeStruct((N, NSA_HEADS * NSA_DH), BF16),
        grid_spec=grid_spec,
        compiler_params=_cparams(("parallel", "parallel", "parallel")),
        name="nsa_win",
    )(slopes, U, kw_pad, kw_pad, kw_pad, vw_pad, vw_pad, vw_pad)


SEL_QB = 256
SEL_KV = 512


def _sel_pairs(seq):
    qs, ks = [], []
    for qi in range(seq // SEL_QB):
        for kj in range((qi * SEL_QB) // SEL_KV + 1):
            qs.append(qi)
            ks.append(kj)
    return np.asarray(qs, np.int32), np.asarray(ks, np.int32)


def _nsa_sel_kernel(qi_ref, kj_ref, slope_ref, q_ref, sb_ref, ka_ref, vt_ref, bias_ref, o_ref,
                    qa_ref, m_ref, acc_ref):
    g = pl.program_id(1)
    p = pl.program_id(2)
    qi = qi_ref[p]
    kj = kj_ref[p]
    QB, KV = SEL_QB, SEL_KV
    nblk = sb_ref.shape[3]

    @pl.when(kj == 0)
    def _():
        for h in range(NSA_HPG):
            qa_ref[h * QB:(h + 1) * QB, 0:NSA_DH] = q_ref[:, h * NSA_DH:(h + 1) * NSA_DH] * (NSA_DH ** -0.5)
            qa_ref[h * QB:(h + 1) * QB, NSA_DH:NSA_DH + nblk] = sb_ref[0, 0]
        m_ref[...] = jnp.full_like(m_ref, NEG_BIG)
        acc_ref[...] = jnp.zeros_like(acc_ref)

    off = qi * QB - kj * KV
    offf = off.astype(F32)

    def step(masked):
        s = _dot_nt(ka_ref[0, 0], qa_ref[...])
        vt = vt_ref[0, 0]
        if masked:
            r = lax.broadcasted_iota(jnp.int32, (KV, QB), 0)
            c = lax.broadcasted_iota(jnp.int32, (KV, QB), 1)
            causal = (c - r + off) >= 0
        for h in range(NSA_HPG):
            cols = slice(h * QB, (h + 1) * QB)
            sh = s[:, cols] + bias_ref[h]
            if masked:
                sh = jnp.where(causal, sh, NEG_BIG)
            delta = -slope_ref[g * NSA_HPG + h] * offf
            m_old = m_ref[h:h + 1, :]
            m_new = jnp.maximum(m_old, jnp.max(sh, axis=0, keepdims=True) + delta)
            e = jnp.exp(sh - (m_new - delta))
            alpha = jnp.exp(m_old - m_new)
            acc_ref[:, cols] = alpha * acc_ref[:, cols] + _dot(vt, e.astype(BF16))
            m_ref[h:h + 1, :] = m_new

    last = (qi * QB) // KV

    @pl.when(kj < last)
    def _():
        step(False)

    @pl.when(kj == last)
    def _():
        step(True)
        for h in range(NSA_HPG):
            cols = slice(h * QB, (h + 1) * QB)
            o = acc_ref[0:NSA_DH, cols] / acc_ref[NSA_DH:NSA_DH + 1, cols]
            o_ref[:, h * NSA_DH:(h + 1) * NSA_DH] = o.T.astype(o_ref.dtype)


def nsa_sel(U, selbias, k_aug, vt_slc, slopes, *, batch, seq):
    N = U.shape[0]
    QB, KV = SEL_QB, SEL_KV
    nq = seq // QB
    G = NSA_GROUPS
    W = NSA_HPG * NSA_DH
    n_slc = selbias.shape[3]
    ka_w = k_aug.shape[3]
    qs, ks = _sel_pairs(seq)
    rc = (np.arange(QB)[None, :] - np.arange(KV)[:, None]).astype(np.float32)
    bias = -slopes[:, None, None] * jnp.asarray(rc)[None]
    grid_spec = pltpu.PrefetchScalarGridSpec(
        num_scalar_prefetch=3,
        grid=(batch, G, len(qs)),
        in_specs=[
            pl.BlockSpec((QB, W), lambda b, g, p, qi, kj, s: (b * nq + qi[p], U_NSA_Q // W + g)),
            pl.BlockSpec((1, 1, QB, n_slc), lambda b, g, p, qi, kj, s: (b, g, qi[p], 0)),
            pl.BlockSpec((1, 1, KV, ka_w), lambda b, g, p, qi, kj, s: (b, g, kj[p], 0)),
            pl.BlockSpec((1, 1, NSA_DH + 8, KV), lambda b, g, p, qi, kj, s: (b, g, 0, kj[p])),
            pl.BlockSpec((NSA_HPG, KV, QB), lambda b, g, p, qi, kj, s: (g, 0, 0)),
        ],
        out_specs=pl.BlockSpec((QB, W), lambda b, g, p, qi, kj, s: (b * nq + qi[p], g)),
        scratch_shapes=[
            pltpu.VMEM((NSA_HPG * QB, ka_w), BF16),
            pltpu.VMEM((NSA_HPG, QB), F32),
            pltpu.VMEM((NSA_DH + 8, NSA_HPG * QB), F32),
        ],
    )
    return pl.pallas_call(
        _nsa_sel_kernel,
        out_shape=jax.ShapeDtypeStruct((N, NSA_HEADS * NSA_DH), BF16),
        grid_spec=grid_spec,
        compiler_params=_cparams(("parallel", "parallel", "arbitrary")),
        name="nsa_sel",
    )(jnp.asarray(qs), jnp.asarray(ks), slopes, U, selbias, k_aug, vt_slc, bias)


def _nsa_combine_kernel(oc_ref, os_ref, ow_ref, gate_ref, e_ref, o_ref):
    ge = _dot_split(_sigmoid(gate_ref[...]), e_ref[...])
    Wd = NSA_HEADS * NSA_DH
    o = (ge[:, :Wd] * oc_ref[...].astype(F32) + ge[:, Wd:2 * Wd] * os_ref[...].astype(F32)
         + ge[:, 2 * Wd:] * ow_ref[...].astype(F32))
    o_ref[...] = o.astype(o_ref.dtype)


def nsa_combine(o_c, o_s, o_w, Us):
    N, Wd = o_c.shape
    tm = 512
    e = np.zeros((128, 3 * Wd), np.float32)
    for h in range(NSA_HEADS):
        for j in range(3):
            e[h * 3 + j, j * Wd + h * NSA_DH:j * Wd + (h + 1) * NSA_DH] = 1.0
    blk = pl.BlockSpec((tm, Wd), lambda i: (i, 0))
    return pl.pallas_call(
        _nsa_combine_kernel,
        out_shape=jax.ShapeDtypeStruct((N, Wd), BF16),
        grid=(N // tm,),
        in_specs=[blk, blk, blk, pl.BlockSpec((tm, 128), lambda i: (i, S_NSA_GATE // 128)),
                  pl.BlockSpec((128, 3 * Wd), lambda i: (0, 0))],
        out_specs=blk,
        compiler_params=_cparams(("parallel",)),
        name="nsa_combine",
    )(o_c, o_s, o_w, Us, jnp.asarray(e, BF16))


def _merge_kernel(ya_ref, yb_ref, yc_ref, ga_ref, gb_ref, gc_ref, p_ref, o_ref):
    m = (_sigmoid(ga_ref[...].astype(F32)) * _dot(ya_ref[...], p_ref[0])
         + _sigmoid(gb_ref[...].astype(F32)) * _dot(yb_ref[...], p_ref[1])
         + _sigmoid(gc_ref[...].astype(F32)) * _dot(yc_ref[...], p_ref[2]))
    o_ref[...] = m.astype(o_ref.dtype)


def merge(y_a, y_b, y_c, U, p_merge):
    N = y_a.shape[0]
    tm, tn = 1024, 512
    yb = pl.BlockSpec((tm, BRANCH_DIM), lambda i, j: (i, 0))
    gate = lambda br: pl.BlockSpec((tm, tn), lambda i, j: (i, (U_GATE + br * D_MODEL) // tn + j))
    return pl.pallas_call(
        _merge_kernel,
        out_shape=jax.ShapeDtypeStruct((N, D_MODEL), BF16),
        grid=(N // tm, D_MODEL // tn),
        in_specs=[yb, yb, yb, gate(0), gate(1), gate(2),
                  pl.BlockSpec((3, BRANCH_DIM, tn), lambda i, j: (0, 0, j))],
        out_specs=pl.BlockSpec((tm, tn), lambda i, j: (i, j)),
        compiler_params=_cparams(("parallel", "parallel")),
        name="merge",
    )(y_a, y_b, y_c, U, U, U, p_merge)


def _ffn_kernel(h_ref, w1_ref, w3_ref, w2_ref, o_ref, acc_ref):
    f = pl.program_id(1)

    @pl.when(f == 0)
    def _():
        acc_ref[...] = jnp.zeros_like(acc_ref)

    h = h_ref[...]
    a = _dot(h, w1_ref[...])
    z = (a * _sigmoid(a) * _dot(h, w3_ref[...])).astype(BF16)
    acc_ref[...] += _dot(z, w2_ref[...])

    @pl.when(f == pl.num_programs(1) - 1)
    def _():
        o_ref[...] = acc_ref[...].astype(o_ref.dtype)


def ffn(h, w1, w3, w2):
    N, D = h.shape
    F = w1.shape[1]
    tm, tf = 1024, 512
    return pl.pallas_call(
        _ffn_kernel,
        out_shape=jax.ShapeDtypeStruct((N, D), BF16),
        grid=(N // tm, F // tf),
        in_specs=[pl.BlockSpec((tm, D), lambda i, f: (i, 0)),
                  pl.BlockSpec((D, tf), lambda i, f: (0, f)),
                  pl.BlockSpec((D, tf), lambda i, f: (0, f)),
                  pl.BlockSpec((tf, D), lambda i, f: (f, 0))],
        out_specs=pl.BlockSpec((tm, D), lambda i, f: (i, 0)),
        scratch_shapes=[pltpu.VMEM((tm, D), F32)],
        compiler_params=_cparams(("parallel", "arbitrary")),
        name="ffn",
    )(h, w1, w3, w2)


def _route_kernel(lg_ref, cb_ref, sel_ref):
    lg = lg_ref[...]
    lane = lax.broadcasted_iota(jnp.int32, lg.shape, 1)
    x = jnp.where(lane < N_EXPERTS, lg, NEG_BIG)
    v1 = jnp.max(x, axis=-1, keepdims=True)
    i1 = jnp.min(jnp.where(x == v1, lane, 1024), axis=-1, keepdims=True)
    x2 = jnp.where(lane == i1, NEG_BIG, x)
    v2 = jnp.max(x2, axis=-1, keepdims=True)
    i2 = jnp.min(jnp.where(x2 == v2, lane, 1024), axis=-1, keepdims=True)
    e2 = jnp.exp(v2 - v1)
    w1 = 1.0 / (1.0 + e2)
    w2 = e2 / (1.0 + e2)
    cb_ref[...] = jnp.where(lane == i1, w1, 0.0) + jnp.where(lane == i2, w2, 0.0)
    sel_ref[...] = jnp.where((lane == i1) | (lane == i2), 1.0, 0.0)


def route(logits):
    N = logits.shape[0]
    tm = 1024
    blk = pl.BlockSpec((tm, 128), lambda i: (i, 0))
    return pl.pallas_call(
        _route_kernel,
        out_shape=(jax.ShapeDtypeStruct((N, 128), F32), jax.ShapeDtypeStruct((N, 128), F32)),
        grid=(N // tm,),
        in_specs=[blk],
        out_specs=(blk, blk),
        compiler_params=_cparams(("parallel",)),
        name="route",
    )(logits)


MOE_TM = 512
MOE_WB = 512


def _moe_plan(sel):
    N, E = sel.shape
    tm, wb = MOE_TM, MOE_WB
    NT = 2 * N // tm + E
    R = NT * tm
    P = NT + E * (N // wb)
    i32 = jnp.int32
    cs = jnp.cumsum(sel.astype(i32), axis=0)
    cnt = cs[-1]
    tiles_e = (cnt + tm - 1) // tm
    tile_end = jnp.cumsum(tiles_e)
    tile_start = tile_end - tiles_e
    total_tiles = tile_end[-1]
    dest = jnp.where(sel, tile_start[None, :] * tm + cs - 1, R)
    ti = jnp.arange(NT, dtype=i32)
    tile_valid = ti < total_tiles
    tile_e = jnp.minimum(jnp.searchsorted(tile_end, ti, side="right").astype(i32), E - 1)
    k_lo = (ti - tile_start[tile_e]) * tm
    k_hi = k_lo + jnp.clip(cnt[tile_e] - k_lo, 1, tm) - 1
    find = jax.vmap(lambda col, q: jnp.searchsorted(col, q, side="left"), in_axes=(1, None), out_axes=1)
    pick = lambda m: jnp.take_along_axis(m, tile_e[:, None], axis=1)[:, 0].astype(i32)
    blo = jnp.where(tile_valid, pick(find(cs, k_lo + 1)) // wb, 0)
    bhi = jnp.where(tile_valid, pick(find(cs, k_hi + 1)) // wb, -1)
    npair = bhi - blo + 1
    pend = jnp.cumsum(npair)
    pstart = pend - npair
    total_p = pend[-1]
    pi = jnp.arange(P, dtype=i32)
    pvalid = pi < total_p
    ptile = jnp.minimum(jnp.searchsorted(pend, pi, side="right").astype(i32), NT - 1)
    pblk = blo[ptile] + pi - pstart[ptile]
    pfirst = pvalid & (pi == pstart[ptile])
    ptile = jnp.where(pvalid, ptile, ptile[total_p - 1])
    pblk = jnp.where(pvalid, pblk, pblk[total_p - 1])
    pexp = tile_e[ptile]
    order = jnp.argsort(jnp.where(pvalid, pblk * NT + ptile, jnp.iinfo(jnp.int32).max))
    s_valid = pvalid
    s_tile = jnp.where(s_valid, ptile[order], ptile[order][total_p - 1])
    s_blk = jnp.where(s_valid, pblk[order], pblk[order][total_p - 1])
    s_exp = tile_e[s_tile]
    s_first = s_valid & ((pi == 0) | (s_blk != jnp.roll(s_blk, 1)))
    e_src = jnp.where(tile_valid, ti, 0)
    e_exp = jnp.where(tile_valid, tile_e, tile_e[jnp.maximum(total_tiles - 1, 0)])
    b2i = lambda x: x.astype(i32)
    return dict(dest=dest, g=(ptile, pblk, pexp, b2i(pvalid), b2i(pfirst)),
                e=(e_exp, b2i(tile_valid), e_src), s=(s_tile, s_blk, s_exp, b2i(s_valid), b2i(s_first)), NT=NT, P=P)


def _moe_group_kernel(pt_ref, pb_ref, pe_ref, pv_ref, pf_ref, dest_ref, h_ref, o_ref):
    p = pl.program_id(0)

    @pl.when(pv_ref[p] == 1)
    def _():
        rel = dest_ref[0] - pt_ref[p] * MOE_TM
        row = lax.broadcasted_iota(jnp.int32, (MOE_TM, MOE_WB), 0)
        onehot = jnp.where(rel == row, 1.0, 0.0).astype(BF16)
        rows = _dot(onehot, h_ref[...]).astype(o_ref.dtype)

        @pl.when(pf_ref[p] == 1)
        def _():
            o_ref[...] = rows

        @pl.when(pf_ref[p] == 0)
        def _():
            o_ref[...] += rows


def _moe_expert_kernel(te_ref, tv_ref, ts_ref, h_ref, w1_ref, w3_ref, w2_ref, o_ref, acc_ref):
    i = pl.program_id(0)
    f = pl.program_id(1)
    nf = pl.num_programs(1)

    @pl.when(tv_ref[i] == 1)
    def _():
        @pl.when(f == 0)
        def _():
            acc_ref[...] = jnp.zeros_like(acc_ref)

        h = h_ref[...]
        a = _dot(h, w1_ref[0])
        z = (a * _sigmoid(a) * _dot(h, w3_ref[0])).astype(BF16)
        acc_ref[...] += _dot(z, w2_ref[0])

        @pl.when(f == nf - 1)
        def _():
            o_ref[...] = acc_ref[...].astype(o_ref.dtype)

    @pl.when((tv_ref[i] == 0) & (f == nf - 1))
    def _():
        o_ref[...] = jnp.zeros_like(o_ref)


def _moe_ungroup_kernel(st_ref, sb_ref, se_ref, sv_ref, sf_ref, dest_ref, w_ref, y_ref, o_ref):
    p = pl.program_id(0)

    @pl.when(sv_ref[p] == 1)
    def _():
        rel = dest_ref[0] - st_ref[p] * MOE_TM
        col = lax.broadcasted_iota(jnp.int32, (MOE_WB, MOE_TM), 1)
        onehot_t = jnp.where(rel == col, 1.0, 0.0).astype(BF16)
        part = _dot(onehot_t, y_ref[...]) * w_ref[0]

        @pl.when(sf_ref[p] == 1)
        def _():
            o_ref[...] = part

        @pl.when(sf_ref[p] == 0)
        def _():
            o_ref[...] += part


def moe(h, combine, selm, w1, w3, w2):
    N, D = h.shape
    E, _, F = w1.shape
    tm, wb, tf = MOE_TM, MOE_WB, 512
    plan = _moe_plan(selm[:, :E] > 0.5)
    NT, P = plan["NT"], plan["P"]
    R = NT * tm
    dest_t = plan["dest"].T
    h_sorted = pl.pallas_call(
        _moe_group_kernel,
        out_shape=jax.ShapeDtypeStruct((R, D), BF16),
        grid_spec=pltpu.PrefetchScalarGridSpec(
            num_scalar_prefetch=5,
            grid=(P,),
            in_specs=[pl.BlockSpec((1, 1, wb), lambda p, pt, pb, pe, pv, pf: (pe[p], 0, pb[p])),
                      pl.BlockSpec((wb, D), lambda p, pt, pb, pe, pv, pf: (pb[p], 0))],
            out_specs=pl.BlockSpec((tm, D), lambda p, pt, pb, pe, pv, pf: (pt[p], 0)),
        ),
        compiler_params=_cparams(("arbitrary",)),
        name="moe_group",
    )(*plan["g"], dest_t.reshape(E, 1, N), h)
    y_sorted = pl.pallas_call(
        _moe_expert_kernel,
        out_shape=jax.ShapeDtypeStruct((R, D), BF16),
        grid_spec=pltpu.PrefetchScalarGridSpec(
            num_scalar_prefetch=3,
            grid=(NT, F // tf),
            in_specs=[pl.BlockSpec((tm, D), lambda i, f, te, tv, ts: (ts[i], 0)),
                      pl.BlockSpec((1, D, tf), lambda i, f, te, tv, ts: (te[i], 0, f * tv[i])),
                      pl.BlockSpec((1, D, tf), lambda i, f, te, tv, ts: (te[i], 0, f * tv[i])),
                      pl.BlockSpec((1, tf, D), lambda i, f, te, tv, ts: (te[i], f * tv[i], 0))],
            out_specs=pl.BlockSpec((tm, D), lambda i, f, te, tv, ts: (i, 0)),
            scratch_shapes=[pltpu.VMEM((tm, D), F32)],
        ),
        compiler_params=_cparams(("arbitrary", "arbitrary")),
        name="moe_expert",
    )(*plan["e"], h_sorted, w1, w3, w2)
    return pl.pallas_call(
        _moe_ungroup_kernel,
        out_shape=jax.ShapeDtypeStruct((N, D), F32),
        grid_spec=pltpu.PrefetchScalarGridSpec(
            num_scalar_prefetch=5,
            grid=(P,),
            in_specs=[pl.BlockSpec((1, wb, 1), lambda p, st, sb, se, sv, sf: (se[p], sb[p], 0)),
                      pl.BlockSpec((1, wb, 1), lambda p, st, sb, se, sv, sf: (se[p], sb[p], 0)),
                      pl.BlockSpec((tm, D), lambda p, st, sb, se, sv, sf: (st[p], 0))],
            out_specs=pl.BlockSpec((wb, D), lambda p, st, sb, se, sv, sf: (sb[p], 0)),
        ),
        compiler_params=_cparams(("arbitrary",)),
        name="moe_ungroup",
    )(*plan["s"], dest_t.reshape(E, N, 1), combine[:, :E].T.reshape(E, N, 1), y_sorted)


def _project_weights(w_in):
    gla_w = 2 * GLA_HEADS * GLA_DK + 2 * GLA_HEADS * GLA_DV + GLA_RANK
    rw_w = 3 * RWKV_DIM + RWKV_W_RANK + RWKV_A_RANK + RWKV_G_RANK
    kvw = NSA_GROUPS * NSA_DH
    nsa_w = NSA_HEADS * NSA_DH + 6 * kvw + NSA_HEADS * 3
    o_rw = gla_w
    o_nsa = gla_w + rw_w
    o_gate = o_nsa + nsa_w
    big = jnp.concatenate([
        w_in[:, 0:gla_w - GLA_RANK],
        w_in[:, o_rw:o_rw + 3 * RWKV_DIM],
        w_in[:, o_nsa:o_nsa + NSA_HEADS * NSA_DH + 6 * kvw],
        w_in[:, o_gate:],
    ], axis=1).astype(BF16)
    D = w_in.shape[0]
    z = lambda n: jnp.zeros((D, n), w_in.dtype)
    small = jnp.concatenate([
        w_in[:, gla_w - GLA_RANK:gla_w], z(128 - GLA_RANK),
        w_in[:, o_rw + 3 * RWKV_DIM:o_rw + rw_w], z(256 - RWKV_G_RANK),
        w_in[:, o_gate - NSA_HEADS * 3:o_gate], z(128 - NSA_HEADS * 3),
    ], axis=1).astype(BF16)
    return big, small


def _mixer(h, batch, seq, w_in, gla_a2, gla_a_b, gla_norm, rwkv_mu, rwkv_w0, rwkv_w2, rwkv_a0, rwkv_a2, rwkv_g2,
           rwkv_k_k, rwkv_k_a, rwkv_r_k, rwkv_ln_w, rwkv_ln_b,
           nsa_pos_k, nsa_w1_k, nsa_w2_k, nsa_pos_v, nsa_w1_v, nsa_w2_v, p_merge, w_out, slopes):
    N = h.shape[0]
    w_big, w_small = _project_weights(w_in)
    tm_u = 2048 if N % 2048 == 0 else N
    U = matmul(h, w_big, BF16, tm_u, 512)
    Us = matmul(h, w_small, F32, tm_u, S_COLS)

    y_a = gla(U, Us, gla_a2, gla_a_b, gla_norm, batch=batch, seq=seq)

    r, k, v, kap, b, lw, gate = rwkv_prep(U, Us, rwkv_mu, rwkv_w0, rwkv_w2, rwkv_a0, rwkv_a2, rwkv_g2,
                                          rwkv_k_k, rwkv_k_a, seq=seq)
    y_b = rwkv_chunk(r, k, v, kap, b, lw, gate, rwkv_r_k.reshape(-1), rwkv_ln_w, rwkv_ln_b, batch=batch, seq=seq)

    kv6 = U[:, U_NSA_KV:U_NSA_KV + 6 * NSA_GROUPS * NSA_DH]
    kv6 = kv6.reshape(batch, seq, 6, NSA_GROUPS, NSA_DH).transpose(2, 0, 3, 1, 4)
    kvc = nsa_compress(kv6[0:2], jnp.stack([nsa_pos_k, nsa_pos_v]), jnp.stack([nsa_w1_k, nsa_w1_v]),
                       jnp.stack([nsa_w2_k, nsa_w2_v]))
    o_c, selbias = nsa_cmp(U, kvc, slopes, batch=batch, seq=seq)
    n_slc = seq // NSA_SLC_LEN
    blk_id = jnp.arange(seq) // NSA_SLC_LEN
    onehot = (blk_id[:, None] == jnp.arange(n_slc)[None, :]).astype(BF16)
    k_aug = jnp.concatenate([kv6[2], jnp.broadcast_to(onehot, (batch, NSA_GROUPS, seq, n_slc))], axis=-1)
    ones_rows = jnp.zeros((batch, NSA_GROUPS, 8, seq), BF16).at[:, :, 0, :].set(1.0)
    vt_aug = jnp.concatenate([kv6[3].transpose(0, 1, 3, 2), ones_rows], axis=2)
    o_s = nsa_sel(U, selbias, k_aug, vt_aug, slopes, batch=batch, seq=seq)
    pad = ((0, 0), (0, 0), (NSA_WINDOW, 0), (0, 0))
    o_w = nsa_win(U, jnp.pad(kv6[4], pad), jnp.pad(kv6[5], pad), slopes, batch=batch, seq=seq)
    y_c = nsa_combine(o_c, o_s, o_w, Us)

    merged = merge(y_a, y_b, y_c, U, p_merge.astype(BF16))
    return matmul(merged, w_out.astype(BF16), BF16, 1024 if N % 1024 == 0 else N, 1024)


def kernel(x, c, norm_mix, norm_ffn, ada_w, ada_b, w_in, gla_a2, gla_a_b, gla_norm, rwkv_mu, rwkv_w0, rwkv_w2, rwkv_a0, rwkv_a2, rwkv_g2, rwkv_k_k, rwkv_k_a, rwkv_r_k, rwkv_ln_w, rwkv_ln_b, nsa_pos_k, nsa_w1_k, nsa_w2_k, nsa_pos_v, nsa_w1_v, nsa_w2_v, p_merge, w_out, ffn_w1, ffn_w3, ffn_w2, moe_router, moe_w1, moe_w3, moe_w2, final_norm):
    B, T, D = x.shape
    depth = w_in.shape[0]
    N = B * T
    xs = x.reshape(N, D)
    ada = ada_all(c, ada_w, ada_b)
    slopes = jnp.exp2(-8.0 * jnp.arange(1, NSA_HEADS + 1, dtype=F32) / NSA_HEADS)
    y = None
    g_prev = None
    for l in range(depth):
        sh1, sc1, g1, sh2, sc2, g2 = jnp.split(ada[l], 6, axis=-1)
        if y is None:
            (h,) = resmod(xs, None, None, norm_mix[l], sh1, sc1, seq=T)
        else:
            xs, h = resmod(xs, y, g_prev, norm_mix[l], sh1, sc1, seq=T)
        y = _mixer(h, B, T, w_in[l], gla_a2[l], gla_a_b[l], gla_norm[l], rwkv_mu[l], rwkv_w0[l], rwkv_w2[l],
                   rwkv_a0[l], rwkv_a2[l], rwkv_g2[l], rwkv_k_k[l], rwkv_k_a[l], rwkv_r_k[l],
                   rwkv_ln_w[l], rwkv_ln_b[l], nsa_pos_k[l], nsa_w1_k[l], nsa_w2_k[l],
                   nsa_pos_v[l], nsa_w1_v[l], nsa_w2_v[l], p_merge[l], w_out[l], slopes)
        if l % 2 == 0:
            xs, h = resmod(xs, y, g1, norm_ffn[l], sh2, sc2, seq=T)
            y = ffn(h, ffn_w1[l // 2].astype(BF16), ffn_w3[l // 2].astype(BF16), ffn_w2[l // 2].astype(BF16))
        else:
            rt = jnp.zeros((D, 128), F32).at[:, :N_EXPERTS].set(moe_router[l // 2])
            xs, h, logits = resmod(xs, y, g1, norm_ffn[l], sh2, sc2, seq=T, router=rt)
            combine, selm = route(logits)
            y = moe(h, combine, selm, moe_w1[l // 2].astype(BF16), moe_w3[l // 2].astype(BF16),
                    moe_w2[l // 2].astype(BF16))
        g_prev = g2
    (out,) = resmod(xs, y, g_prev, final_norm, None, None, seq=T, final=True)
    return out.reshape(B, T, D)
```

```python
import functools

import numpy as np
import jax
import jax.numpy as jnp
from jax import lax
from jax.experimental import pallas as pl
from jax.experimental.pallas import tpu as pltpu

F32 = jnp.float32
BF16 = jnp.bfloat16
HI = lax.Precision.HIGHEST

V7X_VMEM_LIMIT_BYTES = 56 * 1024 * 1024

D_MODEL = 2048
NORM_EPS = 1e-6
NEG_BIG = -1e30

GLA_HEADS = 4
GLA_DK = 128
GLA_DV = 256
GLA_RANK = 16
GLA_NORMALIZER = 16.0
CHUNK = 64

RWKV_HEADS = 16
RWKV_N = 64
RWKV_DIM = RWKV_HEADS * RWKV_N
RWKV_W_RANK = 64
RWKV_A_RANK = 64
RWKV_G_RANK = 160
RWKV_LN_EPS = 64e-5
RWKV_HB = 4

NSA_HEADS = 16
NSA_GROUPS = 4
NSA_HPG = 4
NSA_DH = 64
NSA_CMP_LEN = 32
NSA_CMP_STRIDE = 16
NSA_CMP_HIDDEN = 128
NSA_SLC_LEN = 64
NSA_N_SEL = 16
NSA_WINDOW = 512
NSA_QB = 256
SEL_MASK_BIAS = 131072.0

BRANCH_DIM = 1024
D_FF = 5632
N_EXPERTS = 8

U_GLA_Q, U_GLA_K, U_GLA_V, U_GLA_G = 0, 512, 1024, 2048
U_RWKV_R, U_RWKV_K, U_RWKV_V = 3072, 4096, 5120
U_NSA_Q = 6144
U_NSA_KV = 7168
U_GATE = 8704
U_COLS = U_GATE + 3 * D_MODEL
S_GLA_A = 0
S_RWKV_WA = 128
S_RWKV_G = 256
S_NSA_GATE = 512
S_COLS = 640


def _cparams(sem, vmem=V7X_VMEM_LIMIT_BYTES):
    return pltpu.CompilerParams(dimension_semantics=sem, vmem_limit_bytes=vmem)


def _sigmoid(x):
    return 1.0 / (1.0 + jnp.exp(-x))


def _softplus(x):
    return jnp.maximum(x, 0.0) + jnp.log(1.0 + jnp.exp(-jnp.abs(x)))


def _dot(a, b):
    return jnp.dot(a, b, preferred_element_type=F32)


def _dot_nt(a, b):
    return lax.dot_general(a, b, (((1,), (1,)), ((), ())), preferred_element_type=F32)


def _dot_tn(a, b):
    return lax.dot_general(a, b, (((0,), (0,)), ((), ())), preferred_element_type=F32)


def _dot_hi(a, b):
    return jnp.dot(a, b, preferred_element_type=F32, precision=HI)


def _dot_x3(a, b):
    ah = a.astype(BF16)
    al = (a - ah.astype(F32)).astype(BF16)
    bh = b.astype(BF16)
    bl = (b - bh.astype(F32)).astype(BF16)
    return _dot(ah, bh) + _dot(ah, bl) + _dot(al, bh)


def _dot_sel(w, x):
    wb = w.astype(BF16)
    hi = x.astype(BF16)
    r1 = x - hi.astype(F32)
    mid = r1.astype(BF16)
    lo = (r1 - mid.astype(F32)).astype(BF16)
    return _dot(wb, hi) + _dot(wb, mid) + _dot(wb, lo)


def _dot_split(x, w):
    hi = x.astype(BF16)
    lo = (x - hi.astype(F32)).astype(BF16)
    return _dot(hi, w) + _dot(lo, w)


def _ada_kernel(c_ref, w_ref, b_ref, o_ref):
    c = c_ref[...]
    o_ref[0] = _dot_hi(c * _sigmoid(c), w_ref[0]) + b_ref[0]


def ada_all(c, ada_w, ada_b):
    L, D, N6 = ada_w.shape
    B = c.shape[0]
    tn = 1024
    return pl.pallas_call(
        _ada_kernel,
        out_shape=jax.ShapeDtypeStruct((L, B, N6), F32),
        grid=(L, N6 // tn),
        in_specs=[
            pl.BlockSpec((B, D), lambda l, j: (0, 0)),
            pl.BlockSpec((1, D, tn), lambda l, j: (l, 0, j)),
            pl.BlockSpec((1, 1, tn), lambda l, j: (l, 0, j)),
        ],
        out_specs=pl.BlockSpec((1, B, tn), lambda l, j: (l, 0, j)),
        compiler_params=_cparams(("parallel", "parallel")),
        name="ada",
    )(c, ada_w, ada_b.reshape(L, 1, N6))


def _resmod_kernel(*refs, has_res, final, router):
    it = iter(refs)
    x_ref = next(it)
    y_ref = next(it) if has_res else None
    g_ref = next(it) if has_res else None
    ng_ref = next(it)
    sh_ref = None if final else next(it)
    sc_ref = None if final else next(it)
    rt_ref = next(it) if router else None
    xo_ref = next(it) if (has_res and not final) else None
    h_ref = next(it)
    lg_ref = next(it) if router else None

    x = x_ref[...]
    if has_res:
        x = x + g_ref[0] * y_ref[...].astype(F32)
        if xo_ref is not None:
            xo_ref[...] = x
    ms = jnp.mean(x * x, axis=-1, keepdims=True)
    h = x * lax.rsqrt(ms + NORM_EPS) * ng_ref[...]
    if not final:
        h = h * (1.0 + sc_ref[0]) + sh_ref[0]
    h_ref[...] = h.astype(h_ref.dtype)
    if router:
        lg_ref[...] = _dot_hi(h, rt_ref[...])


def resmod(x, y, g, norm_g, shift, scale, *, seq, router=None, final=False):
    N, D = x.shape
    tm = 512
    spb = seq // tm
    has_res = y is not None
    row = lambda i: (i, 0)
    per_b = lambda i: (i // spb, 0, 0)
    ins, specs = [x], [pl.BlockSpec((tm, D), row)]
    if has_res:
        ins += [y, g.reshape(-1, 1, D)]
        specs += [pl.BlockSpec((tm, D), row), pl.BlockSpec((1, 1, D), per_b)]
    ins.append(norm_g.reshape(1, D))
    specs.append(pl.BlockSpec((1, D), lambda i: (0, 0)))
    if not final:
        ins += [shift.reshape(-1, 1, D), scale.reshape(-1, 1, D)]
        specs += [pl.BlockSpec((1, 1, D), per_b), pl.BlockSpec((1, 1, D), per_b)]
    if router is not None:
        ins.append(router)
        specs.append(pl.BlockSpec(router.shape, lambda i: (0, 0)))
    outs, ospecs = [], []
    if has_res and not final:
        outs.append(jax.ShapeDtypeStruct((N, D), F32))
        ospecs.append(pl.BlockSpec((tm, D), row))
    outs.append(jax.ShapeDtypeStruct((N, D), F32 if final else BF16))
    ospecs.append(pl.BlockSpec((tm, D), row))
    if router is not None:
        outs.append(jax.ShapeDtypeStruct((N, 128), F32))
        ospecs.append(pl.BlockSpec((tm, 128), row))
    res = pl.pallas_call(
        functools.partial(_resmod_kernel, has_res=has_res, final=final, router=router is not None),
        out_shape=tuple(outs),
        grid=(N // tm,),
        in_specs=specs,
        out_specs=tuple(ospecs),
        compiler_params=_cparams(("parallel",)),
        name="resmod",
    )(*ins)
    return res


def _mm_kernel(x_ref, w_ref, o_ref):
    o_ref[...] = _dot(x_ref[...], w_ref[...]).astype(o_ref.dtype)


def matmul(x, w, out_dtype, tm, tn):
    M, K = x.shape
    N = w.shape[1]
    return pl.pallas_call(
        _mm_kernel,
        out_shape=jax.ShapeDtypeStruct((M, N), out_dtype),
        grid=(M // tm, N // tn),
        in_specs=[pl.BlockSpec((tm, K), lambda i, j: (i, 0)), pl.BlockSpec((K, tn), lambda i, j: (0, j))],
        out_specs=pl.BlockSpec((tm, tn), lambda i, j: (i, j)),
        compiler_params=_cparams(("parallel", "parallel")),
        name="matmul",
    )(x, w)


GLA_TC = 512


def _gla_kernel(q_ref, k_ref, v_ref, g_ref, alr_ref, a2_ref, ab_ref, ng_ref, o_ref, st_ref, la_ref):
    @pl.when(pl.program_id(1) == 0)
    def _():
        st_ref[...] = jnp.zeros_like(st_ref)

    C, DK, DV = CHUNK, GLA_DK, GLA_DV
    la = _dot_x3(alr_ref[...], a2_ref[...]) + ab_ref[...]
    la_ref[...] = -_softplus(-la) / GLA_NORMALIZER
    ri = lax.broadcasted_iota(jnp.int32, (C, C), 0)
    ci = lax.broadcasted_iota(jnp.int32, (C, C), 1)
    causal = ri >= ci
    tril = causal.astype(F32)
    H = range(GLA_HEADS)

    def body(c, carry):
        sl = pl.ds(pl.multiple_of(c * C, C), C)
        bc_all = _dot_sel(tril, la_ref[sl, :])
        bcs = [bc_all[:, h * DK:(h + 1) * DK] for h in H]
        bls = [bc[C - 1:C, :] for bc in bcs]
        ks = [k_ref[sl, h * DK:(h + 1) * DK].astype(F32) for h in H]
        vs = [v_ref[sl, h * DV:(h + 1) * DV] for h in H]
        qds = [(q_ref[sl, h * DK:(h + 1) * DK].astype(F32) * (DK ** -0.5) * jnp.exp(bc)).astype(BF16)
               for h, bc in zip(H, bcs)]
        kds = [(k * jnp.exp(-bc)).astype(BF16) for k, bc in zip(ks, bcs)]
        kls = [(k * jnp.exp(bl - bc)).astype(BF16) for k, bl, bc in zip(ks, bls, bcs)]
        atts = [jnp.where(causal, _dot_nt(qd, kd), 0.0).astype(BF16) for qd, kd in zip(qds, kds)]
        sts = [st_ref[h] for h in H]
        os_ = [_dot(att, v) + _dot_nt(qd, st.astype(BF16)) for att, v, qd, st in zip(atts, vs, qds, sts)]
        for h in H:
            st_ref[h] = sts[h] * jnp.exp(bls[h]) + _dot_tn(vs[h], kls[h])
        for h in H:
            o = os_[h]
            o = o * lax.rsqrt(jnp.mean(o * o, axis=-1, keepdims=True) + NORM_EPS) * ng_ref[...]
            gg = g_ref[sl, h * DV:(h + 1) * DV].astype(F32)
            o_ref[sl, h * DV:(h + 1) * DV] = (o * (gg * _sigmoid(gg))).astype(o_ref.dtype)
        return carry

    lax.fori_loop(0, GLA_TC // C, body, 0)


def gla(U, Us, a2, a_b, norm_g, *, batch, seq):
    N = U.shape[0]
    nt = seq // GLA_TC
    HK, HV = GLA_HEADS * GLA_DK, GLA_HEADS * GLA_DV
    a2p = jnp.zeros((128, HK), F32).at[:GLA_RANK].set(a2)
    row = lambda b, i: b * nt + i
    return pl.pallas_call(
        _gla_kernel,
        out_shape=jax.ShapeDtypeStruct((N, HV), BF16),
        grid=(batch, nt),
        in_specs=[
            pl.BlockSpec((GLA_TC, HK), lambda b, i: (row(b, i), U_GLA_Q // HK)),
            pl.BlockSpec((GLA_TC, HK), lambda b, i: (row(b, i), U_GLA_K // HK)),
            pl.BlockSpec((GLA_TC, HV), lambda b, i: (row(b, i), U_GLA_V // HV)),
            pl.BlockSpec((GLA_TC, HV), lambda b, i: (row(b, i), U_GLA_G // HV)),
            pl.BlockSpec((GLA_TC, 128), lambda b, i: (row(b, i), S_GLA_A // 128)),
            pl.BlockSpec((128, HK), lambda b, i: (0, 0)),
            pl.BlockSpec((1, HK), lambda b, i: (0, 0)),
            pl.BlockSpec((1, GLA_DV), lambda b, i: (0, 0)),
        ],
        out_specs=pl.BlockSpec((GLA_TC, HV), lambda b, i: (row(b, i), 0)),
        scratch_shapes=[pltpu.VMEM((GLA_HEADS, GLA_DV, GLA_DK), F32), pltpu.VMEM((GLA_TC, HK), F32)],
        compiler_params=_cparams(("parallel", "arbitrary")),
        name="gla",
    )(U, U, U, U, Us, a2p, a_b.reshape(1, -1), norm_g.reshape(1, -1))


RWKV_TM = 256


def _seg_ones(n=256, seg=RWKV_N):
    i = np.arange(n)
    return jnp.asarray((i[:, None] // seg == i[None, :] // seg).astype(np.float32), BF16)


def _seg_sum(x, bd):
    outs = [_dot_split(x[:, s:s + 256], bd) for s in range(0, x.shape[1], 256)]
    return outs[0] if len(outs) == 1 else jnp.concatenate(outs, axis=1)


def _shift_lerp(u_ref, p_ref, mu, first):
    u = u_ref[...].astype(F32)
    prev_last = jnp.where(first, 0.0, p_ref[7:8, :].astype(F32))
    rolled = pltpu.roll(u, 1, 0)
    is_row0 = lax.broadcasted_iota(jnp.int32, u.shape, 0) == 0
    sh = jnp.where(is_row0, prev_last, rolled)
    return u + (sh - u) * mu


def _rwkv_prep_kernel(r_ref, k_ref, v_ref, wa_ref, gl_ref, rp_ref, kp_ref, vp_ref, wap_ref, glp_ref,
                      mur_ref, muk_ref, muv_ref, muwa_ref, mug_ref, w0_ref, w2_ref, a0_ref, a2_ref, g2_ref,
                      kk_ref, ka_ref, bd_ref,
                      ro_ref, ko_ref, vo_ref, kap_ref, bo_ref, lw_ref, go_ref, *, spb):
    first = (pl.program_id(0) % spb) == 0
    r = _shift_lerp(r_ref, rp_ref, mur_ref[...], first)
    k = _shift_lerp(k_ref, kp_ref, muk_ref[...], first)
    v = _shift_lerp(v_ref, vp_ref, muv_ref[...], first)
    wa = _shift_lerp(wa_ref, wap_ref, muwa_ref[...], first)
    gl = _shift_lerp(gl_ref, glp_ref, mug_ref[...], first)
    w_log = -_softplus(-(w0_ref[...] + _dot_x3(jnp.tanh(wa), w2_ref[...]))) - 0.5
    lw_ref[...] = -jnp.exp(w_log)
    a = _sigmoid(a0_ref[...] + _dot_x3(wa, a2_ref[...]))
    go_ref[...] = _dot(_sigmoid(gl).astype(BF16), g2_ref[...]).astype(go_ref.dtype)
    kk = k * kk_ref[...]
    nrm = jnp.sqrt(_seg_sum(kk * kk, bd_ref[...]))
    kk = kk / jnp.maximum(nrm, 1e-12)
    ro_ref[...] = r.astype(ro_ref.dtype)
    ko_ref[...] = (k * (1.0 + (a - 1.0) * ka_ref[...])).astype(ko_ref.dtype)
    vo_ref[...] = v.astype(vo_ref.dtype)
    kap_ref[...] = kk.astype(kap_ref.dtype)
    bo_ref[...] = (kk * a).astype(bo_ref.dtype)


def rwkv_prep(U, Us, mu, w0, w2, a0, a2, g2, k_k, k_a, *, seq):
    N = U.shape[0]
    tm = RWKV_TM
    spb = seq // tm
    R = RWKV_DIM
    cur = lambda cb: (lambda i: (i, cb))
    prv = lambda cb: (lambda i: (jnp.maximum(i * (tm // 8) - 1, 0), cb))
    mu_r, mu_k, mu_v = mu[:R], mu[R:2 * R], mu[2 * R:3 * R]
    mu_wa = mu[3 * R:3 * R + 128]
    mu_g = jnp.zeros((256,), F32).at[:RWKV_G_RANK].set(mu[3 * R + 128:])
    w2p = jnp.zeros((128, R), F32).at[:RWKV_W_RANK].set(w2)
    a2p = jnp.zeros((128, R), F32).at[RWKV_W_RANK:].set(a2)
    g2p = jnp.zeros((256, R), BF16).at[:RWKV_G_RANK].set(g2.astype(BF16))
    vec = lambda a: a.reshape(1, -1)
    full = lambda a: pl.BlockSpec(a.shape, lambda i: (0, 0))
    params = [vec(mu_r), vec(mu_k), vec(mu_v), vec(mu_wa), vec(mu_g), vec(w0), w2p, vec(a0), a2p, g2p,
              vec(k_k), vec(k_a), _seg_ones()]
    in_specs = [
        pl.BlockSpec((tm, R), cur(U_RWKV_R // R)), pl.BlockSpec((tm, R), cur(U_RWKV_K // R)),
        pl.BlockSpec((tm, R), cur(U_RWKV_V // R)),
        pl.BlockSpec((tm, 128), cur(S_RWKV_WA // 128)), pl.BlockSpec((tm, 256), cur(S_RWKV_G // 256)),
        pl.BlockSpec((8, R), prv(U_RWKV_R // R)), pl.BlockSpec((8, R), prv(U_RWKV_K // R)),
        pl.BlockSpec((8, R), prv(U_RWKV_V // R)),
        pl.BlockSpec((8, 128), prv(S_RWKV_WA // 128)), pl.BlockSpec((8, 256), prv(S_RWKV_G // 256)),
    ] + [full(p) for p in params]
    out = lambda dt: jax.ShapeDtypeStruct((N, R), dt)
    ospec = pl.BlockSpec((tm, R), lambda i: (i, 0))
    return pl.pallas_call(
        functools.partial(_rwkv_prep_kernel, spb=spb),
        out_shape=(out(BF16), out(BF16), out(BF16), out(BF16), out(BF16), out(F32), out(BF16)),
        grid=(N // tm,),
        in_specs=in_specs,
        out_specs=(ospec,) * 7,
        compiler_params=_cparams(("parallel",)),
        name="rwkv_prep",
    )(U, U, U, Us, Us, U, U, U, Us, Us, *params)


RWKV_TC = 512
RWKV_W = RWKV_HB * RWKV_N


def _rwkv_masks():
    W, C = RWKV_W, CHUNK
    i = np.arange(W)
    same = i[:, None] // C == i[None, :] // C
    m_bd = same.astype(np.float32)
    low_s = (same & (i[:, None] % C > i[None, :] % C)).astype(np.float32)
    low_i = (same & (i[:, None] % C >= i[None, :] % C)).astype(np.float32)
    tril = np.tril(np.ones((C, C), np.float32))
    return (jnp.asarray(m_bd), jnp.asarray(low_s), jnp.asarray(low_i), jnp.asarray(tril),
            jnp.asarray(np.eye(W, dtype=np.float32)))


def _tile4(x):
    return jnp.concatenate([x] * RWKV_HB, axis=0)


def _rwkv_chunk_kernel(r_ref, k_ref, v_ref, kap_ref, b_ref, lw_ref, g_ref,
                       mbd_ref, lows_ref, lowi_ref, tril_ref, eye_ref, bd_ref,
                       rk_ref, lnw_ref, lnb_ref, o_ref, st_ref):
    @pl.when(pl.program_id(1) == 0)
    def _():
        st_ref[...] = jnp.zeros_like(st_ref)

    C, W = CHUNK, RWKV_W
    n_batch = r_ref.shape[0]
    n_groups = r_ref.shape[2] // W
    m_bd = mbd_ref[...]
    low_s = lows_ref[...]
    low_i = lowi_ref[...]
    tril = tril_ref[...]
    eye = eye_ref[...]
    bd = bd_ref[...]

    def expand(x):
        return (_tile4(x) * m_bd).astype(BF16)

    def body(c, carry):
        sl = pl.ds(pl.multiple_of(c * C, C), C)
        lw_alls = [lw_ref[bi, sl, :] for bi in range(n_batch)]
        cum_alls = [_dot_sel(tril, lw) for lw in lw_alls]
        chains = [(bi, slice(gi * W, (gi + 1) * W)) for bi in range(n_batch) for gi in range(n_groups)]
        G = range(len(chains))
        lns = [ln for _, ln in chains]
        cums = [cum_alls[bi][:, ln] for bi, ln in chains]
        clasts = [cum[C - 1:C, :] for cum in cums]
        rs = [r_ref[bi, sl, ln].astype(F32) for bi, ln in chains]
        ks = [k_ref[bi, sl, ln].astype(F32) for bi, ln in chains]
        vs = [v_ref[bi, sl, ln].astype(F32) for bi, ln in chains]
        bs = [b_ref[bi, sl, ln].astype(F32) for bi, ln in chains]
        p_invs = [jnp.exp(-cum) for cum in cums]
        xes = [jnp.concatenate([expand(kap_ref[bi, sl, ln].astype(F32) * jnp.exp(cum - lw_alls[bi][:, ln])),
                                expand(r * jnp.exp(cum))], axis=0)
               for (bi, ln), cum, r in zip(chains, cums, rs)]
        hes = [jnp.concatenate([expand(b * pi), expand(k * pi)], axis=0) for b, k, pi in zip(bs, ks, p_invs)]
        scs = [_dot_nt(xe, he) for xe, he in zip(xes, hes)]
        ams = [sc[:W, :W] * low_s for sc in scs]
        tinvs = [eye - a_m for a_m in ams]
        ambs = [a_m.astype(BF16) for a_m in ams]
        pws = [_dot(ab, ab) for ab in ambs]
        n_lev = int(np.log2(C))
        for lev in range(1, n_lev):
            pwbs = [pw.astype(BF16) for pw in pws]
            if lev < n_lev - 1:
                outs = [_dot(jnp.concatenate([pwb, tinv.astype(BF16)], axis=0), pwb)
                        for pwb, tinv in zip(pwbs, tinvs)]
                pws = [o[:W] for o in outs]
                tinvs = [tinv + o[W:] for tinv, o in zip(tinvs, outs)]
            else:
                tinvs = [tinv + _dot(tinv.astype(BF16), pwb) for tinv, pwb in zip(tinvs, pwbs)]
        ves = [expand(v) for v in vs]
        bqv = [_dot(jnp.concatenate([(sc[:W, W:] * low_s).astype(BF16), (sc[W:, W:] * low_i).astype(BF16)], axis=0),
                    ve) for sc, ve in zip(scs, ves)]
        bmv = [o[:W] for o in bqv]
        qkv = [o[W:] for o in bqv]
        sts = [st_ref[gi] for gi in G]
        xss = [_dot_nt(xe, st.astype(BF16)) for xe, st in zip(xes, sts)]
        us = [_dot(tinv.astype(BF16), (xs[:W] + bv).astype(BF16)) for tinv, xs, bv in zip(tinvs, xss, bmv)]
        kbs = [jnp.concatenate([expand(k * jnp.exp(cl - cum)), expand(b * jnp.exp(cl - cum))], axis=0)
               for k, b, cl, cum in zip(ks, bs, clasts, cums)]
        for gi in G:
            vu = jnp.concatenate([ves[gi], (-us[gi]).astype(BF16)], axis=0)
            st_ref[gi] = sts[gi] * jnp.exp(clasts[gi]) + _dot_tn(vu, kbs[gi]) * m_bd
        for gi in G:
            ln = lns[gi]
            qb = (scs[gi][W:, :W] * low_i).astype(BF16)
            y_e = xss[gi][W:] + qkv[gi] - _dot(qb, us[gi].astype(BF16))
            y = y_e[0:C] + y_e[C:2 * C] + y_e[2 * C:3 * C] + y_e[3 * C:4 * C]
            mu = _seg_sum(y, bd) * (1.0 / RWKV_N)
            yc = y - mu
            var = _seg_sum(yc * yc, bd) * (1.0 / RWKV_N)
            yn = yc * lax.rsqrt(var + RWKV_LN_EPS) * lnw_ref[:, ln] + lnb_ref[:, ln]
            bonus = _seg_sum(rs[gi] * ks[gi] * rk_ref[:, ln], bd) * vs[gi]
            bi = chains[gi][0]
            o_ref[bi, sl, ln] = ((yn + bonus) * g_ref[bi, sl, ln].astype(F32)).astype(o_ref.dtype)
        return carry

    lax.fori_loop(0, RWKV_TC // C, body, 0)


RWKV_NB = 1


def rwkv_chunk(r, k, v, kap, b, lw, gate, r_k, ln_w, ln_b, *, batch, seq):
    N, R = r.shape
    nt = seq // RWKV_TC
    W = RWKV_W
    nb = RWKV_NB if batch % RWKV_NB == 0 else 1
    blk = pl.BlockSpec((nb, RWKV_TC, R), lambda bb, i: (bb, i, 0))
    masks = _rwkv_masks() + (_seg_ones(),)
    full = lambda a: pl.BlockSpec(a.shape, lambda bb, i: (0, 0))
    pvec = pl.BlockSpec((1, R), lambda bb, i: (0, 0))
    seqs = [a.reshape(batch, seq, R) for a in (r, k, v, kap, b, lw, gate)]
    out = pl.pallas_call(
        _rwkv_chunk_kernel,
        out_shape=jax.ShapeDtypeStruct((batch, seq, R), BF16),
        grid=(batch // nb, nt),
        in_specs=[blk] * 7 + [full(m) for m in masks] + [pvec] * 3,
        out_specs=blk,
        scratch_shapes=[pltpu.VMEM((nb * (R // W), W, W), F32)],
        compiler_params=_cparams(("parallel", "arbitrary")),
        name="rwkv_chunk",
    )(*seqs, *masks, r_k.reshape(1, R), ln_w.reshape(1, R), ln_b.reshape(1, R))
    return out.reshape(N, R)


def _gelu_tanh(x):
    return 0.5 * x * (1.0 + jnp.tanh(np.sqrt(2.0 / np.pi) * (x + 0.044715 * (x * x * x))))


def _nsa_compress_kernel(x_ref, pos_ref, w1_ref, w2_ref, o_ref):
    x = x_ref[0, 0]
    w1 = w1_ref[0]
    half = w1.shape[0] // 2
    nrow = x.shape[0]
    ha = _dot(x, w1[:half])
    hb = _dot(x, w1[half:])
    h = ha + pltpu.roll(hb, nrow - 1, 0)
    pb = _dot(pos_ref[0], w1)
    h = _gelu_tanh(h + pb[0:1, :])
    o_ref[0, 0, 0] = _dot(h.astype(BF16), w2_ref[0]).astype(o_ref.dtype)


def nsa_compress(kv_cmp, pos, w1, w2):
    two, B, G, T, dh = kv_cmp.shape
    nr = T // NSA_CMP_STRIDE
    x = kv_cmp.reshape(two * B, G, nr, NSA_CMP_STRIDE * dh)
    posf = jnp.broadcast_to(pos.reshape(two, 1, NSA_CMP_LEN * dh), (two, 8, NSA_CMP_LEN * dh)).astype(BF16)
    return pl.pallas_call(
        _nsa_compress_kernel,
        out_shape=jax.ShapeDtypeStruct((two, B, G, nr, dh), BF16),
        grid=(two, B, G),
        in_specs=[
            pl.BlockSpec((1, 1, nr, NSA_CMP_STRIDE * dh), lambda s, b, g: (s * B + b, g, 0, 0)),
            pl.BlockSpec((1, 8, NSA_CMP_LEN * dh), lambda s, b, g: (s, 0, 0)),
            pl.BlockSpec((1, NSA_CMP_LEN * dh, NSA_CMP_HIDDEN), lambda s, b, g: (s, 0, 0)),
            pl.BlockSpec((1, NSA_CMP_HIDDEN, dh), lambda s, b, g: (s, 0, 0)),
        ],
        out_specs=pl.BlockSpec((1, 1, 1, nr, dh), lambda s, b, g: (s, b, g, 0, 0)),
        compiler_params=_cparams(("parallel", "parallel", "parallel")),
        name="nsa_compress",
    )(x, posf, w1.astype(BF16), w2.astype(BF16))


def _nsa_cmp_kernel(slope_ref, q_ref, kc_ref, vc_ref, ov_ref, oc_ref, sb_ref, *, n_slc, n_sel):
    g = pl.program_id(1)
    qi = pl.program_id(2)
    QB = NSA_QB
    ncp = kc_ref.shape[3]
    kc = kc_ref[0, 0, 0]
    vc = vc_ref[0, 0, 0]
    t = qi * QB + lax.broadcasted_iota(jnp.int32, (QB, ncp), 0)
    n = lax.broadcasted_iota(jnp.int32, (QB, ncp), 1)
    valid = (n * NSA_CMP_STRIDE + (NSA_CMP_LEN - 1)) <= t
    adist = jnp.abs(t.astype(F32) - (n.astype(F32) * NSA_CMP_STRIDE + (NSA_CMP_LEN - 1) / 2.0))
    psum = jnp.zeros((QB, ncp), F32)
    for h in range(NSA_HPG):
        qh = q_ref[:, h * NSA_DH:(h + 1) * NSA_DH] * (NSA_DH ** -0.5)
        s = _dot_nt(qh, kc) - slope_ref[g * NSA_HPG + h] * adist
        s = jnp.where(valid, s, NEG_BIG)
        m = jnp.max(s, axis=-1, keepdims=True)
        e = jnp.where(valid, jnp.exp(s - m), 0.0)
        l = jnp.sum(e, axis=-1, keepdims=True)
        p = e / jnp.maximum(l, 1e-30)
        oc_ref[:, h * NSA_DH:(h + 1) * NSA_DH] = _dot(p.astype(BF16), vc).astype(oc_ref.dtype)
        psum = psum + p
    imp = lax.dot_general(ov_ref[...], psum, (((1,), (1,)), ((), ())), preferred_element_type=F32, precision=HI)
    j = lax.broadcasted_iota(jnp.int32, (n_slc, QB), 0)
    tt = qi * QB + lax.broadcasted_iota(jnp.int32, (n_slc, QB), 1)
    cur = jnp.right_shift(tt, 6)
    forced = (j == 0) | (j == cur) | (j == cur - 1)
    cand = (j >= 1) & (j <= cur - 2)
    cur_row = cur[0:1, :]
    rank = jnp.zeros((n_slc, QB), jnp.int32)
    for jp in range(1, n_slc):
        row = imp[jp:jp + 1, :]
        ahead = (row > imp) | ((row == imp) & (jp < j))
        rank = rank + jnp.where(ahead & (jp <= cur_row - 2), 1, 0)
    sel = forced | (cand & (rank < n_sel - 3))
    sb = jnp.where(sel, 0.0, -SEL_MASK_BIAS)
    sb_ref[0, 0] = sb.T.astype(sb_ref.dtype)


def nsa_cmp(U, kvc, slopes, *, batch, seq):
    N = U.shape[0]
    QB = NSA_QB
    nq = seq // QB
    ncp = kvc.shape[3]
    n_slc = seq // NSA_SLC_LEN
    n_sel = min(NSA_N_SEL, n_slc)
    nn = np.arange(ncp)
    jj = np.arange(n_slc)
    ov = ((nn[None, :] * NSA_CMP_STRIDE + NSA_CMP_LEN - 1 >= jj[:, None] * NSA_SLC_LEN)
          & (nn[None, :] * NSA_CMP_STRIDE <= jj[:, None] * NSA_SLC_LEN + NSA_SLC_LEN - 1)
          & (nn[None, :] < ncp - 1)).astype(np.float32)
    G = NSA_GROUPS
    W = NSA_HPG * NSA_DH
    grid_spec = pltpu.PrefetchScalarGridSpec(
        num_scalar_prefetch=1,
        grid=(batch, G, nq),
        in_specs=[
            pl.BlockSpec((QB, W), lambda b, g, i, s: (b * nq + i, U_NSA_Q // W + g)),
            pl.BlockSpec((1, 1, 1, ncp, NSA_DH), lambda b, g, i, s: (0, b, g, 0, 0)),
            pl.BlockSpec((1, 1, 1, ncp, NSA_DH), lambda b, g, i, s: (1, b, g, 0, 0)),
            pl.BlockSpec((n_slc, ncp), lambda b, g, i, s: (0, 0)),
        ],
        out_specs=(
            pl.BlockSpec((QB, W), lambda b, g, i, s: (b * nq + i, g)),
            pl.BlockSpec((1, 1, QB, n_slc), lambda b, g, i, s: (b, g, i, 0)),
        ),
    )
    return pl.pallas_call(
        functools.partial(_nsa_cmp_kernel, n_slc=n_slc, n_sel=n_sel),
        out_shape=(jax.ShapeDtypeStruct((N, NSA_HEADS * NSA_DH), BF16),
                   jax.ShapeDtypeStruct((batch, G, seq, n_slc), BF16)),
        grid_spec=grid_spec,
        compiler_params=_cparams(("parallel", "parallel", "parallel")),
        name="nsa_cmp",
    )(slopes, U, kvc, kvc, jnp.asarray(ov))


def _nsa_win_kernel(q_ref, k0_ref, k1_ref, k2_ref, v0_ref, v1_ref, v2_ref, bias_ref, o_ref, qs_ref):
    qi = pl.program_id(2)
    QB = NSA_QB
    KW = NSA_WINDOW + QB
    for h in range(NSA_HPG):
        qs_ref[h * QB:(h + 1) * QB, :] = q_ref[:, h * NSA_DH:(h + 1) * NSA_DH] * (NSA_DH ** -0.5)
    k = jnp.concatenate([k0_ref[0, 0], k1_ref[0, 0], k2_ref[0, 0]], axis=0)
    vt = jnp.concatenate([v0_ref[0, 0], v1_ref[0, 0], v2_ref[0, 0]], axis=1)
    s = _dot_nt(k, qs_ref[...])

    def finish(before_start):
        for h in range(NSA_HPG):
            cols = slice(h * QB, (h + 1) * QB)
            sh = s[:, cols] + bias_ref[h]
            if before_start:
                r = lax.broadcasted_iota(jnp.int32, (KW, QB), 0)
                sh = jnp.where(r >= NSA_WINDOW - qi * QB, sh, NEG_BIG)
            m = jnp.max(sh, axis=0, keepdims=True)
            e = jnp.exp(sh - m)
            acc = _dot(vt, e.astype(BF16))
            o = acc[0:NSA_DH] / acc[NSA_DH:NSA_DH + 1]
            o_ref[:, h * NSA_DH:(h + 1) * NSA_DH] = o.T.astype(o_ref.dtype)

    @pl.when(qi * QB < NSA_WINDOW)
    def _():
        finish(True)

    @pl.when(qi * QB >= NSA_WINDOW)
    def _():
        finish(False)


def nsa_win(U, kw_pad, vtw_pad, slopes, *, batch, seq):
    N = U.shape[0]
    QB = NSA_QB
    KW = NSA_WINDOW + QB
    nq = seq // QB
    G = NSA_GROUPS
    W = NSA_HPG * NSA_DH
    dist = (np.arange(QB)[None, :] - np.arange(KW)[:, None] + NSA_WINDOW).astype(np.float32)
    inside = jnp.asarray((dist >= 0) & (dist < NSA_WINDOW))
    bias = jnp.where(inside[None], -slopes[:, None, None] * jnp.asarray(dist)[None], NEG_BIG)
    kb = lambda off: pl.BlockSpec((1, 1, QB, NSA_DH), lambda b, g, i: (b, g, i + off, 0))
    vb = lambda off: pl.BlockSpec((1, 1, NSA_DH + 8, QB), lambda b, g, i: (b, g, 0, i + off))
    return pl.pallas_call(
        _nsa_win_kernel,
        out_shape=jax.ShapeDtypeStruct((N, NSA_HEADS * NSA_DH), BF16),
        grid=(batch, G, nq),
        in_specs=[pl.BlockSpec((QB, W), lambda b, g, i: (b * nq + i, U_NSA_Q // W + g)),
                  kb(0), kb(1), kb(2), vb(0), vb(1), vb(2),
                  pl.BlockSpec((NSA_HPG, KW, QB), lambda b, g, i: (g, 0, 0))],
        out_specs=pl.BlockSpec((QB, W), lambda b, g, i: (b * nq + i, g)),
        scratch_shapes=[pltpu.VMEM((NSA_HPG * QB, NSA_DH), BF16)],
        compiler_params=_cparams(("parallel", "parallel", "arbitrary")),
        name="nsa_win",
    )(U, kw_pad, kw_pad, kw_pad, vtw_pad, vtw_pad, vtw_pad, bias)


SEL_QB = 256
SEL_KV = 512


def _sel_pairs(seq):
    qs, ks = [], []
    for qi in range(seq // SEL_QB):
        for kj in range((qi * SEL_QB) // SEL_KV + 1):
            qs.append(qi)
            ks.append(kj)
    return np.asarray(qs, np.int32), np.asarray(ks, np.int32)


def _nsa_sel_kernel(qi_ref, kj_ref, slope_ref, q_ref, sb_ref, ka_ref, vt_ref, bias_ref, o_ref,
                    qa_ref, m_ref, acc_ref):
    g = pl.program_id(1)
    p = pl.program_id(2)
    qi = qi_ref[p]
    kj = kj_ref[p]
    QB, KV = SEL_QB, SEL_KV
    nblk = sb_ref.shape[3]

    @pl.when(kj == 0)
    def _():
        for h in range(NSA_HPG):
            qa_ref[h * QB:(h + 1) * QB, 0:NSA_DH] = q_ref[:, h * NSA_DH:(h + 1) * NSA_DH] * (NSA_DH ** -0.5)
            qa_ref[h * QB:(h + 1) * QB, NSA_DH:NSA_DH + nblk] = sb_ref[0, 0]
        m_ref[...] = jnp.full_like(m_ref, NEG_BIG)
        acc_ref[...] = jnp.zeros_like(acc_ref)

    off = qi * QB - kj * KV
    offf = off.astype(F32)

    def step(masked):
        s = _dot_nt(ka_ref[0, 0], qa_ref[...])
        vt = vt_ref[0, 0]
        if masked:
            r = lax.broadcasted_iota(jnp.int32, (KV, QB), 0)
            c = lax.broadcasted_iota(jnp.int32, (KV, QB), 1)
            causal = (c - r + off) >= 0
        for h in range(NSA_HPG):
            cols = slice(h * QB, (h + 1) * QB)
            sh = s[:, cols] + bias_ref[h]
            if masked:
                sh = jnp.where(causal, sh, NEG_BIG)
            delta = -slope_ref[g * NSA_HPG + h] * offf
            m_old = m_ref[h:h + 1, :]
            m_new = jnp.maximum(m_old, jnp.max(sh, axis=0, keepdims=True) + delta)
            e = jnp.exp(sh - (m_new - delta))
            alpha = jnp.exp(m_old - m_new)
            acc_ref[:, cols] = alpha * acc_ref[:, cols] + _dot(vt, e.astype(BF16))
            m_ref[h:h + 1, :] = m_new

    last = (qi * QB) // KV

    @pl.when(kj < last)
    def _():
        step(False)

    @pl.when(kj == last)
    def _():
        step(True)
        for h in range(NSA_HPG):
            cols = slice(h * QB, (h + 1) * QB)
            o = acc_ref[0:NSA_DH, cols] / acc_ref[NSA_DH:NSA_DH + 1, cols]
            o_ref[:, h * NSA_DH:(h + 1) * NSA_DH] = o.T.astype(o_ref.dtype)


def nsa_sel(U, selbias, k_aug, vt_slc, slopes, *, batch, seq):
    N = U.shape[0]
    QB, KV = SEL_QB, SEL_KV
    nq = seq // QB
    G = NSA_GROUPS
    W = NSA_HPG * NSA_DH
    n_slc = selbias.shape[3]
    ka_w = k_aug.shape[3]
    qs, ks = _sel_pairs(seq)
    rc = (np.arange(QB)[None, :] - np.arange(KV)[:, None]).astype(np.float32)
    bias = -slopes[:, None, None] * jnp.asarray(rc)[None]
    grid_spec = pltpu.PrefetchScalarGridSpec(
        num_scalar_prefetch=3,
        grid=(batch, G, len(qs)),
        in_specs=[
            pl.BlockSpec((QB, W), lambda b, g, p, qi, kj, s: (b * nq + qi[p], U_NSA_Q // W + g)),
            pl.BlockSpec((1, 1, QB, n_slc), lambda b, g, p, qi, kj, s: (b, g, qi[p], 0)),
            pl.BlockSpec((1, 1, KV, ka_w), lambda b, g, p, qi, kj, s: (b, g, kj[p], 0)),
            pl.BlockSpec((1, 1, NSA_DH + 8, KV), lambda b, g, p, qi, kj, s: (b, g, 0, kj[p])),
            pl.BlockSpec((NSA_HPG, KV, QB), lambda b, g, p, qi, kj, s: (g, 0, 0)),
        ],
        out_specs=pl.BlockSpec((QB, W), lambda b, g, p, qi, kj, s: (b * nq + qi[p], g)),
        scratch_shapes=[
            pltpu.VMEM((NSA_HPG * QB, ka_w), BF16),
            pltpu.VMEM((NSA_HPG, QB), F32),
            pltpu.VMEM((NSA_DH + 8, NSA_HPG * QB), F32),
        ],
    )
    return pl.pallas_call(
        _nsa_sel_kernel,
        out_shape=jax.ShapeDtypeStruct((N, NSA_HEADS * NSA_DH), BF16),
        grid_spec=grid_spec,
        compiler_params=_cparams(("parallel", "parallel", "arbitrary")),
        name="nsa_sel",
    )(jnp.asarray(qs), jnp.asarray(ks), slopes, U, selbias, k_aug, vt_slc, bias)


def _nsa_combine_kernel(oc_ref, os_ref, ow_ref, gate_ref, e_ref, o_ref):
    ge = _dot_split(_sigmoid(gate_ref[...]), e_ref[...])
    Wd = NSA_HEADS * NSA_DH
    o = (ge[:, :Wd] * oc_ref[...].astype(F32) + ge[:, Wd:2 * Wd] * os_ref[...].astype(F32)
         + ge[:, 2 * Wd:] * ow_ref[...].astype(F32))
    o_ref[...] = o.astype(o_ref.dtype)


def nsa_combine(o_c, o_s, o_w, Us):
    N, Wd = o_c.shape
    tm = 512
    e = np.zeros((128, 3 * Wd), np.float32)
    for h in range(NSA_HEADS):
        for j in range(3):
            e[h * 3 + j, j * Wd + h * NSA_DH:j * Wd + (h + 1) * NSA_DH] = 1.0
    blk = pl.BlockSpec((tm, Wd), lambda i: (i, 0))
    return pl.pallas_call(
        _nsa_combine_kernel,
        out_shape=jax.ShapeDtypeStruct((N, Wd), BF16),
        grid=(N // tm,),
        in_specs=[blk, blk, blk, pl.BlockSpec((tm, 128), lambda i: (i, S_NSA_GATE // 128)),
                  pl.BlockSpec((128, 3 * Wd), lambda i: (0, 0))],
        out_specs=blk,
        compiler_params=_cparams(("parallel",)),
        name="nsa_combine",
    )(o_c, o_s, o_w, Us, jnp.asarray(e, BF16))


def _merge_kernel(ya_ref, yb_ref, yc_ref, ga_ref, gb_ref, gc_ref, p_ref, o_ref):
    m = (_sigmoid(ga_ref[...].astype(F32)) * _dot(ya_ref[...], p_ref[0])
         + _sigmoid(gb_ref[...].astype(F32)) * _dot(yb_ref[...], p_ref[1])
         + _sigmoid(gc_ref[...].astype(F32)) * _dot(yc_ref[...], p_ref[2]))
    o_ref[...] = m.astype(o_ref.dtype)


def merge(y_a, y_b, y_c, U, p_merge):
    N = y_a.shape[0]
    tm, tn = 1024, 512
    yb = pl.BlockSpec((tm, BRANCH_DIM), lambda i, j: (i, 0))
    gate = lambda br: pl.BlockSpec((tm, tn), lambda i, j: (i, (U_GATE + br * D_MODEL) // tn + j))
    return pl.pallas_call(
        _merge_kernel,
        out_shape=jax.ShapeDtypeStruct((N, D_MODEL), BF16),
        grid=(N // tm, D_MODEL // tn),
        in_specs=[yb, yb, yb, gate(0), gate(1), gate(2),
                  pl.BlockSpec((3, BRANCH_DIM, tn), lambda i, j: (0, 0, j))],
        out_specs=pl.BlockSpec((tm, tn), lambda i, j: (i, j)),
        compiler_params=_cparams(("parallel", "parallel")),
        name="merge",
    )(y_a, y_b, y_c, U, U, U, p_merge)


def _ffn_kernel(h_ref, w1_ref, w3_ref, w2_ref, o_ref, acc_ref):
    f = pl.program_id(1)

    @pl.when(f == 0)
    def _():
        acc_ref[...] = jnp.zeros_like(acc_ref)

    h = h_ref[...]
    a = _dot(h, w1_ref[...])
    z = (a * _sigmoid(a) * _dot(h, w3_ref[...])).astype(BF16)
    acc_ref[...] += _dot(z, w2_ref[...])

    @pl.when(f == pl.num_programs(1) - 1)
    def _():
        o_ref[...] = acc_ref[...].astype(o_ref.dtype)


def ffn(h, w1, w3, w2):
    N, D = h.shape
    F = w1.shape[1]
    tm, tf = 1024, 512
    return pl.pallas_call(
        _ffn_kernel,
        out_shape=jax.ShapeDtypeStruct((N, D), BF16),
        grid=(N // tm, F // tf),
        in_specs=[pl.BlockSpec((tm, D), lambda i, f: (i, 0)),
                  pl.BlockSpec((D, tf), lambda i, f: (0, f)),
                  pl.BlockSpec((D, tf), lambda i, f: (0, f)),
                  pl.BlockSpec((tf, D), lambda i, f: (f, 0))],
        out_specs=pl.BlockSpec((tm, D), lambda i, f: (i, 0)),
        scratch_shapes=[pltpu.VMEM((tm, D), F32)],
        compiler_params=_cparams(("parallel", "arbitrary")),
        name="ffn",
    )(h, w1, w3, w2)


def _route_kernel(lg_ref, cb_ref, sel_ref):
    lg = lg_ref[...]
    lane = lax.broadcasted_iota(jnp.int32, lg.shape, 1)
    x = jnp.where(lane < N_EXPERTS, lg, NEG_BIG)
    v1 = jnp.max(x, axis=-1, keepdims=True)
    i1 = jnp.min(jnp.where(x == v1, lane, 1024), axis=-1, keepdims=True)
    x2 = jnp.where(lane == i1, NEG_BIG, x)
    v2 = jnp.max(x2, axis=-1, keepdims=True)
    i2 = jnp.min(jnp.where(x2 == v2, lane, 1024), axis=-1, keepdims=True)
    e2 = jnp.exp(v2 - v1)
    w1 = 1.0 / (1.0 + e2)
    w2 = e2 / (1.0 + e2)
    cb_ref[...] = jnp.where(lane == i1, w1, 0.0) + jnp.where(lane == i2, w2, 0.0)
    sel_ref[...] = jnp.where((lane == i1) | (lane == i2), 1.0, 0.0)


def route(logits):
    N = logits.shape[0]
    tm = 1024
    blk = pl.BlockSpec((tm, 128), lambda i: (i, 0))
    return pl.pallas_call(
        _route_kernel,
        out_shape=(jax.ShapeDtypeStruct((N, 128), F32), jax.ShapeDtypeStruct((N, 128), F32)),
        grid=(N // tm,),
        in_specs=[blk],
        out_specs=(blk, blk),
        compiler_params=_cparams(("parallel",)),
        name="route",
    )(logits)


MOE_TM = 512
MOE_WB = 512
MOE_GT = 256


def _moe_plan(sel):
    N, E = sel.shape
    tm, wb, gt = MOE_TM, MOE_WB, MOE_GT
    sub = tm // gt
    NT = 2 * N // tm + E
    NG = NT * sub
    R = NT * tm
    P = NG + E * (N // wb)
    i32 = jnp.int32
    cs = jnp.cumsum(sel.astype(i32), axis=0)
    cnt = cs[-1]
    tiles_e = (cnt + tm - 1) // tm
    tile_end = jnp.cumsum(tiles_e)
    tile_start = tile_end - tiles_e
    total_tiles = tile_end[-1]
    dest = jnp.where(sel, tile_start[None, :] * tm + cs - 1, R)
    ti = jnp.arange(NT, dtype=i32)
    tile_valid = ti < total_tiles
    tile_e = jnp.minimum(jnp.searchsorted(tile_end, ti, side="right").astype(i32), E - 1)
    gi = jnp.arange(NG, dtype=i32)
    g_valid = tile_valid[gi // sub]
    g_e = tile_e[gi // sub]
    k_lo = gi * gt - tile_start[g_e] * tm
    nonempty = g_valid & (k_lo < cnt[g_e])
    k_hi = jnp.minimum(k_lo + gt, cnt[g_e]) - 1
    find = jax.vmap(lambda col, q: jnp.searchsorted(col, q, side="left"), in_axes=(1, None), out_axes=1)
    pick = lambda m: jnp.take_along_axis(m, g_e[:, None], axis=1)[:, 0].astype(i32)
    blo = jnp.where(nonempty, pick(find(cs, k_lo + 1)) // wb, 0)
    bhi = jnp.where(nonempty, pick(find(cs, k_hi + 1)) // wb, jnp.where(g_valid, 0, -1))
    npair = bhi - blo + 1
    pend = jnp.cumsum(npair)
    pstart = pend - npair
    total_p = pend[-1]
    pi = jnp.arange(P, dtype=i32)
    pvalid = pi < total_p
    ptile = jnp.minimum(jnp.searchsorted(pend, pi, side="right").astype(i32), NG - 1)
    pblk = blo[ptile] + pi - pstart[ptile]
    pfirst = pvalid & (pi == pstart[ptile])
    ptile = jnp.where(pvalid, ptile, ptile[total_p - 1])
    pblk = jnp.where(pvalid, pblk, pblk[total_p - 1])
    pexp = g_e[ptile]
    order = jnp.argsort(jnp.where(pvalid, pblk * NG + ptile, jnp.iinfo(jnp.int32).max))
    s_valid = pvalid
    s_tile = jnp.where(s_valid, ptile[order], ptile[order][total_p - 1])
    s_blk = jnp.where(s_valid, pblk[order], pblk[order][total_p - 1])
    s_exp = g_e[s_tile]
    s_first = s_valid & ((pi == 0) | (s_blk != jnp.roll(s_blk, 1)))
    e_src = jnp.where(tile_valid, ti, 0)
    e_exp = jnp.where(tile_valid, tile_e, tile_e[jnp.maximum(total_tiles - 1, 0)])
    b2i = lambda x: x.astype(i32)
    return dict(dest=dest, g=(ptile, pblk, pexp, b2i(pvalid), b2i(pfirst)),
                e=(e_exp, b2i(tile_valid), e_src), s=(s_tile, s_blk, s_exp, b2i(s_valid), b2i(s_first)), NT=NT, P=P)


def _moe_group_kernel(pt_ref, pb_ref, pe_ref, pv_ref, pf_ref, dest_ref, h_ref, o_ref):
    p = pl.program_id(0)

    @pl.when(pv_ref[p] == 1)
    def _():
        rel = dest_ref[0] - pt_ref[p] * MOE_GT
        row = lax.broadcasted_iota(jnp.int32, (MOE_GT, MOE_WB), 0)
        onehot = jnp.where(rel == row, 1.0, 0.0).astype(BF16)
        rows = _dot(onehot, h_ref[...]).astype(o_ref.dtype)

        @pl.when(pf_ref[p] == 1)
        def _():
            o_ref[...] = rows

        @pl.when(pf_ref[p] == 0)
        def _():
            o_ref[...] += rows


def _moe_expert_kernel(te_ref, tv_ref, ts_ref, h_ref, w1_ref, w3_ref, w2_ref, o_ref, acc_ref):
    i = pl.program_id(0)
    f = pl.program_id(1)
    nf = pl.num_programs(1)

    @pl.when(tv_ref[i] == 1)
    def _():
        @pl.when(f == 0)
        def _():
            acc_ref[...] = jnp.zeros_like(acc_ref)

        h = h_ref[...]
        a = _dot(h, w1_ref[0])
        z = (a * _sigmoid(a) * _dot(h, w3_ref[0])).astype(BF16)
        acc_ref[...] += _dot(z, w2_ref[0])

        @pl.when(f == nf - 1)
        def _():
            o_ref[...] = acc_ref[...].astype(o_ref.dtype)

    @pl.when((tv_ref[i] == 0) & (f == nf - 1))
    def _():
        o_ref[...] = jnp.zeros_like(o_ref)


def _moe_ungroup_kernel(st_ref, sb_ref, se_ref, sv_ref, sf_ref, dest_ref, w_ref, y_ref, o_ref):
    p = pl.program_id(0)

    @pl.when(sv_ref[p] == 1)
    def _():
        rel = dest_ref[0] - st_ref[p] * MOE_GT
        col = lax.broadcasted_iota(jnp.int32, (MOE_WB, MOE_GT), 1)
        onehot_t = jnp.where(rel == col, 1.0, 0.0).astype(BF16)
        part = _dot(onehot_t, y_ref[...]) * w_ref[0]

        @pl.when(sf_ref[p] == 1)
        def _():
            o_ref[...] = part

        @pl.when(sf_ref[p] == 0)
        def _():
            o_ref[...] += part


def moe(h, combine, selm, w1, w3, w2):
    N, D = h.shape
    E, _, F = w1.shape
    tm, wb, gt, tf = MOE_TM, MOE_WB, MOE_GT, 512
    plan = _moe_plan(selm[:, :E] > 0.5)
    NT, P = plan["NT"], plan["P"]
    R = NT * tm
    dest_t = plan["dest"].T
    h_sorted = pl.pallas_call(
        _moe_group_kernel,
        out_shape=jax.ShapeDtypeStruct((R, D), BF16),
        grid_spec=pltpu.PrefetchScalarGridSpec(
            num_scalar_prefetch=5,
            grid=(P,),
            in_specs=[pl.BlockSpec((1, 1, wb), lambda p, pt, pb, pe, pv, pf: (pe[p], 0, pb[p])),
                      pl.BlockSpec((wb, D), lambda p, pt, pb, pe, pv, pf: (pb[p], 0))],
            out_specs=pl.BlockSpec((gt, D), lambda p, pt, pb, pe, pv, pf: (pt[p], 0)),
        ),
        compiler_params=_cparams(("arbitrary",)),
        name="moe_group",
    )(*plan["g"], dest_t.reshape(E, 1, N), h)
    y_sorted = pl.pallas_call(
        _moe_expert_kernel,
        out_shape=jax.ShapeDtypeStruct((R, D), BF16),
        grid_spec=pltpu.PrefetchScalarGridSpec(
            num_scalar_prefetch=3,
            grid=(NT, F // tf),
            in_specs=[pl.BlockSpec((tm, D), lambda i, f, te, tv, ts: (ts[i], 0)),
                      pl.BlockSpec((1, D, tf), lambda i, f, te, tv, ts: (te[i], 0, f * tv[i])),
                      pl.BlockSpec((1, D, tf), lambda i, f, te, tv, ts: (te[i], 0, f * tv[i])),
                      pl.BlockSpec((1, tf, D), lambda i, f, te, tv, ts: (te[i], f * tv[i], 0))],
            out_specs=pl.BlockSpec((tm, D), lambda i, f, te, tv, ts: (i, 0)),
            scratch_shapes=[pltpu.VMEM((tm, D), F32)],
        ),
        compiler_params=_cparams(("arbitrary", "arbitrary")),
        name="moe_expert",
    )(*plan["e"], h_sorted, w1, w3, w2)
    return pl.pallas_call(
        _moe_ungroup_kernel,
        out_shape=jax.ShapeDtypeStruct((N, D), F32),
        grid_spec=pltpu.PrefetchScalarGridSpec(
            num_scalar_prefetch=5,
            grid=(P,),
            in_specs=[pl.BlockSpec((1, wb, 1), lambda p, st, sb, se, sv, sf: (se[p], sb[p], 0)),
                      pl.BlockSpec((1, wb, 1), lambda p, st, sb, se, sv, sf: (se[p], sb[p], 0)),
                      pl.BlockSpec((gt, D), lambda p, st, sb, se, sv, sf: (st[p], 0))],
            out_specs=pl.BlockSpec((wb, D), lambda p, st, sb, se, sv, sf: (sb[p], 0)),
        ),
        compiler_params=_cparams(("arbitrary",)),
        name="moe_ungroup",
    )(*plan["s"], dest_t.reshape(E, N, 1), combine[:, :E].T.reshape(E, N, 1), y_sorted)


def _project_weights(w_in):
    gla_w = 2 * GLA_HEADS * GLA_DK + 2 * GLA_HEADS * GLA_DV + GLA_RANK
    rw_w = 3 * RWKV_DIM + RWKV_W_RANK + RWKV_A_RANK + RWKV_G_RANK
    kvw = NSA_GROUPS * NSA_DH
    nsa_w = NSA_HEADS * NSA_DH + 6 * kvw + NSA_HEADS * 3
    o_rw = gla_w
    o_nsa = gla_w + rw_w
    o_gate = o_nsa + nsa_w
    big = jnp.concatenate([
        w_in[:, 0:gla_w - GLA_RANK],
        w_in[:, o_rw:o_rw + 3 * RWKV_DIM],
        w_in[:, o_nsa:o_nsa + NSA_HEADS * NSA_DH + 6 * kvw],
        w_in[:, o_gate:],
    ], axis=1).astype(BF16)
    D = w_in.shape[0]
    z = lambda n: jnp.zeros((D, n), w_in.dtype)
    small = jnp.concatenate([
        w_in[:, gla_w - GLA_RANK:gla_w], z(128 - GLA_RANK),
        w_in[:, o_rw + 3 * RWKV_DIM:o_rw + rw_w], z(256 - RWKV_G_RANK),
        w_in[:, o_gate - NSA_HEADS * 3:o_gate], z(128 - NSA_HEADS * 3),
    ], axis=1).astype(BF16)
    return big, small


def _mixer(h, batch, seq, w_in, gla_a2, gla_a_b, gla_norm, rwkv_mu, rwkv_w0, rwkv_w2, rwkv_a0, rwkv_a2, rwkv_g2,
           rwkv_k_k, rwkv_k_a, rwkv_r_k, rwkv_ln_w, rwkv_ln_b,
           nsa_pos_k, nsa_w1_k, nsa_w2_k, nsa_pos_v, nsa_w1_v, nsa_w2_v, p_merge, w_out, slopes):
    N = h.shape[0]
    w_big, w_small = _project_weights(w_in)
    tm_u = 2048 if N % 2048 == 0 else N
    U = matmul(h, w_big, BF16, tm_u, 512)
    Us = matmul(h, w_small, F32, tm_u, S_COLS)

    y_a = gla(U, Us, gla_a2, gla_a_b, gla_norm, batch=batch, seq=seq)

    r, k, v, kap, b, lw, gate = rwkv_prep(U, Us, rwkv_mu, rwkv_w0, rwkv_w2, rwkv_a0, rwkv_a2, rwkv_g2,
                                          rwkv_k_k, rwkv_k_a, seq=seq)
    y_b = rwkv_chunk(r, k, v, kap, b, lw, gate, rwkv_r_k.reshape(-1), rwkv_ln_w, rwkv_ln_b, batch=batch, seq=seq)

    kv6 = U[:, U_NSA_KV:U_NSA_KV + 6 * NSA_GROUPS * NSA_DH]
    kv6 = kv6.reshape(batch, seq, 6, NSA_GROUPS, NSA_DH).transpose(2, 0, 3, 1, 4)
    kvc = nsa_compress(kv6[0:2], jnp.stack([nsa_pos_k, nsa_pos_v]), jnp.stack([nsa_w1_k, nsa_w1_v]),
                       jnp.stack([nsa_w2_k, nsa_w2_v]))
    o_c, selbias = nsa_cmp(U, kvc, slopes, batch=batch, seq=seq)
    n_slc = seq // NSA_SLC_LEN
    blk_id = jnp.arange(seq) // NSA_SLC_LEN
    onehot = (blk_id[:, None] == jnp.arange(n_slc)[None, :]).astype(BF16)
    k_aug = jnp.concatenate([kv6[2], jnp.broadcast_to(onehot, (batch, NSA_GROUPS, seq, n_slc))], axis=-1)
    ones_rows = jnp.zeros((batch, NSA_GROUPS, 8, seq), BF16).at[:, :, 0, :].set(1.0)
    vt_aug = jnp.concatenate([kv6[3].transpose(0, 1, 3, 2), ones_rows], axis=2)
    o_s = nsa_sel(U, selbias, k_aug, vt_aug, slopes, batch=batch, seq=seq)
    kw_pad = jnp.pad(kv6[4], ((0, 0), (0, 0), (NSA_WINDOW, 0), (0, 0)))
    vtw = jnp.concatenate([kv6[5].transpose(0, 1, 3, 2), ones_rows], axis=2)
    vtw_pad = jnp.pad(vtw, ((0, 0), (0, 0), (0, 0), (NSA_WINDOW, 0)))
    o_w = nsa_win(U, kw_pad, vtw_pad, slopes, batch=batch, seq=seq)
    y_c = nsa_combine(o_c, o_s, o_w, Us)

    merged = merge(y_a, y_b, y_c, U, p_merge.astype(BF16))
    return matmul(merged, w_out.astype(BF16), BF16, 1024 if N % 1024 == 0 else N, 1024)


def kernel(x, c, norm_mix, norm_ffn, ada_w, ada_b, w_in, gla_a2, gla_a_b, gla_norm, rwkv_mu, rwkv_w0, rwkv_w2, rwkv_a0, rwkv_a2, rwkv_g2, rwkv_k_k, rwkv_k_a, rwkv_r_k, rwkv_ln_w, rwkv_ln_b, nsa_pos_k, nsa_w1_k, nsa_w2_k, nsa_pos_v, nsa_w1_v, nsa_w2_v, p_merge, w_out, ffn_w1, ffn_w3, ffn_w2, moe_router, moe_w1, moe_w3, moe_w2, final_norm):
    B, T, D = x.shape
    depth = w_in.shape[0]
    N = B * T
    xs = x.reshape(N, D)
    ada = ada_all(c, ada_w, ada_b)
    slopes = jnp.exp2(-8.0 * jnp.arange(1, NSA_HEADS + 1, dtype=F32) / NSA_HEADS)
    y = None
    g_prev = None
    for l in range(depth):
        sh1, sc1, g1, sh2, sc2, g2 = jnp.split(ada[l], 6, axis=-1)
        if y is None:
            (h,) = resmod(xs, None, None, norm_mix[l], sh1, sc1, seq=T)
        else:
            xs, h = resmod(xs, y, g_prev, norm_mix[l], sh1, sc1, seq=T)
        y = _mixer(h, B, T, w_in[l], gla_a2[l], gla_a_b[l], gla_norm[l], rwkv_mu[l], rwkv_w0[l], rwkv_w2[l],
                   rwkv_a0[l], rwkv_a2[l], rwkv_g2[l], rwkv_k_k[l], rwkv_k_a[l], rwkv_r_k[l],
                   rwkv_ln_w[l], rwkv_ln_b[l], nsa_pos_k[l], nsa_w1_k[l], nsa_w2_k[l],
                   nsa_pos_v[l], nsa_w1_v[l], nsa_w2_v[l], p_merge[l], w_out[l], slopes)
        if l % 2 == 0:
            xs, h = resmod(xs, y, g1, norm_ffn[l], sh2, sc2, seq=T)
            y = ffn(h, ffn_w1[l // 2].astype(BF16), ffn_w3[l // 2].astype(BF16), ffn_w2[l // 2].astype(BF16))
        else:
            rt = jnp.zeros((D, 128), F32).at[:, :N_EXPERTS].set(moe_router[l // 2])
            xs, h, logits = resmod(xs, y, g1, norm_ffn[l], sh2, sc2, seq=T, router=rt)
            combine, selm = route(logits)
            y = moe(h, combine, selm, moe_w1[l // 2].astype(BF16), moe_w3[l // 2].astype(BF16),
                    moe_w2[l // 2].astype(BF16))
        g_prev = g2
    (out,) = resmod(xs, y, g_prev, final_norm, None, None, seq=T, final=True)
    return out.reshape(B, T, D)
```

```python
import functools

import numpy as np
import jax
import jax.numpy as jnp
from jax import lax
from jax.experimental import pallas as pl
from jax.experimental.pallas import tpu as pltpu

F32 = jnp.float32
BF16 = jnp.bfloat16
HI = lax.Precision.HIGHEST

V7X_VMEM_LIMIT_BYTES = 56 * 1024 * 1024

D_MODEL = 2048
NORM_EPS = 1e-6
NEG_BIG = -1e30

GLA_HEADS = 4
GLA_DK = 128
GLA_DV = 256
GLA_RANK = 16
GLA_NORMALIZER = 16.0
CHUNK = 64

RWKV_HEADS = 16
RWKV_N = 64
RWKV_DIM = RWKV_HEADS * RWKV_N
RWKV_W_RANK = 64
RWKV_A_RANK = 64
RWKV_G_RANK = 160
RWKV_LN_EPS = 64e-5
RWKV_HB = 4

NSA_HEADS = 16
NSA_GROUPS = 4
NSA_HPG = 4
NSA_DH = 64
NSA_CMP_LEN = 32
NSA_CMP_STRIDE = 16
NSA_CMP_HIDDEN = 128
NSA_SLC_LEN = 64
NSA_N_SEL = 16
NSA_WINDOW = 512
NSA_QB = 256
CMP_QB = 512
SEL_MASK_BIAS = 131072.0

BRANCH_DIM = 1024
D_FF = 5632
N_EXPERTS = 8

U_GLA_Q, U_GLA_K, U_GLA_V, U_GLA_G = 0, 512, 1024, 2048
U_RWKV_R, U_RWKV_K, U_RWKV_V = 3072, 4096, 5120
U_NSA_Q = 6144
U_NSA_KV = 7168
U_GATE = 8704
U_COLS = U_GATE + 3 * D_MODEL
S_GLA_A = 0
S_RWKV_WA = 128
S_RWKV_G = 256
S_NSA_GATE = 512
S_COLS = 640


def _cparams(sem, vmem=V7X_VMEM_LIMIT_BYTES):
    return pltpu.CompilerParams(dimension_semantics=sem, vmem_limit_bytes=vmem)


def _sigmoid(x):
    return 1.0 / (1.0 + jnp.exp(-x))


def _softplus(x):
    return jnp.maximum(x, 0.0) + jnp.log(1.0 + jnp.exp(-jnp.abs(x)))


def _dot(a, b):
    return jnp.dot(a, b, preferred_element_type=F32)


def _dot_nt(a, b):
    return lax.dot_general(a, b, (((1,), (1,)), ((), ())), preferred_element_type=F32)


def _dot_tn(a, b):
    return lax.dot_general(a, b, (((0,), (0,)), ((), ())), preferred_element_type=F32)


def _dot_hi(a, b):
    return jnp.dot(a, b, preferred_element_type=F32, precision=HI)


def _dot_x3(a, b):
    ah = a.astype(BF16)
    al = (a - ah.astype(F32)).astype(BF16)
    bh = b.astype(BF16)
    bl = (b - bh.astype(F32)).astype(BF16)
    return _dot(ah, bh) + _dot(ah, bl) + _dot(al, bh)


def _dot_sel(w, x):
    wb = w.astype(BF16)
    hi = x.astype(BF16)
    r1 = x - hi.astype(F32)
    mid = r1.astype(BF16)
    lo = (r1 - mid.astype(F32)).astype(BF16)
    return _dot(wb, hi) + _dot(wb, mid) + _dot(wb, lo)


def _dot_split(x, w):
    hi = x.astype(BF16)
    lo = (x - hi.astype(F32)).astype(BF16)
    return _dot(hi, w) + _dot(lo, w)


def _ada_kernel(c_ref, w_ref, b_ref, o_ref):
    c = c_ref[...]
    o_ref[0] = _dot_hi(c * _sigmoid(c), w_ref[0]) + b_ref[0]


def ada_all(c, ada_w, ada_b):
    L, D, N6 = ada_w.shape
    B = c.shape[0]
    tn = 1024
    return pl.pallas_call(
        _ada_kernel,
        out_shape=jax.ShapeDtypeStruct((L, B, N6), F32),
        grid=(L, N6 // tn),
        in_specs=[
            pl.BlockSpec((B, D), lambda l, j: (0, 0)),
            pl.BlockSpec((1, D, tn), lambda l, j: (l, 0, j)),
            pl.BlockSpec((1, 1, tn), lambda l, j: (l, 0, j)),
        ],
        out_specs=pl.BlockSpec((1, B, tn), lambda l, j: (l, 0, j)),
        compiler_params=_cparams(("parallel", "parallel")),
        name="ada",
    )(c, ada_w, ada_b.reshape(L, 1, N6))


def _resmod_kernel(*refs, has_res, final, router):
    it = iter(refs)
    x_ref = next(it)
    y_ref = next(it) if has_res else None
    g_ref = next(it) if has_res else None
    ng_ref = next(it)
    sh_ref = None if final else next(it)
    sc_ref = None if final else next(it)
    rt_ref = next(it) if router else None
    xo_ref = next(it) if (has_res and not final) else None
    h_ref = next(it)
    lg_ref = next(it) if router else None

    x = x_ref[...]
    if has_res:
        x = x + g_ref[0] * y_ref[...].astype(F32)
        if xo_ref is not None:
            xo_ref[...] = x
    ms = jnp.mean(x * x, axis=-1, keepdims=True)
    h = x * lax.rsqrt(ms + NORM_EPS) * ng_ref[...]
    if not final:
        h = h * (1.0 + sc_ref[0]) + sh_ref[0]
    h_ref[...] = h.astype(h_ref.dtype)
    if router:
        lg_ref[...] = _dot_hi(h, rt_ref[...])


def resmod(x, y, g, norm_g, shift, scale, *, seq, router=None, final=False):
    N, D = x.shape
    tm = 512
    spb = seq // tm
    has_res = y is not None
    row = lambda i: (i, 0)
    per_b = lambda i: (i // spb, 0, 0)
    ins, specs = [x], [pl.BlockSpec((tm, D), row)]
    if has_res:
        ins += [y, g.reshape(-1, 1, D)]
        specs += [pl.BlockSpec((tm, D), row), pl.BlockSpec((1, 1, D), per_b)]
    ins.append(norm_g.reshape(1, D))
    specs.append(pl.BlockSpec((1, D), lambda i: (0, 0)))
    if not final:
        ins += [shift.reshape(-1, 1, D), scale.reshape(-1, 1, D)]
        specs += [pl.BlockSpec((1, 1, D), per_b), pl.BlockSpec((1, 1, D), per_b)]
    if router is not None:
        ins.append(router)
        specs.append(pl.BlockSpec(router.shape, lambda i: (0, 0)))
    outs, ospecs = [], []
    if has_res and not final:
        outs.append(jax.ShapeDtypeStruct((N, D), F32))
        ospecs.append(pl.BlockSpec((tm, D), row))
    outs.append(jax.ShapeDtypeStruct((N, D), F32 if final else BF16))
    ospecs.append(pl.BlockSpec((tm, D), row))
    if router is not None:
        outs.append(jax.ShapeDtypeStruct((N, 128), F32))
        ospecs.append(pl.BlockSpec((tm, 128), row))
    res = pl.pallas_call(
        functools.partial(_resmod_kernel, has_res=has_res, final=final, router=router is not None),
        out_shape=tuple(outs),
        grid=(N // tm,),
        in_specs=specs,
        out_specs=tuple(ospecs),
        compiler_params=_cparams(("parallel",)),
        name="resmod",
    )(*ins)
    return res


def _mm_kernel(x_ref, w_ref, o_ref):
    o_ref[...] = _dot(x_ref[...], w_ref[...]).astype(o_ref.dtype)


def matmul(x, w, out_dtype, tm, tn):
    M, K = x.shape
    N = w.shape[1]
    return pl.pallas_call(
        _mm_kernel,
        out_shape=jax.ShapeDtypeStruct((M, N), out_dtype),
        grid=(M // tm, N // tn),
        in_specs=[pl.BlockSpec((tm, K), lambda i, j: (i, 0)), pl.BlockSpec((K, tn), lambda i, j: (0, j))],
        out_specs=pl.BlockSpec((tm, tn), lambda i, j: (i, j)),
        compiler_params=_cparams(("parallel", "parallel")),
        name="matmul",
    )(x, w)


GLA_TC = 512


def _gla_kernel(q_ref, k_ref, v_ref, g_ref, alr_ref, a2_ref, ab_ref, ng_ref, o_ref, st_ref, la_ref):
    @pl.when(pl.program_id(1) == 0)
    def _():
        st_ref[...] = jnp.zeros_like(st_ref)

    C, DK, DV = CHUNK, GLA_DK, GLA_DV
    la = _dot_x3(alr_ref[...], a2_ref[...]) + ab_ref[...]
    la_ref[...] = -_softplus(-la) / GLA_NORMALIZER
    ri = lax.broadcasted_iota(jnp.int32, (C, C), 0)
    ci = lax.broadcasted_iota(jnp.int32, (C, C), 1)
    causal = ri >= ci
    tril = causal.astype(F32)
    H = range(GLA_HEADS)

    def body(c, carry):
        sl = pl.ds(pl.multiple_of(c * C, C), C)
        bc_all = _dot_sel(tril, la_ref[sl, :])
        bcs = [bc_all[:, h * DK:(h + 1) * DK] for h in H]
        bls = [bc[C - 1:C, :] for bc in bcs]
        ks = [k_ref[sl, h * DK:(h + 1) * DK].astype(F32) for h in H]
        vs = [v_ref[sl, h * DV:(h + 1) * DV] for h in H]
        qds = [(q_ref[sl, h * DK:(h + 1) * DK].astype(F32) * (DK ** -0.5) * jnp.exp(bc)).astype(BF16)
               for h, bc in zip(H, bcs)]
        kds = [(k * jnp.exp(-bc)).astype(BF16) for k, bc in zip(ks, bcs)]
        kls = [(k * jnp.exp(bl - bc)).astype(BF16) for k, bl, bc in zip(ks, bls, bcs)]
        atts = [jnp.where(causal, _dot_nt(qd, kd), 0.0).astype(BF16) for qd, kd in zip(qds, kds)]
        sts = [st_ref[h] for h in H]
        os_ = [_dot(att, v) + _dot_nt(qd, st.astype(BF16)) for att, v, qd, st in zip(atts, vs, qds, sts)]
        for h in H:
            st_ref[h] = sts[h] * jnp.exp(bls[h]) + _dot_tn(vs[h], kls[h])
        for h in H:
            o = os_[h]
            o = o * lax.rsqrt(jnp.mean(o * o, axis=-1, keepdims=True) + NORM_EPS) * ng_ref[...]
            gg = g_ref[sl, h * DV:(h + 1) * DV].astype(F32)
            o_ref[sl, h * DV:(h + 1) * DV] = (o * (gg * _sigmoid(gg))).astype(o_ref.dtype)
        return carry

    lax.fori_loop(0, GLA_TC // C, body, 0)


def gla(U, Us, a2, a_b, norm_g, *, batch, seq):
    N = U.shape[0]
    nt = seq // GLA_TC
    HK, HV = GLA_HEADS * GLA_DK, GLA_HEADS * GLA_DV
    a2p = jnp.zeros((128, HK), F32).at[:GLA_RANK].set(a2)
    row = lambda b, i: b * nt + i
    return pl.pallas_call(
        _gla_kernel,
        out_shape=jax.ShapeDtypeStruct((N, HV), BF16),
        grid=(batch, nt),
        in_specs=[
            pl.BlockSpec((GLA_TC, HK), lambda b, i: (row(b, i), U_GLA_Q // HK)),
            pl.BlockSpec((GLA_TC, HK), lambda b, i: (row(b, i), U_GLA_K // HK)),
            pl.BlockSpec((GLA_TC, HV), lambda b, i: (row(b, i), U_GLA_V // HV)),
            pl.BlockSpec((GLA_TC, HV), lambda b, i: (row(b, i), U_GLA_G // HV)),
            pl.BlockSpec((GLA_TC, 128), lambda b, i: (row(b, i), S_GLA_A // 128)),
            pl.BlockSpec((128, HK), lambda b, i: (0, 0)),
            pl.BlockSpec((1, HK), lambda b, i: (0, 0)),
            pl.BlockSpec((1, GLA_DV), lambda b, i: (0, 0)),
        ],
        out_specs=pl.BlockSpec((GLA_TC, HV), lambda b, i: (row(b, i), 0)),
        scratch_shapes=[pltpu.VMEM((GLA_HEADS, GLA_DV, GLA_DK), F32), pltpu.VMEM((GLA_TC, HK), F32)],
        compiler_params=_cparams(("parallel", "arbitrary")),
        name="gla",
    )(U, U, U, U, Us, a2p, a_b.reshape(1, -1), norm_g.reshape(1, -1))


RWKV_TM = 256


def _seg_ones(n=256, seg=RWKV_N):
    i = np.arange(n)
    return jnp.asarray((i[:, None] // seg == i[None, :] // seg).astype(np.float32), BF16)


def _seg_sum(x, bd):
    outs = [_dot_split(x[:, s:s + 256], bd) for s in range(0, x.shape[1], 256)]
    return outs[0] if len(outs) == 1 else jnp.concatenate(outs, axis=1)


def _shift_lerp(u_ref, p_ref, mu, first):
    u = u_ref[...].astype(F32)
    prev_last = jnp.where(first, 0.0, p_ref[7:8, :].astype(F32))
    rolled = pltpu.roll(u, 1, 0)
    is_row0 = lax.broadcasted_iota(jnp.int32, u.shape, 0) == 0
    sh = jnp.where(is_row0, prev_last, rolled)
    return u + (sh - u) * mu


def _rwkv_prep_kernel(r_ref, k_ref, v_ref, wa_ref, gl_ref, rp_ref, kp_ref, vp_ref, wap_ref, glp_ref,
                      mur_ref, muk_ref, muv_ref, muwa_ref, mug_ref, w0_ref, w2_ref, a0_ref, a2_ref, g2_ref,
                      kk_ref, ka_ref, bd_ref,
                      ro_ref, ko_ref, vo_ref, kap_ref, bo_ref, lw_ref, go_ref, *, spb):
    first = (pl.program_id(0) % spb) == 0
    r = _shift_lerp(r_ref, rp_ref, mur_ref[...], first)
    k = _shift_lerp(k_ref, kp_ref, muk_ref[...], first)
    v = _shift_lerp(v_ref, vp_ref, muv_ref[...], first)
    wa = _shift_lerp(wa_ref, wap_ref, muwa_ref[...], first)
    gl = _shift_lerp(gl_ref, glp_ref, mug_ref[...], first)
    w_log = -_softplus(-(w0_ref[...] + _dot_x3(jnp.tanh(wa), w2_ref[...]))) - 0.5
    lw_ref[...] = -jnp.exp(w_log)
    a = _sigmoid(a0_ref[...] + _dot_x3(wa, a2_ref[...]))
    go_ref[...] = _dot(_sigmoid(gl).astype(BF16), g2_ref[...]).astype(go_ref.dtype)
    kk = k * kk_ref[...]
    nrm = jnp.sqrt(_seg_sum(kk * kk, bd_ref[...]))
    kk = kk / jnp.maximum(nrm, 1e-12)
    ro_ref[...] = r.astype(ro_ref.dtype)
    ko_ref[...] = (k * (1.0 + (a - 1.0) * ka_ref[...])).astype(ko_ref.dtype)
    vo_ref[...] = v.astype(vo_ref.dtype)
    kap_ref[...] = kk.astype(kap_ref.dtype)
    bo_ref[...] = (kk * a).astype(bo_ref.dtype)


def rwkv_prep(U, Us, mu, w0, w2, a0, a2, g2, k_k, k_a, *, seq):
    N = U.shape[0]
    tm = RWKV_TM
    spb = seq // tm
    R = RWKV_DIM
    cur = lambda cb: (lambda i: (i, cb))
    prv = lambda cb: (lambda i: (jnp.maximum(i * (tm // 8) - 1, 0), cb))
    mu_r, mu_k, mu_v = mu[:R], mu[R:2 * R], mu[2 * R:3 * R]
    mu_wa = mu[3 * R:3 * R + 128]
    mu_g = jnp.zeros((256,), F32).at[:RWKV_G_RANK].set(mu[3 * R + 128:])
    w2p = jnp.zeros((128, R), F32).at[:RWKV_W_RANK].set(w2)
    a2p = jnp.zeros((128, R), F32).at[RWKV_W_RANK:].set(a2)
    g2p = jnp.zeros((256, R), BF16).at[:RWKV_G_RANK].set(g2.astype(BF16))
    vec = lambda a: a.reshape(1, -1)
    full = lambda a: pl.BlockSpec(a.shape, lambda i: (0, 0))
    params = [vec(mu_r), vec(mu_k), vec(mu_v), vec(mu_wa), vec(mu_g), vec(w0), w2p, vec(a0), a2p, g2p,
              vec(k_k), vec(k_a), _seg_ones()]
    in_specs = [
        pl.BlockSpec((tm, R), cur(U_RWKV_R // R)), pl.BlockSpec((tm, R), cur(U_RWKV_K // R)),
        pl.BlockSpec((tm, R), cur(U_RWKV_V // R)),
        pl.BlockSpec((tm, 128), cur(S_RWKV_WA // 128)), pl.BlockSpec((tm, 256), cur(S_RWKV_G // 256)),
        pl.BlockSpec((8, R), prv(U_RWKV_R // R)), pl.BlockSpec((8, R), prv(U_RWKV_K // R)),
        pl.BlockSpec((8, R), prv(U_RWKV_V // R)),
        pl.BlockSpec((8, 128), prv(S_RWKV_WA // 128)), pl.BlockSpec((8, 256), prv(S_RWKV_G // 256)),
    ] + [full(p) for p in params]
    out = lambda dt: jax.ShapeDtypeStruct((N, R), dt)
    ospec = pl.BlockSpec((tm, R), lambda i: (i, 0))
    return pl.pallas_call(
        functools.partial(_rwkv_prep_kernel, spb=spb),
        out_shape=(out(BF16), out(BF16), out(BF16), out(BF16), out(BF16), out(F32), out(BF16)),
        grid=(N // tm,),
        in_specs=in_specs,
        out_specs=(ospec,) * 7,
        compiler_params=_cparams(("parallel",)),
        name="rwkv_prep",
    )(U, U, U, Us, Us, U, U, U, Us, Us, *params)


RWKV_TC = 512
RWKV_W = RWKV_HB * RWKV_N


def _rwkv_masks():
    W, C = RWKV_W, CHUNK
    i = np.arange(W)
    same = i[:, None] // C == i[None, :] // C
    m_bd = same.astype(np.float32)
    low_s = (same & (i[:, None] % C > i[None, :] % C)).astype(np.float32)
    low_i = (same & (i[:, None] % C >= i[None, :] % C)).astype(np.float32)
    tril = np.tril(np.ones((C, C), np.float32))
    return (jnp.asarray(m_bd), jnp.asarray(low_s), jnp.asarray(low_i), jnp.asarray(tril),
            jnp.asarray(np.eye(W, dtype=np.float32)))


def _tile4(x):
    return jnp.concatenate([x] * RWKV_HB, axis=0)


def _rwkv_chunk_kernel(r_ref, k_ref, v_ref, kap_ref, b_ref, lw_ref, g_ref,
                       mbd_ref, lows_ref, lowi_ref, tril_ref, eye_ref, bd_ref,
                       rk_ref, lnw_ref, lnb_ref, o_ref, st_ref):
    @pl.when(pl.program_id(1) == 0)
    def _():
        st_ref[...] = jnp.zeros_like(st_ref)

    C, W = CHUNK, RWKV_W
    n_batch = r_ref.shape[0]
    n_groups = r_ref.shape[2] // W
    m_bd = mbd_ref[...]
    low_s = lows_ref[...]
    low_i = lowi_ref[...]
    tril = tril_ref[...]
    eye = eye_ref[...]
    bd = bd_ref[...]

    def expand(x):
        return (_tile4(x) * m_bd).astype(BF16)

    def body(c, carry):
        sl = pl.ds(pl.multiple_of(c * C, C), C)
        lw_alls = [lw_ref[bi, sl, :] for bi in range(n_batch)]
        cum_alls = [_dot_sel(tril, lw) for lw in lw_alls]
        chains = [(bi, slice(gi * W, (gi + 1) * W)) for bi in range(n_batch) for gi in range(n_groups)]
        G = range(len(chains))
        lns = [ln for _, ln in chains]
        cums = [cum_alls[bi][:, ln] for bi, ln in chains]
        clasts = [cum[C - 1:C, :] for cum in cums]
        rs = [r_ref[bi, sl, ln].astype(F32) for bi, ln in chains]
        ks = [k_ref[bi, sl, ln].astype(F32) for bi, ln in chains]
        vs = [v_ref[bi, sl, ln].astype(F32) for bi, ln in chains]
        bs = [b_ref[bi, sl, ln].astype(F32) for bi, ln in chains]
        p_invs = [jnp.exp(-cum) for cum in cums]
        xes = [jnp.concatenate([expand(kap_ref[bi, sl, ln].astype(F32) * jnp.exp(cum - lw_alls[bi][:, ln])),
                                expand(r * jnp.exp(cum))], axis=0)
               for (bi, ln), cum, r in zip(chains, cums, rs)]
        hes = [jnp.concatenate([expand(b * pi), expand(k * pi)], axis=0) for b, k, pi in zip(bs, ks, p_invs)]
        scs = [_dot_nt(xe, he) for xe, he in zip(xes, hes)]
        ams = [sc[:W, :W] * low_s for sc in scs]
        tinvs = [eye - a_m for a_m in ams]
        ambs = [a_m.astype(BF16) for a_m in ams]
        pws = [_dot(ab, ab) for ab in ambs]
        n_lev = int(np.log2(C))
        for lev in range(1, n_lev):
            pwbs = [pw.astype(BF16) for pw in pws]
            if lev < n_lev - 1:
                outs = [_dot(jnp.concatenate([pwb, tinv.astype(BF16)], axis=0), pwb)
                        for pwb, tinv in zip(pwbs, tinvs)]
                pws = [o[:W] for o in outs]
                tinvs = [tinv + o[W:] for tinv, o in zip(tinvs, outs)]
            else:
                tinvs = [tinv + _dot(tinv.astype(BF16), pwb) for tinv, pwb in zip(tinvs, pwbs)]
        ves = [expand(v) for v in vs]
        bqv = [_dot(jnp.concatenate([(sc[:W, W:] * low_s).astype(BF16), (sc[W:, W:] * low_i).astype(BF16)], axis=0),
                    ve) for sc, ve in zip(scs, ves)]
        bmv = [o[:W] for o in bqv]
        qkv = [o[W:] for o in bqv]
        sts = [st_ref[gi] for gi in G]
        xss = [_dot_nt(xe, st.astype(BF16)) for xe, st in zip(xes, sts)]
        us = [_dot(tinv.astype(BF16), (xs[:W] + bv).astype(BF16)) for tinv, xs, bv in zip(tinvs, xss, bmv)]
        kbs = [jnp.concatenate([expand(k * jnp.exp(cl - cum)), expand(b * jnp.exp(cl - cum))], axis=0)
               for k, b, cl, cum in zip(ks, bs, clasts, cums)]
        for gi in G:
            vu = jnp.concatenate([ves[gi], (-us[gi]).astype(BF16)], axis=0)
            st_ref[gi] = sts[gi] * jnp.exp(clasts[gi]) + _dot_tn(vu, kbs[gi]) * m_bd
        for gi in G:
            ln = lns[gi]
            qb = (scs[gi][W:, :W] * low_i).astype(BF16)
            y_e = xss[gi][W:] + qkv[gi] - _dot(qb, us[gi].astype(BF16))
            y = y_e[0:C] + y_e[C:2 * C] + y_e[2 * C:3 * C] + y_e[3 * C:4 * C]
            mu = _seg_sum(y, bd) * (1.0 / RWKV_N)
            yc = y - mu
            var = _seg_sum(yc * yc, bd) * (1.0 / RWKV_N)
            yn = yc * lax.rsqrt(var + RWKV_LN_EPS) * lnw_ref[:, ln] + lnb_ref[:, ln]
            bonus = _seg_sum(rs[gi] * ks[gi] * rk_ref[:, ln], bd) * vs[gi]
            bi = chains[gi][0]
            o_ref[bi, sl, ln] = ((yn + bonus) * g_ref[bi, sl, ln].astype(F32)).astype(o_ref.dtype)
        return carry

    lax.fori_loop(0, RWKV_TC // C, body, 0)


RWKV_NB = 1


def rwkv_chunk(r, k, v, kap, b, lw, gate, r_k, ln_w, ln_b, *, batch, seq):
    N, R = r.shape
    nt = seq // RWKV_TC
    W = RWKV_W
    nb = RWKV_NB if batch % RWKV_NB == 0 else 1
    blk = pl.BlockSpec((nb, RWKV_TC, R), lambda bb, i: (bb, i, 0))
    masks = _rwkv_masks() + (_seg_ones(),)
    full = lambda a: pl.BlockSpec(a.shape, lambda bb, i: (0, 0))
    pvec = pl.BlockSpec((1, R), lambda bb, i: (0, 0))
    seqs = [a.reshape(batch, seq, R) for a in (r, k, v, kap, b, lw, gate)]
    out = pl.pallas_call(
        _rwkv_chunk_kernel,
        out_shape=jax.ShapeDtypeStruct((batch, seq, R), BF16),
        grid=(batch // nb, nt),
        in_specs=[blk] * 7 + [full(m) for m in masks] + [pvec] * 3,
        out_specs=blk,
        scratch_shapes=[pltpu.VMEM((nb * (R // W), W, W), F32)],
        compiler_params=_cparams(("parallel", "arbitrary")),
        name="rwkv_chunk",
    )(*seqs, *masks, r_k.reshape(1, R), ln_w.reshape(1, R), ln_b.reshape(1, R))
    return out.reshape(N, R)


def _gelu_tanh(x):
    return 0.5 * x * (1.0 + jnp.tanh(np.sqrt(2.0 / np.pi) * (x + 0.044715 * (x * x * x))))


def _nsa_compress_kernel(x_ref, pos_ref, w1_ref, w2_ref, o_ref):
    x = x_ref[0, 0]
    w1 = w1_ref[0]
    half = w1.shape[0] // 2
    nrow = x.shape[0]
    ha = _dot(x, w1[:half])
    hb = _dot(x, w1[half:])
    h = ha + pltpu.roll(hb, nrow - 1, 0)
    pb = _dot(pos_ref[0], w1)
    h = _gelu_tanh(h + pb[0:1, :])
    o_ref[0, 0, 0] = _dot(h.astype(BF16), w2_ref[0]).astype(o_ref.dtype)


def nsa_compress(kv_cmp, pos, w1, w2):
    two, B, G, T, dh = kv_cmp.shape
    nr = T // NSA_CMP_STRIDE
    x = kv_cmp.reshape(two * B, G, nr, NSA_CMP_STRIDE * dh)
    posf = jnp.broadcast_to(pos.reshape(two, 1, NSA_CMP_LEN * dh), (two, 8, NSA_CMP_LEN * dh)).astype(BF16)
    return pl.pallas_call(
        _nsa_compress_kernel,
        out_shape=jax.ShapeDtypeStruct((two, B, G, nr, dh), BF16),
        grid=(two, B, G),
        in_specs=[
            pl.BlockSpec((1, 1, nr, NSA_CMP_STRIDE * dh), lambda s, b, g: (s * B + b, g, 0, 0)),
            pl.BlockSpec((1, 8, NSA_CMP_LEN * dh), lambda s, b, g: (s, 0, 0)),
            pl.BlockSpec((1, NSA_CMP_LEN * dh, NSA_CMP_HIDDEN), lambda s, b, g: (s, 0, 0)),
            pl.BlockSpec((1, NSA_CMP_HIDDEN, dh), lambda s, b, g: (s, 0, 0)),
        ],
        out_specs=pl.BlockSpec((1, 1, 1, nr, dh), lambda s, b, g: (s, b, g, 0, 0)),
        compiler_params=_cparams(("parallel", "parallel", "parallel")),
        name="nsa_compress",
    )(x, posf, w1.astype(BF16), w2.astype(BF16))


def _nsa_cmp_kernel(q_ref, kc_ref, vct_ref, ov_ref, cb_ref, oc_ref, sb_ref, qs_ref, *, n_slc, n_sel):
    qi = pl.program_id(2)
    QB = CMP_QB
    ncp = kc_ref.shape[2]
    for h in range(NSA_HPG):
        qs_ref[h * QB:(h + 1) * QB, :] = q_ref[:, h * NSA_DH:(h + 1) * NSA_DH] * (NSA_DH ** -0.5)
    s = _dot_nt(kc_ref[0, 0], qs_ref[...])
    vct = vct_ref[0, 0]
    psum = jnp.zeros((ncp, QB), F32)
    for h in range(NSA_HPG):
        sh = s[:, h * QB:(h + 1) * QB] + cb_ref[h]
        m = jnp.maximum(jnp.max(sh, axis=0, keepdims=True), -1e20)
        e = jnp.exp(sh - m)
        acc = _dot(vct, e.astype(BF16))
        inv_l = 1.0 / jnp.maximum(acc[NSA_DH:NSA_DH + 1], 1e-30)
        oc_ref[:, h * NSA_DH:(h + 1) * NSA_DH] = (acc[0:NSA_DH] * inv_l).T.astype(oc_ref.dtype)
        psum = psum + e * inv_l
    imp = _dot_sel(ov_ref[...], psum)
    j = lax.broadcasted_iota(jnp.int32, (n_slc, QB), 0)
    tt = qi * QB + lax.broadcasted_iota(jnp.int32, (n_slc, QB), 1)
    cur = jnp.right_shift(tt, 6)
    forced = (j == 0) | (j == cur) | (j == cur - 1)
    cand = (j >= 1) & (j <= cur - 2)
    n_free = n_sel - 3

    def emit(sel):
        sb_ref[0, 0] = jnp.where(sel, 0.0, -SEL_MASK_BIAS).T.astype(sb_ref.dtype)

    last_cur = (qi * QB + QB - 1) // NSA_SLC_LEN

    @pl.when(last_cur - 2 <= n_free)
    def _():
        emit(forced | cand)

    @pl.when(last_cur - 2 > n_free)
    def _():
        cur_row = cur[0:1, :]
        rank = jnp.zeros((n_slc, QB), jnp.int32)
        for jp in range(1, n_slc):
            row = imp[jp:jp + 1, :]
            ahead = (row > imp) | ((row == imp) & (j > jp))
            rank = rank + jnp.where(ahead & (jp <= cur_row - 2), 1, 0)
        emit(forced | (cand & (rank < n_free)))


def nsa_cmp(U, kc, vct, slopes, *, batch, seq):
    N = U.shape[0]
    QB = CMP_QB
    nq = seq // QB
    ncp = kc.shape[2]
    n_slc = seq // NSA_SLC_LEN
    n_sel = min(NSA_N_SEL, n_slc)
    nn = np.arange(ncp)
    jj = np.arange(n_slc)
    ov = ((nn[None, :] * NSA_CMP_STRIDE + NSA_CMP_LEN - 1 >= jj[:, None] * NSA_SLC_LEN)
          & (nn[None, :] * NSA_CMP_STRIDE <= jj[:, None] * NSA_SLC_LEN + NSA_SLC_LEN - 1)
          & (nn[None, :] < ncp - 1)).astype(np.float32)
    tt = np.arange(seq)
    ended = jnp.asarray(nn[:, None] * NSA_CMP_STRIDE + NSA_CMP_LEN - 1 <= tt[None, :])
    adist = np.abs(tt[None, :] - (nn[:, None] * NSA_CMP_STRIDE + (NSA_CMP_LEN - 1) / 2.0)).astype(np.float32)
    cbias = jnp.where(ended[None], -slopes[:, None, None] * jnp.asarray(adist)[None], NEG_BIG)
    G = NSA_GROUPS
    W = NSA_HPG * NSA_DH
    return pl.pallas_call(
        functools.partial(_nsa_cmp_kernel, n_slc=n_slc, n_sel=n_sel),
        out_shape=(jax.ShapeDtypeStruct((N, NSA_HEADS * NSA_DH), BF16),
                   jax.ShapeDtypeStruct((batch, G, seq, n_slc), BF16)),
        grid=(batch, G, nq),
        in_specs=[
            pl.BlockSpec((QB, W), lambda b, g, i: (b * nq + i, U_NSA_Q // W + g)),
            pl.BlockSpec((1, 1, ncp, NSA_DH), lambda b, g, i: (b, g, 0, 0)),
            pl.BlockSpec((1, 1, NSA_DH + 8, ncp), lambda b, g, i: (b, g, 0, 0)),
            pl.BlockSpec((n_slc, ncp), lambda b, g, i: (0, 0)),
            pl.BlockSpec((NSA_HPG, ncp, QB), lambda b, g, i: (g, 0, i)),
        ],
        out_specs=(
            pl.BlockSpec((QB, W), lambda b, g, i: (b * nq + i, g)),
            pl.BlockSpec((1, 1, QB, n_slc), lambda b, g, i: (b, g, i, 0)),
        ),
        scratch_shapes=[pltpu.VMEM((NSA_HPG * QB, NSA_DH), BF16)],
        compiler_params=_cparams(("parallel", "parallel", "arbitrary")),
        name="nsa_cmp",
    )(U, kc, vct, jnp.asarray(ov), cbias)


def _nsa_win_kernel(q_ref, k0_ref, k1_ref, k2_ref, v0_ref, v1_ref, v2_ref, bias_ref, o_ref, qs_ref):
    qi = pl.program_id(2)
    QB = NSA_QB
    KW = NSA_WINDOW + QB
    for h in range(NSA_HPG):
        qs_ref[h * QB:(h + 1) * QB, :] = q_ref[:, h * NSA_DH:(h + 1) * NSA_DH] * (NSA_DH ** -0.5)
    k = jnp.concatenate([k0_ref[0, 0], k1_ref[0, 0], k2_ref[0, 0]], axis=0)
    vt = jnp.concatenate([v0_ref[0, 0], v1_ref[0, 0], v2_ref[0, 0]], axis=1)
    s = _dot_nt(k, qs_ref[...])

    def finish(before_start):
        for h in range(NSA_HPG):
            cols = slice(h * QB, (h + 1) * QB)
            sh = s[:, cols] + bias_ref[h]
            if before_start:
                r = lax.broadcasted_iota(jnp.int32, (KW, QB), 0)
                sh = jnp.where(r >= NSA_WINDOW - qi * QB, sh, NEG_BIG)
            m = jnp.max(sh, axis=0, keepdims=True)
            e = jnp.exp(sh - m)
            acc = _dot(vt, e.astype(BF16))
            o = acc[0:NSA_DH] / acc[NSA_DH:NSA_DH + 1]
            o_ref[:, h * NSA_DH:(h + 1) * NSA_DH] = o.T.astype(o_ref.dtype)

    @pl.when(qi * QB < NSA_WINDOW)
    def _():
        finish(True)

    @pl.when(qi * QB >= NSA_WINDOW)
    def _():
        finish(False)


def nsa_win(U, kw_pad, vtw_pad, slopes, *, batch, seq):
    N = U.shape[0]
    QB = NSA_QB
    KW = NSA_WINDOW + QB
    nq = seq // QB
    G = NSA_GROUPS
    W = NSA_HPG * NSA_DH
    dist = (np.arange(QB)[None, :] - np.arange(KW)[:, None] + NSA_WINDOW).astype(np.float32)
    inside = jnp.asarray((dist >= 0) & (dist < NSA_WINDOW))
    bias = jnp.where(inside[None], -slopes[:, None, None] * jnp.asarray(dist)[None], NEG_BIG)
    kb = lambda off: pl.BlockSpec((1, 1, QB, NSA_DH), lambda b, g, i: (b, g, i + off, 0))
    vb = lambda off: pl.BlockSpec((1, 1, NSA_DH + 8, QB), lambda b, g, i: (b, g, 0, i + off))
    return pl.pallas_call(
        _nsa_win_kernel,
        out_shape=jax.ShapeDtypeStruct((N, NSA_HEADS * NSA_DH), BF16),
        grid=(batch, G, nq),
        in_specs=[pl.BlockSpec((QB, W), lambda b, g, i: (b * nq + i, U_NSA_Q // W + g)),
                  kb(0), kb(1), kb(2), vb(0), vb(1), vb(2),
                  pl.BlockSpec((NSA_HPG, KW, QB), lambda b, g, i: (g, 0, 0))],
        out_specs=pl.BlockSpec((QB, W), lambda b, g, i: (b * nq + i, g)),
        scratch_shapes=[pltpu.VMEM((NSA_HPG * QB, NSA_DH), BF16)],
        compiler_params=_cparams(("parallel", "parallel", "arbitrary")),
        name="nsa_win",
    )(U, kw_pad, kw_pad, kw_pad, vtw_pad, vtw_pad, vtw_pad, bias)


SEL_QB = 512
SEL_KV = 512


def _sel_pairs(seq):
    qs, ks = [], []
    for qi in range(seq // SEL_QB):
        for kj in range((qi * SEL_QB) // SEL_KV + 1):
            qs.append(qi)
            ks.append(kj)
    return np.asarray(qs, np.int32), np.asarray(ks, np.int32)


def _nsa_sel_kernel(qi_ref, kj_ref, slope_ref, q_ref, sb_ref, qx_ref, ka_ref, vt_ref, o_ref,
                    qa_ref, m_ref, acc_ref):
    g = pl.program_id(1)
    p = pl.program_id(2)
    qi = qi_ref[p]
    kj = kj_ref[p]
    QB, KV = SEL_QB, SEL_KV
    nblk = sb_ref.shape[3]
    base = NSA_DH + nblk

    @pl.when(kj == 0)
    def _():
        for h in range(NSA_HPG):
            rows = slice(h * QB, (h + 1) * QB)
            qa_ref[rows, 0:NSA_DH] = q_ref[:, h * NSA_DH:(h + 1) * NSA_DH] * (NSA_DH ** -0.5)
            qa_ref[rows, NSA_DH:base] = sb_ref[0, 0]
            qa_ref[rows, base:] = jnp.broadcast_to(qx_ref[0, h:h + 1, :], (QB, qa_ref.shape[1] - base))
        m_ref[...] = jnp.full_like(m_ref, NEG_BIG)
        acc_ref[...] = jnp.zeros_like(acc_ref)

    off = qi * QB - kj * KV
    tile_start = (kj * KV).astype(F32)

    def step(masked):
        s = _dot_nt(ka_ref[0, 0], qa_ref[...])
        vt = vt_ref[0, 0]
        if masked:
            r = lax.broadcasted_iota(jnp.int32, (KV, QB), 0)
            c = lax.broadcasted_iota(jnp.int32, (KV, QB), 1)
            causal = (c - r + off) >= 0
        for h in range(NSA_HPG):
            cols = slice(h * QB, (h + 1) * QB)
            sh = s[:, cols]
            if masked:
                sh = jnp.where(causal, sh, NEG_BIG)
            delta = slope_ref[g * NSA_HPG + h] * tile_start
            m_old = m_ref[h:h + 1, :]
            m_new = jnp.maximum(m_old, jnp.max(sh, axis=0, keepdims=True) + delta)
            e = jnp.exp(sh - (m_new - delta))
            alpha = jnp.exp(m_old - m_new)
            acc_ref[:, cols] = alpha * acc_ref[:, cols] + _dot(vt, e.astype(BF16))
            m_ref[h:h + 1, :] = m_new

    last = (qi * QB) // KV

    @pl.when(kj < last)
    def _():
        step(False)

    @pl.when(kj == last)
    def _():
        step(True)
        for h in range(NSA_HPG):
            cols = slice(h * QB, (h + 1) * QB)
            o = acc_ref[0:NSA_DH, cols] / acc_ref[NSA_DH:NSA_DH + 1, cols]
            o_ref[:, h * NSA_DH:(h + 1) * NSA_DH] = o.T.astype(o_ref.dtype)


def sel_key_columns(seq):
    n_slc = seq // NSA_SLC_LEN
    pos = np.arange(seq)
    onehot = (pos[:, None] // NSA_SLC_LEN == np.arange(n_slc)[None, :]).astype(np.float32)
    r = pos % SEL_KV
    extra = np.zeros((seq, 128), np.float32)
    extra[:, 0] = extra[:, 2] = (r // 32) * 32
    extra[:, 1] = extra[:, 3] = r % 32
    return jnp.asarray(np.concatenate([onehot, extra], axis=1), BF16)


def nsa_sel(U, selbias, k_aug, vt_slc, slopes, *, batch, seq):
    N = U.shape[0]
    QB, KV = SEL_QB, SEL_KV
    nq = seq // QB
    G = NSA_GROUPS
    W = NSA_HPG * NSA_DH
    n_slc = selbias.shape[3]
    ka_w = k_aug.shape[3]
    qs, ks = _sel_pairs(seq)
    s_hi = slopes.astype(BF16)
    s_lo = (slopes - s_hi.astype(F32)).astype(BF16)
    qx = jnp.zeros((NSA_HEADS, 128), BF16).at[:, 0].set(s_hi).at[:, 1].set(s_hi).at[:, 2].set(s_lo).at[:, 3].set(s_lo)
    qx = jnp.pad(qx.reshape(G, NSA_HPG, 128), ((0, 0), (0, 8 - NSA_HPG), (0, 0)))
    grid_spec = pltpu.PrefetchScalarGridSpec(
        num_scalar_prefetch=3,
        grid=(batch, G, len(qs)),
        in_specs=[
            pl.BlockSpec((QB, W), lambda b, g, p, qi, kj, s: (b * nq + qi[p], U_NSA_Q // W + g)),
            pl.BlockSpec((1, 1, QB, n_slc), lambda b, g, p, qi, kj, s: (b, g, qi[p], 0)),
            pl.BlockSpec((1, 8, 128), lambda b, g, p, qi, kj, s: (g, 0, 0)),
            pl.BlockSpec((1, 1, KV, ka_w), lambda b, g, p, qi, kj, s: (b, g, kj[p], 0)),
            pl.BlockSpec((1, 1, NSA_DH + 8, KV), lambda b, g, p, qi, kj, s: (b, g, 0, kj[p])),
        ],
        out_specs=pl.BlockSpec((QB, W), lambda b, g, p, qi, kj, s: (b * nq + qi[p], g)),
        scratch_shapes=[
            pltpu.VMEM((NSA_HPG * QB, ka_w), BF16),
            pltpu.VMEM((NSA_HPG, QB), F32),
            pltpu.VMEM((NSA_DH + 8, NSA_HPG * QB), F32),
        ],
    )
    return pl.pallas_call(
        _nsa_sel_kernel,
        out_shape=jax.ShapeDtypeStruct((N, NSA_HEADS * NSA_DH), BF16),
        grid_spec=grid_spec,
        compiler_params=_cparams(("parallel", "parallel", "arbitrary")),
        name="nsa_sel",
    )(jnp.asarray(qs), jnp.asarray(ks), slopes, U, selbias, qx, k_aug, vt_slc)


def _nsa_combine_kernel(oc_ref, os_ref, ow_ref, gate_ref, e_ref, o_ref):
    ge = _dot_split(_sigmoid(gate_ref[...]), e_ref[...])
    Wd = NSA_HEADS * NSA_DH
    o = (ge[:, :Wd] * oc_ref[...].astype(F32) + ge[:, Wd:2 * Wd] * os_ref[...].astype(F32)
         + ge[:, 2 * Wd:] * ow_ref[...].astype(F32))
    o_ref[...] = o.astype(o_ref.dtype)


def nsa_combine(o_c, o_s, o_w, Us):
    N, Wd = o_c.shape
    tm = 512
    e = np.zeros((128, 3 * Wd), np.float32)
    for h in range(NSA_HEADS):
        for j in range(3):
            e[h * 3 + j, j * Wd + h * NSA_DH:j * Wd + (h + 1) * NSA_DH] = 1.0
    blk = pl.BlockSpec((tm, Wd), lambda i: (i, 0))
    return pl.pallas_call(
        _nsa_combine_kernel,
        out_shape=jax.ShapeDtypeStruct((N, Wd), BF16),
        grid=(N // tm,),
        in_specs=[blk, blk, blk, pl.BlockSpec((tm, 128), lambda i: (i, S_NSA_GATE // 128)),
                  pl.BlockSpec((128, 3 * Wd), lambda i: (0, 0))],
        out_specs=blk,
        compiler_params=_cparams(("parallel",)),
        name="nsa_combine",
    )(o_c, o_s, o_w, Us, jnp.asarray(e, BF16))


def _merge_kernel(ya_ref, yb_ref, yc_ref, ga_ref, gb_ref, gc_ref, p_ref, o_ref):
    m = (_sigmoid(ga_ref[...].astype(F32)) * _dot(ya_ref[...], p_ref[0])
         + _sigmoid(gb_ref[...].astype(F32)) * _dot(yb_ref[...], p_ref[1])
         + _sigmoid(gc_ref[...].astype(F32)) * _dot(yc_ref[...], p_ref[2]))
    o_ref[...] = m.astype(o_ref.dtype)


def merge(y_a, y_b, y_c, U, p_merge):
    N = y_a.shape[0]
    tm, tn = 1024, 512
    yb = pl.BlockSpec((tm, BRANCH_DIM), lambda i, j: (i, 0))
    gate = lambda br: pl.BlockSpec((tm, tn), lambda i, j: (i, (U_GATE + br * D_MODEL) // tn + j))
    return pl.pallas_call(
        _merge_kernel,
        out_shape=jax.ShapeDtypeStruct((N, D_MODEL), BF16),
        grid=(N // tm, D_MODEL // tn),
        in_specs=[yb, yb, yb, gate(0), gate(1), gate(2),
                  pl.BlockSpec((3, BRANCH_DIM, tn), lambda i, j: (0, 0, j))],
        out_specs=pl.BlockSpec((tm, tn), lambda i, j: (i, j)),
        compiler_params=_cparams(("parallel", "parallel")),
        name="merge",
    )(y_a, y_b, y_c, U, U, U, p_merge)


def _ffn_kernel(h_ref, w1_ref, w3_ref, w2_ref, o_ref, acc_ref):
    f = pl.program_id(1)

    @pl.when(f == 0)
    def _():
        acc_ref[...] = jnp.zeros_like(acc_ref)

    h = h_ref[...]
    a = _dot(h, w1_ref[...])
    z = (a * _sigmoid(a) * _dot(h, w3_ref[...])).astype(BF16)
    acc_ref[...] += _dot(z, w2_ref[...])

    @pl.when(f == pl.num_programs(1) - 1)
    def _():
        o_ref[...] = acc_ref[...].astype(o_ref.dtype)


def ffn(h, w1, w3, w2):
    N, D = h.shape
    F = w1.shape[1]
    tm, tf = 1024, 512
    return pl.pallas_call(
        _ffn_kernel,
        out_shape=jax.ShapeDtypeStruct((N, D), BF16),
        grid=(N // tm, F // tf),
        in_specs=[pl.BlockSpec((tm, D), lambda i, f: (i, 0)),
                  pl.BlockSpec((D, tf), lambda i, f: (0, f)),
                  pl.BlockSpec((D, tf), lambda i, f: (0, f)),
                  pl.BlockSpec((tf, D), lambda i, f: (f, 0))],
        out_specs=pl.BlockSpec((tm, D), lambda i, f: (i, 0)),
        scratch_shapes=[pltpu.VMEM((tm, D), F32)],
        compiler_params=_cparams(("parallel", "arbitrary")),
        name="ffn",
    )(h, w1, w3, w2)


def _route_kernel(lg_ref, cb_ref, sel_ref):
    lg = lg_ref[...]
    lane = lax.broadcasted_iota(jnp.int32, lg.shape, 1)
    x = jnp.where(lane < N_EXPERTS, lg, NEG_BIG)
    v1 = jnp.max(x, axis=-1, keepdims=True)
    i1 = jnp.min(jnp.where(x == v1, lane, 1024), axis=-1, keepdims=True)
    x2 = jnp.where(lane == i1, NEG_BIG, x)
    v2 = jnp.max(x2, axis=-1, keepdims=True)
    i2 = jnp.min(jnp.where(x2 == v2, lane, 1024), axis=-1, keepdims=True)
    e2 = jnp.exp(v2 - v1)
    w1 = 1.0 / (1.0 + e2)
    w2 = e2 / (1.0 + e2)
    cb_ref[...] = jnp.where(lane == i1, w1, 0.0) + jnp.where(lane == i2, w2, 0.0)
    sel_ref[...] = jnp.where((lane == i1) | (lane == i2), 1.0, 0.0)


def route(logits):
    N = logits.shape[0]
    tm = 1024
    blk = pl.BlockSpec((tm, 128), lambda i: (i, 0))
    return pl.pallas_call(
        _route_kernel,
        out_shape=(jax.ShapeDtypeStruct((N, 128), F32), jax.ShapeDtypeStruct((N, 128), F32)),
        grid=(N // tm,),
        in_specs=[blk],
        out_specs=(blk, blk),
        compiler_params=_cparams(("parallel",)),
        name="route",
    )(logits)


MOE_TM = 512
MOE_WB = 512
MOE_GT = 256


def _moe_plan(sel):
    N, E = sel.shape
    tm, wb, gt = MOE_TM, MOE_WB, MOE_GT
    sub = tm // gt
    NT = 2 * N // tm + E
    NG = NT * sub
    R = NT * tm
    P = NG + E * (N // wb)
    i32 = jnp.int32
    cs = jnp.cumsum(sel.astype(i32), axis=0)
    cnt = cs[-1]
    tiles_e = (cnt + tm - 1) // tm
    tile_end = jnp.cumsum(tiles_e)
    tile_start = tile_end - tiles_e
    total_tiles = tile_end[-1]
    dest = jnp.where(sel, tile_start[None, :] * tm + cs - 1, R)
    ti = jnp.arange(NT, dtype=i32)
    tile_valid = ti < total_tiles
    tile_e = jnp.minimum(jnp.searchsorted(tile_end, ti, side="right").astype(i32), E - 1)
    gi = jnp.arange(NG, dtype=i32)
    g_valid = tile_valid[gi // sub]
    g_e = tile_e[gi // sub]
    k_lo = gi * gt - tile_start[g_e] * tm
    nonempty = g_valid & (k_lo < cnt[g_e])
    k_hi = jnp.minimum(k_lo + gt, cnt[g_e]) - 1
    find = jax.vmap(lambda col, q: jnp.searchsorted(col, q, side="left"), in_axes=(1, None), out_axes=1)
    pick = lambda m: jnp.take_along_axis(m, g_e[:, None], axis=1)[:, 0].astype(i32)
    blo = jnp.where(nonempty, pick(find(cs, k_lo + 1)) // wb, 0)
    bhi = jnp.where(nonempty, pick(find(cs, k_hi + 1)) // wb, jnp.where(g_valid, 0, -1))
    npair = bhi - blo + 1
    pend = jnp.cumsum(npair)
    pstart = pend - npair
    total_p = pend[-1]
    pi = jnp.arange(P, dtype=i32)
    pvalid = pi < total_p
    ptile = jnp.minimum(jnp.searchsorted(pend, pi, side="right").astype(i32), NG - 1)
    pblk = blo[ptile] + pi - pstart[ptile]
    pfirst = pvalid & (pi == pstart[ptile])
    ptile = jnp.where(pvalid, ptile, ptile[total_p - 1])
    pblk = jnp.where(pvalid, pblk, pblk[total_p - 1])
    pexp = g_e[ptile]
    order = jnp.argsort(jnp.where(pvalid, pblk * NG + ptile, jnp.iinfo(jnp.int32).max))
    s_valid = pvalid
    s_tile = jnp.where(s_valid, ptile[order], ptile[order][total_p - 1])
    s_blk = jnp.where(s_valid, pblk[order], pblk[order][total_p - 1])
    s_exp = g_e[s_tile]
    s_first = s_valid & ((pi == 0) | (s_blk != jnp.roll(s_blk, 1)))
    e_src = jnp.where(tile_valid, ti, 0)
    e_exp = jnp.where(tile_valid, tile_e, tile_e[jnp.maximum(total_tiles - 1, 0)])
    b2i = lambda x: x.astype(i32)
    return dict(dest=dest, g=(ptile, pblk, pexp, b2i(pvalid), b2i(pfirst)),
                e=(e_exp, b2i(tile_valid), e_src), s=(s_tile, s_blk, s_exp, b2i(s_valid), b2i(s_first)), NT=NT, P=P)


def _moe_group_kernel(pt_ref, pb_ref, pe_ref, pv_ref, pf_ref, dest_ref, h_ref, o_ref):
    p = pl.program_id(0)

    @pl.when(pv_ref[p] == 1)
    def _():
        rel = dest_ref[0] - pt_ref[p] * MOE_GT
        row = lax.broadcasted_iota(jnp.int32, (MOE_GT, MOE_WB), 0)
        onehot = jnp.where(rel == row, 1.0, 0.0).astype(BF16)
        rows = _dot(onehot, h_ref[...]).astype(o_ref.dtype)

        @pl.when(pf_ref[p] == 1)
        def _():
            o_ref[...] = rows

        @pl.when(pf_ref[p] == 0)
        def _():
            o_ref[...] += rows


def _moe_expert_kernel(te_ref, tv_ref, ts_ref, h_ref, w1_ref, w3_ref, w2_ref, o_ref, acc_ref):
    i = pl.program_id(0)
    f = pl.program_id(1)
    nf = pl.num_programs(1)

    @pl.when(tv_ref[i] == 1)
    def _():
        @pl.when(f == 0)
        def _():
            acc_ref[...] = jnp.zeros_like(acc_ref)

        h = h_ref[...]
        a = _dot(h, w1_ref[0])
        z = (a * _sigmoid(a) * _dot(h, w3_ref[0])).astype(BF16)
        acc_ref[...] += _dot(z, w2_ref[0])

        @pl.when(f == nf - 1)
        def _():
            o_ref[...] = acc_ref[...].astype(o_ref.dtype)

    @pl.when((tv_ref[i] == 0) & (f == nf - 1))
    def _():
        o_ref[...] = jnp.zeros_like(o_ref)


def _moe_ungroup_kernel(st_ref, sb_ref, se_ref, sv_ref, sf_ref, dest_ref, w_ref, y_ref, o_ref):
    p = pl.program_id(0)

    @pl.when(sv_ref[p] == 1)
    def _():
        rel = dest_ref[0] - st_ref[p] * MOE_GT
        col = lax.broadcasted_iota(jnp.int32, (MOE_WB, MOE_GT), 1)
        onehot_t = jnp.where(rel == col, 1.0, 0.0).astype(BF16)
        part = _dot(onehot_t, y_ref[...]) * w_ref[0]

        @pl.when(sf_ref[p] == 1)
        def _():
            o_ref[...] = part

        @pl.when(sf_ref[p] == 0)
        def _():
            o_ref[...] += part


def moe(h, combine, selm, w1, w3, w2):
    N, D = h.shape
    E, _, F = w1.shape
    tm, wb, gt, tf = MOE_TM, MOE_WB, MOE_GT, 512
    plan = _moe_plan(selm[:, :E] > 0.5)
    NT, P = plan["NT"], plan["P"]
    R = NT * tm
    dest_t = plan["dest"].T
    h_sorted = pl.pallas_call(
        _moe_group_kernel,
        out_shape=jax.ShapeDtypeStruct((R, D), BF16),
        grid_spec=pltpu.PrefetchScalarGridSpec(
            num_scalar_prefetch=5,
            grid=(P,),
            in_specs=[pl.BlockSpec((1, 1, wb), lambda p, pt, pb, pe, pv, pf: (pe[p], 0, pb[p])),
                      pl.BlockSpec((wb, D), lambda p, pt, pb, pe, pv, pf: (pb[p], 0))],
            out_specs=pl.BlockSpec((gt, D), lambda p, pt, pb, pe, pv, pf: (pt[p], 0)),
        ),
        compiler_params=_cparams(("arbitrary",)),
        name="moe_group",
    )(*plan["g"], dest_t.reshape(E, 1, N), h)
    y_sorted = pl.pallas_call(
        _moe_expert_kernel,
        out_shape=jax.ShapeDtypeStruct((R, D), BF16),
        grid_spec=pltpu.PrefetchScalarGridSpec(
            num_scalar_prefetch=3,
            grid=(NT, F // tf),
            in_specs=[pl.BlockSpec((tm, D), lambda i, f, te, tv, ts: (ts[i], 0)),
                      pl.BlockSpec((1, D, tf), lambda i, f, te, tv, ts: (te[i], 0, f * tv[i])),
                      pl.BlockSpec((1, D, tf), lambda i, f, te, tv, ts: (te[i], 0, f * tv[i])),
                      pl.BlockSpec((1, tf, D), lambda i, f, te, tv, ts: (te[i], f * tv[i], 0))],
            out_specs=pl.BlockSpec((tm, D), lambda i, f, te, tv, ts: (i, 0)),
            scratch_shapes=[pltpu.VMEM((tm, D), F32)],
        ),
        compiler_params=_cparams(("arbitrary", "arbitrary")),
        name="moe_expert",
    )(*plan["e"], h_sorted, w1, w3, w2)
    return pl.pallas_call(
        _moe_ungroup_kernel,
        out_shape=jax.ShapeDtypeStruct((N, D), F32),
        grid_spec=pltpu.PrefetchScalarGridSpec(
            num_scalar_prefetch=5,
            grid=(P,),
            in_specs=[pl.BlockSpec((1, wb, 1), lambda p, st, sb, se, sv, sf: (se[p], sb[p], 0)),
                      pl.BlockSpec((1, wb, 1), lambda p, st, sb, se, sv, sf: (se[p], sb[p], 0)),
                      pl.BlockSpec((gt, D), lambda p, st, sb, se, sv, sf: (st[p], 0))],
            out_specs=pl.BlockSpec((wb, D), lambda p, st, sb, se, sv, sf: (sb[p], 0)),
        ),
        compiler_params=_cparams(("arbitrary",)),
        name="moe_ungroup",
    )(*plan["s"], dest_t.reshape(E, N, 1), combine[:, :E].T.reshape(E, N, 1), y_sorted)


def _project_weights(w_in):
    gla_w = 2 * GLA_HEADS * GLA_DK + 2 * GLA_HEADS * GLA_DV + GLA_RANK
    rw_w = 3 * RWKV_DIM + RWKV_W_RANK + RWKV_A_RANK + RWKV_G_RANK
    kvw = NSA_GROUPS * NSA_DH
    nsa_w = NSA_HEADS * NSA_DH + 6 * kvw + NSA_HEADS * 3
    o_rw = gla_w
    o_nsa = gla_w + rw_w
    o_gate = o_nsa + nsa_w
    big = jnp.concatenate([
        w_in[:, 0:gla_w - GLA_RANK],
        w_in[:, o_rw:o_rw + 3 * RWKV_DIM],
        w_in[:, o_nsa:o_nsa + NSA_HEADS * NSA_DH + 6 * kvw],
        w_in[:, o_gate:],
    ], axis=1).astype(BF16)
    D = w_in.shape[0]
    z = lambda n: jnp.zeros((D, n), w_in.dtype)
    small = jnp.concatenate([
        w_in[:, gla_w - GLA_RANK:gla_w], z(128 - GLA_RANK),
        w_in[:, o_rw + 3 * RWKV_DIM:o_rw + rw_w], z(256 - RWKV_G_RANK),
        w_in[:, o_gate - NSA_HEADS * 3:o_gate], z(128 - NSA_HEADS * 3),
    ], axis=1).astype(BF16)
    return big, small


def _mixer(h, batch, seq, w_in, gla_a2, gla_a_b, gla_norm, rwkv_mu, rwkv_w0, rwkv_w2, rwkv_a0, rwkv_a2, rwkv_g2,
           rwkv_k_k, rwkv_k_a, rwkv_r_k, rwkv_ln_w, rwkv_ln_b,
           nsa_pos_k, nsa_w1_k, nsa_w2_k, nsa_pos_v, nsa_w1_v, nsa_w2_v, p_merge, w_out, slopes):
    N = h.shape[0]
    w_big, w_small = _project_weights(w_in)
    tm_u = 2048 if N % 2048 == 0 else N
    U = matmul(h, w_big, BF16, tm_u, 512)
    Us = matmul(h, w_small, F32, tm_u, S_COLS)

    y_a = gla(U, Us, gla_a2, gla_a_b, gla_norm, batch=batch, seq=seq)

    r, k, v, kap, b, lw, gate = rwkv_prep(U, Us, rwkv_mu, rwkv_w0, rwkv_w2, rwkv_a0, rwkv_a2, rwkv_g2,
                                          rwkv_k_k, rwkv_k_a, seq=seq)
    y_b = rwkv_chunk(r, k, v, kap, b, lw, gate, rwkv_r_k.reshape(-1), rwkv_ln_w, rwkv_ln_b, batch=batch, seq=seq)

    kv6 = U[:, U_NSA_KV:U_NSA_KV + 6 * NSA_GROUPS * NSA_DH]
    kv6 = kv6.reshape(batch, seq, 6, NSA_GROUPS, NSA_DH).transpose(2, 0, 3, 1, 4)
    kvc = nsa_compress(kv6[0:2], jnp.stack([nsa_pos_k, nsa_pos_v]), jnp.stack([nsa_w1_k, nsa_w1_v]),
                       jnp.stack([nsa_w2_k, nsa_w2_v]))
    ncp = kvc.shape[3]
    ones_c = jnp.zeros((batch, NSA_GROUPS, 8, ncp), BF16).at[:, :, 0, :].set(1.0)
    vct = jnp.concatenate([kvc[1].transpose(0, 1, 3, 2), ones_c], axis=2)
    o_c, selbias = nsa_cmp(U, kvc[0], vct, slopes, batch=batch, seq=seq)
    n_slc = seq // NSA_SLC_LEN
    k_aug = jnp.concatenate([kv6[2], jnp.broadcast_to(sel_key_columns(seq), (batch, NSA_GROUPS, seq, n_slc + 128))],
                            axis=-1)
    ones_rows = jnp.zeros((batch, NSA_GROUPS, 8, seq), BF16).at[:, :, 0, :].set(1.0)
    vt_aug = jnp.concatenate([kv6[3].transpose(0, 1, 3, 2), ones_rows], axis=2)
    o_s = nsa_sel(U, selbias, k_aug, vt_aug, slopes, batch=batch, seq=seq)
    kw_pad = jnp.pad(kv6[4], ((0, 0), (0, 0), (NSA_WINDOW, 0), (0, 0)))
    vtw = jnp.concatenate([kv6[5].transpose(0, 1, 3, 2), ones_rows], axis=2)
    vtw_pad = jnp.pad(vtw, ((0, 0), (0, 0), (0, 0), (NSA_WINDOW, 0)))
    o_w = nsa_win(U, kw_pad, vtw_pad, slopes, batch=batch, seq=seq)
    y_c = nsa_combine(o_c, o_s, o_w, Us)

    merged = merge(y_a, y_b, y_c, U, p_merge.astype(BF16))
    return matmul(merged, w_out.astype(BF16), BF16, 1024 if N % 1024 == 0 else N, 1024)


def kernel(x, c, norm_mix, norm_ffn, ada_w, ada_b, w_in, gla_a2, gla_a_b, gla_norm, rwkv_mu, rwkv_w0, rwkv_w2, rwkv_a0, rwkv_a2, rwkv_g2, rwkv_k_k, rwkv_k_a, rwkv_r_k, rwkv_ln_w, rwkv_ln_b, nsa_pos_k, nsa_w1_k, nsa_w2_k, nsa_pos_v, nsa_w1_v, nsa_w2_v, p_merge, w_out, ffn_w1, ffn_w3, ffn_w2, moe_router, moe_w1, moe_w3, moe_w2, final_norm):
    B, T, D = x.shape
    depth = w_in.shape[0]
    N = B * T
    xs = x.reshape(N, D)
    ada = ada_all(c, ada_w, ada_b)
    slopes = jnp.exp2(-8.0 * jnp.arange(1, NSA_HEADS + 1, dtype=F32) / NSA_HEADS)
    y = None
    g_prev = None
    for l in range(depth):
        sh1, sc1, g1, sh2, sc2, g2 = jnp.split(ada[l], 6, axis=-1)
        if y is None:
            (h,) = resmod(xs, None, None, norm_mix[l], sh1, sc1, seq=T)
        else:
            xs, h = resmod(xs, y, g_prev, norm_mix[l], sh1, sc1, seq=T)
        y = _mixer(h, B, T, w_in[l], gla_a2[l], gla_a_b[l], gla_norm[l], rwkv_mu[l], rwkv_w0[l], rwkv_w2[l],
                   rwkv_a0[l], rwkv_a2[l], rwkv_g2[l], rwkv_k_k[l], rwkv_k_a[l], rwkv_r_k[l],
                   rwkv_ln_w[l], rwkv_ln_b[l], nsa_pos_k[l], nsa_w1_k[l], nsa_w2_k[l],
                   nsa_pos_v[l], nsa_w1_v[l], nsa_w2_v[l], p_merge[l], w_out[l], slopes)
        if l % 2 == 0:
            xs, h = resmod(xs, y, g1, norm_ffn[l], sh2, sc2, seq=T)
            y = ffn(h, ffn_w1[l // 2].astype(BF16), ffn_w3[l // 2].astype(BF16), ffn_w2[l // 2].astype(BF16))
        else:
            rt = jnp.zeros((D, 128), F32).at[:, :N_EXPERTS].set(moe_router[l // 2])
            xs, h, logits = resmod(xs, y, g1, norm_ffn[l], sh2, sc2, seq=T, router=rt)
            combine, selm = route(logits)
            y = moe(h, combine, selm, moe_w1[l // 2].astype(BF16), moe_w3[l // 2].astype(BF16),
                    moe_w2[l // 2].astype(BF16))
        g_prev = g2
    (out,) = resmod(xs, y, g_prev, final_norm, None, None, seq=T, final=True)
    return out.reshape(B, T, D)
```

```python
import functools

import numpy as np
import jax
import jax.numpy as jnp
from jax import lax
from jax.experimental import pallas as pl
from jax.experimental.pallas import tpu as pltpu

F32 = jnp.float32
BF16 = jnp.bfloat16
HI = lax.Precision.HIGHEST

V7X_VMEM_LIMIT_BYTES = 56 * 1024 * 1024

D_MODEL = 2048
NORM_EPS = 1e-6
NEG_BIG = -1e30

GLA_HEADS = 4
GLA_DK = 128
GLA_DV = 256
GLA_RANK = 16
GLA_NORMALIZER = 16.0
CHUNK = 64

RWKV_HEADS = 16
RWKV_N = 64
RWKV_DIM = RWKV_HEADS * RWKV_N
RWKV_W_RANK = 64
RWKV_A_RANK = 64
RWKV_G_RANK = 160
RWKV_LN_EPS = 64e-5
RWKV_HB = 4

NSA_HEADS = 16
NSA_GROUPS = 4
NSA_HPG = 4
NSA_DH = 64
NSA_CMP_LEN = 32
NSA_CMP_STRIDE = 16
NSA_CMP_HIDDEN = 128
NSA_SLC_LEN = 64
NSA_N_SEL = 16
NSA_WINDOW = 512
NSA_QB = 256
CMP_QB = 512
SEL_MASK_BIAS = 131072.0

BRANCH_DIM = 1024
D_FF = 5632
N_EXPERTS = 8

U_GLA_Q, U_GLA_K, U_GLA_V, U_GLA_G = 0, 512, 1024, 2048
U_RWKV_R, U_RWKV_K, U_RWKV_V = 3072, 4096, 5120
U_NSA_Q = 6144
U_NSA_KV = 7168
U_GATE = 8704
U_COLS = U_GATE + 3 * D_MODEL
S_GLA_A = 0
S_RWKV_WA = 128
S_RWKV_G = 256
S_NSA_GATE = 512
S_COLS = 640


def _cparams(sem, vmem=V7X_VMEM_LIMIT_BYTES):
    return pltpu.CompilerParams(dimension_semantics=sem, vmem_limit_bytes=vmem)


def _sigmoid(x):
    return 1.0 / (1.0 + jnp.exp(-x))


def _softplus(x):
    return jnp.maximum(x, 0.0) + jnp.log(1.0 + jnp.exp(-jnp.abs(x)))


def _dot(a, b):
    return jnp.dot(a, b, preferred_element_type=F32)


def _dot_nt(a, b):
    return lax.dot_general(a, b, (((1,), (1,)), ((), ())), preferred_element_type=F32)


def _dot_tn(a, b):
    return lax.dot_general(a, b, (((0,), (0,)), ((), ())), preferred_element_type=F32)


def _dot_hi(a, b):
    return jnp.dot(a, b, preferred_element_type=F32, precision=HI)


def _dot_x3(a, b):
    ah = a.astype(BF16)
    al = (a - ah.astype(F32)).astype(BF16)
    bh = b.astype(BF16)
    bl = (b - bh.astype(F32)).astype(BF16)
    return _dot(ah, bh) + _dot(ah, bl) + _dot(al, bh)


def _dot_sel(w, x):
    wb = w.astype(BF16)
    hi = x.astype(BF16)
    r1 = x - hi.astype(F32)
    mid = r1.astype(BF16)
    lo = (r1 - mid.astype(F32)).astype(BF16)
    return _dot(wb, hi) + _dot(wb, mid) + _dot(wb, lo)


def _dot_split(x, w):
    hi = x.astype(BF16)
    lo = (x - hi.astype(F32)).astype(BF16)
    return _dot(hi, w) + _dot(lo, w)


def _ada_kernel(c_ref, w_ref, b_ref, o_ref):
    c = c_ref[...]
    o_ref[0] = _dot_hi(c * _sigmoid(c), w_ref[0]) + b_ref[0]


def ada_all(c, ada_w, ada_b):
    L, D, N6 = ada_w.shape
    B = c.shape[0]
    tn = 1024
    return pl.pallas_call(
        _ada_kernel,
        out_shape=jax.ShapeDtypeStruct((L, B, N6), F32),
        grid=(L, N6 // tn),
        in_specs=[
            pl.BlockSpec((B, D), lambda l, j: (0, 0)),
            pl.BlockSpec((1, D, tn), lambda l, j: (l, 0, j)),
            pl.BlockSpec((1, 1, tn), lambda l, j: (l, 0, j)),
        ],
        out_specs=pl.BlockSpec((1, B, tn), lambda l, j: (l, 0, j)),
        compiler_params=_cparams(("parallel", "parallel")),
        name="ada",
    )(c, ada_w, ada_b.reshape(L, 1, N6))


def _resmod_kernel(*refs, has_res, final, router):
    it = iter(refs)
    x_ref = next(it)
    y_ref = next(it) if has_res else None
    g_ref = next(it) if has_res else None
    ng_ref = next(it)
    sh_ref = None if final else next(it)
    sc_ref = None if final else next(it)
    rt_ref = next(it) if router else None
    xo_ref = next(it) if (has_res and not final) else None
    h_ref = next(it)
    lg_ref = next(it) if router else None

    x = x_ref[...]
    if has_res:
        x = x + g_ref[0] * y_ref[...].astype(F32)
        if xo_ref is not None:
            xo_ref[...] = x
    ms = jnp.mean(x * x, axis=-1, keepdims=True)
    h = x * lax.rsqrt(ms + NORM_EPS) * ng_ref[...]
    if not final:
        h = h * (1.0 + sc_ref[0]) + sh_ref[0]
    h_ref[...] = h.astype(h_ref.dtype)
    if router:
        lg_ref[...] = _dot_hi(h, rt_ref[...])


def resmod(x, y, g, norm_g, shift, scale, *, seq, router=None, final=False):
    N, D = x.shape
    tm = 512
    spb = seq // tm
    has_res = y is not None
    row = lambda i: (i, 0)
    per_b = lambda i: (i // spb, 0, 0)
    ins, specs = [x], [pl.BlockSpec((tm, D), row)]
    if has_res:
        ins += [y, g.reshape(-1, 1, D)]
        specs += [pl.BlockSpec((tm, D), row), pl.BlockSpec((1, 1, D), per_b)]
    ins.append(norm_g.reshape(1, D))
    specs.append(pl.BlockSpec((1, D), lambda i: (0, 0)))
    if not final:
        ins += [shift.reshape(-1, 1, D), scale.reshape(-1, 1, D)]
        specs += [pl.BlockSpec((1, 1, D), per_b), pl.BlockSpec((1, 1, D), per_b)]
    if router is not None:
        ins.append(router)
        specs.append(pl.BlockSpec(router.shape, lambda i: (0, 0)))
    outs, ospecs = [], []
    if has_res and not final:
        outs.append(jax.ShapeDtypeStruct((N, D), F32))
        ospecs.append(pl.BlockSpec((tm, D), row))
    outs.append(jax.ShapeDtypeStruct((N, D), F32 if final else BF16))
    ospecs.append(pl.BlockSpec((tm, D), row))
    if router is not None:
        outs.append(jax.ShapeDtypeStruct((N, 128), F32))
        ospecs.append(pl.BlockSpec((tm, 128), row))
    res = pl.pallas_call(
        functools.partial(_resmod_kernel, has_res=has_res, final=final, router=router is not None),
        out_shape=tuple(outs),
        grid=(N // tm,),
        in_specs=specs,
        out_specs=tuple(ospecs),
        compiler_params=_cparams(("parallel",)),
        name="resmod",
    )(*ins)
    return res


def _mm_kernel(x_ref, w_ref, o_ref):
    o_ref[...] = _dot(x_ref[...], w_ref[...]).astype(o_ref.dtype)


def matmul(x, w, out_dtype, tm, tn):
    M, K = x.shape
    N = w.shape[1]
    return pl.pallas_call(
        _mm_kernel,
        out_shape=jax.ShapeDtypeStruct((M, N), out_dtype),
        grid=(M // tm, N // tn),
        in_specs=[pl.BlockSpec((tm, K), lambda i, j: (i, 0)), pl.BlockSpec((K, tn), lambda i, j: (0, j))],
        out_specs=pl.BlockSpec((tm, tn), lambda i, j: (i, j)),
        compiler_params=_cparams(("parallel", "parallel")),
        name="matmul",
    )(x, w)


GLA_TC = 512


def _gla_kernel(q_ref, k_ref, v_ref, g_ref, alr_ref, a2_ref, ab_ref, ng_ref, o_ref, st_ref, la_ref):
    @pl.when(pl.program_id(1) == 0)
    def _():
        st_ref[...] = jnp.zeros_like(st_ref)

    C, DK, DV = CHUNK, GLA_DK, GLA_DV
    la = _dot_x3(alr_ref[...], a2_ref[...]) + ab_ref[...]
    la_ref[...] = -_softplus(-la) / GLA_NORMALIZER
    ri = lax.broadcasted_iota(jnp.int32, (C, C), 0)
    ci = lax.broadcasted_iota(jnp.int32, (C, C), 1)
    causal = ri >= ci
    tril = causal.astype(F32)
    H = range(GLA_HEADS)

    def body(c, carry):
        sl = pl.ds(pl.multiple_of(c * C, C), C)
        bc_all = _dot_sel(tril, la_ref[sl, :])
        bcs = [bc_all[:, h * DK:(h + 1) * DK] for h in H]
        bls = [bc[C - 1:C, :] for bc in bcs]
        ks = [k_ref[sl, h * DK:(h + 1) * DK].astype(F32) for h in H]
        vs = [v_ref[sl, h * DV:(h + 1) * DV] for h in H]
        qds = [(q_ref[sl, h * DK:(h + 1) * DK].astype(F32) * (DK ** -0.5) * jnp.exp(bc)).astype(BF16)
               for h, bc in zip(H, bcs)]
        kds = [(k * jnp.exp(-bc)).astype(BF16) for k, bc in zip(ks, bcs)]
        kls = [(k * jnp.exp(bl - bc)).astype(BF16) for k, bl, bc in zip(ks, bls, bcs)]
        atts = [jnp.where(causal, _dot_nt(qd, kd), 0.0).astype(BF16) for qd, kd in zip(qds, kds)]
        sts = [st_ref[h] for h in H]
        os_ = [_dot(att, v) + _dot_nt(qd, st.astype(BF16)) for att, v, qd, st in zip(atts, vs, qds, sts)]
        for h in H:
            st_ref[h] = sts[h] * jnp.exp(bls[h]) + _dot_tn(vs[h], kls[h])
        for h in H:
            o = os_[h]
            o = o * lax.rsqrt(jnp.mean(o * o, axis=-1, keepdims=True) + NORM_EPS) * ng_ref[...]
            gg = g_ref[sl, h * DV:(h + 1) * DV].astype(F32)
            o_ref[sl, h * DV:(h + 1) * DV] = (o * (gg * _sigmoid(gg))).astype(o_ref.dtype)
        return carry

    lax.fori_loop(0, GLA_TC // C, body, 0)


def gla(U, Us, a2, a_b, norm_g, *, batch, seq):
    N = U.shape[0]
    nt = seq // GLA_TC
    HK, HV = GLA_HEADS * GLA_DK, GLA_HEADS * GLA_DV
    a2p = jnp.zeros((128, HK), F32).at[:GLA_RANK].set(a2)
    row = lambda b, i: b * nt + i
    return pl.pallas_call(
        _gla_kernel,
        out_shape=jax.ShapeDtypeStruct((N, HV), BF16),
        grid=(batch, nt),
        in_specs=[
            pl.BlockSpec((GLA_TC, HK), lambda b, i: (row(b, i), U_GLA_Q // HK)),
            pl.BlockSpec((GLA_TC, HK), lambda b, i: (row(b, i), U_GLA_K // HK)),
            pl.BlockSpec((GLA_TC, HV), lambda b, i: (row(b, i), U_GLA_V // HV)),
            pl.BlockSpec((GLA_TC, HV), lambda b, i: (row(b, i), U_GLA_G // HV)),
            pl.BlockSpec((GLA_TC, 128), lambda b, i: (row(b, i), S_GLA_A // 128)),
            pl.BlockSpec((128, HK), lambda b, i: (0, 0)),
            pl.BlockSpec((1, HK), lambda b, i: (0, 0)),
            pl.BlockSpec((1, GLA_DV), lambda b, i: (0, 0)),
        ],
        out_specs=pl.BlockSpec((GLA_TC, HV), lambda b, i: (row(b, i), 0)),
        scratch_shapes=[pltpu.VMEM((GLA_HEADS, GLA_DV, GLA_DK), F32), pltpu.VMEM((GLA_TC, HK), F32)],
        compiler_params=_cparams(("parallel", "arbitrary")),
        name="gla",
    )(U, U, U, U, Us, a2p, a_b.reshape(1, -1), norm_g.reshape(1, -1))


RWKV_TM = 256


def _seg_ones(n=256, seg=RWKV_N):
    i = np.arange(n)
    return jnp.asarray((i[:, None] // seg == i[None, :] // seg).astype(np.float32), BF16)


def _seg_sum(x, bd):
    outs = [_dot_split(x[:, s:s + 256], bd) for s in range(0, x.shape[1], 256)]
    return outs[0] if len(outs) == 1 else jnp.concatenate(outs, axis=1)


def _shift_lerp(u_ref, p_ref, mu, first):
    u = u_ref[...].astype(F32)
    prev_last = jnp.where(first, 0.0, p_ref[7:8, :].astype(F32))
    rolled = pltpu.roll(u, 1, 0)
    is_row0 = lax.broadcasted_iota(jnp.int32, u.shape, 0) == 0
    sh = jnp.where(is_row0, prev_last, rolled)
    return u + (sh - u) * mu


def _rwkv_prep_kernel(r_ref, k_ref, v_ref, wa_ref, gl_ref, rp_ref, kp_ref, vp_ref, wap_ref, glp_ref,
                      mur_ref, muk_ref, muv_ref, muwa_ref, mug_ref, w0_ref, w2_ref, a0_ref, a2_ref, g2_ref,
                      kk_ref, ka_ref, bd_ref,
                      ro_ref, ko_ref, vo_ref, kap_ref, bo_ref, lw_ref, go_ref, *, spb):
    first = (pl.program_id(0) % spb) == 0
    r = _shift_lerp(r_ref, rp_ref, mur_ref[...], first)
    k = _shift_lerp(k_ref, kp_ref, muk_ref[...], first)
    v = _shift_lerp(v_ref, vp_ref, muv_ref[...], first)
    wa = _shift_lerp(wa_ref, wap_ref, muwa_ref[...], first)
    gl = _shift_lerp(gl_ref, glp_ref, mug_ref[...], first)
    w_log = -_softplus(-(w0_ref[...] + _dot_x3(jnp.tanh(wa), w2_ref[...]))) - 0.5
    lw_ref[...] = -jnp.exp(w_log)
    a = _sigmoid(a0_ref[...] + _dot_x3(wa, a2_ref[...]))
    go_ref[...] = _dot(_sigmoid(gl).astype(BF16), g2_ref[...]).astype(go_ref.dtype)
    kk = k * kk_ref[...]
    nrm = jnp.sqrt(_seg_sum(kk * kk, bd_ref[...]))
    kk = kk / jnp.maximum(nrm, 1e-12)
    ro_ref[...] = r.astype(ro_ref.dtype)
    ko_ref[...] = (k * (1.0 + (a - 1.0) * ka_ref[...])).astype(ko_ref.dtype)
    vo_ref[...] = v.astype(vo_ref.dtype)
    kap_ref[...] = kk.astype(kap_ref.dtype)
    bo_ref[...] = (kk * a).astype(bo_ref.dtype)


def rwkv_prep(U, Us, mu, w0, w2, a0, a2, g2, k_k, k_a, *, seq):
    N = U.shape[0]
    tm = RWKV_TM
    spb = seq // tm
    R = RWKV_DIM
    cur = lambda cb: (lambda i: (i, cb))
    prv = lambda cb: (lambda i: (jnp.maximum(i * (tm // 8) - 1, 0), cb))
    mu_r, mu_k, mu_v = mu[:R], mu[R:2 * R], mu[2 * R:3 * R]
    mu_wa = mu[3 * R:3 * R + 128]
    mu_g = jnp.zeros((256,), F32).at[:RWKV_G_RANK].set(mu[3 * R + 128:])
    w2p = jnp.zeros((128, R), F32).at[:RWKV_W_RANK].set(w2)
    a2p = jnp.zeros((128, R), F32).at[RWKV_W_RANK:].set(a2)
    g2p = jnp.zeros((256, R), BF16).at[:RWKV_G_RANK].set(g2.astype(BF16))
    vec = lambda a: a.reshape(1, -1)
    full = lambda a: pl.BlockSpec(a.shape, lambda i: (0, 0))
    params = [vec(mu_r), vec(mu_k), vec(mu_v), vec(mu_wa), vec(mu_g), vec(w0), w2p, vec(a0), a2p, g2p,
              vec(k_k), vec(k_a), _seg_ones()]
    in_specs = [
        pl.BlockSpec((tm, R), cur(U_RWKV_R // R)), pl.BlockSpec((tm, R), cur(U_RWKV_K // R)),
        pl.BlockSpec((tm, R), cur(U_RWKV_V // R)),
        pl.BlockSpec((tm, 128), cur(S_RWKV_WA // 128)), pl.BlockSpec((tm, 256), cur(S_RWKV_G // 256)),
        pl.BlockSpec((8, R), prv(U_RWKV_R // R)), pl.BlockSpec((8, R), prv(U_RWKV_K // R)),
        pl.BlockSpec((8, R), prv(U_RWKV_V // R)),
        pl.BlockSpec((8, 128), prv(S_RWKV_WA // 128)), pl.BlockSpec((8, 256), prv(S_RWKV_G // 256)),
    ] + [full(p) for p in params]
    out = lambda dt: jax.ShapeDtypeStruct((N, R), dt)
    ospec = pl.BlockSpec((tm, R), lambda i: (i, 0))
    return pl.pallas_call(
        functools.partial(_rwkv_prep_kernel, spb=spb),
        out_shape=(out(BF16), out(BF16), out(BF16), out(BF16), out(BF16), out(F32), out(BF16)),
        grid=(N // tm,),
        in_specs=in_specs,
        out_specs=(ospec,) * 7,
        compiler_params=_cparams(("parallel",)),
        name="rwkv_prep",
    )(U, U, U, Us, Us, U, U, U, Us, Us, *params)


RWKV_TC = 512
RWKV_W = RWKV_HB * RWKV_N


def _rwkv_masks():
    W, C = RWKV_W, CHUNK
    i = np.arange(W)
    same = i[:, None] // C == i[None, :] // C
    m_bd = same.astype(np.float32)
    low_s = (same & (i[:, None] % C > i[None, :] % C)).astype(np.float32)
    low_i = (same & (i[:, None] % C >= i[None, :] % C)).astype(np.float32)
    tril = np.tril(np.ones((C, C), np.float32))
    return (jnp.asarray(m_bd), jnp.asarray(low_s), jnp.asarray(low_i), jnp.asarray(tril),
            jnp.asarray(np.eye(W, dtype=np.float32)))


def _tile4(x):
    return jnp.concatenate([x] * RWKV_HB, axis=0)


def _rwkv_chunk_kernel(r_ref, k_ref, v_ref, kap_ref, b_ref, lw_ref, g_ref,
                       mbd_ref, lows_ref, lowi_ref, tril_ref, eye_ref, bd_ref,
                       rk_ref, lnw_ref, lnb_ref, o_ref, st_ref):
    @pl.when(pl.program_id(1) == 0)
    def _():
        st_ref[...] = jnp.zeros_like(st_ref)

    C, W = CHUNK, RWKV_W
    n_batch = r_ref.shape[0]
    n_groups = r_ref.shape[2] // W
    m_bd = mbd_ref[...]
    low_s = lows_ref[...]
    low_i = lowi_ref[...]
    tril = tril_ref[...]
    eye = eye_ref[...]
    bd = bd_ref[...]

    def expand(x):
        return (_tile4(x) * m_bd).astype(BF16)

    def body(c, carry):
        sl = pl.ds(pl.multiple_of(c * C, C), C)
        lw_alls = [lw_ref[bi, sl, :] for bi in range(n_batch)]
        cum_alls = [_dot_sel(tril, lw) for lw in lw_alls]
        chains = [(bi, slice(gi * W, (gi + 1) * W)) for bi in range(n_batch) for gi in range(n_groups)]
        G = range(len(chains))
        lns = [ln for _, ln in chains]
        cums = [cum_alls[bi][:, ln] for bi, ln in chains]
        clasts = [cum[C - 1:C, :] for cum in cums]
        rs = [r_ref[bi, sl, ln].astype(F32) for bi, ln in chains]
        ks = [k_ref[bi, sl, ln].astype(F32) for bi, ln in chains]
        vs = [v_ref[bi, sl, ln].astype(F32) for bi, ln in chains]
        bs = [b_ref[bi, sl, ln].astype(F32) for bi, ln in chains]
        p_invs = [jnp.exp(-cum) for cum in cums]
        xes = [jnp.concatenate([expand(kap_ref[bi, sl, ln].astype(F32) * jnp.exp(cum - lw_alls[bi][:, ln])),
                                expand(r * jnp.exp(cum))], axis=0)
               for (bi, ln), cum, r in zip(chains, cums, rs)]
        hes = [jnp.concatenate([expand(b * pi), expand(k * pi)], axis=0) for b, k, pi in zip(bs, ks, p_invs)]
        scs = [_dot_nt(xe, he) for xe, he in zip(xes, hes)]
        ams = [sc[:W, :W] * low_s for sc in scs]
        tinvs = [eye - a_m for a_m in ams]
        ambs = [a_m.astype(BF16) for a_m in ams]
        pws = [_dot(ab, ab) for ab in ambs]
        n_lev = int(np.log2(C))
        for lev in range(1, n_lev):
            pwbs = [pw.astype(BF16) for pw in pws]
            if lev < n_lev - 1:
                outs = [_dot(jnp.concatenate([pwb, tinv.astype(BF16)], axis=0), pwb)
                        for pwb, tinv in zip(pwbs, tinvs)]
                pws = [o[:W] for o in outs]
                tinvs = [tinv + o[W:] for tinv, o in zip(tinvs, outs)]
            else:
                tinvs = [tinv + _dot(tinv.astype(BF16), pwb) for tinv, pwb in zip(tinvs, pwbs)]
        ves = [expand(v) for v in vs]
        bqv = [_dot(jnp.concatenate([(sc[:W, W:] * low_s).astype(BF16), (sc[W:, W:] * low_i).astype(BF16)], axis=0),
                    ve) for sc, ve in zip(scs, ves)]
        bmv = [o[:W] for o in bqv]
        qkv = [o[W:] for o in bqv]
        sts = [st_ref[gi] for gi in G]
        xss = [_dot_nt(xe, st.astype(BF16)) for xe, st in zip(xes, sts)]
        us = [_dot(tinv.astype(BF16), (xs[:W] + bv).astype(BF16)) for tinv, xs, bv in zip(tinvs, xss, bmv)]
        kbs = [jnp.concatenate([expand(k * jnp.exp(cl - cum)), expand(b * jnp.exp(cl - cum))], axis=0)
               for k, b, cl, cum in zip(ks, bs, clasts, cums)]
        for gi in G:
            vu = jnp.concatenate([ves[gi], (-us[gi]).astype(BF16)], axis=0)
            st_ref[gi] = sts[gi] * jnp.exp(clasts[gi]) + _dot_tn(vu, kbs[gi]) * m_bd
        for gi in G:
            ln = lns[gi]
            qb = (scs[gi][W:, :W] * low_i).astype(BF16)
            y_e = xss[gi][W:] + qkv[gi] - _dot(qb, us[gi].astype(BF16))
            y = y_e[0:C] + y_e[C:2 * C] + y_e[2 * C:3 * C] + y_e[3 * C:4 * C]
            mu = _seg_sum(y, bd) * (1.0 / RWKV_N)
            yc = y - mu
            var = _seg_sum(yc * yc, bd) * (1.0 / RWKV_N)
            yn = yc * lax.rsqrt(var + RWKV_LN_EPS) * lnw_ref[:, ln] + lnb_ref[:, ln]
            bonus = _seg_sum(rs[gi] * ks[gi] * rk_ref[:, ln], bd) * vs[gi]
            bi = chains[gi][0]
            o_ref[bi, sl, ln] = ((yn + bonus) * g_ref[bi, sl, ln].astype(F32)).astype(o_ref.dtype)
        return carry

    lax.fori_loop(0, RWKV_TC // C, body, 0)


RWKV_NB = 1


def rwkv_chunk(r, k, v, kap, b, lw, gate, r_k, ln_w, ln_b, *, batch, seq):
    N, R = r.shape
    nt = seq // RWKV_TC
    W = RWKV_W
    nb = RWKV_NB if batch % RWKV_NB == 0 else 1
    blk = pl.BlockSpec((nb, RWKV_TC, R), lambda bb, i: (bb, i, 0))
    masks = _rwkv_masks() + (_seg_ones(),)
    full = lambda a: pl.BlockSpec(a.shape, lambda bb, i: (0, 0))
    pvec = pl.BlockSpec((1, R), lambda bb, i: (0, 0))
    seqs = [a.reshape(batch, seq, R) for a in (r, k, v, kap, b, lw, gate)]
    out = pl.pallas_call(
        _rwkv_chunk_kernel,
        out_shape=jax.ShapeDtypeStruct((batch, seq, R), BF16),
        grid=(batch // nb, nt),
        in_specs=[blk] * 7 + [full(m) for m in masks] + [pvec] * 3,
        out_specs=blk,
        scratch_shapes=[pltpu.VMEM((nb * (R // W), W, W), F32)],
        compiler_params=_cparams(("parallel", "arbitrary")),
        name="rwkv_chunk",
    )(*seqs, *masks, r_k.reshape(1, R), ln_w.reshape(1, R), ln_b.reshape(1, R))
    return out.reshape(N, R)


def _gelu_tanh(x):
    return 0.5 * x * (1.0 + jnp.tanh(np.sqrt(2.0 / np.pi) * (x + 0.044715 * (x * x * x))))


def _nsa_compress_kernel(x_ref, pos_ref, w1_ref, w2_ref, o_ref):
    x = x_ref[0, 0]
    w1 = w1_ref[0]
    half = w1.shape[0] // 2
    nrow = x.shape[0]
    ha = _dot(x, w1[:half])
    hb = _dot(x, w1[half:])
    h = ha + pltpu.roll(hb, nrow - 1, 0)
    pb = _dot(pos_ref[0], w1)
    h = _gelu_tanh(h + pb[0:1, :])
    o_ref[0, 0, 0] = _dot(h.astype(BF16), w2_ref[0]).astype(o_ref.dtype)


def nsa_compress(kv_cmp, pos, w1, w2):
    two, B, G, T, dh = kv_cmp.shape
    nr = T // NSA_CMP_STRIDE
    x = kv_cmp.reshape(two * B, G, nr, NSA_CMP_STRIDE * dh)
    posf = jnp.broadcast_to(pos.reshape(two, 1, NSA_CMP_LEN * dh), (two, 8, NSA_CMP_LEN * dh)).astype(BF16)
    return pl.pallas_call(
        _nsa_compress_kernel,
        out_shape=jax.ShapeDtypeStruct((two, B, G, nr, dh), BF16),
        grid=(two, B, G),
        in_specs=[
            pl.BlockSpec((1, 1, nr, NSA_CMP_STRIDE * dh), lambda s, b, g: (s * B + b, g, 0, 0)),
            pl.BlockSpec((1, 8, NSA_CMP_LEN * dh), lambda s, b, g: (s, 0, 0)),
            pl.BlockSpec((1, NSA_CMP_LEN * dh, NSA_CMP_HIDDEN), lambda s, b, g: (s, 0, 0)),
            pl.BlockSpec((1, NSA_CMP_HIDDEN, dh), lambda s, b, g: (s, 0, 0)),
        ],
        out_specs=pl.BlockSpec((1, 1, 1, nr, dh), lambda s, b, g: (s, b, g, 0, 0)),
        compiler_params=_cparams(("parallel", "parallel", "parallel")),
        name="nsa_compress",
    )(x, posf, w1.astype(BF16), w2.astype(BF16))


def _nsa_cmp_kernel(q_ref, kc_ref, vct_ref, ov_ref, cb_ref, oc_ref, sb_ref, qs_ref, *, n_slc, n_sel):
    qi = pl.program_id(2)
    QB = CMP_QB
    ncp = kc_ref.shape[2]
    for h in range(NSA_HPG):
        qs_ref[h * QB:(h + 1) * QB, :] = q_ref[:, h * NSA_DH:(h + 1) * NSA_DH] * (NSA_DH ** -0.5)
    s = _dot_nt(kc_ref[0, 0], qs_ref[...])
    vct = vct_ref[0, 0]
    psum = jnp.zeros((ncp, QB), F32)
    for h in range(NSA_HPG):
        sh = s[:, h * QB:(h + 1) * QB] + cb_ref[h]
        m = jnp.maximum(jnp.max(sh, axis=0, keepdims=True), -1e20)
        e = jnp.exp(sh - m)
        acc = _dot(vct, e.astype(BF16))
        inv_l = 1.0 / jnp.maximum(acc[NSA_DH:NSA_DH + 1], 1e-30)
        oc_ref[:, h * NSA_DH:(h + 1) * NSA_DH] = (acc[0:NSA_DH] * inv_l).T.astype(oc_ref.dtype)
        psum = psum + e * inv_l
    imp = _dot_sel(ov_ref[...], psum)
    j = lax.broadcasted_iota(jnp.int32, (n_slc, QB), 0)
    tt = qi * QB + lax.broadcasted_iota(jnp.int32, (n_slc, QB), 1)
    cur = jnp.right_shift(tt, 6)
    forced = (j == 0) | (j == cur) | (j == cur - 1)
    cand = (j >= 1) & (j <= cur - 2)
    n_free = n_sel - 3

    def emit(sel):
        sb_ref[0, 0] = jnp.where(sel, 0.0, -SEL_MASK_BIAS).T.astype(sb_ref.dtype)

    last_cur = (qi * QB + QB - 1) // NSA_SLC_LEN

    @pl.when(last_cur - 2 <= n_free)
    def _():
        emit(forced | cand)

    @pl.when(last_cur - 2 > n_free)
    def _():
        cur_row = cur[0:1, :]
        rank = jnp.zeros((n_slc, QB), jnp.int32)
        for jp in range(1, n_slc):
            row = imp[jp:jp + 1, :]
            ahead = (row > imp) | ((row == imp) & (j > jp))
            rank = rank + jnp.where(ahead & (jp <= cur_row - 2), 1, 0)
        emit(forced | (cand & (rank < n_free)))


def nsa_cmp(U, kc, vct, slopes, *, batch, seq):
    N = U.shape[0]
    QB = CMP_QB
    nq = seq // QB
    ncp = kc.shape[2]
    n_slc = seq // NSA_SLC_LEN
    n_sel = min(NSA_N_SEL, n_slc)
    nn = np.arange(ncp)
    jj = np.arange(n_slc)
    ov = ((nn[None, :] * NSA_CMP_STRIDE + NSA_CMP_LEN - 1 >= jj[:, None] * NSA_SLC_LEN)
          & (nn[None, :] * NSA_CMP_STRIDE <= jj[:, None] * NSA_SLC_LEN + NSA_SLC_LEN - 1)
          & (nn[None, :] < ncp - 1)).astype(np.float32)
    tt = np.arange(seq)
    ended = jnp.asarray(nn[:, None] * NSA_CMP_STRIDE + NSA_CMP_LEN - 1 <= tt[None, :])
    adist = np.abs(tt[None, :] - (nn[:, None] * NSA_CMP_STRIDE + (NSA_CMP_LEN - 1) / 2.0)).astype(np.float32)
    cbias = jnp.where(ended[None], -slopes[:, None, None] * jnp.asarray(adist)[None], NEG_BIG)
    G = NSA_GROUPS
    W = NSA_HPG * NSA_DH
    return pl.pallas_call(
        functools.partial(_nsa_cmp_kernel, n_slc=n_slc, n_sel=n_sel),
        out_shape=(jax.ShapeDtypeStruct((N, NSA_HEADS * NSA_DH), BF16),
                   jax.ShapeDtypeStruct((batch, G, seq, n_slc), BF16)),
        grid=(batch, G, nq),
        in_specs=[
            pl.BlockSpec((QB, W), lambda b, g, i: (b * nq + i, U_NSA_Q // W + g)),
            pl.BlockSpec((1, 1, ncp, NSA_DH), lambda b, g, i: (b, g, 0, 0)),
            pl.BlockSpec((1, 1, NSA_DH + 8, ncp), lambda b, g, i: (b, g, 0, 0)),
            pl.BlockSpec((n_slc, ncp), lambda b, g, i: (0, 0)),
            pl.BlockSpec((NSA_HPG, ncp, QB), lambda b, g, i: (g, 0, i)),
        ],
        out_specs=(
            pl.BlockSpec((QB, W), lambda b, g, i: (b * nq + i, g)),
            pl.BlockSpec((1, 1, QB, n_slc), lambda b, g, i: (b, g, i, 0)),
        ),
        scratch_shapes=[pltpu.VMEM((NSA_HPG * QB, NSA_DH), BF16)],
        compiler_params=_cparams(("parallel", "parallel", "arbitrary")),
        name="nsa_cmp",
    )(U, kc, vct, jnp.asarray(ov), cbias)


def _nsa_win_kernel(q_ref, k0_ref, k1_ref, k2_ref, v0_ref, v1_ref, v2_ref, bias_ref, o_ref, qs_ref):
    qi = pl.program_id(2)
    QB = NSA_QB
    KW = NSA_WINDOW + QB
    for h in range(NSA_HPG):
        qs_ref[h * QB:(h + 1) * QB, :] = q_ref[:, h * NSA_DH:(h + 1) * NSA_DH] * (NSA_DH ** -0.5)
    k = jnp.concatenate([k0_ref[0, 0], k1_ref[0, 0], k2_ref[0, 0]], axis=0)
    vt = jnp.concatenate([v0_ref[0, 0], v1_ref[0, 0], v2_ref[0, 0]], axis=1)
    s = _dot_nt(k, qs_ref[...])

    def finish(before_start):
        for h in range(NSA_HPG):
            cols = slice(h * QB, (h + 1) * QB)
            sh = s[:, cols] + bias_ref[h]
            if before_start:
                r = lax.broadcasted_iota(jnp.int32, (KW, QB), 0)
                sh = jnp.where(r >= NSA_WINDOW - qi * QB, sh, NEG_BIG)
            m = jnp.max(sh, axis=0, keepdims=True)
            e = jnp.exp(sh - m)
            acc = _dot(vt, e.astype(BF16))
            o = acc[0:NSA_DH] / acc[NSA_DH:NSA_DH + 1]
            o_ref[:, h * NSA_DH:(h + 1) * NSA_DH] = o.T.astype(o_ref.dtype)

    @pl.when(qi * QB < NSA_WINDOW)
    def _():
        finish(True)

    @pl.when(qi * QB >= NSA_WINDOW)
    def _():
        finish(False)


def nsa_win(U, kw_pad, vtw_pad, slopes, *, batch, seq):
    N = U.shape[0]
    QB = NSA_QB
    KW = NSA_WINDOW + QB
    nq = seq // QB
    G = NSA_GROUPS
    W = NSA_HPG * NSA_DH
    dist = (np.arange(QB)[None, :] - np.arange(KW)[:, None] + NSA_WINDOW).astype(np.float32)
    inside = jnp.asarray((dist >= 0) & (dist < NSA_WINDOW))
    bias = jnp.where(inside[None], -slopes[:, None, None] * jnp.asarray(dist)[None], NEG_BIG)
    kb = lambda off: pl.BlockSpec((1, 1, QB, NSA_DH), lambda b, g, i: (b, g, i + off, 0))
    vb = lambda off: pl.BlockSpec((1, 1, NSA_DH + 8, QB), lambda b, g, i: (b, g, 0, i + off))
    return pl.pallas_call(
        _nsa_win_kernel,
        out_shape=jax.ShapeDtypeStruct((N, NSA_HEADS * NSA_DH), BF16),
        grid=(batch, G, nq),
        in_specs=[pl.BlockSpec((QB, W), lambda b, g, i: (b * nq + i, U_NSA_Q // W + g)),
                  kb(0), kb(1), kb(2), vb(0), vb(1), vb(2),
                  pl.BlockSpec((NSA_HPG, KW, QB), lambda b, g, i: (g, 0, 0))],
        out_specs=pl.BlockSpec((QB, W), lambda b, g, i: (b * nq + i, g)),
        scratch_shapes=[pltpu.VMEM((NSA_HPG * QB, NSA_DH), BF16)],
        compiler_params=_cparams(("parallel", "parallel", "arbitrary")),
        name="nsa_win",
    )(U, kw_pad, kw_pad, kw_pad, vtw_pad, vtw_pad, vtw_pad, bias)


SEL_QB = 512
SEL_KV = 512


def _sel_pairs(seq):
    qs, ks = [], []
    for qi in range(seq // SEL_QB):
        for kj in range((qi * SEL_QB) // SEL_KV + 1):
            qs.append(qi)
            ks.append(kj)
    return np.asarray(qs, np.int32), np.asarray(ks, np.int32)


def _nsa_sel_kernel(qi_ref, kj_ref, slope_ref, q_ref, sb_ref, qx_ref, ka_ref, vt_ref, o_ref,
                    qa_ref, m_ref, acc_ref):
    g = pl.program_id(1)
    p = pl.program_id(2)
    qi = qi_ref[p]
    kj = kj_ref[p]
    QB, KV = SEL_QB, SEL_KV
    nblk = sb_ref.shape[3]
    base = NSA_DH + nblk

    @pl.when(kj == 0)
    def _():
        for h in range(NSA_HPG):
            rows = slice(h * QB, (h + 1) * QB)
            qa_ref[rows, 0:NSA_DH] = q_ref[:, h * NSA_DH:(h + 1) * NSA_DH] * (NSA_DH ** -0.5)
            qa_ref[rows, NSA_DH:base] = sb_ref[0, 0]
            qa_ref[rows, base:] = jnp.broadcast_to(qx_ref[0, h:h + 1, :], (QB, qa_ref.shape[1] - base))
        m_ref[...] = jnp.full_like(m_ref, NEG_BIG)
        acc_ref[...] = jnp.zeros_like(acc_ref)

    off = qi * QB - kj * KV
    tile_start = (kj * KV).astype(F32)

    def step(masked):
        s = _dot_nt(ka_ref[0, 0], qa_ref[...])
        vt = vt_ref[0, 0]
        if masked:
            r = lax.broadcasted_iota(jnp.int32, (KV, QB), 0)
            c = lax.broadcasted_iota(jnp.int32, (KV, QB), 1)
            causal = (c - r + off) >= 0
        for h in range(NSA_HPG):
            cols = slice(h * QB, (h + 1) * QB)
            sh = s[:, cols]
            if masked:
                sh = jnp.where(causal, sh, NEG_BIG)
            delta = slope_ref[g * NSA_HPG + h] * tile_start
            m_old = m_ref[h:h + 1, :]
            m_new = jnp.maximum(m_old, jnp.max(sh, axis=0, keepdims=True) + delta)
            e = jnp.exp(sh - (m_new - delta))
            alpha = jnp.exp(m_old - m_new)
            acc_ref[:, cols] = alpha * acc_ref[:, cols] + _dot(vt, e.astype(BF16))
            m_ref[h:h + 1, :] = m_new

    last = (qi * QB) // KV

    @pl.when(kj < last)
    def _():
        step(False)

    @pl.when(kj == last)
    def _():
        step(True)
        for h in range(NSA_HPG):
            cols = slice(h * QB, (h + 1) * QB)
            o = acc_ref[0:NSA_DH, cols] / acc_ref[NSA_DH:NSA_DH + 1, cols]
            o_ref[:, h * NSA_DH:(h + 1) * NSA_DH] = o.T.astype(o_ref.dtype)


def sel_key_columns(seq):
    n_slc = seq // NSA_SLC_LEN
    pos = np.arange(seq)
    onehot = (pos[:, None] // NSA_SLC_LEN == np.arange(n_slc)[None, :]).astype(np.float32)
    r = pos % SEL_KV
    extra = np.zeros((seq, 128), np.float32)
    extra[:, 0] = extra[:, 2] = (r // 32) * 32
    extra[:, 1] = extra[:, 3] = r % 32
    return jnp.asarray(np.concatenate([onehot, extra], axis=1), BF16)


def nsa_sel(U, selbias, k_aug, vt_slc, slopes, *, batch, seq):
    N = U.shape[0]
    QB, KV = SEL_QB, SEL_KV
    nq = seq // QB
    G = NSA_GROUPS
    W = NSA_HPG * NSA_DH
    n_slc = selbias.shape[3]
    ka_w = k_aug.shape[3]
    qs, ks = _sel_pairs(seq)
    s_hi = slopes.astype(BF16)
    s_lo = (slopes - s_hi.astype(F32)).astype(BF16)
    qx = jnp.zeros((NSA_HEADS, 128), BF16).at[:, 0].set(s_hi).at[:, 1].set(s_hi).at[:, 2].set(s_lo).at[:, 3].set(s_lo)
    qx = jnp.pad(qx.reshape(G, NSA_HPG, 128), ((0, 0), (0, 8 - NSA_HPG), (0, 0)))
    grid_spec = pltpu.PrefetchScalarGridSpec(
        num_scalar_prefetch=3,
        grid=(batch, G, len(qs)),
        in_specs=[
            pl.BlockSpec((QB, W), lambda b, g, p, qi, kj, s: (b * nq + qi[p], U_NSA_Q // W + g)),
            pl.BlockSpec((1, 1, QB, n_slc), lambda b, g, p, qi, kj, s: (b, g, qi[p], 0)),
            pl.BlockSpec((1, 8, 128), lambda b, g, p, qi, kj, s: (g, 0, 0)),
            pl.BlockSpec((1, 1, KV, ka_w), lambda b, g, p, qi, kj, s: (b, g, kj[p], 0)),
            pl.BlockSpec((1, 1, NSA_DH + 8, KV), lambda b, g, p, qi, kj, s: (b, g, 0, kj[p])),
        ],
        out_specs=pl.BlockSpec((QB, W), lambda b, g, p, qi, kj, s: (b * nq + qi[p], g)),
        scratch_shapes=[
            pltpu.VMEM((NSA_HPG * QB, ka_w), BF16),
            pltpu.VMEM((NSA_HPG, QB), F32),
            pltpu.VMEM((NSA_DH + 8, NSA_HPG * QB), F32),
        ],
    )
    return pl.pallas_call(
        _nsa_sel_kernel,
        out_shape=jax.ShapeDtypeStruct((N, NSA_HEADS * NSA_DH), BF16),
        grid_spec=grid_spec,
        compiler_params=_cparams(("parallel", "parallel", "arbitrary")),
        name="nsa_sel",
    )(jnp.asarray(qs), jnp.asarray(ks), slopes, U, selbias, qx, k_aug, vt_slc)


def _nsa_combine_kernel(oc_ref, os_ref, ow_ref, gate_ref, e_ref, o_ref):
    ge = _dot_split(_sigmoid(gate_ref[...]), e_ref[...])
    Wd = NSA_HEADS * NSA_DH
    o = (ge[:, :Wd] * oc_ref[...].astype(F32) + ge[:, Wd:2 * Wd] * os_ref[...].astype(F32)
         + ge[:, 2 * Wd:] * ow_ref[...].astype(F32))
    o_ref[...] = o.astype(o_ref.dtype)


def nsa_combine(o_c, o_s, o_w, Us):
    N, Wd = o_c.shape
    tm = 512
    e = np.zeros((128, 3 * Wd), np.float32)
    for h in range(NSA_HEADS):
        for j in range(3):
            e[h * 3 + j, j * Wd + h * NSA_DH:j * Wd + (h + 1) * NSA_DH] = 1.0
    blk = pl.BlockSpec((tm, Wd), lambda i: (i, 0))
    return pl.pallas_call(
        _nsa_combine_kernel,
        out_shape=jax.ShapeDtypeStruct((N, Wd), BF16),
        grid=(N // tm,),
        in_specs=[blk, blk, blk, pl.BlockSpec((tm, 128), lambda i: (i, S_NSA_GATE // 128)),
                  pl.BlockSpec((128, 3 * Wd), lambda i: (0, 0))],
        out_specs=blk,
        compiler_params=_cparams(("parallel",)),
        name="nsa_combine",
    )(o_c, o_s, o_w, Us, jnp.asarray(e, BF16))


def _merge_kernel(ya_ref, yb_ref, yc_ref, ga_ref, gb_ref, gc_ref, p_ref, o_ref):
    m = (_sigmoid(ga_ref[...].astype(F32)) * _dot(ya_ref[...], p_ref[0])
         + _sigmoid(gb_ref[...].astype(F32)) * _dot(yb_ref[...], p_ref[1])
         + _sigmoid(gc_ref[...].astype(F32)) * _dot(yc_ref[...], p_ref[2]))
    o_ref[...] = m.astype(o_ref.dtype)


def merge(y_a, y_b, y_c, U, p_merge):
    N = y_a.shape[0]
    tm, tn = 1024, 512
    yb = pl.BlockSpec((tm, BRANCH_DIM), lambda i, j: (i, 0))
    gate = lambda br: pl.BlockSpec((tm, tn), lambda i, j: (i, (U_GATE + br * D_MODEL) // tn + j))
    return pl.pallas_call(
        _merge_kernel,
        out_shape=jax.ShapeDtypeStruct((N, D_MODEL), BF16),
        grid=(N // tm, D_MODEL // tn),
        in_specs=[yb, yb, yb, gate(0), gate(1), gate(2),
                  pl.BlockSpec((3, BRANCH_DIM, tn), lambda i, j: (0, 0, j))],
        out_specs=pl.BlockSpec((tm, tn), lambda i, j: (i, j)),
        compiler_params=_cparams(("parallel", "parallel")),
        name="merge",
    )(y_a, y_b, y_c, U, U, U, p_merge)


def _ffn_kernel(h_ref, w1_ref, w3_ref, w2_ref, o_ref, acc_ref):
    f = pl.program_id(1)

    @pl.when(f == 0)
    def _():
        acc_ref[...] = jnp.zeros_like(acc_ref)

    h = h_ref[...]
    a = _dot(h, w1_ref[...])
    z = (a * _sigmoid(a) * _dot(h, w3_ref[...])).astype(BF16)
    acc_ref[...] += _dot(z, w2_ref[...])

    @pl.when(f == pl.num_programs(1) - 1)
    def _():
        o_ref[...] = acc_ref[...].astype(o_ref.dtype)


def ffn(h, w1, w3, w2):
    N, D = h.shape
    F = w1.shape[1]
    tm, tf = 1024, 512
    return pl.pallas_call(
        _ffn_kernel,
        out_shape=jax.ShapeDtypeStruct((N, D), BF16),
        grid=(N // tm, F // tf),
        in_specs=[pl.BlockSpec((tm, D), lambda i, f: (i, 0)),
                  pl.BlockSpec((D, tf), lambda i, f: (0, f)),
                  pl.BlockSpec((D, tf), lambda i, f: (0, f)),
                  pl.BlockSpec((tf, D), lambda i, f: (f, 0))],
        out_specs=pl.BlockSpec((tm, D), lambda i, f: (i, 0)),
        scratch_shapes=[pltpu.VMEM((tm, D), F32)],
        compiler_params=_cparams(("parallel", "arbitrary")),
        name="ffn",
    )(h, w1, w3, w2)


def _route_kernel(lg_ref, cb_ref, sel_ref):
    lg = lg_ref[...]
    lane = lax.broadcasted_iota(jnp.int32, lg.shape, 1)
    x = jnp.where(lane < N_EXPERTS, lg, NEG_BIG)
    v1 = jnp.max(x, axis=-1, keepdims=True)
    i1 = jnp.min(jnp.where(x == v1, lane, 1024), axis=-1, keepdims=True)
    x2 = jnp.where(lane == i1, NEG_BIG, x)
    v2 = jnp.max(x2, axis=-1, keepdims=True)
    i2 = jnp.min(jnp.where(x2 == v2, lane, 1024), axis=-1, keepdims=True)
    e2 = jnp.exp(v2 - v1)
    w1 = 1.0 / (1.0 + e2)
    w2 = e2 / (1.0 + e2)
    cb_ref[...] = jnp.where(lane == i1, w1, 0.0) + jnp.where(lane == i2, w2, 0.0)
    sel_ref[...] = jnp.where((lane == i1) | (lane == i2), 1.0, 0.0)


def route(logits):
    N = logits.shape[0]
    tm = 1024
    blk = pl.BlockSpec((tm, 128), lambda i: (i, 0))
    return pl.pallas_call(
        _route_kernel,
        out_shape=(jax.ShapeDtypeStruct((N, 128), F32), jax.ShapeDtypeStruct((N, 128), F32)),
        grid=(N // tm,),
        in_specs=[blk],
        out_specs=(blk, blk),
        compiler_params=_cparams(("parallel",)),
        name="route",
    )(logits)


MOE_TM = 1024
MOE_TF = 256
MOE_WB = 256
MOE_GT = 256


def _moe_plan(sel):
    N, E = sel.shape
    tm, wb, gt = MOE_TM, MOE_WB, MOE_GT
    sub = tm // gt
    NT = 2 * N // tm + E
    NG = NT * sub
    R = NT * tm
    P = NG + E * (N // wb)
    i32 = jnp.int32
    cs = jnp.cumsum(sel.astype(i32), axis=0)
    cnt = cs[-1]
    tiles_e = (cnt + tm - 1) // tm
    tile_end = jnp.cumsum(tiles_e)
    tile_start = tile_end - tiles_e
    total_tiles = tile_end[-1]
    dest = jnp.where(sel, tile_start[None, :] * tm + cs - 1, R)
    ti = jnp.arange(NT, dtype=i32)
    tile_valid = ti < total_tiles
    tile_e = jnp.minimum(jnp.searchsorted(tile_end, ti, side="right").astype(i32), E - 1)
    gi = jnp.arange(NG, dtype=i32)
    g_valid = tile_valid[gi // sub]
    g_e = tile_e[gi // sub]
    k_lo = gi * gt - tile_start[g_e] * tm
    nonempty = g_valid & (k_lo < cnt[g_e])
    k_hi = jnp.minimum(k_lo + gt, cnt[g_e]) - 1
    find = jax.vmap(lambda col, q: jnp.searchsorted(col, q, side="left"), in_axes=(1, None), out_axes=1)
    pick = lambda m: jnp.take_along_axis(m, g_e[:, None], axis=1)[:, 0].astype(i32)
    blo = jnp.where(nonempty, pick(find(cs, k_lo + 1)) // wb, 0)
    bhi = jnp.where(nonempty, pick(find(cs, k_hi + 1)) // wb, jnp.where(g_valid, 0, -1))
    npair = bhi - blo + 1
    pend = jnp.cumsum(npair)
    pstart = pend - npair
    total_p = pend[-1]
    pi = jnp.arange(P, dtype=i32)
    pvalid = pi < total_p
    ptile = jnp.minimum(jnp.searchsorted(pend, pi, side="right").astype(i32), NG - 1)
    pblk = blo[ptile] + pi - pstart[ptile]
    pfirst = pvalid & (pi == pstart[ptile])
    ptile = jnp.where(pvalid, ptile, ptile[total_p - 1])
    pblk = jnp.where(pvalid, pblk, pblk[total_p - 1])
    pexp = g_e[ptile]
    order = jnp.argsort(jnp.where(pvalid, pblk * NG + ptile, jnp.iinfo(jnp.int32).max))
    s_valid = pvalid
    s_tile = jnp.where(s_valid, ptile[order], ptile[order][total_p - 1])
    s_blk = jnp.where(s_valid, pblk[order], pblk[order][total_p - 1])
    s_exp = g_e[s_tile]
    s_first = s_valid & ((pi == 0) | (s_blk != jnp.roll(s_blk, 1)))
    e_src = jnp.where(tile_valid, ti, 0)
    e_exp = jnp.where(tile_valid, tile_e, tile_e[jnp.maximum(total_tiles - 1, 0)])
    b2i = lambda x: x.astype(i32)
    return dict(dest=dest, g=(ptile, pblk, pexp, b2i(pvalid), b2i(pfirst)),
                e=(e_exp, b2i(tile_valid), e_src), s=(s_tile, s_blk, s_exp, b2i(s_valid), b2i(s_first)), NT=NT, P=P)


def _moe_group_kernel(pt_ref, pb_ref, pe_ref, pv_ref, pf_ref, dest_ref, h_ref, o_ref):
    p = pl.program_id(0)

    @pl.when(pv_ref[p] == 1)
    def _():
        rel = dest_ref[0] - pt_ref[p] * MOE_GT
        row = lax.broadcasted_iota(jnp.int32, (MOE_GT, MOE_WB), 0)
        onehot = jnp.where(rel == row, 1.0, 0.0).astype(BF16)
        rows = _dot(onehot, h_ref[...]).astype(o_ref.dtype)

        @pl.when(pf_ref[p] == 1)
        def _():
            o_ref[...] = rows

        @pl.when(pf_ref[p] == 0)
        def _():
            o_ref[...] += rows


def _moe_expert_kernel(te_ref, tv_ref, ts_ref, h_ref, w1_ref, w3_ref, w2_ref, o_ref, acc_ref):
    i = pl.program_id(0)
    f = pl.program_id(1)
    nf = pl.num_programs(1)

    @pl.when(tv_ref[i] == 1)
    def _():
        @pl.when(f == 0)
        def _():
            acc_ref[...] = jnp.zeros_like(acc_ref)

        h = h_ref[...]
        a = _dot(h, w1_ref[0, 0].astype(BF16))
        z = (a * _sigmoid(a) * _dot(h, w3_ref[0, 0].astype(BF16))).astype(BF16)
        acc_ref[...] += _dot(z, w2_ref[0, 0].astype(BF16))

        @pl.when(f == nf - 1)
        def _():
            o_ref[...] = acc_ref[...].astype(o_ref.dtype)

    @pl.when((tv_ref[i] == 0) & (f == nf - 1))
    def _():
        o_ref[...] = jnp.zeros_like(o_ref)


def _moe_ungroup_kernel(st_ref, sb_ref, se_ref, sv_ref, sf_ref, dest_ref, w_ref, y_ref, o_ref):
    p = pl.program_id(0)

    @pl.when(sv_ref[p] == 1)
    def _():
        rel = dest_ref[0] - st_ref[p] * MOE_GT
        col = lax.broadcasted_iota(jnp.int32, (MOE_WB, MOE_GT), 1)
        onehot_t = jnp.where(rel == col, 1.0, 0.0).astype(BF16)
        part = _dot(onehot_t, y_ref[...]) * w_ref[0]

        @pl.when(sf_ref[p] == 1)
        def _():
            o_ref[...] = part

        @pl.when(sf_ref[p] == 0)
        def _():
            o_ref[...] += part


def moe(h, combine, selm, w1, w3, w2, layer):
    N, D = h.shape
    _, E, _, F = w1.shape
    tm, wb, gt, tf = MOE_TM, MOE_WB, MOE_GT, MOE_TF
    plan = _moe_plan(selm[:, :E] > 0.5)
    NT, P = plan["NT"], plan["P"]
    R = NT * tm
    dest_t = plan["dest"].T
    h_sorted = pl.pallas_call(
        _moe_group_kernel,
        out_shape=jax.ShapeDtypeStruct((R, D), BF16),
        grid_spec=pltpu.PrefetchScalarGridSpec(
            num_scalar_prefetch=5,
            grid=(P,),
            in_specs=[pl.BlockSpec((1, 1, wb), lambda p, pt, pb, pe, pv, pf: (pe[p], 0, pb[p])),
                      pl.BlockSpec((wb, D), lambda p, pt, pb, pe, pv, pf: (pb[p], 0))],
            out_specs=pl.BlockSpec((gt, D), lambda p, pt, pb, pe, pv, pf: (pt[p], 0)),
        ),
        compiler_params=_cparams(("arbitrary",)),
        name="moe_group",
    )(*plan["g"], dest_t.reshape(E, 1, N), h)
    y_sorted = pl.pallas_call(
        _moe_expert_kernel,
        out_shape=jax.ShapeDtypeStruct((R, D), BF16),
        grid_spec=pltpu.PrefetchScalarGridSpec(
            num_scalar_prefetch=3,
            grid=(NT, F // tf),
            in_specs=[pl.BlockSpec((tm, D), lambda i, f, te, tv, ts: (ts[i], 0)),
                      pl.BlockSpec((1, 1, D, tf), lambda i, f, te, tv, ts: (layer, te[i], 0, f * tv[i])),
                      pl.BlockSpec((1, 1, D, tf), lambda i, f, te, tv, ts: (layer, te[i], 0, f * tv[i])),
                      pl.BlockSpec((1, 1, tf, D), lambda i, f, te, tv, ts: (layer, te[i], f * tv[i], 0))],
            out_specs=pl.BlockSpec((tm, D), lambda i, f, te, tv, ts: (i, 0)),
            scratch_shapes=[pltpu.VMEM((tm, D), F32)],
        ),
        compiler_params=_cparams(("arbitrary", "arbitrary")),
        name="moe_expert",
    )(*plan["e"], h_sorted, w1, w3, w2)
    return pl.pallas_call(
        _moe_ungroup_kernel,
        out_shape=jax.ShapeDtypeStruct((N, D), F32),
        grid_spec=pltpu.PrefetchScalarGridSpec(
            num_scalar_prefetch=5,
            grid=(P,),
            in_specs=[pl.BlockSpec((1, wb, 1), lambda p, st, sb, se, sv, sf: (se[p], sb[p], 0)),
                      pl.BlockSpec((1, wb, 1), lambda p, st, sb, se, sv, sf: (se[p], sb[p], 0)),
                      pl.BlockSpec((gt, D), lambda p, st, sb, se, sv, sf: (st[p], 0))],
            out_specs=pl.BlockSpec((wb, D), lambda p, st, sb, se, sv, sf: (sb[p], 0)),
        ),
        compiler_params=_cparams(("arbitrary",)),
        name="moe_ungroup",
    )(*plan["s"], dest_t.reshape(E, N, 1), combine[:, :E].T.reshape(E, N, 1), y_sorted)


def _project_weights(w_in):
    gla_w = 2 * GLA_HEADS * GLA_DK + 2 * GLA_HEADS * GLA_DV + GLA_RANK
    rw_w = 3 * RWKV_DIM + RWKV_W_RANK + RWKV_A_RANK + RWKV_G_RANK
    kvw = NSA_GROUPS * NSA_DH
    nsa_w = NSA_HEADS * NSA_DH + 6 * kvw + NSA_HEADS * 3
    o_rw = gla_w
    o_nsa = gla_w + rw_w
    o_gate = o_nsa + nsa_w
    big = jnp.concatenate([
        w_in[:, 0:gla_w - GLA_RANK],
        w_in[:, o_rw:o_rw + 3 * RWKV_DIM],
        w_in[:, o_nsa:o_nsa + NSA_HEADS * NSA_DH + 6 * kvw],
        w_in[:, o_gate:],
    ], axis=1).astype(BF16)
    D = w_in.shape[0]
    z = lambda n: jnp.zeros((D, n), w_in.dtype)
    small = jnp.concatenate([
        w_in[:, gla_w - GLA_RANK:gla_w], z(128 - GLA_RANK),
        w_in[:, o_rw + 3 * RWKV_DIM:o_rw + rw_w], z(256 - RWKV_G_RANK),
        w_in[:, o_gate - NSA_HEADS * 3:o_gate], z(128 - NSA_HEADS * 3),
    ], axis=1).astype(BF16)
    return big, small


def _mixer(h, batch, seq, w_in, gla_a2, gla_a_b, gla_norm, rwkv_mu, rwkv_w0, rwkv_w2, rwkv_a0, rwkv_a2, rwkv_g2,
           rwkv_k_k, rwkv_k_a, rwkv_r_k, rwkv_ln_w, rwkv_ln_b,
           nsa_pos_k, nsa_w1_k, nsa_w2_k, nsa_pos_v, nsa_w1_v, nsa_w2_v, p_merge, w_out, slopes):
    N = h.shape[0]
    w_big, w_small = _project_weights(w_in)
    tm_u = 2048 if N % 2048 == 0 else N
    U = matmul(h, w_big, BF16, tm_u, 512)
    Us = matmul(h, w_small, F32, tm_u, S_COLS)

    y_a = gla(U, Us, gla_a2, gla_a_b, gla_norm, batch=batch, seq=seq)

    r, k, v, kap, b, lw, gate = rwkv_prep(U, Us, rwkv_mu, rwkv_w0, rwkv_w2, rwkv_a0, rwkv_a2, rwkv_g2,
                                          rwkv_k_k, rwkv_k_a, seq=seq)
    y_b = rwkv_chunk(r, k, v, kap, b, lw, gate, rwkv_r_k.reshape(-1), rwkv_ln_w, rwkv_ln_b, batch=batch, seq=seq)

    kv6 = U[:, U_NSA_KV:U_NSA_KV + 6 * NSA_GROUPS * NSA_DH]
    kv6 = kv6.reshape(batch, seq, 6, NSA_GROUPS, NSA_DH).transpose(2, 0, 3, 1, 4)
    kvc = nsa_compress(kv6[0:2], jnp.stack([nsa_pos_k, nsa_pos_v]), jnp.stack([nsa_w1_k, nsa_w1_v]),
                       jnp.stack([nsa_w2_k, nsa_w2_v]))
    ncp = kvc.shape[3]
    ones_c = jnp.zeros((batch, NSA_GROUPS, 8, ncp), BF16).at[:, :, 0, :].set(1.0)
    vct = jnp.concatenate([kvc[1].transpose(0, 1, 3, 2), ones_c], axis=2)
    o_c, selbias = nsa_cmp(U, kvc[0], vct, slopes, batch=batch, seq=seq)
    n_slc = seq // NSA_SLC_LEN
    k_aug = jnp.concatenate([kv6[2], jnp.broadcast_to(sel_key_columns(seq), (batch, NSA_GROUPS, seq, n_slc + 128))],
                            axis=-1)
    ones_rows = jnp.zeros((batch, NSA_GROUPS, 8, seq), BF16).at[:, :, 0, :].set(1.0)
    vt_aug = jnp.concatenate([kv6[3].transpose(0, 1, 3, 2), ones_rows], axis=2)
    o_s = nsa_sel(U, selbias, k_aug, vt_aug, slopes, batch=batch, seq=seq)
    kw_pad = jnp.pad(kv6[4], ((0, 0), (0, 0), (NSA_WINDOW, 0), (0, 0)))
    vtw = jnp.concatenate([kv6[5].transpose(0, 1, 3, 2), ones_rows], axis=2)
    vtw_pad = jnp.pad(vtw, ((0, 0), (0, 0), (0, 0), (NSA_WINDOW, 0)))
    o_w = nsa_win(U, kw_pad, vtw_pad, slopes, batch=batch, seq=seq)
    y_c = nsa_combine(o_c, o_s, o_w, Us)

    merged = merge(y_a, y_b, y_c, U, p_merge.astype(BF16))
    return matmul(merged, w_out.astype(BF16), BF16, 1024 if N % 1024 == 0 else N, 1024)


def kernel(x, c, norm_mix, norm_ffn, ada_w, ada_b, w_in, gla_a2, gla_a_b, gla_norm, rwkv_mu, rwkv_w0, rwkv_w2, rwkv_a0, rwkv_a2, rwkv_g2, rwkv_k_k, rwkv_k_a, rwkv_r_k, rwkv_ln_w, rwkv_ln_b, nsa_pos_k, nsa_w1_k, nsa_w2_k, nsa_pos_v, nsa_w1_v, nsa_w2_v, p_merge, w_out, ffn_w1, ffn_w3, ffn_w2, moe_router, moe_w1, moe_w3, moe_w2, final_norm):
    B, T, D = x.shape
    depth = w_in.shape[0]
    N = B * T
    xs = x.reshape(N, D)
    ada = ada_all(c, ada_w, ada_b)
    slopes = jnp.exp2(-8.0 * jnp.arange(1, NSA_HEADS + 1, dtype=F32) / NSA_HEADS)
    y = None
    g_prev = None
    for l in range(depth):
        sh1, sc1, g1, sh2, sc2, g2 = jnp.split(ada[l], 6, axis=-1)
        if y is None:
            (h,) = resmod(xs, None, None, norm_mix[l], sh1, sc1, seq=T)
        else:
            xs, h = resmod(xs, y, g_prev, norm_mix[l], sh1, sc1, seq=T)
        y = _mixer(h, B, T, w_in[l], gla_a2[l], gla_a_b[l], gla_norm[l], rwkv_mu[l], rwkv_w0[l], rwkv_w2[l],
                   rwkv_a0[l], rwkv_a2[l], rwkv_g2[l], rwkv_k_k[l], rwkv_k_a[l], rwkv_r_k[l],
                   rwkv_ln_w[l], rwkv_ln_b[l], nsa_pos_k[l], nsa_w1_k[l], nsa_w2_k[l],
                   nsa_pos_v[l], nsa_w1_v[l], nsa_w2_v[l], p_merge[l], w_out[l], slopes)
        if l % 2 == 0:
            xs, h = resmod(xs, y, g1, norm_ffn[l], sh2, sc2, seq=T)
            y = ffn(h, ffn_w1[l // 2].astype(BF16), ffn_w3[l // 2].astype(BF16), ffn_w2[l // 2].astype(BF16))
        else:
            rt = jnp.zeros((D, 128), F32).at[:, :N_EXPERTS].set(moe_router[l // 2])
            xs, h, logits = resmod(xs, y, g1, norm_ffn[l], sh2, sc2, seq=T, router=rt)
            combine, selm = route(logits)
            y = moe(h, combine, selm, moe_w1, moe_w3, moe_w2, l // 2)
        g_prev = g2
    (out,) = resmod(xs, y, g_prev, final_norm, None, None, seq=T, final=True)
    return out.reshape(B, T, D)
```

```python
import functools

import numpy as np
import jax
import jax.numpy as jnp
from jax import lax
from jax.experimental import pallas as pl
from jax.experimental.pallas import tpu as pltpu

F32 = jnp.float32
BF16 = jnp.bfloat16
HI = lax.Precision.HIGHEST

V7X_VMEM_LIMIT_BYTES = 56 * 1024 * 1024

D_MODEL = 2048
NORM_EPS = 1e-6
NEG_BIG = -1e30

GLA_HEADS = 4
GLA_DK = 128
GLA_DV = 256
GLA_RANK = 16
GLA_NORMALIZER = 16.0
CHUNK = 64

RWKV_HEADS = 16
RWKV_N = 64
RWKV_DIM = RWKV_HEADS * RWKV_N
RWKV_W_RANK = 64
RWKV_A_RANK = 64
RWKV_G_RANK = 160
RWKV_LN_EPS = 64e-5
RWKV_HB = 4

NSA_HEADS = 16
NSA_GROUPS = 4
NSA_HPG = 4
NSA_DH = 64
NSA_CMP_LEN = 32
NSA_CMP_STRIDE = 16
NSA_CMP_HIDDEN = 128
NSA_SLC_LEN = 64
NSA_N_SEL = 16
NSA_WINDOW = 512
NSA_QB = 256
CMP_QB = 512
SEL_MASK_BIAS = 131072.0

BRANCH_DIM = 1024
D_FF = 5632
N_EXPERTS = 8

U_GLA_Q, U_GLA_K, U_GLA_V, U_GLA_G = 0, 512, 1024, 2048
U_RWKV_R, U_RWKV_K, U_RWKV_V = 3072, 4096, 5120
U_NSA_Q = 6144
U_NSA_KV = 7168
U_GATE = 8704
U_COLS = U_GATE + 3 * D_MODEL
S_GLA_A = 0
S_RWKV_WA = 128
S_RWKV_G = 256
S_NSA_GATE = 512
S_COLS = 640


def _cparams(sem, vmem=V7X_VMEM_LIMIT_BYTES):
    return pltpu.CompilerParams(dimension_semantics=sem, vmem_limit_bytes=vmem)


def _sigmoid(x):
    return 1.0 / (1.0 + jnp.exp(-x))


def _softplus(x):
    return jnp.maximum(x, 0.0) + jnp.log(1.0 + jnp.exp(-jnp.abs(x)))


def _dot(a, b):
    return jnp.dot(a, b, preferred_element_type=F32)


def _dot_nt(a, b):
    return lax.dot_general(a, b, (((1,), (1,)), ((), ())), preferred_element_type=F32)


def _dot_tn(a, b):
    return lax.dot_general(a, b, (((0,), (0,)), ((), ())), preferred_element_type=F32)


def _dot_hi(a, b):
    return jnp.dot(a, b, preferred_element_type=F32, precision=HI)


def _dot_x3(a, b):
    ah = a.astype(BF16)
    al = (a - ah.astype(F32)).astype(BF16)
    bh = b.astype(BF16)
    bl = (b - bh.astype(F32)).astype(BF16)
    return _dot(ah, bh) + _dot(ah, bl) + _dot(al, bh)


def _dot_sel(w, x):
    wb = w.astype(BF16)
    hi = x.astype(BF16)
    r1 = x - hi.astype(F32)
    mid = r1.astype(BF16)
    lo = (r1 - mid.astype(F32)).astype(BF16)
    return _dot(wb, hi) + _dot(wb, mid) + _dot(wb, lo)


def _dot_split(x, w):
    hi = x.astype(BF16)
    lo = (x - hi.astype(F32)).astype(BF16)
    return _dot(hi, w) + _dot(lo, w)


def _ada_kernel(c_ref, w_ref, b_ref, o_ref):
    c = c_ref[...]
    o_ref[0] = _dot_hi(c * _sigmoid(c), w_ref[0]) + b_ref[0]


def ada_all(c, ada_w, ada_b):
    L, D, N6 = ada_w.shape
    B = c.shape[0]
    tn = 1024
    return pl.pallas_call(
        _ada_kernel,
        out_shape=jax.ShapeDtypeStruct((L, B, N6), F32),
        grid=(L, N6 // tn),
        in_specs=[
            pl.BlockSpec((B, D), lambda l, j: (0, 0)),
            pl.BlockSpec((1, D, tn), lambda l, j: (l, 0, j)),
            pl.BlockSpec((1, 1, tn), lambda l, j: (l, 0, j)),
        ],
        out_specs=pl.BlockSpec((1, B, tn), lambda l, j: (l, 0, j)),
        compiler_params=_cparams(("parallel", "parallel")),
        name="ada",
    )(c, ada_w, ada_b.reshape(L, 1, N6))


def _resmod_kernel(*refs, has_res, final, router):
    it = iter(refs)
    x_ref = next(it)
    y_ref = next(it) if has_res else None
    g_ref = next(it) if has_res else None
    ng_ref = next(it)
    sh_ref = None if final else next(it)
    sc_ref = None if final else next(it)
    rt_ref = next(it) if router else None
    xo_ref = next(it) if (has_res and not final) else None
    h_ref = next(it)
    lg_ref = next(it) if router else None

    x = x_ref[...]
    if has_res:
        x = x + g_ref[0] * y_ref[...].astype(F32)
        if xo_ref is not None:
            xo_ref[...] = x
    ms = jnp.mean(x * x, axis=-1, keepdims=True)
    h = x * lax.rsqrt(ms + NORM_EPS) * ng_ref[...]
    if not final:
        h = h * (1.0 + sc_ref[0]) + sh_ref[0]
    h_ref[...] = h.astype(h_ref.dtype)
    if router:
        lg_ref[...] = _dot_hi(h, rt_ref[...])


def resmod(x, y, g, norm_g, shift, scale, *, seq, router=None, final=False):
    N, D = x.shape
    tm = 512
    spb = seq // tm
    has_res = y is not None
    row = lambda i: (i, 0)
    per_b = lambda i: (i // spb, 0, 0)
    ins, specs = [x], [pl.BlockSpec((tm, D), row)]
    if has_res:
        ins += [y, g.reshape(-1, 1, D)]
        specs += [pl.BlockSpec((tm, D), row), pl.BlockSpec((1, 1, D), per_b)]
    ins.append(norm_g.reshape(1, D))
    specs.append(pl.BlockSpec((1, D), lambda i: (0, 0)))
    if not final:
        ins += [shift.reshape(-1, 1, D), scale.reshape(-1, 1, D)]
        specs += [pl.BlockSpec((1, 1, D), per_b), pl.BlockSpec((1, 1, D), per_b)]
    if router is not None:
        ins.append(router)
        specs.append(pl.BlockSpec(router.shape, lambda i: (0, 0)))
    outs, ospecs = [], []
    if has_res and not final:
        outs.append(jax.ShapeDtypeStruct((N, D), F32))
        ospecs.append(pl.BlockSpec((tm, D), row))
    outs.append(jax.ShapeDtypeStruct((N, D), F32 if final else BF16))
    ospecs.append(pl.BlockSpec((tm, D), row))
    if router is not None:
        outs.append(jax.ShapeDtypeStruct((N, 128), F32))
        ospecs.append(pl.BlockSpec((tm, 128), row))
    res = pl.pallas_call(
        functools.partial(_resmod_kernel, has_res=has_res, final=final, router=router is not None),
        out_shape=tuple(outs),
        grid=(N // tm,),
        in_specs=specs,
        out_specs=tuple(ospecs),
        compiler_params=_cparams(("parallel",)),
        name="resmod",
    )(*ins)
    return res


def _mm_kernel(x_ref, w_ref, o_ref):
    o_ref[...] = _dot(x_ref[...], w_ref[...]).astype(o_ref.dtype)


def matmul(x, w, out_dtype, tm, tn):
    M, K = x.shape
    N = w.shape[1]
    return pl.pallas_call(
        _mm_kernel,
        out_shape=jax.ShapeDtypeStruct((M, N), out_dtype),
        grid=(M // tm, N // tn),
        in_specs=[pl.BlockSpec((tm, K), lambda i, j: (i, 0)), pl.BlockSpec((K, tn), lambda i, j: (0, j))],
        out_specs=pl.BlockSpec((tm, tn), lambda i, j: (i, j)),
        compiler_params=_cparams(("parallel", "parallel")),
        name="matmul",
    )(x, w)


GLA_TC = 512


def _gla_kernel(q_ref, k_ref, v_ref, g_ref, alr_ref, a2_ref, ab_ref, ng_ref, o_ref, st_ref, la_ref):
    @pl.when(pl.program_id(1) == 0)
    def _():
        st_ref[...] = jnp.zeros_like(st_ref)

    C, DK, DV = CHUNK, GLA_DK, GLA_DV
    la = _dot_x3(alr_ref[...], a2_ref[...]) + ab_ref[...]
    la_ref[...] = -_softplus(-la) / GLA_NORMALIZER
    ri = lax.broadcasted_iota(jnp.int32, (C, C), 0)
    ci = lax.broadcasted_iota(jnp.int32, (C, C), 1)
    causal = ri >= ci
    tril = causal.astype(F32)
    H = range(GLA_HEADS)

    def body(c, carry):
        sl = pl.ds(pl.multiple_of(c * C, C), C)
        bc_all = _dot_sel(tril, la_ref[sl, :])
        bcs = [bc_all[:, h * DK:(h + 1) * DK] for h in H]
        bls = [bc[C - 1:C, :] for bc in bcs]
        ks = [k_ref[sl, h * DK:(h + 1) * DK].astype(F32) for h in H]
        vs = [v_ref[sl, h * DV:(h + 1) * DV] for h in H]
        qds = [(q_ref[sl, h * DK:(h + 1) * DK].astype(F32) * (DK ** -0.5) * jnp.exp(bc)).astype(BF16)
               for h, bc in zip(H, bcs)]
        kds = [(k * jnp.exp(-bc)).astype(BF16) for k, bc in zip(ks, bcs)]
        kls = [(k * jnp.exp(bl - bc)).astype(BF16) for k, bl, bc in zip(ks, bls, bcs)]
        atts = [jnp.where(causal, _dot_nt(qd, kd), 0.0).astype(BF16) for qd, kd in zip(qds, kds)]
        sts = [st_ref[h] for h in H]
        os_ = [_dot(att, v) + _dot_nt(qd, st.astype(BF16)) for att, v, qd, st in zip(atts, vs, qds, sts)]
        for h in H:
            st_ref[h] = sts[h] * jnp.exp(bls[h]) + _dot_tn(vs[h], kls[h])
        for h in H:
            o = os_[h]
            o = o * lax.rsqrt(jnp.mean(o * o, axis=-1, keepdims=True) + NORM_EPS) * ng_ref[...]
            gg = g_ref[sl, h * DV:(h + 1) * DV].astype(F32)
            o_ref[sl, h * DV:(h + 1) * DV] = (o * (gg * _sigmoid(gg))).astype(o_ref.dtype)
        return carry

    lax.fori_loop(0, GLA_TC // C, body, 0)


def gla(U, Us, a2, a_b, norm_g, *, batch, seq):
    N = U.shape[0]
    nt = seq // GLA_TC
    HK, HV = GLA_HEADS * GLA_DK, GLA_HEADS * GLA_DV
    a2p = jnp.zeros((128, HK), F32).at[:GLA_RANK].set(a2)
    row = lambda b, i: b * nt + i
    return pl.pallas_call(
        _gla_kernel,
        out_shape=jax.ShapeDtypeStruct((N, HV), BF16),
        grid=(batch, nt),
        in_specs=[
            pl.BlockSpec((GLA_TC, HK), lambda b, i: (row(b, i), U_GLA_Q // HK)),
            pl.BlockSpec((GLA_TC, HK), lambda b, i: (row(b, i), U_GLA_K // HK)),
            pl.BlockSpec((GLA_TC, HV), lambda b, i: (row(b, i), U_GLA_V // HV)),
            pl.BlockSpec((GLA_TC, HV), lambda b, i: (row(b, i), U_GLA_G // HV)),
            pl.BlockSpec((GLA_TC, 128), lambda b, i: (row(b, i), S_GLA_A // 128)),
            pl.BlockSpec((128, HK), lambda b, i: (0, 0)),
            pl.BlockSpec((1, HK), lambda b, i: (0, 0)),
            pl.BlockSpec((1, GLA_DV), lambda b, i: (0, 0)),
        ],
        out_specs=pl.BlockSpec((GLA_TC, HV), lambda b, i: (row(b, i), 0)),
        scratch_shapes=[pltpu.VMEM((GLA_HEADS, GLA_DV, GLA_DK), F32), pltpu.VMEM((GLA_TC, HK), F32)],
        compiler_params=_cparams(("parallel", "arbitrary")),
        name="gla",
    )(U, U, U, U, Us, a2p, a_b.reshape(1, -1), norm_g.reshape(1, -1))


RWKV_TM = 256


def _seg_ones(n=256, seg=RWKV_N):
    i = np.arange(n)
    return jnp.asarray((i[:, None] // seg == i[None, :] // seg).astype(np.float32), BF16)


def _seg_sum(x, bd):
    outs = [_dot_split(x[:, s:s + 256], bd) for s in range(0, x.shape[1], 256)]
    return outs[0] if len(outs) == 1 else jnp.concatenate(outs, axis=1)


def _shift_lerp(u_ref, p_ref, mu, first):
    u = u_ref[...].astype(F32)
    prev_last = jnp.where(first, 0.0, p_ref[7:8, :].astype(F32))
    rolled = pltpu.roll(u, 1, 0)
    is_row0 = lax.broadcasted_iota(jnp.int32, u.shape, 0) == 0
    sh = jnp.where(is_row0, prev_last, rolled)
    return u + (sh - u) * mu


def _rwkv_prep_kernel(r_ref, k_ref, v_ref, wa_ref, gl_ref, rp_ref, kp_ref, vp_ref, wap_ref, glp_ref,
                      mur_ref, muk_ref, muv_ref, muwa_ref, mug_ref, w0_ref, w2_ref, a0_ref, a2_ref, g2_ref,
                      kk_ref, ka_ref, bd_ref,
                      ro_ref, ko_ref, vo_ref, kap_ref, bo_ref, lw_ref, go_ref, *, spb):
    first = (pl.program_id(0) % spb) == 0
    r = _shift_lerp(r_ref, rp_ref, mur_ref[...], first)
    k = _shift_lerp(k_ref, kp_ref, muk_ref[...], first)
    v = _shift_lerp(v_ref, vp_ref, muv_ref[...], first)
    wa = _shift_lerp(wa_ref, wap_ref, muwa_ref[...], first)
    gl = _shift_lerp(gl_ref, glp_ref, mug_ref[...], first)
    w_log = -_softplus(-(w0_ref[...] + _dot_x3(jnp.tanh(wa), w2_ref[...]))) - 0.5
    lw_ref[...] = -jnp.exp(w_log)
    a = _sigmoid(a0_ref[...] + _dot_x3(wa, a2_ref[...]))
    go_ref[...] = _dot(_sigmoid(gl).astype(BF16), g2_ref[...]).astype(go_ref.dtype)
    kk = k * kk_ref[...]
    nrm = jnp.sqrt(_seg_sum(kk * kk, bd_ref[...]))
    kk = kk / jnp.maximum(nrm, 1e-12)
    ro_ref[...] = r.astype(ro_ref.dtype)
    ko_ref[...] = (k * (1.0 + (a - 1.0) * ka_ref[...])).astype(ko_ref.dtype)
    vo_ref[...] = v.astype(vo_ref.dtype)
    kap_ref[...] = kk.astype(kap_ref.dtype)
    bo_ref[...] = (kk * a).astype(bo_ref.dtype)


def rwkv_prep(U, Us, mu, w0, w2, a0, a2, g2, k_k, k_a, *, seq):
    N = U.shape[0]
    tm = RWKV_TM
    spb = seq // tm
    R = RWKV_DIM
    cur = lambda cb: (lambda i: (i, cb))
    prv = lambda cb: (lambda i: (jnp.maximum(i * (tm // 8) - 1, 0), cb))
    mu_r, mu_k, mu_v = mu[:R], mu[R:2 * R], mu[2 * R:3 * R]
    mu_wa = mu[3 * R:3 * R + 128]
    mu_g = jnp.zeros((256,), F32).at[:RWKV_G_RANK].set(mu[3 * R + 128:])
    w2p = jnp.zeros((128, R), F32).at[:RWKV_W_RANK].set(w2)
    a2p = jnp.zeros((128, R), F32).at[RWKV_W_RANK:].set(a2)
    g2p = jnp.zeros((256, R), BF16).at[:RWKV_G_RANK].set(g2.astype(BF16))
    vec = lambda a: a.reshape(1, -1)
    full = lambda a: pl.BlockSpec(a.shape, lambda i: (0, 0))
    params = [vec(mu_r), vec(mu_k), vec(mu_v), vec(mu_wa), vec(mu_g), vec(w0), w2p, vec(a0), a2p, g2p,
              vec(k_k), vec(k_a), _seg_ones()]
    in_specs = [
        pl.BlockSpec((tm, R), cur(U_RWKV_R // R)), pl.BlockSpec((tm, R), cur(U_RWKV_K // R)),
        pl.BlockSpec((tm, R), cur(U_RWKV_V // R)),
        pl.BlockSpec((tm, 128), cur(S_RWKV_WA // 128)), pl.BlockSpec((tm, 256), cur(S_RWKV_G // 256)),
        pl.BlockSpec((8, R), prv(U_RWKV_R // R)), pl.BlockSpec((8, R), prv(U_RWKV_K // R)),
        pl.BlockSpec((8, R), prv(U_RWKV_V // R)),
        pl.BlockSpec((8, 128), prv(S_RWKV_WA // 128)), pl.BlockSpec((8, 256), prv(S_RWKV_G // 256)),
    ] + [full(p) for p in params]
    out = lambda dt: jax.ShapeDtypeStruct((N, R), dt)
    ospec = pl.BlockSpec((tm, R), lambda i: (i, 0))
    return pl.pallas_call(
        functools.partial(_rwkv_prep_kernel, spb=spb),
        out_shape=(out(BF16), out(BF16), out(BF16), out(BF16), out(BF16), out(F32), out(BF16)),
        grid=(N // tm,),
        in_specs=in_specs,
        out_specs=(ospec,) * 7,
        compiler_params=_cparams(("parallel",)),
        name="rwkv_prep",
    )(U, U, U, Us, Us, U, U, U, Us, Us, *params)


RWKV_TC = 512
RWKV_W = RWKV_HB * RWKV_N


def _rwkv_masks():
    W, C = RWKV_W, CHUNK
    i = np.arange(W)
    same = i[:, None] // C == i[None, :] // C
    m_bd = same.astype(np.float32)
    low_s = (same & (i[:, None] % C > i[None, :] % C)).astype(np.float32)
    low_i = (same & (i[:, None] % C >= i[None, :] % C)).astype(np.float32)
    tril = np.tril(np.ones((C, C), np.float32))
    return (jnp.asarray(m_bd), jnp.asarray(low_s), jnp.asarray(low_i), jnp.asarray(tril),
            jnp.asarray(np.eye(W, dtype=np.float32)))


def _tile4(x):
    return jnp.concatenate([x] * RWKV_HB, axis=0)


def _rwkv_chunk_kernel(r_ref, k_ref, v_ref, kap_ref, b_ref, lw_ref, g_ref,
                       mbd_ref, lows_ref, lowi_ref, tril_ref, eye_ref, bd_ref,
                       rk_ref, lnw_ref, lnb_ref, o_ref, st_ref):
    @pl.when(pl.program_id(1) == 0)
    def _():
        st_ref[...] = jnp.zeros_like(st_ref)

    C, W = CHUNK, RWKV_W
    n_batch = r_ref.shape[0]
    n_groups = r_ref.shape[2] // W
    m_bd = mbd_ref[...]
    low_s = lows_ref[...]
    low_i = lowi_ref[...]
    tril = tril_ref[...]
    eye = eye_ref[...]
    bd = bd_ref[...]

    def expand(x):
        return (_tile4(x) * m_bd).astype(BF16)

    def body(c, carry):
        sl = pl.ds(pl.multiple_of(c * C, C), C)
        lw_alls = [lw_ref[bi, sl, :] for bi in range(n_batch)]
        cum_alls = [_dot_sel(tril, lw) for lw in lw_alls]
        chains = [(bi, slice(gi * W, (gi + 1) * W)) for bi in range(n_batch) for gi in range(n_groups)]
        G = range(len(chains))
        lns = [ln for _, ln in chains]
        cums = [cum_alls[bi][:, ln] for bi, ln in chains]
        clasts = [cum[C - 1:C, :] for cum in cums]
        rs = [r_ref[bi, sl, ln].astype(F32) for bi, ln in chains]
        ks = [k_ref[bi, sl, ln].astype(F32) for bi, ln in chains]
        vs = [v_ref[bi, sl, ln].astype(F32) for bi, ln in chains]
        bs = [b_ref[bi, sl, ln].astype(F32) for bi, ln in chains]
        p_invs = [jnp.exp(-cum) for cum in cums]
        xes = [jnp.concatenate([expand(kap_ref[bi, sl, ln].astype(F32) * jnp.exp(cum - lw_alls[bi][:, ln])),
                                expand(r * jnp.exp(cum))], axis=0)
               for (bi, ln), cum, r in zip(chains, cums, rs)]
        hes = [jnp.concatenate([expand(b * pi), expand(k * pi)], axis=0) for b, k, pi in zip(bs, ks, p_invs)]
        scs = [_dot_nt(xe, he) for xe, he in zip(xes, hes)]
        ams = [sc[:W, :W] * low_s for sc in scs]
        tinvs = [eye - a_m for a_m in ams]
        ambs = [a_m.astype(BF16) for a_m in ams]
        pws = [_dot(ab, ab) for ab in ambs]
        n_lev = int(np.log2(C))
        for lev in range(1, n_lev):
            pwbs = [pw.astype(BF16) for pw in pws]
            if lev < n_lev - 1:
                outs = [_dot(jnp.concatenate([pwb, tinv.astype(BF16)], axis=0), pwb)
                        for pwb, tinv in zip(pwbs, tinvs)]
                pws = [o[:W] for o in outs]
                tinvs = [tinv + o[W:] for tinv, o in zip(tinvs, outs)]
            else:
                tinvs = [tinv + _dot(tinv.astype(BF16), pwb) for tinv, pwb in zip(tinvs, pwbs)]
        ves = [expand(v) for v in vs]
        bqv = [_dot(jnp.concatenate([(sc[:W, W:] * low_s).astype(BF16), (sc[W:, W:] * low_i).astype(BF16)], axis=0),
                    ve) for sc, ve in zip(scs, ves)]
        bmv = [o[:W] for o in bqv]
        qkv = [o[W:] for o in bqv]
        sts = [st_ref[gi] for gi in G]
        xss = [_dot_nt(xe, st.astype(BF16)) for xe, st in zip(xes, sts)]
        us = [_dot(tinv.astype(BF16), (xs[:W] + bv).astype(BF16)) for tinv, xs, bv in zip(tinvs, xss, bmv)]
        kbs = [jnp.concatenate([expand(k * jnp.exp(cl - cum)), expand(b * jnp.exp(cl - cum))], axis=0)
               for k, b, cl, cum in zip(ks, bs, clasts, cums)]
        for gi in G:
            vu = jnp.concatenate([ves[gi], (-us[gi]).astype(BF16)], axis=0)
            st_ref[gi] = sts[gi] * jnp.exp(clasts[gi]) + _dot_tn(vu, kbs[gi]) * m_bd
        for gi in G:
            ln = lns[gi]
            qb = (scs[gi][W:, :W] * low_i).astype(BF16)
            y_e = xss[gi][W:] + qkv[gi] - _dot(qb, us[gi].astype(BF16))
            y = y_e[0:C] + y_e[C:2 * C] + y_e[2 * C:3 * C] + y_e[3 * C:4 * C]
            mu = _seg_sum(y, bd) * (1.0 / RWKV_N)
            yc = y - mu
            var = _seg_sum(yc * yc, bd) * (1.0 / RWKV_N)
            yn = yc * lax.rsqrt(var + RWKV_LN_EPS) * lnw_ref[:, ln] + lnb_ref[:, ln]
            bonus = _seg_sum(rs[gi] * ks[gi] * rk_ref[:, ln], bd) * vs[gi]
            bi = chains[gi][0]
            o_ref[bi, sl, ln] = ((yn + bonus) * g_ref[bi, sl, ln].astype(F32)).astype(o_ref.dtype)
        return carry

    lax.fori_loop(0, RWKV_TC // C, body, 0)


RWKV_NB = 1


def rwkv_chunk(r, k, v, kap, b, lw, gate, r_k, ln_w, ln_b, *, batch, seq):
    N, R = r.shape
    nt = seq // RWKV_TC
    W = RWKV_W
    nb = RWKV_NB if batch % RWKV_NB == 0 else 1
    blk = pl.BlockSpec((nb, RWKV_TC, R), lambda bb, i: (bb, i, 0))
    masks = _rwkv_masks() + (_seg_ones(),)
    full = lambda a: pl.BlockSpec(a.shape, lambda bb, i: (0, 0))
    pvec = pl.BlockSpec((1, R), lambda bb, i: (0, 0))
    seqs = [a.reshape(batch, seq, R) for a in (r, k, v, kap, b, lw, gate)]
    out = pl.pallas_call(
        _rwkv_chunk_kernel,
        out_shape=jax.ShapeDtypeStruct((batch, seq, R), BF16),
        grid=(batch // nb, nt),
        in_specs=[blk] * 7 + [full(m) for m in masks] + [pvec] * 3,
        out_specs=blk,
        scratch_shapes=[pltpu.VMEM((nb * (R // W), W, W), F32)],
        compiler_params=_cparams(("parallel", "arbitrary")),
        name="rwkv_chunk",
    )(*seqs, *masks, r_k.reshape(1, R), ln_w.reshape(1, R), ln_b.reshape(1, R))
    return out.reshape(N, R)


def _gelu_tanh(x):
    return 0.5 * x * (1.0 + jnp.tanh(np.sqrt(2.0 / np.pi) * (x + 0.044715 * (x * x * x))))


def _nsa_compress_kernel(x_ref, pos_ref, w1_ref, w2_ref, o_ref):
    x = x_ref[0, 0]
    w1 = w1_ref[0]
    half = w1.shape[0] // 2
    nrow = x.shape[0]
    ha = _dot(x, w1[:half])
    hb = _dot(x, w1[half:])
    h = ha + pltpu.roll(hb, nrow - 1, 0)
    pb = _dot(pos_ref[0], w1)
    h = _gelu_tanh(h + pb[0:1, :])
    o_ref[0, 0, 0] = _dot(h.astype(BF16), w2_ref[0]).astype(o_ref.dtype)


def nsa_compress(kv_cmp, pos, w1, w2):
    two, B, G, T, dh = kv_cmp.shape
    nr = T // NSA_CMP_STRIDE
    x = kv_cmp.reshape(two * B, G, nr, NSA_CMP_STRIDE * dh)
    posf = jnp.broadcast_to(pos.reshape(two, 1, NSA_CMP_LEN * dh), (two, 8, NSA_CMP_LEN * dh)).astype(BF16)
    return pl.pallas_call(
        _nsa_compress_kernel,
        out_shape=jax.ShapeDtypeStruct((two, B, G, nr, dh), BF16),
        grid=(two, B, G),
        in_specs=[
            pl.BlockSpec((1, 1, nr, NSA_CMP_STRIDE * dh), lambda s, b, g: (s * B + b, g, 0, 0)),
            pl.BlockSpec((1, 8, NSA_CMP_LEN * dh), lambda s, b, g: (s, 0, 0)),
            pl.BlockSpec((1, NSA_CMP_LEN * dh, NSA_CMP_HIDDEN), lambda s, b, g: (s, 0, 0)),
            pl.BlockSpec((1, NSA_CMP_HIDDEN, dh), lambda s, b, g: (s, 0, 0)),
        ],
        out_specs=pl.BlockSpec((1, 1, 1, nr, dh), lambda s, b, g: (s, b, g, 0, 0)),
        compiler_params=_cparams(("parallel", "parallel", "parallel")),
        name="nsa_compress",
    )(x, posf, w1.astype(BF16), w2.astype(BF16))


def _nsa_cmp_kernel(q_ref, kc_ref, vct_ref, ov_ref, cb_ref, oc_ref, sb_ref, qs_ref, *, n_slc, n_sel):
    qi = pl.program_id(2)
    QB = CMP_QB
    ncp = kc_ref.shape[2]
    for h in range(NSA_HPG):
        qs_ref[h * QB:(h + 1) * QB, :] = q_ref[:, h * NSA_DH:(h + 1) * NSA_DH] * (NSA_DH ** -0.5)
    s = _dot_nt(kc_ref[0, 0], qs_ref[...])
    vct = vct_ref[0, 0]
    psum = jnp.zeros((ncp, QB), F32)
    for h in range(NSA_HPG):
        sh = s[:, h * QB:(h + 1) * QB] + cb_ref[h]
        m = jnp.maximum(jnp.max(sh, axis=0, keepdims=True), -1e20)
        e = jnp.exp(sh - m)
        acc = _dot(vct, e.astype(BF16))
        inv_l = 1.0 / jnp.maximum(acc[NSA_DH:NSA_DH + 1], 1e-30)
        oc_ref[:, h * NSA_DH:(h + 1) * NSA_DH] = (acc[0:NSA_DH] * inv_l).T.astype(oc_ref.dtype)
        psum = psum + e * inv_l
    imp = _dot_sel(ov_ref[...], psum)
    j = lax.broadcasted_iota(jnp.int32, (n_slc, QB), 0)
    tt = qi * QB + lax.broadcasted_iota(jnp.int32, (n_slc, QB), 1)
    cur = jnp.right_shift(tt, 6)
    forced = (j == 0) | (j == cur) | (j == cur - 1)
    cand = (j >= 1) & (j <= cur - 2)
    n_free = n_sel - 3

    def emit(sel):
        sb_ref[0, 0] = jnp.where(sel, 0.0, -SEL_MASK_BIAS).T.astype(sb_ref.dtype)

    last_cur = (qi * QB + QB - 1) // NSA_SLC_LEN

    @pl.when(last_cur - 2 <= n_free)
    def _():
        emit(forced | cand)

    @pl.when(last_cur - 2 > n_free)
    def _():
        cur_row = cur[0:1, :]
        rank = jnp.zeros((n_slc, QB), jnp.int32)
        for jp in range(1, n_slc):
            row = imp[jp:jp + 1, :]
            ahead = (row > imp) | ((row == imp) & (j > jp))
            rank = rank + jnp.where(ahead & (jp <= cur_row - 2), 1, 0)
        emit(forced | (cand & (rank < n_free)))


def nsa_cmp(U, kc, vct, slopes, *, batch, seq):
    N = U.shape[0]
    QB = CMP_QB
    nq = seq // QB
    ncp = kc.shape[2]
    n_slc = seq // NSA_SLC_LEN
    n_sel = min(NSA_N_SEL, n_slc)
    nn = np.arange(ncp)
    jj = np.arange(n_slc)
    ov = ((nn[None, :] * NSA_CMP_STRIDE + NSA_CMP_LEN - 1 >= jj[:, None] * NSA_SLC_LEN)
          & (nn[None, :] * NSA_CMP_STRIDE <= jj[:, None] * NSA_SLC_LEN + NSA_SLC_LEN - 1)
          & (nn[None, :] < ncp - 1)).astype(np.float32)
    tt = np.arange(seq)
    ended = jnp.asarray(nn[:, None] * NSA_CMP_STRIDE + NSA_CMP_LEN - 1 <= tt[None, :])
    adist = np.abs(tt[None, :] - (nn[:, None] * NSA_CMP_STRIDE + (NSA_CMP_LEN - 1) / 2.0)).astype(np.float32)
    cbias = jnp.where(ended[None], -slopes[:, None, None] * jnp.asarray(adist)[None], NEG_BIG)
    G = NSA_GROUPS
    W = NSA_HPG * NSA_DH
    return pl.pallas_call(
        functools.partial(_nsa_cmp_kernel, n_slc=n_slc, n_sel=n_sel),
        out_shape=(jax.ShapeDtypeStruct((N, NSA_HEADS * NSA_DH), BF16),
                   jax.ShapeDtypeStruct((batch, G, seq, n_slc), BF16)),
        grid=(batch, G, nq),
        in_specs=[
            pl.BlockSpec((QB, W), lambda b, g, i: (b * nq + i, U_NSA_Q // W + g)),
            pl.BlockSpec((1, 1, ncp, NSA_DH), lambda b, g, i: (b, g, 0, 0)),
            pl.BlockSpec((1, 1, NSA_DH + 8, ncp), lambda b, g, i: (b, g, 0, 0)),
            pl.BlockSpec((n_slc, ncp), lambda b, g, i: (0, 0)),
            pl.BlockSpec((NSA_HPG, ncp, QB), lambda b, g, i: (g, 0, i)),
        ],
        out_specs=(
            pl.BlockSpec((QB, W), lambda b, g, i: (b * nq + i, g)),
            pl.BlockSpec((1, 1, QB, n_slc), lambda b, g, i: (b, g, i, 0)),
        ),
        scratch_shapes=[pltpu.VMEM((NSA_HPG * QB, NSA_DH), BF16)],
        compiler_params=_cparams(("parallel", "parallel", "arbitrary")),
        name="nsa_cmp",
    )(U, kc, vct, jnp.asarray(ov), cbias)


def _nsa_win_kernel(q_ref, k0_ref, k1_ref, k2_ref, v0_ref, v1_ref, v2_ref, bias_ref, o_ref, qs_ref):
    qi = pl.program_id(2)
    QB = NSA_QB
    KW = NSA_WINDOW + QB
    for h in range(NSA_HPG):
        qs_ref[h * QB:(h + 1) * QB, :] = q_ref[:, h * NSA_DH:(h + 1) * NSA_DH] * (NSA_DH ** -0.5)
    k = jnp.concatenate([k0_ref[0, 0], k1_ref[0, 0], k2_ref[0, 0]], axis=0)
    vt = jnp.concatenate([v0_ref[0, 0], v1_ref[0, 0], v2_ref[0, 0]], axis=1)
    s = _dot_nt(k, qs_ref[...])

    def finish(before_start):
        for h in range(NSA_HPG):
            cols = slice(h * QB, (h + 1) * QB)
            sh = s[:, cols] + bias_ref[h]
            if before_start:
                r = lax.broadcasted_iota(jnp.int32, (KW, QB), 0)
                sh = jnp.where(r >= NSA_WINDOW - qi * QB, sh, NEG_BIG)
            m = jnp.max(sh, axis=0, keepdims=True)
            e = jnp.exp(sh - m)
            acc = _dot(vt, e.astype(BF16))
            o = acc[0:NSA_DH] / acc[NSA_DH:NSA_DH + 1]
            o_ref[:, h * NSA_DH:(h + 1) * NSA_DH] = o.T.astype(o_ref.dtype)

    @pl.when(qi * QB < NSA_WINDOW)
    def _():
        finish(True)

    @pl.when(qi * QB >= NSA_WINDOW)
    def _():
        finish(False)


def nsa_win(U, kw_pad, vtw_pad, slopes, *, batch, seq):
    N = U.shape[0]
    QB = NSA_QB
    KW = NSA_WINDOW + QB
    nq = seq // QB
    G = NSA_GROUPS
    W = NSA_HPG * NSA_DH
    dist = (np.arange(QB)[None, :] - np.arange(KW)[:, None] + NSA_WINDOW).astype(np.float32)
    inside = jnp.asarray((dist >= 0) & (dist < NSA_WINDOW))
    bias = jnp.where(inside[None], -slopes[:, None, None] * jnp.asarray(dist)[None], NEG_BIG)
    kb = lambda off: pl.BlockSpec((1, 1, QB, NSA_DH), lambda b, g, i: (b, g, i + off, 0))
    vb = lambda off: pl.BlockSpec((1, 1, NSA_DH + 8, QB), lambda b, g, i: (b, g, 0, i + off))
    return pl.pallas_call(
        _nsa_win_kernel,
        out_shape=jax.ShapeDtypeStruct((N, NSA_HEADS * NSA_DH), BF16),
        grid=(batch, G, nq),
        in_specs=[pl.BlockSpec((QB, W), lambda b, g, i: (b * nq + i, U_NSA_Q // W + g)),
                  kb(0), kb(1), kb(2), vb(0), vb(1), vb(2),
                  pl.BlockSpec((NSA_HPG, KW, QB), lambda b, g, i: (g, 0, 0))],
        out_specs=pl.BlockSpec((QB, W), lambda b, g, i: (b * nq + i, g)),
        scratch_shapes=[pltpu.VMEM((NSA_HPG * QB, NSA_DH), BF16)],
        compiler_params=_cparams(("parallel", "parallel", "arbitrary")),
        name="nsa_win",
    )(U, kw_pad, kw_pad, kw_pad, vtw_pad, vtw_pad, vtw_pad, bias)


SEL_QB = 512
SEL_KV = 512


def _sel_pairs(seq):
    qs, ks = [], []
    for qi in range(seq // SEL_QB):
        for kj in range((qi * SEL_QB) // SEL_KV + 1):
            qs.append(qi)
            ks.append(kj)
    return np.asarray(qs, np.int32), np.asarray(ks, np.int32)


def _nsa_sel_kernel(qi_ref, kj_ref, slope_ref, q_ref, sb_ref, qx_ref, ka_ref, vt_ref, o_ref,
                    qa_ref, m_ref, acc_ref):
    g = pl.program_id(1)
    p = pl.program_id(2)
    qi = qi_ref[p]
    kj = kj_ref[p]
    QB, KV = SEL_QB, SEL_KV
    nblk = sb_ref.shape[3]
    base = NSA_DH + nblk

    @pl.when(kj == 0)
    def _():
        for h in range(NSA_HPG):
            rows = slice(h * QB, (h + 1) * QB)
            qa_ref[rows, 0:NSA_DH] = q_ref[:, h * NSA_DH:(h + 1) * NSA_DH] * (NSA_DH ** -0.5)
            qa_ref[rows, NSA_DH:base] = sb_ref[0, 0]
            qa_ref[rows, base:] = jnp.broadcast_to(qx_ref[0, h:h + 1, :], (QB, qa_ref.shape[1] - base))
        m_ref[...] = jnp.full_like(m_ref, NEG_BIG)
        acc_ref[...] = jnp.zeros_like(acc_ref)

    off = qi * QB - kj * KV
    tile_start = (kj * KV).astype(F32)

    def step(masked):
        s = _dot_nt(ka_ref[0, 0], qa_ref[...])
        vt = vt_ref[0, 0]
        if masked:
            r = lax.broadcasted_iota(jnp.int32, (KV, QB), 0)
            c = lax.broadcasted_iota(jnp.int32, (KV, QB), 1)
            causal = (c - r + off) >= 0
        for h in range(NSA_HPG):
            cols = slice(h * QB, (h + 1) * QB)
            sh = s[:, cols]
            if masked:
                sh = jnp.where(causal, sh, NEG_BIG)
            delta = slope_ref[g * NSA_HPG + h] * tile_start
            m_old = m_ref[h:h + 1, :]
            m_new = jnp.maximum(m_old, jnp.max(sh, axis=0, keepdims=True) + delta)
            e = jnp.exp(sh - (m_new - delta))
            alpha = jnp.exp(m_old - m_new)
            acc_ref[:, cols] = alpha * acc_ref[:, cols] + _dot(vt, e.astype(BF16))
            m_ref[h:h + 1, :] = m_new

    last = (qi * QB) // KV

    @pl.when(kj < last)
    def _():
        step(False)

    @pl.when(kj == last)
    def _():
        step(True)
        for h in range(NSA_HPG):
            cols = slice(h * QB, (h + 1) * QB)
            o = acc_ref[0:NSA_DH, cols] / acc_ref[NSA_DH:NSA_DH + 1, cols]
            o_ref[:, h * NSA_DH:(h + 1) * NSA_DH] = o.T.astype(o_ref.dtype)


def sel_key_columns(seq):
    n_slc = seq // NSA_SLC_LEN
    pos = np.arange(seq)
    onehot = (pos[:, None] // NSA_SLC_LEN == np.arange(n_slc)[None, :]).astype(np.float32)
    r = pos % SEL_KV
    extra = np.zeros((seq, 128), np.float32)
    extra[:, 0] = extra[:, 2] = (r // 32) * 32
    extra[:, 1] = extra[:, 3] = r % 32
    return jnp.asarray(np.concatenate([onehot, extra], axis=1), BF16)


def nsa_sel(U, selbias, k_aug, vt_slc, slopes, *, batch, seq):
    N = U.shape[0]
    QB, KV = SEL_QB, SEL_KV
    nq = seq // QB
    G = NSA_GROUPS
    W = NSA_HPG * NSA_DH
    n_slc = selbias.shape[3]
    ka_w = k_aug.shape[3]
    qs, ks = _sel_pairs(seq)
    s_hi = slopes.astype(BF16)
    s_lo = (slopes - s_hi.astype(F32)).astype(BF16)
    qx = jnp.zeros((NSA_HEADS, 128), BF16).at[:, 0].set(s_hi).at[:, 1].set(s_hi).at[:, 2].set(s_lo).at[:, 3].set(s_lo)
    qx = jnp.pad(qx.reshape(G, NSA_HPG, 128), ((0, 0), (0, 8 - NSA_HPG), (0, 0)))
    grid_spec = pltpu.PrefetchScalarGridSpec(
        num_scalar_prefetch=3,
        grid=(batch, G, len(qs)),
        in_specs=[
            pl.BlockSpec((QB, W), lambda b, g, p, qi, kj, s: (b * nq + qi[p], U_NSA_Q // W + g)),
            pl.BlockSpec((1, 1, QB, n_slc), lambda b, g, p, qi, kj, s: (b, g, qi[p], 0)),
            pl.BlockSpec((1, 8, 128), lambda b, g, p, qi, kj, s: (g, 0, 0)),
            pl.BlockSpec((1, 1, KV, ka_w), lambda b, g, p, qi, kj, s: (b, g, kj[p], 0)),
            pl.BlockSpec((1, 1, NSA_DH + 8, KV), lambda b, g, p, qi, kj, s: (b, g, 0, kj[p])),
        ],
        out_specs=pl.BlockSpec((QB, W), lambda b, g, p, qi, kj, s: (b * nq + qi[p], g)),
        scratch_shapes=[
            pltpu.VMEM((NSA_HPG * QB, ka_w), BF16),
            pltpu.VMEM((NSA_HPG, QB), F32),
            pltpu.VMEM((NSA_DH + 8, NSA_HPG * QB), F32),
        ],
    )
    return pl.pallas_call(
        _nsa_sel_kernel,
        out_shape=jax.ShapeDtypeStruct((N, NSA_HEADS * NSA_DH), BF16),
        grid_spec=grid_spec,
        compiler_params=_cparams(("parallel", "parallel", "arbitrary")),
        name="nsa_sel",
    )(jnp.asarray(qs), jnp.asarray(ks), slopes, U, selbias, qx, k_aug, vt_slc)


def _nsa_combine_kernel(oc_ref, os_ref, ow_ref, gate_ref, e_ref, o_ref):
    ge = _dot_split(_sigmoid(gate_ref[...]), e_ref[...])
    Wd = NSA_HEADS * NSA_DH
    o = (ge[:, :Wd] * oc_ref[...].astype(F32) + ge[:, Wd:2 * Wd] * os_ref[...].astype(F32)
         + ge[:, 2 * Wd:] * ow_ref[...].astype(F32))
    o_ref[...] = o.astype(o_ref.dtype)


def nsa_combine(o_c, o_s, o_w, Us):
    N, Wd = o_c.shape
    tm = 512
    e = np.zeros((128, 3 * Wd), np.float32)
    for h in range(NSA_HEADS):
        for j in range(3):
            e[h * 3 + j, j * Wd + h * NSA_DH:j * Wd + (h + 1) * NSA_DH] = 1.0
    blk = pl.BlockSpec((tm, Wd), lambda i: (i, 0))
    return pl.pallas_call(
        _nsa_combine_kernel,
        out_shape=jax.ShapeDtypeStruct((N, Wd), BF16),
        grid=(N // tm,),
        in_specs=[blk, blk, blk, pl.BlockSpec((tm, 128), lambda i: (i, S_NSA_GATE // 128)),
                  pl.BlockSpec((128, 3 * Wd), lambda i: (0, 0))],
        out_specs=blk,
        compiler_params=_cparams(("parallel",)),
        name="nsa_combine",
    )(o_c, o_s, o_w, Us, jnp.asarray(e, BF16))


def _merge_kernel(ya_ref, yb_ref, yc_ref, ga_ref, gb_ref, gc_ref, p_ref, o_ref):
    m = (_sigmoid(ga_ref[...].astype(F32)) * _dot(ya_ref[...], p_ref[0])
         + _sigmoid(gb_ref[...].astype(F32)) * _dot(yb_ref[...], p_ref[1])
         + _sigmoid(gc_ref[...].astype(F32)) * _dot(yc_ref[...], p_ref[2]))
    o_ref[...] = m.astype(o_ref.dtype)


def merge(y_a, y_b, y_c, U, p_merge):
    N = y_a.shape[0]
    tm, tn = 1024, 512
    yb = pl.BlockSpec((tm, BRANCH_DIM), lambda i, j: (i, 0))
    gate = lambda br: pl.BlockSpec((tm, tn), lambda i, j: (i, (U_GATE + br * D_MODEL) // tn + j))
    return pl.pallas_call(
        _merge_kernel,
        out_shape=jax.ShapeDtypeStruct((N, D_MODEL), BF16),
        grid=(N // tm, D_MODEL // tn),
        in_specs=[yb, yb, yb, gate(0), gate(1), gate(2),
                  pl.BlockSpec((3, BRANCH_DIM, tn), lambda i, j: (0, 0, j))],
        out_specs=pl.BlockSpec((tm, tn), lambda i, j: (i, j)),
        compiler_params=_cparams(("parallel", "parallel")),
        name="merge",
    )(y_a, y_b, y_c, U, U, U, p_merge)


def _ffn_kernel(h_ref, w1_ref, w3_ref, w2_ref, o_ref, acc_ref):
    f = pl.program_id(1)

    @pl.when(f == 0)
    def _():
        acc_ref[...] = jnp.zeros_like(acc_ref)

    h = h_ref[...]
    a = _dot(h, w1_ref[...])
    z = (a * _sigmoid(a) * _dot(h, w3_ref[...])).astype(BF16)
    acc_ref[...] += _dot(z, w2_ref[...])

    @pl.when(f == pl.num_programs(1) - 1)
    def _():
        o_ref[...] = acc_ref[...].astype(o_ref.dtype)


def ffn(h, w1, w3, w2):
    N, D = h.shape
    F = w1.shape[1]
    tm, tf = 1024, 512
    return pl.pallas_call(
        _ffn_kernel,
        out_shape=jax.ShapeDtypeStruct((N, D), BF16),
        grid=(N // tm, F // tf),
        in_specs=[pl.BlockSpec((tm, D), lambda i, f: (i, 0)),
                  pl.BlockSpec((D, tf), lambda i, f: (0, f)),
                  pl.BlockSpec((D, tf), lambda i, f: (0, f)),
                  pl.BlockSpec((tf, D), lambda i, f: (f, 0))],
        out_specs=pl.BlockSpec((tm, D), lambda i, f: (i, 0)),
        scratch_shapes=[pltpu.VMEM((tm, D), F32)],
        compiler_params=_cparams(("parallel", "arbitrary")),
        name="ffn",
    )(h, w1, w3, w2)


def _route_kernel(lg_ref, cb_ref, sel_ref):
    lg = lg_ref[...]
    lane = lax.broadcasted_iota(jnp.int32, lg.shape, 1)
    x = jnp.where(lane < N_EXPERTS, lg, NEG_BIG)
    v1 = jnp.max(x, axis=-1, keepdims=True)
    i1 = jnp.min(jnp.where(x == v1, lane, 1024), axis=-1, keepdims=True)
    x2 = jnp.where(lane == i1, NEG_BIG, x)
    v2 = jnp.max(x2, axis=-1, keepdims=True)
    i2 = jnp.min(jnp.where(x2 == v2, lane, 1024), axis=-1, keepdims=True)
    e2 = jnp.exp(v2 - v1)
    w1 = 1.0 / (1.0 + e2)
    w2 = e2 / (1.0 + e2)
    cb_ref[...] = jnp.where(lane == i1, w1, 0.0) + jnp.where(lane == i2, w2, 0.0)
    sel_ref[...] = jnp.where((lane == i1) | (lane == i2), 1.0, 0.0)


def route(logits):
    N = logits.shape[0]
    tm = 1024
    blk = pl.BlockSpec((tm, 128), lambda i: (i, 0))
    return pl.pallas_call(
        _route_kernel,
        out_shape=(jax.ShapeDtypeStruct((N, 128), F32), jax.ShapeDtypeStruct((N, 128), F32)),
        grid=(N // tm,),
        in_specs=[blk],
        out_specs=(blk, blk),
        compiler_params=_cparams(("parallel",)),
        name="route",
    )(logits)


MOE_TM = 1024
MOE_TF = 256
MOE_WB = 512
MOE_GT = 256


def _moe_plan(sel):
    N, E = sel.shape
    tm, wb, gt = MOE_TM, MOE_WB, MOE_GT
    sub = tm // gt
    NT = 2 * N // tm + E
    NG = NT * sub
    R = NT * tm
    P = NG + E * (N // wb)
    i32 = jnp.int32
    cs = jnp.cumsum(sel.astype(i32), axis=0)
    cnt = cs[-1]
    tiles_e = (cnt + tm - 1) // tm
    tile_end = jnp.cumsum(tiles_e)
    tile_start = tile_end - tiles_e
    total_tiles = tile_end[-1]
    dest = jnp.where(sel, tile_start[None, :] * tm + cs - 1, R)
    ti = jnp.arange(NT, dtype=i32)
    tile_valid = ti < total_tiles
    tile_e = jnp.minimum(jnp.searchsorted(tile_end, ti, side="right").astype(i32), E - 1)
    gi = jnp.arange(NG, dtype=i32)
    g_valid = tile_valid[gi // sub]
    g_e = tile_e[gi // sub]
    k_lo = gi * gt - tile_start[g_e] * tm
    nonempty = g_valid & (k_lo < cnt[g_e])
    k_hi = jnp.minimum(k_lo + gt, cnt[g_e]) - 1
    find = jax.vmap(lambda col, q: jnp.searchsorted(col, q, side="left"), in_axes=(1, None), out_axes=1)
    pick = lambda m: jnp.take_along_axis(m, g_e[:, None], axis=1)[:, 0].astype(i32)
    blo = jnp.where(nonempty, pick(find(cs, k_lo + 1)) // wb, 0)
    bhi = jnp.where(nonempty, pick(find(cs, k_hi + 1)) // wb, jnp.where(g_valid, 0, -1))
    npair = bhi - blo + 1
    pend = jnp.cumsum(npair)
    pstart = pend - npair
    total_p = pend[-1]
    pi = jnp.arange(P, dtype=i32)
    pvalid = pi < total_p
    ptile = jnp.minimum(jnp.searchsorted(pend, pi, side="right").astype(i32), NG - 1)
    pblk = blo[ptile] + pi - pstart[ptile]
    pfirst = pvalid & (pi == pstart[ptile])
    ptile = jnp.where(pvalid, ptile, ptile[total_p - 1])
    pblk = jnp.where(pvalid, pblk, pblk[total_p - 1])
    pexp = g_e[ptile]
    order = jnp.argsort(jnp.where(pvalid, pblk * NG + ptile, jnp.iinfo(jnp.int32).max))
    s_valid = pvalid
    s_tile = jnp.where(s_valid, ptile[order], ptile[order][total_p - 1])
    s_blk = jnp.where(s_valid, pblk[order], pblk[order][total_p - 1])
    s_exp = g_e[s_tile]
    s_first = s_valid & ((pi == 0) | (s_blk != jnp.roll(s_blk, 1)))
    e_src = jnp.where(tile_valid, ti, 0)
    e_exp = jnp.where(tile_valid, tile_e, tile_e[jnp.maximum(total_tiles - 1, 0)])
    b2i = lambda x: x.astype(i32)
    return dict(dest=dest, g=(ptile, pblk, pexp, b2i(pvalid), b2i(pfirst)),
                e=(e_exp, b2i(tile_valid), e_src), s=(s_tile, s_blk, s_exp, b2i(s_valid), b2i(s_first)), NT=NT, P=P)


def _moe_group_kernel(pt_ref, pb_ref, pe_ref, pv_ref, pf_ref, dest_ref, h_ref, o_ref):
    p = pl.program_id(0)

    @pl.when(pv_ref[p] == 1)
    def _():
        rel = dest_ref[0] - pt_ref[p] * MOE_GT
        row = lax.broadcasted_iota(jnp.int32, (MOE_GT, MOE_WB), 0)
        onehot = jnp.where(rel == row, 1.0, 0.0).astype(BF16)
        rows = _dot(onehot, h_ref[...]).astype(o_ref.dtype)

        @pl.when(pf_ref[p] == 1)
        def _():
            o_ref[...] = rows

        @pl.when(pf_ref[p] == 0)
        def _():
            o_ref[...] += rows


def _moe_expert_kernel(te_ref, tv_ref, ts_ref, h_ref, w1_ref, w3_ref, w2_ref, o_ref, acc_ref):
    i = pl.program_id(0)
    f = pl.program_id(1)
    nf = pl.num_programs(1)

    @pl.when(tv_ref[i] == 1)
    def _():
        @pl.when(f == 0)
        def _():
            acc_ref[...] = jnp.zeros_like(acc_ref)

        h = h_ref[...]
        a = _dot(h, w1_ref[0, 0].astype(BF16))
        z = (a * _sigmoid(a) * _dot(h, w3_ref[0, 0].astype(BF16))).astype(BF16)
        acc_ref[...] += _dot(z, w2_ref[0, 0].astype(BF16))

        @pl.when(f == nf - 1)
        def _():
            o_ref[...] = acc_ref[...].astype(o_ref.dtype)

    @pl.when((tv_ref[i] == 0) & (f == nf - 1))
    def _():
        o_ref[...] = jnp.zeros_like(o_ref)


def _moe_ungroup_kernel(st_ref, sb_ref, se_ref, sv_ref, sf_ref, dest_ref, w_ref, y_ref, o_ref):
    p = pl.program_id(0)

    @pl.when(sv_ref[p] == 1)
    def _():
        rel = dest_ref[0] - st_ref[p] * MOE_GT
        col = lax.broadcasted_iota(jnp.int32, (MOE_WB, MOE_GT), 1)
        onehot_t = jnp.where(rel == col, 1.0, 0.0).astype(BF16)
        part = _dot(onehot_t, y_ref[...]) * w_ref[0]

        @pl.when(sf_ref[p] == 1)
        def _():
            o_ref[...] = part

        @pl.when(sf_ref[p] == 0)
        def _():
            o_ref[...] += part


def moe(h, combine, selm, w1, w3, w2, layer):
    N, D = h.shape
    _, E, _, F = w1.shape
    tm, wb, gt, tf = MOE_TM, MOE_WB, MOE_GT, MOE_TF
    plan = _moe_plan(selm[:, :E] > 0.5)
    NT, P = plan["NT"], plan["P"]
    R = NT * tm
    dest_t = plan["dest"].T
    h_sorted = pl.pallas_call(
        _moe_group_kernel,
        out_shape=jax.ShapeDtypeStruct((R, D), BF16),
        grid_spec=pltpu.PrefetchScalarGridSpec(
            num_scalar_prefetch=5,
            grid=(P,),
            in_specs=[pl.BlockSpec((1, 1, wb), lambda p, pt, pb, pe, pv, pf: (pe[p], 0, pb[p])),
                      pl.BlockSpec((wb, D), lambda p, pt, pb, pe, pv, pf: (pb[p], 0))],
            out_specs=pl.BlockSpec((gt, D), lambda p, pt, pb, pe, pv, pf: (pt[p], 0)),
        ),
        compiler_params=_cparams(("arbitrary",)),
        name="moe_group",
    )(*plan["g"], dest_t.reshape(E, 1, N), h)
    y_sorted = pl.pallas_call(
        _moe_expert_kernel,
        out_shape=jax.ShapeDtypeStruct((R, D), BF16),
        grid_spec=pltpu.PrefetchScalarGridSpec(
            num_scalar_prefetch=3,
            grid=(NT, F // tf),
            in_specs=[pl.BlockSpec((tm, D), lambda i, f, te, tv, ts: (ts[i], 0)),
                      pl.BlockSpec((1, 1, D, tf), lambda i, f, te, tv, ts: (layer, te[i], 0, f * tv[i])),
                      pl.BlockSpec((1, 1, D, tf), lambda i, f, te, tv, ts: (layer, te[i], 0, f * tv[i])),
                      pl.BlockSpec((1, 1, tf, D), lambda i, f, te, tv, ts: (layer, te[i], f * tv[i], 0))],
            out_specs=pl.BlockSpec((tm, D), lambda i, f, te, tv, ts: (i, 0)),
            scratch_shapes=[pltpu.VMEM((tm, D), F32)],
        ),
        compiler_params=_cparams(("arbitrary", "arbitrary")),
        name="moe_expert",
    )(*plan["e"], h_sorted, w1, w3, w2)
    return pl.pallas_call(
        _moe_ungroup_kernel,
        out_shape=jax.ShapeDtypeStruct((N, D), F32),
        grid_spec=pltpu.PrefetchScalarGridSpec(
            num_scalar_prefetch=5,
            grid=(P,),
            in_specs=[pl.BlockSpec((1, wb, 1), lambda p, st, sb, se, sv, sf: (se[p], sb[p], 0)),
                      pl.BlockSpec((1, wb, 1), lambda p, st, sb, se, sv, sf: (se[p], sb[p], 0)),
                      pl.BlockSpec((gt, D), lambda p, st, sb, se, sv, sf: (st[p], 0))],
            out_specs=pl.BlockSpec((wb, D), lambda p, st, sb, se, sv, sf: (sb[p], 0)),
        ),
        compiler_params=_cparams(("arbitrary",)),
        name="moe_ungroup",
    )(*plan["s"], dest_t.reshape(E, N, 1), combine[:, :E].T.reshape(E, N, 1), y_sorted)


def _project_weights(w_in):
    gla_w = 2 * GLA_HEADS * GLA_DK + 2 * GLA_HEADS * GLA_DV + GLA_RANK
    rw_w = 3 * RWKV_DIM + RWKV_W_RANK + RWKV_A_RANK + RWKV_G_RANK
    kvw = NSA_GROUPS * NSA_DH
    nsa_w = NSA_HEADS * NSA_DH + 6 * kvw + NSA_HEADS * 3
    o_rw = gla_w
    o_nsa = gla_w + rw_w
    o_gate = o_nsa + nsa_w
    big = jnp.concatenate([
        w_in[:, 0:gla_w - GLA_RANK],
        w_in[:, o_rw:o_rw + 3 * RWKV_DIM],
        w_in[:, o_nsa:o_nsa + NSA_HEADS * NSA_DH + 6 * kvw],
        w_in[:, o_gate:],
    ], axis=1).astype(BF16)
    D = w_in.shape[0]
    z = lambda n: jnp.zeros((D, n), w_in.dtype)
    small = jnp.concatenate([
        w_in[:, gla_w - GLA_RANK:gla_w], z(128 - GLA_RANK),
        w_in[:, o_rw + 3 * RWKV_DIM:o_rw + rw_w], z(256 - RWKV_G_RANK),
        w_in[:, o_gate - NSA_HEADS * 3:o_gate], z(128 - NSA_HEADS * 3),
    ], axis=1).astype(BF16)
    return big, small


def _mixer(h, batch, seq, w_in, gla_a2, gla_a_b, gla_norm, rwkv_mu, rwkv_w0, rwkv_w2, rwkv_a0, rwkv_a2, rwkv_g2,
           rwkv_k_k, rwkv_k_a, rwkv_r_k, rwkv_ln_w, rwkv_ln_b,
           nsa_pos_k, nsa_w1_k, nsa_w2_k, nsa_pos_v, nsa_w1_v, nsa_w2_v, p_merge, w_out, slopes):
    N = h.shape[0]
    w_big, w_small = _project_weights(w_in)
    tm_u = 2048 if N % 2048 == 0 else N
    U = matmul(h, w_big, BF16, tm_u, 512)
    Us = matmul(h, w_small, F32, tm_u, S_COLS)

    y_a = gla(U, Us, gla_a2, gla_a_b, gla_norm, batch=batch, seq=seq)

    r, k, v, kap, b, lw, gate = rwkv_prep(U, Us, rwkv_mu, rwkv_w0, rwkv_w2, rwkv_a0, rwkv_a2, rwkv_g2,
                                          rwkv_k_k, rwkv_k_a, seq=seq)
    y_b = rwkv_chunk(r, k, v, kap, b, lw, gate, rwkv_r_k.reshape(-1), rwkv_ln_w, rwkv_ln_b, batch=batch, seq=seq)

    kv6 = U[:, U_NSA_KV:U_NSA_KV + 6 * NSA_GROUPS * NSA_DH]
    kv6 = kv6.reshape(batch, seq, 6, NSA_GROUPS, NSA_DH).transpose(2, 0, 3, 1, 4)
    kvc = nsa_compress(kv6[0:2], jnp.stack([nsa_pos_k, nsa_pos_v]), jnp.stack([nsa_w1_k, nsa_w1_v]),
                       jnp.stack([nsa_w2_k, nsa_w2_v]))
    ncp = kvc.shape[3]
    ones_c = jnp.zeros((batch, NSA_GROUPS, 8, ncp), BF16).at[:, :, 0, :].set(1.0)
    vct = jnp.concatenate([kvc[1].transpose(0, 1, 3, 2), ones_c], axis=2)
    o_c, selbias = nsa_cmp(U, kvc[0], vct, slopes, batch=batch, seq=seq)
    n_slc = seq // NSA_SLC_LEN
    k_aug = jnp.concatenate([kv6[2], jnp.broadcast_to(sel_key_columns(seq), (batch, NSA_GROUPS, seq, n_slc + 128))],
                            axis=-1)
    ones_rows = jnp.zeros((batch, NSA_GROUPS, 8, seq), BF16).at[:, :, 0, :].set(1.0)
    vt_aug = jnp.concatenate([kv6[3].transpose(0, 1, 3, 2), ones_rows], axis=2)
    o_s = nsa_sel(U, selbias, k_aug, vt_aug, slopes, batch=batch, seq=seq)
    kw_pad = jnp.pad(kv6[4], ((0, 0), (0, 0), (NSA_WINDOW, 0), (0, 0)))
    vtw = jnp.concatenate([kv6[5].transpose(0, 1, 3, 2), ones_rows], axis=2)
    vtw_pad = jnp.pad(vtw, ((0, 0), (0, 0), (0, 0), (NSA_WINDOW, 0)))
    o_w = nsa_win(U, kw_pad, vtw_pad, slopes, batch=batch, seq=seq)
    y_c = nsa_combine(o_c, o_s, o_w, Us)

    merged = merge(y_a, y_b, y_c, U, p_merge.astype(BF16))
    return matmul(merged, w_out.astype(BF16), BF16, 1024 if N % 1024 == 0 else N, 1024)


def kernel(x, c, norm_mix, norm_ffn, ada_w, ada_b, w_in, gla_a2, gla_a_b, gla_norm, rwkv_mu, rwkv_w0, rwkv_w2, rwkv_a0, rwkv_a2, rwkv_g2, rwkv_k_k, rwkv_k_a, rwkv_r_k, rwkv_ln_w, rwkv_ln_b, nsa_pos_k, nsa_w1_k, nsa_w2_k, nsa_pos_v, nsa_w1_v, nsa_w2_v, p_merge, w_out, ffn_w1, ffn_w3, ffn_w2, moe_router, moe_w1, moe_w3, moe_w2, final_norm):
    B, T, D = x.shape
    depth = w_in.shape[0]
    N = B * T
    xs = x.reshape(N, D)
    ada = ada_all(c, ada_w, ada_b)
    slopes = jnp.exp2(-8.0 * jnp.arange(1, NSA_HEADS + 1, dtype=F32) / NSA_HEADS)
    y = None
    g_prev = None
    for l in range(depth):
        sh1, sc1, g1, sh2, sc2, g2 = jnp.split(ada[l], 6, axis=-1)
        if y is None:
            (h,) = resmod(xs, None, None, norm_mix[l], sh1, sc1, seq=T)
        else:
            xs, h = resmod(xs, y, g_prev, norm_mix[l], sh1, sc1, seq=T)
        y = _mixer(h, B, T, w_in[l], gla_a2[l], gla_a_b[l], gla_norm[l], rwkv_mu[l], rwkv_w0[l], rwkv_w2[l],
                   rwkv_a0[l], rwkv_a2[l], rwkv_g2[l], rwkv_k_k[l], rwkv_k_a[l], rwkv_r_k[l],
                   rwkv_ln_w[l], rwkv_ln_b[l], nsa_pos_k[l], nsa_w1_k[l], nsa_w2_k[l],
                   nsa_pos_v[l], nsa_w1_v[l], nsa_w2_v[l], p_merge[l], w_out[l], slopes)
        if l % 2 == 0:
            xs, h = resmod(xs, y, g1, norm_ffn[l], sh2, sc2, seq=T)
            y = ffn(h, ffn_w1[l // 2].astype(BF16), ffn_w3[l // 2].astype(BF16), ffn_w2[l // 2].astype(BF16))
        else:
            rt = jnp.zeros((D, 128), F32).at[:, :N_EXPERTS].set(moe_router[l // 2])
            xs, h, logits = resmod(xs, y, g1, norm_ffn[l], sh2, sc2, seq=T, router=rt)
            combine, selm = route(logits)
            y = moe(h, combine, selm, moe_w1, moe_w3, moe_w2, l // 2)
        g_prev = g2
    (out,) = resmod(xs, y, g_prev, final_norm, None, None, seq=T, final=True)
    return out.reshape(B, T, D)
```

```python
import functools

import numpy as np
import jax
import jax.numpy as jnp
from jax import lax
from jax.experimental import pallas as pl
from jax.experimental.pallas import tpu as pltpu

F32 = jnp.float32
BF16 = jnp.bfloat16
HI = lax.Precision.HIGHEST

V7X_VMEM_LIMIT_BYTES = 56 * 1024 * 1024

D_MODEL = 2048
NORM_EPS = 1e-6
NEG_BIG = -1e30

GLA_HEADS = 4
GLA_DK = 128
GLA_DV = 256
GLA_RANK = 16
GLA_NORMALIZER = 16.0
CHUNK = 64

RWKV_HEADS = 16
RWKV_N = 64
RWKV_DIM = RWKV_HEADS * RWKV_N
RWKV_W_RANK = 64
RWKV_A_RANK = 64
RWKV_G_RANK = 160
RWKV_LN_EPS = 64e-5
RWKV_HB = 4

NSA_HEADS = 16
NSA_GROUPS = 4
NSA_HPG = 4
NSA_DH = 64
NSA_CMP_LEN = 32
NSA_CMP_STRIDE = 16
NSA_CMP_HIDDEN = 128
NSA_SLC_LEN = 64
NSA_N_SEL = 16
NSA_WINDOW = 512
NSA_QB = 256
CMP_QB = 512
SEL_MASK_BIAS = 131072.0

BRANCH_DIM = 1024
D_FF = 5632
N_EXPERTS = 8

U_GLA_Q, U_GLA_K, U_GLA_V, U_GLA_G = 0, 512, 1024, 2048
U_RWKV_R, U_RWKV_K, U_RWKV_V = 3072, 4096, 5120
U_NSA_Q = 6144
U_NSA_KV = 7168
U_GATE = 8704
U_COLS = U_GATE + 3 * D_MODEL
S_GLA_A = 0
S_RWKV_WA = 128
S_RWKV_G = 256
S_NSA_GATE = 512
S_COLS = 640


def _cparams(sem, vmem=V7X_VMEM_LIMIT_BYTES):
    return pltpu.CompilerParams(dimension_semantics=sem, vmem_limit_bytes=vmem)


def _sigmoid(x):
    return 1.0 / (1.0 + jnp.exp(-x))


def _softplus(x):
    return jnp.maximum(x, 0.0) + jnp.log(1.0 + jnp.exp(-jnp.abs(x)))


def _dot(a, b):
    return jnp.dot(a, b, preferred_element_type=F32)


def _dot_nt(a, b):
    return lax.dot_general(a, b, (((1,), (1,)), ((), ())), preferred_element_type=F32)


def _dot_tn(a, b):
    return lax.dot_general(a, b, (((0,), (0,)), ((), ())), preferred_element_type=F32)


def _dot_hi(a, b):
    return jnp.dot(a, b, preferred_element_type=F32, precision=HI)


def _dot_x3(a, b):
    ah = a.astype(BF16)
    al = (a - ah.astype(F32)).astype(BF16)
    bh = b.astype(BF16)
    bl = (b - bh.astype(F32)).astype(BF16)
    return _dot(ah, bh) + _dot(ah, bl) + _dot(al, bh)


def _dot_sel(w, x):
    wb = w.astype(BF16)
    hi = x.astype(BF16)
    r1 = x - hi.astype(F32)
    mid = r1.astype(BF16)
    lo = (r1 - mid.astype(F32)).astype(BF16)
    return _dot(wb, hi) + _dot(wb, mid) + _dot(wb, lo)


def _dot_split(x, w):
    hi = x.astype(BF16)
    lo = (x - hi.astype(F32)).astype(BF16)
    return _dot(hi, w) + _dot(lo, w)


def _ada_kernel(c_ref, w_ref, b_ref, o_ref):
    c = c_ref[...]
    o_ref[0] = _dot_hi(c * _sigmoid(c), w_ref[0]) + b_ref[0]


def ada_all(c, ada_w, ada_b):
    L, D, N6 = ada_w.shape
    B = c.shape[0]
    tn = 1024
    return pl.pallas_call(
        _ada_kernel,
        out_shape=jax.ShapeDtypeStruct((L, B, N6), F32),
        grid=(L, N6 // tn),
        in_specs=[
            pl.BlockSpec((B, D), lambda l, j: (0, 0)),
            pl.BlockSpec((1, D, tn), lambda l, j: (l, 0, j)),
            pl.BlockSpec((1, 1, tn), lambda l, j: (l, 0, j)),
        ],
        out_specs=pl.BlockSpec((1, B, tn), lambda l, j: (l, 0, j)),
        compiler_params=_cparams(("parallel", "parallel")),
        name="ada",
    )(c, ada_w, ada_b.reshape(L, 1, N6))


def _resmod_kernel(*refs, has_res, final, router):
    it = iter(refs)
    x_ref = next(it)
    y_ref = next(it) if has_res else None
    g_ref = next(it) if has_res else None
    ng_ref = next(it)
    sh_ref = None if final else next(it)
    sc_ref = None if final else next(it)
    rt_ref = next(it) if router else None
    xo_ref = next(it) if (has_res and not final) else None
    h_ref = next(it)
    lg_ref = next(it) if router else None

    x = x_ref[...]
    if has_res:
        x = x + g_ref[0] * y_ref[...].astype(F32)
        if xo_ref is not None:
            xo_ref[...] = x
    ms = jnp.mean(x * x, axis=-1, keepdims=True)
    h = x * lax.rsqrt(ms + NORM_EPS) * ng_ref[...]
    if not final:
        h = h * (1.0 + sc_ref[0]) + sh_ref[0]
    h_ref[...] = h.astype(h_ref.dtype)
    if router:
        lg_ref[...] = _dot_hi(h, rt_ref[...])


def resmod(x, y, g, norm_g, shift, scale, *, seq, router=None, final=False):
    N, D = x.shape
    tm = 512
    spb = seq // tm
    has_res = y is not None
    row = lambda i: (i, 0)
    per_b = lambda i: (i // spb, 0, 0)
    ins, specs = [x], [pl.BlockSpec((tm, D), row)]
    if has_res:
        ins += [y, g.reshape(-1, 1, D)]
        specs += [pl.BlockSpec((tm, D), row), pl.BlockSpec((1, 1, D), per_b)]
    ins.append(norm_g.reshape(1, D))
    specs.append(pl.BlockSpec((1, D), lambda i: (0, 0)))
    if not final:
        ins += [shift.reshape(-1, 1, D), scale.reshape(-1, 1, D)]
        specs += [pl.BlockSpec((1, 1, D), per_b), pl.BlockSpec((1, 1, D), per_b)]
    if router is not None:
        ins.append(router)
        specs.append(pl.BlockSpec(router.shape, lambda i: (0, 0)))
    outs, ospecs = [], []
    if has_res and not final:
        outs.append(jax.ShapeDtypeStruct((N, D), F32))
        ospecs.append(pl.BlockSpec((tm, D), row))
    outs.append(jax.ShapeDtypeStruct((N, D), F32 if final else BF16))
    ospecs.append(pl.BlockSpec((tm, D), row))
    if router is not None:
        outs.append(jax.ShapeDtypeStruct((N, 128), F32))
        ospecs.append(pl.BlockSpec((tm, 128), row))
    res = pl.pallas_call(
        functools.partial(_resmod_kernel, has_res=has_res, final=final, router=router is not None),
        out_shape=tuple(outs),
        grid=(N // tm,),
        in_specs=specs,
        out_specs=tuple(ospecs),
        compiler_params=_cparams(("parallel",)),
        name="resmod",
    )(*ins)
    return res


def _mm_kernel(x_ref, w_ref, o_ref):
    o_ref[...] = _dot(x_ref[...], w_ref[...]).astype(o_ref.dtype)


def matmul(x, w, out_dtype, tm, tn):
    M, K = x.shape
    N = w.shape[1]
    return pl.pallas_call(
        _mm_kernel,
        out_shape=jax.ShapeDtypeStruct((M, N), out_dtype),
        grid=(M // tm, N // tn),
        in_specs=[pl.BlockSpec((tm, K), lambda i, j: (i, 0)), pl.BlockSpec((K, tn), lambda i, j: (0, j))],
        out_specs=pl.BlockSpec((tm, tn), lambda i, j: (i, j)),
        compiler_params=_cparams(("parallel", "parallel")),
        name="matmul",
    )(x, w)


GLA_TC = 512


def _gla_kernel(q_ref, k_ref, v_ref, g_ref, alr_ref, a2_ref, ab_ref, ng_ref, o_ref, st_ref, la_ref):
    @pl.when(pl.program_id(1) == 0)
    def _():
        st_ref[...] = jnp.zeros_like(st_ref)

    C, DK, DV = CHUNK, GLA_DK, GLA_DV
    la = _dot_x3(alr_ref[...], a2_ref[...]) + ab_ref[...]
    la_ref[...] = -_softplus(-la) / GLA_NORMALIZER
    ri = lax.broadcasted_iota(jnp.int32, (C, C), 0)
    ci = lax.broadcasted_iota(jnp.int32, (C, C), 1)
    causal = ri >= ci
    tril = causal.astype(F32)
    H = range(GLA_HEADS)

    def body(c, carry):
        sl = pl.ds(pl.multiple_of(c * C, C), C)
        bc_all = _dot_sel(tril, la_ref[sl, :])
        bcs = [bc_all[:, h * DK:(h + 1) * DK] for h in H]
        bls = [bc[C - 1:C, :] for bc in bcs]
        ks = [k_ref[sl, h * DK:(h + 1) * DK].astype(F32) for h in H]
        vs = [v_ref[sl, h * DV:(h + 1) * DV] for h in H]
        qds = [(q_ref[sl, h * DK:(h + 1) * DK].astype(F32) * (DK ** -0.5) * jnp.exp(bc)).astype(BF16)
               for h, bc in zip(H, bcs)]
        kds = [(k * jnp.exp(-bc)).astype(BF16) for k, bc in zip(ks, bcs)]
        kls = [(k * jnp.exp(bl - bc)).astype(BF16) for k, bl, bc in zip(ks, bls, bcs)]
        atts = [jnp.where(causal, _dot_nt(qd, kd), 0.0).astype(BF16) for qd, kd in zip(qds, kds)]
        sts = [st_ref[h] for h in H]
        os_ = [_dot(att, v) + _dot_nt(qd, st.astype(BF16)) for att, v, qd, st in zip(atts, vs, qds, sts)]
        for h in H:
            st_ref[h] = sts[h] * jnp.exp(bls[h]) + _dot_tn(vs[h], kls[h])
        for h in H:
            o = os_[h]
            o = o * lax.rsqrt(jnp.mean(o * o, axis=-1, keepdims=True) + NORM_EPS) * ng_ref[...]
            gg = g_ref[sl, h * DV:(h + 1) * DV].astype(F32)
            o_ref[sl, h * DV:(h + 1) * DV] = (o * (gg * _sigmoid(gg))).astype(o_ref.dtype)
        return carry

    lax.fori_loop(0, GLA_TC // C, body, 0)


def gla(U, Us, a2, a_b, norm_g, *, batch, seq):
    N = U.shape[0]
    nt = seq // GLA_TC
    HK, HV = GLA_HEADS * GLA_DK, GLA_HEADS * GLA_DV
    a2p = jnp.zeros((128, HK), F32).at[:GLA_RANK].set(a2)
    row = lambda b, i: b * nt + i
    return pl.pallas_call(
        _gla_kernel,
        out_shape=jax.ShapeDtypeStruct((N, HV), BF16),
        grid=(batch, nt),
        in_specs=[
            pl.BlockSpec((GLA_TC, HK), lambda b, i: (row(b, i), U_GLA_Q // HK)),
            pl.BlockSpec((GLA_TC, HK), lambda b, i: (row(b, i), U_GLA_K // HK)),
            pl.BlockSpec((GLA_TC, HV), lambda b, i: (row(b, i), U_GLA_V // HV)),
            pl.BlockSpec((GLA_TC, HV), lambda b, i: (row(b, i), U_GLA_G // HV)),
            pl.BlockSpec((GLA_TC, 128), lambda b, i: (row(b, i), S_GLA_A // 128)),
            pl.BlockSpec((128, HK), lambda b, i: (0, 0)),
            pl.BlockSpec((1, HK), lambda b, i: (0, 0)),
            pl.BlockSpec((1, GLA_DV), lambda b, i: (0, 0)),
        ],
        out_specs=pl.BlockSpec((GLA_TC, HV), lambda b, i: (row(b, i), 0)),
        scratch_shapes=[pltpu.VMEM((GLA_HEADS, GLA_DV, GLA_DK), F32), pltpu.VMEM((GLA_TC, HK), F32)],
        compiler_params=_cparams(("parallel", "arbitrary")),
        name="gla",
    )(U, U, U, U, Us, a2p, a_b.reshape(1, -1), norm_g.reshape(1, -1))


RWKV_TM = 256


def _seg_ones(n=256, seg=RWKV_N):
    i = np.arange(n)
    return jnp.asarray((i[:, None] // seg == i[None, :] // seg).astype(np.float32), BF16)


def _seg_sum(x, bd):
    outs = [_dot_split(x[:, s:s + 256], bd) for s in range(0, x.shape[1], 256)]
    return outs[0] if len(outs) == 1 else jnp.concatenate(outs, axis=1)


def _shift_lerp(u_ref, p_ref, mu, first):
    u = u_ref[...].astype(F32)
    prev_last = jnp.where(first, 0.0, p_ref[7:8, :].astype(F32))
    rolled = pltpu.roll(u, 1, 0)
    is_row0 = lax.broadcasted_iota(jnp.int32, u.shape, 0) == 0
    sh = jnp.where(is_row0, prev_last, rolled)
    return u + (sh - u) * mu


def _rwkv_prep_kernel(r_ref, k_ref, v_ref, wa_ref, gl_ref, rp_ref, kp_ref, vp_ref, wap_ref, glp_ref,
                      mur_ref, muk_ref, muv_ref, muwa_ref, mug_ref, w0_ref, w2_ref, a0_ref, a2_ref, g2_ref,
                      kk_ref, ka_ref, bd_ref,
                      ro_ref, ko_ref, vo_ref, kap_ref, bo_ref, lw_ref, go_ref, *, spb):
    first = (pl.program_id(0) % spb) == 0
    r = _shift_lerp(r_ref, rp_ref, mur_ref[...], first)
    k = _shift_lerp(k_ref, kp_ref, muk_ref[...], first)
    v = _shift_lerp(v_ref, vp_ref, muv_ref[...], first)
    wa = _shift_lerp(wa_ref, wap_ref, muwa_ref[...], first)
    gl = _shift_lerp(gl_ref, glp_ref, mug_ref[...], first)
    w_log = -_softplus(-(w0_ref[...] + _dot_x3(jnp.tanh(wa), w2_ref[...]))) - 0.5
    lw_ref[...] = -jnp.exp(w_log)
    a = _sigmoid(a0_ref[...] + _dot_x3(wa, a2_ref[...]))
    go_ref[...] = _dot(_sigmoid(gl).astype(BF16), g2_ref[...]).astype(go_ref.dtype)
    kk = k * kk_ref[...]
    nrm = jnp.sqrt(_seg_sum(kk * kk, bd_ref[...]))
    kk = kk / jnp.maximum(nrm, 1e-12)
    ro_ref[...] = r.astype(ro_ref.dtype)
    ko_ref[...] = (k * (1.0 + (a - 1.0) * ka_ref[...])).astype(ko_ref.dtype)
    vo_ref[...] = v.astype(vo_ref.dtype)
    kap_ref[...] = kk.astype(kap_ref.dtype)
    bo_ref[...] = (kk * a).astype(bo_ref.dtype)


def rwkv_prep(U, Us, mu, w0, w2, a0, a2, g2, k_k, k_a, *, seq):
    N = U.shape[0]
    tm = RWKV_TM
    spb = seq // tm
    R = RWKV_DIM
    cur = lambda cb: (lambda i: (i, cb))
    prv = lambda cb: (lambda i: (jnp.maximum(i * (tm // 8) - 1, 0), cb))
    mu_r, mu_k, mu_v = mu[:R], mu[R:2 * R], mu[2 * R:3 * R]
    mu_wa = mu[3 * R:3 * R + 128]
    mu_g = jnp.zeros((256,), F32).at[:RWKV_G_RANK].set(mu[3 * R + 128:])
    w2p = jnp.zeros((128, R), F32).at[:RWKV_W_RANK].set(w2)
    a2p = jnp.zeros((128, R), F32).at[RWKV_W_RANK:].set(a2)
    g2p = jnp.zeros((256, R), BF16).at[:RWKV_G_RANK].set(g2.astype(BF16))
    vec = lambda a: a.reshape(1, -1)
    full = lambda a: pl.BlockSpec(a.shape, lambda i: (0, 0))
    params = [vec(mu_r), vec(mu_k), vec(mu_v), vec(mu_wa), vec(mu_g), vec(w0), w2p, vec(a0), a2p, g2p,
              vec(k_k), vec(k_a), _seg_ones()]
    in_specs = [
        pl.BlockSpec((tm, R), cur(U_RWKV_R // R)), pl.BlockSpec((tm, R), cur(U_RWKV_K // R)),
        pl.BlockSpec((tm, R), cur(U_RWKV_V // R)),
        pl.BlockSpec((tm, 128), cur(S_RWKV_WA // 128)), pl.BlockSpec((tm, 256), cur(S_RWKV_G // 256)),
        pl.BlockSpec((8, R), prv(U_RWKV_R // R)), pl.BlockSpec((8, R), prv(U_RWKV_K // R)),
        pl.BlockSpec((8, R), prv(U_RWKV_V // R)),
        pl.BlockSpec((8, 128), prv(S_RWKV_WA // 128)), pl.BlockSpec((8, 256), prv(S_RWKV_G // 256)),
    ] + [full(p) for p in params]
    out = lambda dt: jax.ShapeDtypeStruct((N, R), dt)
    ospec = pl.BlockSpec((tm, R), lambda i: (i, 0))
    return pl.pallas_call(
        functools.partial(_rwkv_prep_kernel, spb=spb),
        out_shape=(out(BF16), out(BF16), out(BF16), out(BF16), out(BF16), out(F32), out(BF16)),
        grid=(N // tm,),
        in_specs=in_specs,
        out_specs=(ospec,) * 7,
        compiler_params=_cparams(("parallel",)),
        name="rwkv_prep",
    )(U, U, U, Us, Us, U, U, U, Us, Us, *params)


RWKV_TC = 512
RWKV_W = RWKV_HB * RWKV_N


def _rwkv_masks():
    W, C = RWKV_W, CHUNK
    i = np.arange(W)
    same = i[:, None] // C == i[None, :] // C
    m_bd = same.astype(np.float32)
    low_s = (same & (i[:, None] % C > i[None, :] % C)).astype(np.float32)
    low_i = (same & (i[:, None] % C >= i[None, :] % C)).astype(np.float32)
    tril = np.tril(np.ones((C, C), np.float32))
    return (jnp.asarray(m_bd), jnp.asarray(low_s), jnp.asarray(low_i), jnp.asarray(tril),
            jnp.asarray(np.eye(W, dtype=np.float32)))


def _tile4(x):
    return jnp.concatenate([x] * RWKV_HB, axis=0)


def _rwkv_chunk_kernel(r_ref, k_ref, v_ref, kap_ref, b_ref, lw_ref, g_ref,
                       mbd_ref, lows_ref, lowi_ref, tril_ref, eye_ref, bd_ref,
                       rk_ref, lnw_ref, lnb_ref, o_ref, st_ref):
    @pl.when(pl.program_id(1) == 0)
    def _():
        st_ref[...] = jnp.zeros_like(st_ref)

    C, W = CHUNK, RWKV_W
    n_batch = r_ref.shape[0]
    n_groups = r_ref.shape[2] // W
    m_bd = mbd_ref[...]
    low_s = lows_ref[...]
    low_i = lowi_ref[...]
    tril = tril_ref[...]
    eye = eye_ref[...]
    bd = bd_ref[...]

    def expand(x):
        return (_tile4(x) * m_bd).astype(BF16)

    def body(c, carry):
        sl = pl.ds(pl.multiple_of(c * C, C), C)
        lw_alls = [lw_ref[bi, sl, :] for bi in range(n_batch)]
        cum_alls = [_dot_sel(tril, lw) for lw in lw_alls]
        chains = [(bi, slice(gi * W, (gi + 1) * W)) for bi in range(n_batch) for gi in range(n_groups)]
        G = range(len(chains))
        lns = [ln for _, ln in chains]
        cums = [cum_alls[bi][:, ln] for bi, ln in chains]
        clasts = [cum[C - 1:C, :] for cum in cums]
        rs = [r_ref[bi, sl, ln].astype(F32) for bi, ln in chains]
        ks = [k_ref[bi, sl, ln].astype(F32) for bi, ln in chains]
        vs = [v_ref[bi, sl, ln].astype(F32) for bi, ln in chains]
        bs = [b_ref[bi, sl, ln].astype(F32) for bi, ln in chains]
        p_invs = [jnp.exp(-cum) for cum in cums]
        xes = [jnp.concatenate([expand(kap_ref[bi, sl, ln].astype(F32) * jnp.exp(cum - lw_alls[bi][:, ln])),
                                expand(r * jnp.exp(cum))], axis=0)
               for (bi, ln), cum, r in zip(chains, cums, rs)]
        hes = [jnp.concatenate([expand(b * pi), expand(k * pi)], axis=0) for b, k, pi in zip(bs, ks, p_invs)]
        scs = [_dot_nt(xe, he) for xe, he in zip(xes, hes)]
        ams = [sc[:W, :W] * low_s for sc in scs]
        tinvs = [eye - a_m for a_m in ams]
        ambs = [a_m.astype(BF16) for a_m in ams]
        pws = [_dot(ab, ab) for ab in ambs]
        n_lev = int(np.log2(C))
        for lev in range(1, n_lev):
            pwbs = [pw.astype(BF16) for pw in pws]
            if lev < n_lev - 1:
                outs = [_dot(jnp.concatenate([pwb, tinv.astype(BF16)], axis=0), pwb)
                        for pwb, tinv in zip(pwbs, tinvs)]
                pws = [o[:W] for o in outs]
                tinvs = [tinv + o[W:] for tinv, o in zip(tinvs, outs)]
            else:
                tinvs = [tinv + _dot(tinv.astype(BF16), pwb) for tinv, pwb in zip(tinvs, pwbs)]
        ves = [expand(v) for v in vs]
        bqv = [_dot(jnp.concatenate([(sc[:W, W:] * low_s).astype(BF16), (sc[W:, W:] * low_i).astype(BF16)], axis=0),
                    ve) for sc, ve in zip(scs, ves)]
        bmv = [o[:W] for o in bqv]
        qkv = [o[W:] for o in bqv]
        sts = [st_ref[gi] for gi in G]
        xss = [_dot_nt(xe, st.astype(BF16)) for xe, st in zip(xes, sts)]
        us = [_dot(tinv.astype(BF16), (xs[:W] + bv).astype(BF16)) for tinv, xs, bv in zip(tinvs, xss, bmv)]
        kbs = [jnp.concatenate([expand(k * jnp.exp(cl - cum)), expand(b * jnp.exp(cl - cum))], axis=0)
               for k, b, cl, cum in zip(ks, bs, clasts, cums)]
        for gi in G:
            vu = jnp.concatenate([ves[gi], (-us[gi]).astype(BF16)], axis=0)
            st_ref[gi] = sts[gi] * jnp.exp(clasts[gi]) + _dot_tn(vu, kbs[gi]) * m_bd
        for gi in G:
            ln = lns[gi]
            qb = (scs[gi][W:, :W] * low_i).astype(BF16)
            y_e = xss[gi][W:] + qkv[gi] - _dot(qb, us[gi].astype(BF16))
            y = y_e[0:C] + y_e[C:2 * C] + y_e[2 * C:3 * C] + y_e[3 * C:4 * C]
            mu = _seg_sum(y, bd) * (1.0 / RWKV_N)
            yc = y - mu
            var = _seg_sum(yc * yc, bd) * (1.0 / RWKV_N)
            yn = yc * lax.rsqrt(var + RWKV_LN_EPS) * lnw_ref[:, ln] + lnb_ref[:, ln]
            bonus = _seg_sum(rs[gi] * ks[gi] * rk_ref[:, ln], bd) * vs[gi]
            bi = chains[gi][0]
            o_ref[bi, sl, ln] = ((yn + bonus) * g_ref[bi, sl, ln].astype(F32)).astype(o_ref.dtype)
        return carry

    lax.fori_loop(0, RWKV_TC // C, body, 0)


RWKV_NB = 1


def rwkv_chunk(r, k, v, kap, b, lw, gate, r_k, ln_w, ln_b, *, batch, seq):
    N, R = r.shape
    nt = seq // RWKV_TC
    W = RWKV_W
    nb = RWKV_NB if batch % RWKV_NB == 0 else 1
    blk = pl.BlockSpec((nb, RWKV_TC, R), lambda bb, i: (bb, i, 0))
    masks = _rwkv_masks() + (_seg_ones(),)
    full = lambda a: pl.BlockSpec(a.shape, lambda bb, i: (0, 0))
    pvec = pl.BlockSpec((1, R), lambda bb, i: (0, 0))
    seqs = [a.reshape(batch, seq, R) for a in (r, k, v, kap, b, lw, gate)]
    out = pl.pallas_call(
        _rwkv_chunk_kernel,
        out_shape=jax.ShapeDtypeStruct((batch, seq, R), BF16),
        grid=(batch // nb, nt),
        in_specs=[blk] * 7 + [full(m) for m in masks] + [pvec] * 3,
        out_specs=blk,
        scratch_shapes=[pltpu.VMEM((nb * (R // W), W, W), F32)],
        compiler_params=_cparams(("parallel", "arbitrary")),
        name="rwkv_chunk",
    )(*seqs, *masks, r_k.reshape(1, R), ln_w.reshape(1, R), ln_b.reshape(1, R))
    return out.reshape(N, R)


def _gelu_tanh(x):
    return 0.5 * x * (1.0 + jnp.tanh(np.sqrt(2.0 / np.pi) * (x + 0.044715 * (x * x * x))))


def _nsa_compress_kernel(x_ref, pos_ref, w1_ref, w2_ref, o_ref):
    x = x_ref[0, 0]
    w1 = w1_ref[0]
    half = w1.shape[0] // 2
    nrow = x.shape[0]
    ha = _dot(x, w1[:half])
    hb = _dot(x, w1[half:])
    h = ha + pltpu.roll(hb, nrow - 1, 0)
    pb = _dot(pos_ref[0], w1)
    h = _gelu_tanh(h + pb[0:1, :])
    o_ref[0, 0, 0] = _dot(h.astype(BF16), w2_ref[0]).astype(o_ref.dtype)


def nsa_compress(kv_cmp, pos, w1, w2):
    two, B, G, T, dh = kv_cmp.shape
    nr = T // NSA_CMP_STRIDE
    x = kv_cmp.reshape(two * B, G, nr, NSA_CMP_STRIDE * dh)
    posf = jnp.broadcast_to(pos.reshape(two, 1, NSA_CMP_LEN * dh), (two, 8, NSA_CMP_LEN * dh)).astype(BF16)
    return pl.pallas_call(
        _nsa_compress_kernel,
        out_shape=jax.ShapeDtypeStruct((two, B, G, nr, dh), BF16),
        grid=(two, B, G),
        in_specs=[
            pl.BlockSpec((1, 1, nr, NSA_CMP_STRIDE * dh), lambda s, b, g: (s * B + b, g, 0, 0)),
            pl.BlockSpec((1, 8, NSA_CMP_LEN * dh), lambda s, b, g: (s, 0, 0)),
            pl.BlockSpec((1, NSA_CMP_LEN * dh, NSA_CMP_HIDDEN), lambda s, b, g: (s, 0, 0)),
            pl.BlockSpec((1, NSA_CMP_HIDDEN, dh), lambda s, b, g: (s, 0, 0)),
        ],
        out_specs=pl.BlockSpec((1, 1, 1, nr, dh), lambda s, b, g: (s, b, g, 0, 0)),
        compiler_params=_cparams(("parallel", "parallel", "parallel")),
        name="nsa_compress",
    )(x, posf, w1.astype(BF16), w2.astype(BF16))


def _nsa_cmp_kernel(q_ref, kc_ref, vct_ref, ov_ref, cb_ref, oc_ref, sb_ref, qs_ref, *, n_slc, n_sel):
    qi = pl.program_id(2)
    QB = CMP_QB
    ncp = kc_ref.shape[2]
    for h in range(NSA_HPG):
        qs_ref[h * QB:(h + 1) * QB, :] = q_ref[:, h * NSA_DH:(h + 1) * NSA_DH] * (NSA_DH ** -0.5)
    s = _dot_nt(kc_ref[0, 0], qs_ref[...])
    vct = vct_ref[0, 0]
    psum = jnp.zeros((ncp, QB), F32)
    for h in range(NSA_HPG):
        sh = s[:, h * QB:(h + 1) * QB] + cb_ref[h]
        m = jnp.maximum(jnp.max(sh, axis=0, keepdims=True), -1e20)
        e = jnp.exp(sh - m)
        acc = _dot(vct, e.astype(BF16))
        inv_l = 1.0 / jnp.maximum(acc[NSA_DH:NSA_DH + 1], 1e-30)
        oc_ref[:, h * NSA_DH:(h + 1) * NSA_DH] = (acc[0:NSA_DH] * inv_l).T.astype(oc_ref.dtype)
        psum = psum + e * inv_l
    imp = _dot_sel(ov_ref[...], psum)
    j = lax.broadcasted_iota(jnp.int32, (n_slc, QB), 0)
    tt = qi * QB + lax.broadcasted_iota(jnp.int32, (n_slc, QB), 1)
    cur = jnp.right_shift(tt, 6)
    forced = (j == 0) | (j == cur) | (j == cur - 1)
    cand = (j >= 1) & (j <= cur - 2)
    n_free = n_sel - 3

    def emit(sel):
        sb_ref[0, 0] = jnp.where(sel, 0.0, -SEL_MASK_BIAS).T.astype(sb_ref.dtype)

    last_cur = (qi * QB + QB - 1) // NSA_SLC_LEN

    @pl.when(last_cur - 2 <= n_free)
    def _():
        emit(forced | cand)

    @pl.when(last_cur - 2 > n_free)
    def _():
        cur_row = cur[0:1, :]
        rank = jnp.zeros((n_slc, QB), jnp.int32)
        for jp in range(1, n_slc):
            row = imp[jp:jp + 1, :]
            ahead = (row > imp) | ((row == imp) & (j > jp))
            rank = rank + jnp.where(ahead & (jp <= cur_row - 2), 1, 0)
        emit(forced | (cand & (rank < n_free)))


def nsa_cmp(U, kc, vct, slopes, *, batch, seq):
    N = U.shape[0]
    QB = CMP_QB
    nq = seq // QB
    ncp = kc.shape[2]
    n_slc = seq // NSA_SLC_LEN
    n_sel = min(NSA_N_SEL, n_slc)
    nn = np.arange(ncp)
    jj = np.arange(n_slc)
    ov = ((nn[None, :] * NSA_CMP_STRIDE + NSA_CMP_LEN - 1 >= jj[:, None] * NSA_SLC_LEN)
          & (nn[None, :] * NSA_CMP_STRIDE <= jj[:, None] * NSA_SLC_LEN + NSA_SLC_LEN - 1)
          & (nn[None, :] < ncp - 1)).astype(np.float32)
    tt = np.arange(seq)
    ended = jnp.asarray(nn[:, None] * NSA_CMP_STRIDE + NSA_CMP_LEN - 1 <= tt[None, :])
    adist = np.abs(tt[None, :] - (nn[:, None] * NSA_CMP_STRIDE + (NSA_CMP_LEN - 1) / 2.0)).astype(np.float32)
    cbias = jnp.where(ended[None], -slopes[:, None, None] * jnp.asarray(adist)[None], NEG_BIG)
    G = NSA_GROUPS
    W = NSA_HPG * NSA_DH
    return pl.pallas_call(
        functools.partial(_nsa_cmp_kernel, n_slc=n_slc, n_sel=n_sel),
        out_shape=(jax.ShapeDtypeStruct((N, NSA_HEADS * NSA_DH), BF16),
                   jax.ShapeDtypeStruct((batch, G, seq, n_slc), BF16)),
        grid=(batch, G, nq),
        in_specs=[
            pl.BlockSpec((QB, W), lambda b, g, i: (b * nq + i, U_NSA_Q // W + g)),
            pl.BlockSpec((1, 1, ncp, NSA_DH), lambda b, g, i: (b, g, 0, 0)),
            pl.BlockSpec((1, 1, NSA_DH + 8, ncp), lambda b, g, i: (b, g, 0, 0)),
            pl.BlockSpec((n_slc, ncp), lambda b, g, i: (0, 0)),
            pl.BlockSpec((NSA_HPG, ncp, QB), lambda b, g, i: (g, 0, i)),
        ],
        out_specs=(
            pl.BlockSpec((QB, W), lambda b, g, i: (b * nq + i, g)),
            pl.BlockSpec((1, 1, QB, n_slc), lambda b, g, i: (b, g, i, 0)),
        ),
        scratch_shapes=[pltpu.VMEM((NSA_HPG * QB, NSA_DH), BF16)],
        compiler_params=_cparams(("parallel", "parallel", "arbitrary")),
        name="nsa_cmp",
    )(U, kc, vct, jnp.asarray(ov), cbias)


def _nsa_win_kernel(*refs):
    nb = NSA_WINDOW // NSA_QB + 1
    q_ref, k_refs, v_refs = refs[0], refs[1:1 + nb], refs[1 + nb:1 + 2 * nb]
    bias_ref, o_ref, qs_ref = refs[1 + 2 * nb:]
    qi = pl.program_id(2)
    QB = NSA_QB
    KW = NSA_WINDOW + QB
    for h in range(NSA_HPG):
        qs_ref[h * QB:(h + 1) * QB, :] = q_ref[:, h * NSA_DH:(h + 1) * NSA_DH] * (NSA_DH ** -0.5)
    k = jnp.concatenate([r[0, 0] for r in k_refs], axis=0)
    vt = jnp.concatenate([r[0, 0] for r in v_refs], axis=1)
    s = _dot_nt(k, qs_ref[...])

    def finish(before_start):
        for h in range(NSA_HPG):
            cols = slice(h * QB, (h + 1) * QB)
            sh = s[:, cols] + bias_ref[h]
            if before_start:
                r = lax.broadcasted_iota(jnp.int32, (KW, QB), 0)
                sh = jnp.where(r >= NSA_WINDOW - qi * QB, sh, NEG_BIG)
            m = jnp.max(sh, axis=0, keepdims=True)
            e = jnp.exp(sh - m)
            acc = _dot(vt, e.astype(BF16))
            o = acc[0:NSA_DH] / acc[NSA_DH:NSA_DH + 1]
            o_ref[:, h * NSA_DH:(h + 1) * NSA_DH] = o.T.astype(o_ref.dtype)

    @pl.when(qi * QB < NSA_WINDOW)
    def _():
        finish(True)

    @pl.when(qi * QB >= NSA_WINDOW)
    def _():
        finish(False)


def nsa_win(U, kw_pad, vtw_pad, slopes, *, batch, seq):
    N = U.shape[0]
    QB = NSA_QB
    KW = NSA_WINDOW + QB
    nq = seq // QB
    G = NSA_GROUPS
    W = NSA_HPG * NSA_DH
    dist = (np.arange(QB)[None, :] - np.arange(KW)[:, None] + NSA_WINDOW).astype(np.float32)
    inside = jnp.asarray((dist >= 0) & (dist < NSA_WINDOW))
    bias = jnp.where(inside[None], -slopes[:, None, None] * jnp.asarray(dist)[None], NEG_BIG)
    nb = NSA_WINDOW // QB + 1
    kb = lambda off: pl.BlockSpec((1, 1, QB, NSA_DH), lambda b, g, i: (b, g, i + off, 0))
    vb = lambda off: pl.BlockSpec((1, 1, NSA_DH + 8, QB), lambda b, g, i: (b, g, 0, i + off))
    return pl.pallas_call(
        _nsa_win_kernel,
        out_shape=jax.ShapeDtypeStruct((N, NSA_HEADS * NSA_DH), BF16),
        grid=(batch, G, nq),
        in_specs=[pl.BlockSpec((QB, W), lambda b, g, i: (b * nq + i, U_NSA_Q // W + g)),
                  *[kb(o) for o in range(nb)], *[vb(o) for o in range(nb)],
                  pl.BlockSpec((NSA_HPG, KW, QB), lambda b, g, i: (g, 0, 0))],
        out_specs=pl.BlockSpec((QB, W), lambda b, g, i: (b * nq + i, g)),
        scratch_shapes=[pltpu.VMEM((NSA_HPG * QB, NSA_DH), BF16)],
        compiler_params=_cparams(("parallel", "parallel", "arbitrary")),
        name="nsa_win",
    )(U, *([kw_pad] * nb), *([vtw_pad] * nb), bias)


SEL_QB = 512
SEL_KV = 512


def _sel_pairs(seq):
    qs, ks = [], []
    for qi in range(seq // SEL_QB):
        for kj in range((qi * SEL_QB) // SEL_KV + 1):
            qs.append(qi)
            ks.append(kj)
    return np.asarray(qs, np.int32), np.asarray(ks, np.int32)


def _nsa_sel_kernel(qi_ref, kj_ref, slope_ref, q_ref, sb_ref, qx_ref, ka_ref, vt_ref, o_ref,
                    qa_ref, m_ref, acc_ref):
    g = pl.program_id(1)
    p = pl.program_id(2)
    qi = qi_ref[p]
    kj = kj_ref[p]
    QB, KV = SEL_QB, SEL_KV
    nblk = sb_ref.shape[3]
    base = NSA_DH + nblk

    @pl.when(kj == 0)
    def _():
        for h in range(NSA_HPG):
            rows = slice(h * QB, (h + 1) * QB)
            qa_ref[rows, 0:NSA_DH] = q_ref[:, h * NSA_DH:(h + 1) * NSA_DH] * (NSA_DH ** -0.5)
            qa_ref[rows, NSA_DH:base] = sb_ref[0, 0]
            qa_ref[rows, base:] = jnp.broadcast_to(qx_ref[0, h:h + 1, :], (QB, qa_ref.shape[1] - base))
        m_ref[...] = jnp.full_like(m_ref, NEG_BIG)
        acc_ref[...] = jnp.zeros_like(acc_ref)

    off = qi * QB - kj * KV
    tile_start = (kj * KV).astype(F32)

    def step(masked):
        s = _dot_nt(ka_ref[0, 0], qa_ref[...])
        vt = vt_ref[0, 0]
        if masked:
            r = lax.broadcasted_iota(jnp.int32, (KV, QB), 0)
            c = lax.broadcasted_iota(jnp.int32, (KV, QB), 1)
            causal = (c - r + off) >= 0
        for h in range(NSA_HPG):
            cols = slice(h * QB, (h + 1) * QB)
            sh = s[:, cols]
            if masked:
                sh = jnp.where(causal, sh, NEG_BIG)
            delta = slope_ref[g * NSA_HPG + h] * tile_start
            m_old = m_ref[h:h + 1, :]
            m_new = jnp.maximum(m_old, jnp.max(sh, axis=0, keepdims=True) + delta)
            e = jnp.exp(sh - (m_new - delta))
            alpha = jnp.exp(m_old - m_new)
            acc_ref[:, cols] = alpha * acc_ref[:, cols] + _dot(vt, e.astype(BF16))
            m_ref[h:h + 1, :] = m_new

    last = (qi * QB) // KV

    @pl.when(kj < last)
    def _():
        step(False)

    @pl.when(kj == last)
    def _():
        step(True)
        for h in range(NSA_HPG):
            cols = slice(h * QB, (h + 1) * QB)
            o = acc_ref[0:NSA_DH, cols] / acc_ref[NSA_DH:NSA_DH + 1, cols]
            o_ref[:, h * NSA_DH:(h + 1) * NSA_DH] = o.T.astype(o_ref.dtype)


def sel_key_columns(seq):
    n_slc = seq // NSA_SLC_LEN
    pos = np.arange(seq)
    onehot = (pos[:, None] // NSA_SLC_LEN == np.arange(n_slc)[None, :]).astype(np.float32)
    r = pos % SEL_KV
    extra = np.zeros((seq, 128), np.float32)
    extra[:, 0] = extra[:, 2] = (r // 32) * 32
    extra[:, 1] = extra[:, 3] = r % 32
    return jnp.asarray(np.concatenate([onehot, extra], axis=1), BF16)


def nsa_sel(U, selbias, k_aug, vt_slc, slopes, *, batch, seq):
    N = U.shape[0]
    QB, KV = SEL_QB, SEL_KV
    nq = seq // QB
    G = NSA_GROUPS
    W = NSA_HPG * NSA_DH
    n_slc = selbias.shape[3]
    ka_w = k_aug.shape[3]
    qs, ks = _sel_pairs(seq)
    s_hi = slopes.astype(BF16)
    s_lo = (slopes - s_hi.astype(F32)).astype(BF16)
    qx = jnp.zeros((NSA_HEADS, 128), BF16).at[:, 0].set(s_hi).at[:, 1].set(s_hi).at[:, 2].set(s_lo).at[:, 3].set(s_lo)
    qx = jnp.pad(qx.reshape(G, NSA_HPG, 128), ((0, 0), (0, 8 - NSA_HPG), (0, 0)))
    grid_spec = pltpu.PrefetchScalarGridSpec(
        num_scalar_prefetch=3,
        grid=(batch, G, len(qs)),
        in_specs=[
            pl.BlockSpec((QB, W), lambda b, g, p, qi, kj, s: (b * nq + qi[p], U_NSA_Q // W + g)),
            pl.BlockSpec((1, 1, QB, n_slc), lambda b, g, p, qi, kj, s: (b, g, qi[p], 0)),
            pl.BlockSpec((1, 8, 128), lambda b, g, p, qi, kj, s: (g, 0, 0)),
            pl.BlockSpec((1, 1, KV, ka_w), lambda b, g, p, qi, kj, s: (b, g, kj[p], 0)),
            pl.BlockSpec((1, 1, NSA_DH + 8, KV), lambda b, g, p, qi, kj, s: (b, g, 0, kj[p])),
        ],
        out_specs=pl.BlockSpec((QB, W), lambda b, g, p, qi, kj, s: (b * nq + qi[p], g)),
        scratch_shapes=[
            pltpu.VMEM((NSA_HPG * QB, ka_w), BF16),
            pltpu.VMEM((NSA_HPG, QB), F32),
            pltpu.VMEM((NSA_DH + 8, NSA_HPG * QB), F32),
        ],
    )
    return pl.pallas_call(
        _nsa_sel_kernel,
        out_shape=jax.ShapeDtypeStruct((N, NSA_HEADS * NSA_DH), BF16),
        grid_spec=grid_spec,
        compiler_params=_cparams(("parallel", "parallel", "arbitrary")),
        name="nsa_sel",
    )(jnp.asarray(qs), jnp.asarray(ks), slopes, U, selbias, qx, k_aug, vt_slc)


def _nsa_combine_kernel(oc_ref, os_ref, ow_ref, gate_ref, e_ref, o_ref):
    ge = _dot_split(_sigmoid(gate_ref[...]), e_ref[...])
    Wd = NSA_HEADS * NSA_DH
    o = (ge[:, :Wd] * oc_ref[...].astype(F32) + ge[:, Wd:2 * Wd] * os_ref[...].astype(F32)
         + ge[:, 2 * Wd:] * ow_ref[...].astype(F32))
    o_ref[...] = o.astype(o_ref.dtype)


def nsa_combine(o_c, o_s, o_w, Us):
    N, Wd = o_c.shape
    tm = 512
    e = np.zeros((128, 3 * Wd), np.float32)
    for h in range(NSA_HEADS):
        for j in range(3):
            e[h * 3 + j, j * Wd + h * NSA_DH:j * Wd + (h + 1) * NSA_DH] = 1.0
    blk = pl.BlockSpec((tm, Wd), lambda i: (i, 0))
    return pl.pallas_call(
        _nsa_combine_kernel,
        out_shape=jax.ShapeDtypeStruct((N, Wd), BF16),
        grid=(N // tm,),
        in_specs=[blk, blk, blk, pl.BlockSpec((tm, 128), lambda i: (i, S_NSA_GATE // 128)),
                  pl.BlockSpec((128, 3 * Wd), lambda i: (0, 0))],
        out_specs=blk,
        compiler_params=_cparams(("parallel",)),
        name="nsa_combine",
    )(o_c, o_s, o_w, Us, jnp.asarray(e, BF16))


def _merge_kernel(ya_ref, yb_ref, yc_ref, ga_ref, gb_ref, gc_ref, p_ref, o_ref):
    m = (_sigmoid(ga_ref[...].astype(F32)) * _dot(ya_ref[...], p_ref[0])
         + _sigmoid(gb_ref[...].astype(F32)) * _dot(yb_ref[...], p_ref[1])
         + _sigmoid(gc_ref[...].astype(F32)) * _dot(yc_ref[...], p_ref[2]))
    o_ref[...] = m.astype(o_ref.dtype)


def merge(y_a, y_b, y_c, U, p_merge):
    N = y_a.shape[0]
    tm, tn = 1024, 512
    yb = pl.BlockSpec((tm, BRANCH_DIM), lambda i, j: (i, 0))
    gate = lambda br: pl.BlockSpec((tm, tn), lambda i, j: (i, (U_GATE + br * D_MODEL) // tn + j))
    return pl.pallas_call(
        _merge_kernel,
        out_shape=jax.ShapeDtypeStruct((N, D_MODEL), BF16),
        grid=(N // tm, D_MODEL // tn),
        in_specs=[yb, yb, yb, gate(0), gate(1), gate(2),
                  pl.BlockSpec((3, BRANCH_DIM, tn), lambda i, j: (0, 0, j))],
        out_specs=pl.BlockSpec((tm, tn), lambda i, j: (i, j)),
        compiler_params=_cparams(("parallel", "parallel")),
        name="merge",
    )(y_a, y_b, y_c, U, U, U, p_merge)


def _ffn_kernel(h_ref, w1_ref, w3_ref, w2_ref, o_ref, acc_ref):
    f = pl.program_id(1)

    @pl.when(f == 0)
    def _():
        acc_ref[...] = jnp.zeros_like(acc_ref)

    h = h_ref[...]
    a = _dot(h, w1_ref[...])
    z = (a * _sigmoid(a) * _dot(h, w3_ref[...])).astype(BF16)
    acc_ref[...] += _dot(z, w2_ref[...])

    @pl.when(f == pl.num_programs(1) - 1)
    def _():
        o_ref[...] = acc_ref[...].astype(o_ref.dtype)


def ffn(h, w1, w3, w2):
    N, D = h.shape
    F = w1.shape[1]
    tm, tf = 1024, 512
    return pl.pallas_call(
        _ffn_kernel,
        out_shape=jax.ShapeDtypeStruct((N, D), BF16),
        grid=(N // tm, F // tf),
        in_specs=[pl.BlockSpec((tm, D), lambda i, f: (i, 0)),
                  pl.BlockSpec((D, tf), lambda i, f: (0, f)),
                  pl.BlockSpec((D, tf), lambda i, f: (0, f)),
                  pl.BlockSpec((tf, D), lambda i, f: (f, 0))],
        out_specs=pl.BlockSpec((tm, D), lambda i, f: (i, 0)),
        scratch_shapes=[pltpu.VMEM((tm, D), F32)],
        compiler_params=_cparams(("parallel", "arbitrary")),
        name="ffn",
    )(h, w1, w3, w2)


def _route_kernel(lg_ref, cb_ref, sel_ref):
    lg = lg_ref[...]
    lane = lax.broadcasted_iota(jnp.int32, lg.shape, 1)
    x = jnp.where(lane < N_EXPERTS, lg, NEG_BIG)
    v1 = jnp.max(x, axis=-1, keepdims=True)
    i1 = jnp.min(jnp.where(x == v1, lane, 1024), axis=-1, keepdims=True)
    x2 = jnp.where(lane == i1, NEG_BIG, x)
    v2 = jnp.max(x2, axis=-1, keepdims=True)
    i2 = jnp.min(jnp.where(x2 == v2, lane, 1024), axis=-1, keepdims=True)
    e2 = jnp.exp(v2 - v1)
    w1 = 1.0 / (1.0 + e2)
    w2 = e2 / (1.0 + e2)
    cb_ref[...] = jnp.where(lane == i1, w1, 0.0) + jnp.where(lane == i2, w2, 0.0)
    sel_ref[...] = jnp.where((lane == i1) | (lane == i2), 1.0, 0.0)


def route(logits):
    N = logits.shape[0]
    tm = 1024
    blk = pl.BlockSpec((tm, 128), lambda i: (i, 0))
    return pl.pallas_call(
        _route_kernel,
        out_shape=(jax.ShapeDtypeStruct((N, 128), F32), jax.ShapeDtypeStruct((N, 128), F32)),
        grid=(N // tm,),
        in_specs=[blk],
        out_specs=(blk, blk),
        compiler_params=_cparams(("parallel",)),
        name="route",
    )(logits)


MOE_TM = 1024
MOE_TF = 256
MOE_WB = 512
MOE_GT = 256


def _moe_plan(sel):
    N, E = sel.shape
    tm, wb, gt = MOE_TM, MOE_WB, MOE_GT
    sub = tm // gt
    NT = 2 * N // tm + E
    NG = NT * sub
    R = NT * tm
    P = NG + E * (N // wb)
    i32 = jnp.int32
    cs = jnp.cumsum(sel.astype(i32), axis=0)
    cnt = cs[-1]
    tiles_e = (cnt + tm - 1) // tm
    tile_end = jnp.cumsum(tiles_e)
    tile_start = tile_end - tiles_e
    total_tiles = tile_end[-1]
    dest = jnp.where(sel, tile_start[None, :] * tm + cs - 1, R)
    ti = jnp.arange(NT, dtype=i32)
    tile_valid = ti < total_tiles
    count_le = lambda ends, q: jnp.sum((ends[None, :] <= q[:, None]).astype(i32), axis=1)
    tile_e = jnp.minimum(count_le(tile_end, ti), E - 1)
    gi = jnp.arange(NG, dtype=i32)
    g_valid = tile_valid[gi // sub]
    g_e = tile_e[gi // sub]
    k_lo = gi * gt - tile_start[g_e] * tm
    nonempty = g_valid & (k_lo < cnt[g_e])
    k_hi = jnp.minimum(k_lo + gt, cnt[g_e]) - 1
    cb = cs[wb - 1::wb, :].T[g_e]
    blk_of = lambda k: jnp.sum((cb <= k[:, None]).astype(i32), axis=1)
    blo = jnp.where(nonempty, blk_of(k_lo), 0)
    bhi = jnp.where(nonempty, blk_of(k_hi), jnp.where(g_valid, 0, -1))
    npair = bhi - blo + 1
    pend = jnp.cumsum(npair)
    pstart = pend - npair
    total_p = pend[-1]
    pi = jnp.arange(P, dtype=i32)
    pvalid = pi < total_p
    ptile = jnp.minimum(count_le(pend, pi), NG - 1)
    pblk = blo[ptile] + pi - pstart[ptile]
    pfirst = pvalid & (pi == pstart[ptile])
    ptile = jnp.where(pvalid, ptile, ptile[total_p - 1])
    pblk = jnp.where(pvalid, pblk, pblk[total_p - 1])
    pexp = g_e[ptile]
    order = jnp.argsort(jnp.where(pvalid, pblk * NG + ptile, jnp.iinfo(jnp.int32).max))
    s_valid = pvalid
    s_tile = jnp.where(s_valid, ptile[order], ptile[order][total_p - 1])
    s_blk = jnp.where(s_valid, pblk[order], pblk[order][total_p - 1])
    s_exp = g_e[s_tile]
    s_first = s_valid & ((pi == 0) | (s_blk != jnp.roll(s_blk, 1)))
    e_src = jnp.where(tile_valid, ti, 0)
    e_exp = jnp.where(tile_valid, tile_e, tile_e[jnp.maximum(total_tiles - 1, 0)])
    b2i = lambda x: x.astype(i32)
    return dict(dest=dest, g=(ptile, pblk, pexp, b2i(pvalid), b2i(pfirst)),
                e=(e_exp, b2i(tile_valid), e_src), s=(s_tile, s_blk, s_exp, b2i(s_valid), b2i(s_first)), NT=NT, P=P)


def _moe_group_kernel(pt_ref, pb_ref, pe_ref, pv_ref, pf_ref, dest_ref, h_ref, o_ref):
    p = pl.program_id(0)

    @pl.when(pv_ref[p] == 1)
    def _():
        rel = dest_ref[0] - pt_ref[p] * MOE_GT
        row = lax.broadcasted_iota(jnp.int32, (MOE_GT, MOE_WB), 0)
        onehot = jnp.where(rel == row, 1.0, 0.0).astype(BF16)
        rows = _dot(onehot, h_ref[...]).astype(o_ref.dtype)

        @pl.when(pf_ref[p] == 1)
        def _():
            o_ref[...] = rows

        @pl.when(pf_ref[p] == 0)
        def _():
            o_ref[...] += rows


def _moe_expert_kernel(te_ref, tv_ref, ts_ref, h_ref, w1_ref, w3_ref, w2_ref, o_ref, acc_ref):
    i = pl.program_id(0)
    f = pl.program_id(1)
    nf = pl.num_programs(1)

    @pl.when(tv_ref[i] == 1)
    def _():
        @pl.when(f == 0)
        def _():
            acc_ref[...] = jnp.zeros_like(acc_ref)

        h = h_ref[...]
        a = _dot(h, w1_ref[0, 0].astype(BF16))
        z = (a * _sigmoid(a) * _dot(h, w3_ref[0, 0].astype(BF16))).astype(BF16)
        acc_ref[...] += _dot(z, w2_ref[0, 0].astype(BF16))

        @pl.when(f == nf - 1)
        def _():
            o_ref[...] = acc_ref[...].astype(o_ref.dtype)

    @pl.when((tv_ref[i] == 0) & (f == nf - 1))
    def _():
        o_ref[...] = jnp.zeros_like(o_ref)


def _moe_ungroup_kernel(st_ref, sb_ref, se_ref, sv_ref, sf_ref, dest_ref, w_ref, y_ref, o_ref):
    p = pl.program_id(0)

    @pl.when(sv_ref[p] == 1)
    def _():
        rel = dest_ref[0] - st_ref[p] * MOE_GT
        col = lax.broadcasted_iota(jnp.int32, (MOE_WB, MOE_GT), 1)
        onehot_t = jnp.where(rel == col, 1.0, 0.0).astype(BF16)
        part = _dot(onehot_t, y_ref[...]) * w_ref[0]

        @pl.when(sf_ref[p] == 1)
        def _():
            o_ref[...] = part

        @pl.when(sf_ref[p] == 0)
        def _():
            o_ref[...] += part


def moe(h, combine, selm, w1, w3, w2, layer):
    N, D = h.shape
    _, E, _, F = w1.shape
    tm, wb, gt, tf = MOE_TM, MOE_WB, MOE_GT, MOE_TF
    plan = _moe_plan(selm[:, :E] > 0.5)
    NT, P = plan["NT"], plan["P"]
    R = NT * tm
    dest_t = plan["dest"].T
    h_sorted = pl.pallas_call(
        _moe_group_kernel,
        out_shape=jax.ShapeDtypeStruct((R, D), BF16),
        grid_spec=pltpu.PrefetchScalarGridSpec(
            num_scalar_prefetch=5,
            grid=(P,),
            in_specs=[pl.BlockSpec((1, 1, wb), lambda p, pt, pb, pe, pv, pf: (pe[p], 0, pb[p])),
                      pl.BlockSpec((wb, D), lambda p, pt, pb, pe, pv, pf: (pb[p], 0))],
            out_specs=pl.BlockSpec((gt, D), lambda p, pt, pb, pe, pv, pf: (pt[p], 0)),
        ),
        compiler_params=_cparams(("arbitrary",)),
        name="moe_group",
    )(*plan["g"], dest_t.reshape(E, 1, N), h)
    y_sorted = pl.pallas_call(
        _moe_expert_kernel,
        out_shape=jax.ShapeDtypeStruct((R, D), BF16),
        grid_spec=pltpu.PrefetchScalarGridSpec(
            num_scalar_prefetch=3,
            grid=(NT, F // tf),
            in_specs=[pl.BlockSpec((tm, D), lambda i, f, te, tv, ts: (ts[i], 0)),
                      pl.BlockSpec((1, 1, D, tf), lambda i, f, te, tv, ts: (layer, te[i], 0, f * tv[i])),
                      pl.BlockSpec((1, 1, D, tf), lambda i, f, te, tv, ts: (layer, te[i], 0, f * tv[i])),
                      pl.BlockSpec((1, 1, tf, D), lambda i, f, te, tv, ts: (layer, te[i], f * tv[i], 0))],
            out_specs=pl.BlockSpec((tm, D), lambda i, f, te, tv, ts: (i, 0)),
            scratch_shapes=[pltpu.VMEM((tm, D), F32)],
        ),
        compiler_params=_cparams(("arbitrary", "arbitrary")),
        name="moe_expert",
    )(*plan["e"], h_sorted, w1, w3, w2)
    return pl.pallas_call(
        _moe_ungroup_kernel,
        out_shape=jax.ShapeDtypeStruct((N, D), F32),
        grid_spec=pltpu.PrefetchScalarGridSpec(
            num_scalar_prefetch=5,
            grid=(P,),
            in_specs=[pl.BlockSpec((1, wb, 1), lambda p, st, sb, se, sv, sf: (se[p], sb[p], 0)),
                      pl.BlockSpec((1, wb, 1), lambda p, st, sb, se, sv, sf: (se[p], sb[p], 0)),
                      pl.BlockSpec((gt, D), lambda p, st, sb, se, sv, sf: (st[p], 0))],
            out_specs=pl.BlockSpec((wb, D), lambda p, st, sb, se, sv, sf: (sb[p], 0)),
        ),
        compiler_params=_cparams(("arbitrary",)),
        name="moe_ungroup",
    )(*plan["s"], dest_t.reshape(E, N, 1), combine[:, :E].T.reshape(E, N, 1), y_sorted)


def _project_weights(w_in):
    gla_w = 2 * GLA_HEADS * GLA_DK + 2 * GLA_HEADS * GLA_DV + GLA_RANK
    rw_w = 3 * RWKV_DIM + RWKV_W_RANK + RWKV_A_RANK + RWKV_G_RANK
    kvw = NSA_GROUPS * NSA_DH
    nsa_w = NSA_HEADS * NSA_DH + 6 * kvw + NSA_HEADS * 3
    o_rw = gla_w
    o_nsa = gla_w + rw_w
    o_gate = o_nsa + nsa_w
    big = jnp.concatenate([
        w_in[:, 0:gla_w - GLA_RANK],
        w_in[:, o_rw:o_rw + 3 * RWKV_DIM],
        w_in[:, o_nsa:o_nsa + NSA_HEADS * NSA_DH + 6 * kvw],
        w_in[:, o_gate:],
    ], axis=1).astype(BF16)
    D = w_in.shape[0]
    z = lambda n: jnp.zeros((D, n), w_in.dtype)
    small = jnp.concatenate([
        w_in[:, gla_w - GLA_RANK:gla_w], z(128 - GLA_RANK),
        w_in[:, o_rw + 3 * RWKV_DIM:o_rw + rw_w], z(256 - RWKV_G_RANK),
        w_in[:, o_gate - NSA_HEADS * 3:o_gate], z(128 - NSA_HEADS * 3),
    ], axis=1).astype(BF16)
    return big, small


def _mixer(h, batch, seq, w_in, gla_a2, gla_a_b, gla_norm, rwkv_mu, rwkv_w0, rwkv_w2, rwkv_a0, rwkv_a2, rwkv_g2,
           rwkv_k_k, rwkv_k_a, rwkv_r_k, rwkv_ln_w, rwkv_ln_b,
           nsa_pos_k, nsa_w1_k, nsa_w2_k, nsa_pos_v, nsa_w1_v, nsa_w2_v, p_merge, w_out, slopes):
    N = h.shape[0]
    w_big, w_small = _project_weights(w_in)
    tm_u = 2048 if N % 2048 == 0 else N
    U = matmul(h, w_big, BF16, tm_u, 512)
    Us = matmul(h, w_small, F32, tm_u, S_COLS)

    y_a = gla(U, Us, gla_a2, gla_a_b, gla_norm, batch=batch, seq=seq)

    r, k, v, kap, b, lw, gate = rwkv_prep(U, Us, rwkv_mu, rwkv_w0, rwkv_w2, rwkv_a0, rwkv_a2, rwkv_g2,
                                          rwkv_k_k, rwkv_k_a, seq=seq)
    y_b = rwkv_chunk(r, k, v, kap, b, lw, gate, rwkv_r_k.reshape(-1), rwkv_ln_w, rwkv_ln_b, batch=batch, seq=seq)

    kv6 = U[:, U_NSA_KV:U_NSA_KV + 6 * NSA_GROUPS * NSA_DH]
    kv6 = kv6.reshape(batch, seq, 6, NSA_GROUPS, NSA_DH).transpose(2, 0, 3, 1, 4)
    kvc = nsa_compress(kv6[0:2], jnp.stack([nsa_pos_k, nsa_pos_v]), jnp.stack([nsa_w1_k, nsa_w1_v]),
                       jnp.stack([nsa_w2_k, nsa_w2_v]))
    ncp = kvc.shape[3]
    ones_c = jnp.zeros((batch, NSA_GROUPS, 8, ncp), BF16).at[:, :, 0, :].set(1.0)
    vct = jnp.concatenate([kvc[1].transpose(0, 1, 3, 2), ones_c], axis=2)
    o_c, selbias = nsa_cmp(U, kvc[0], vct, slopes, batch=batch, seq=seq)
    n_slc = seq // NSA_SLC_LEN
    k_aug = jnp.concatenate([kv6[2], jnp.broadcast_to(sel_key_columns(seq), (batch, NSA_GROUPS, seq, n_slc + 128))],
                            axis=-1)
    ones_rows = jnp.zeros((batch, NSA_GROUPS, 8, seq), BF16).at[:, :, 0, :].set(1.0)
    vt_aug = jnp.concatenate([kv6[3].transpose(0, 1, 3, 2), ones_rows], axis=2)
    o_s = nsa_sel(U, selbias, k_aug, vt_aug, slopes, batch=batch, seq=seq)
    kw_pad = jnp.pad(kv6[4], ((0, 0), (0, 0), (NSA_WINDOW, 0), (0, 0)))
    vtw = jnp.concatenate([kv6[5].transpose(0, 1, 3, 2), ones_rows], axis=2)
    vtw_pad = jnp.pad(vtw, ((0, 0), (0, 0), (0, 0), (NSA_WINDOW, 0)))
    o_w = nsa_win(U, kw_pad, vtw_pad, slopes, batch=batch, seq=seq)
    y_c = nsa_combine(o_c, o_s, o_w, Us)

    merged = merge(y_a, y_b, y_c, U, p_merge.astype(BF16))
    return matmul(merged, w_out.astype(BF16), BF16, 1024 if N % 1024 == 0 else N, 1024)


def kernel(x, c, norm_mix, norm_ffn, ada_w, ada_b, w_in, gla_a2, gla_a_b, gla_norm, rwkv_mu, rwkv_w0, rwkv_w2, rwkv_a0, rwkv_a2, rwkv_g2, rwkv_k_k, rwkv_k_a, rwkv_r_k, rwkv_ln_w, rwkv_ln_b, nsa_pos_k, nsa_w1_k, nsa_w2_k, nsa_pos_v, nsa_w1_v, nsa_w2_v, p_merge, w_out, ffn_w1, ffn_w3, ffn_w2, moe_router, moe_w1, moe_w3, moe_w2, final_norm):
    B, T, D = x.shape
    depth = w_in.shape[0]
    N = B * T
    xs = x.reshape(N, D)
    ada = ada_all(c, ada_w, ada_b)
    slopes = jnp.exp2(-8.0 * jnp.arange(1, NSA_HEADS + 1, dtype=F32) / NSA_HEADS)
    y = None
    g_prev = None
    for l in range(depth):
        sh1, sc1, g1, sh2, sc2, g2 = jnp.split(ada[l], 6, axis=-1)
        if y is None:
            (h,) = resmod(xs, None, None, norm_mix[l], sh1, sc1, seq=T)
        else:
            xs, h = resmod(xs, y, g_prev, norm_mix[l], sh1, sc1, seq=T)
        y = _mixer(h, B, T, w_in[l], gla_a2[l], gla_a_b[l], gla_norm[l], rwkv_mu[l], rwkv_w0[l], rwkv_w2[l],
                   rwkv_a0[l], rwkv_a2[l], rwkv_g2[l], rwkv_k_k[l], rwkv_k_a[l], rwkv_r_k[l],
                   rwkv_ln_w[l], rwkv_ln_b[l], nsa_pos_k[l], nsa_w1_k[l], nsa_w2_k[l],
                   nsa_pos_v[l], nsa_w1_v[l], nsa_w2_v[l], p_merge[l], w_out[l], slopes)
        if l % 2 == 0:
            xs, h = resmod(xs, y, g1, norm_ffn[l], sh2, sc2, seq=T)
            y = ffn(h, ffn_w1[l // 2].astype(BF16), ffn_w3[l // 2].astype(BF16), ffn_w2[l // 2].astype(BF16))
        else:
            rt = jnp.zeros((D, 128), F32).at[:, :N_EXPERTS].set(moe_router[l // 2])
            xs, h, logits = resmod(xs, y, g1, norm_ffn[l], sh2, sc2, seq=T, router=rt)
            combine, selm = route(logits)
            y = moe(h, combine, selm, moe_w1, moe_w3, moe_w2, l // 2)
        g_prev = g2
    (out,) = resmod(xs, y, g_prev, final_norm, None, None, seq=T, final=True)
    return out.reshape(B, T, D)
```

```python
import functools

import numpy as np
import jax
import jax.numpy as jnp
from jax import lax
from jax.experimental import pallas as pl
from jax.experimental.pallas import tpu as pltpu

F32 = jnp.float32
BF16 = jnp.bfloat16
HI = lax.Precision.HIGHEST

V7X_VMEM_LIMIT_BYTES = 56 * 1024 * 1024

D_MODEL = 2048
NORM_EPS = 1e-6
NEG_BIG = -1e30

GLA_HEADS = 4
GLA_DK = 128
GLA_DV = 256
GLA_RANK = 16
GLA_NORMALIZER = 16.0
CHUNK = 64

RWKV_HEADS = 16
RWKV_N = 64
RWKV_DIM = RWKV_HEADS * RWKV_N
RWKV_W_RANK = 64
RWKV_A_RANK = 64
RWKV_G_RANK = 160
RWKV_LN_EPS = 64e-5
RWKV_HB = 4

NSA_HEADS = 16
NSA_GROUPS = 4
NSA_HPG = 4
NSA_DH = 64
NSA_CMP_LEN = 32
NSA_CMP_STRIDE = 16
NSA_CMP_HIDDEN = 128
NSA_SLC_LEN = 64
NSA_N_SEL = 16
NSA_WINDOW = 512
NSA_QB = 256
CMP_QB = 512
SEL_MASK_BIAS = 131072.0

BRANCH_DIM = 1024
D_FF = 5632
N_EXPERTS = 8

U_GLA_Q, U_GLA_K, U_GLA_V, U_GLA_G = 0, 512, 1024, 2048
U_RWKV_R, U_RWKV_K, U_RWKV_V = 3072, 4096, 5120
U_NSA_Q = 6144
U_NSA_KV = 7168
U_GATE = 8704
U_COLS = U_GATE + 3 * D_MODEL
S_GLA_A = 0
S_RWKV_WA = 128
S_RWKV_G = 256
S_NSA_GATE = 512
S_COLS = 640


def _cparams(sem, vmem=V7X_VMEM_LIMIT_BYTES):
    return pltpu.CompilerParams(dimension_semantics=sem, vmem_limit_bytes=vmem)


def _sigmoid(x):
    return 1.0 / (1.0 + jnp.exp(-x))


def _softplus(x):
    return jnp.maximum(x, 0.0) + jnp.log(1.0 + jnp.exp(-jnp.abs(x)))


def _dot(a, b):
    return jnp.dot(a, b, preferred_element_type=F32)


def _dot_nt(a, b):
    return lax.dot_general(a, b, (((1,), (1,)), ((), ())), preferred_element_type=F32)


def _dot_tn(a, b):
    return lax.dot_general(a, b, (((0,), (0,)), ((), ())), preferred_element_type=F32)


def _dot_hi(a, b):
    return jnp.dot(a, b, preferred_element_type=F32, precision=HI)


def _dot_x3(a, b):
    ah = a.astype(BF16)
    al = (a - ah.astype(F32)).astype(BF16)
    bh = b.astype(BF16)
    bl = (b - bh.astype(F32)).astype(BF16)
    return _dot(ah, bh) + _dot(ah, bl) + _dot(al, bh)


def _dot_sel(w, x):
    wb = w.astype(BF16)
    hi = x.astype(BF16)
    r1 = x - hi.astype(F32)
    mid = r1.astype(BF16)
    lo = (r1 - mid.astype(F32)).astype(BF16)
    return _dot(wb, hi) + _dot(wb, mid) + _dot(wb, lo)


def _dot_split(x, w):
    hi = x.astype(BF16)
    lo = (x - hi.astype(F32)).astype(BF16)
    return _dot(hi, w) + _dot(lo, w)


def _ada_kernel(c_ref, w_ref, b_ref, o_ref):
    c = c_ref[...]
    o_ref[0] = _dot_hi(c * _sigmoid(c), w_ref[0]) + b_ref[0]


def ada_all(c, ada_w, ada_b):
    L, D, N6 = ada_w.shape
    B = c.shape[0]
    tn = 1024
    return pl.pallas_call(
        _ada_kernel,
        out_shape=jax.ShapeDtypeStruct((L, B, N6), F32),
        grid=(L, N6 // tn),
        in_specs=[
            pl.BlockSpec((B, D), lambda l, j: (0, 0)),
            pl.BlockSpec((1, D, tn), lambda l, j: (l, 0, j)),
            pl.BlockSpec((1, 1, tn), lambda l, j: (l, 0, j)),
        ],
        out_specs=pl.BlockSpec((1, B, tn), lambda l, j: (l, 0, j)),
        compiler_params=_cparams(("parallel", "parallel")),
        name="ada",
    )(c, ada_w, ada_b.reshape(L, 1, N6))


def _resmod_kernel(*refs, has_res, final, router):
    it = iter(refs)
    x_ref = next(it)
    y_ref = next(it) if has_res else None
    g_ref = next(it) if has_res else None
    ng_ref = next(it)
    sh_ref = None if final else next(it)
    sc_ref = None if final else next(it)
    rt_ref = next(it) if router else None
    xo_ref = next(it) if (has_res and not final) else None
    h_ref = next(it)
    lg_ref = next(it) if router else None

    x = x_ref[...]
    if has_res:
        x = x + g_ref[0] * y_ref[...].astype(F32)
        if xo_ref is not None:
            xo_ref[...] = x
    ms = jnp.mean(x * x, axis=-1, keepdims=True)
    h = x * lax.rsqrt(ms + NORM_EPS) * ng_ref[...]
    if not final:
        h = h * (1.0 + sc_ref[0]) + sh_ref[0]
    h_ref[...] = h.astype(h_ref.dtype)
    if router:
        lg_ref[...] = _dot_hi(h, rt_ref[...])


def resmod(x, y, g, norm_g, shift, scale, *, seq, router=None, final=False):
    N, D = x.shape
    tm = 512
    spb = seq // tm
    has_res = y is not None
    row = lambda i: (i, 0)
    per_b = lambda i: (i // spb, 0, 0)
    ins, specs = [x], [pl.BlockSpec((tm, D), row)]
    if has_res:
        ins += [y, g.reshape(-1, 1, D)]
        specs += [pl.BlockSpec((tm, D), row), pl.BlockSpec((1, 1, D), per_b)]
    ins.append(norm_g.reshape(1, D))
    specs.append(pl.BlockSpec((1, D), lambda i: (0, 0)))
    if not final:
        ins += [shift.reshape(-1, 1, D), scale.reshape(-1, 1, D)]
        specs += [pl.BlockSpec((1, 1, D), per_b), pl.BlockSpec((1, 1, D), per_b)]
    if router is not None:
        ins.append(router)
        specs.append(pl.BlockSpec(router.shape, lambda i: (0, 0)))
    outs, ospecs = [], []
    if has_res and not final:
        outs.append(jax.ShapeDtypeStruct((N, D), F32))
        ospecs.append(pl.BlockSpec((tm, D), row))
    outs.append(jax.ShapeDtypeStruct((N, D), F32 if final else BF16))
    ospecs.append(pl.BlockSpec((tm, D), row))
    if router is not None:
        outs.append(jax.ShapeDtypeStruct((N, 128), F32))
        ospecs.append(pl.BlockSpec((tm, 128), row))
    res = pl.pallas_call(
        functools.partial(_resmod_kernel, has_res=has_res, final=final, router=router is not None),
        out_shape=tuple(outs),
        grid=(N // tm,),
        in_specs=specs,
        out_specs=tuple(ospecs),
        compiler_params=_cparams(("parallel",)),
        name="resmod",
    )(*ins)
    return res


def _mm_kernel(x_ref, w_ref, o_ref):
    o_ref[...] = _dot(x_ref[...], w_ref[...]).astype(o_ref.dtype)


def matmul(x, w, out_dtype, tm, tn):
    M, K = x.shape
    N = w.shape[1]
    return pl.pallas_call(
        _mm_kernel,
        out_shape=jax.ShapeDtypeStruct((M, N), out_dtype),
        grid=(M // tm, N // tn),
        in_specs=[pl.BlockSpec((tm, K), lambda i, j: (i, 0)), pl.BlockSpec((K, tn), lambda i, j: (0, j))],
        out_specs=pl.BlockSpec((tm, tn), lambda i, j: (i, j)),
        compiler_params=_cparams(("parallel", "parallel")),
        name="matmul",
    )(x, w)


GLA_TC = 512


def _gla_kernel(q_ref, k_ref, v_ref, g_ref, alr_ref, a2_ref, ab_ref, ng_ref, o_ref, st_ref, la_ref):
    @pl.when(pl.program_id(1) == 0)
    def _():
        st_ref[...] = jnp.zeros_like(st_ref)

    C, DK, DV = CHUNK, GLA_DK, GLA_DV
    la = _dot_x3(alr_ref[...], a2_ref[...]) + ab_ref[...]
    la_ref[...] = -_softplus(-la) / GLA_NORMALIZER
    ri = lax.broadcasted_iota(jnp.int32, (C, C), 0)
    ci = lax.broadcasted_iota(jnp.int32, (C, C), 1)
    causal = ri >= ci
    tril = causal.astype(F32)
    H = range(GLA_HEADS)

    def body(c, carry):
        sl = pl.ds(pl.multiple_of(c * C, C), C)
        bc_all = _dot_sel(tril, la_ref[sl, :])
        bcs = [bc_all[:, h * DK:(h + 1) * DK] for h in H]
        bls = [bc[C - 1:C, :] for bc in bcs]
        ks = [k_ref[sl, h * DK:(h + 1) * DK].astype(F32) for h in H]
        vs = [v_ref[sl, h * DV:(h + 1) * DV] for h in H]
        qds = [(q_ref[sl, h * DK:(h + 1) * DK].astype(F32) * (DK ** -0.5) * jnp.exp(bc)).astype(BF16)
               for h, bc in zip(H, bcs)]
        kds = [(k * jnp.exp(-bc)).astype(BF16) for k, bc in zip(ks, bcs)]
        kls = [(k * jnp.exp(bl - bc)).astype(BF16) for k, bl, bc in zip(ks, bls, bcs)]
        atts = [jnp.where(causal, _dot_nt(qd, kd), 0.0).astype(BF16) for qd, kd in zip(qds, kds)]
        sts = [st_ref[h] for h in H]
        os_ = [_dot(att, v) + _dot_nt(qd, st.astype(BF16)) for att, v, qd, st in zip(atts, vs, qds, sts)]
        for h in H:
            st_ref[h] = sts[h] * jnp.exp(bls[h]) + _dot_tn(vs[h], kls[h])
        for h in H:
            o = os_[h]
            o = o * lax.rsqrt(jnp.mean(o * o, axis=-1, keepdims=True) + NORM_EPS) * ng_ref[...]
            gg = g_ref[sl, h * DV:(h + 1) * DV].astype(F32)
            o_ref[sl, h * DV:(h + 1) * DV] = (o * (gg * _sigmoid(gg))).astype(o_ref.dtype)
        return carry

    lax.fori_loop(0, GLA_TC // C, body, 0)


def gla(U, Us, a2, a_b, norm_g, *, batch, seq):
    N = U.shape[0]
    nt = seq // GLA_TC
    HK, HV = GLA_HEADS * GLA_DK, GLA_HEADS * GLA_DV
    a2p = jnp.zeros((128, HK), F32).at[:GLA_RANK].set(a2)
    row = lambda b, i: b * nt + i
    return pl.pallas_call(
        _gla_kernel,
        out_shape=jax.ShapeDtypeStruct((N, HV), BF16),
        grid=(batch, nt),
        in_specs=[
            pl.BlockSpec((GLA_TC, HK), lambda b, i: (row(b, i), U_GLA_Q // HK)),
            pl.BlockSpec((GLA_TC, HK), lambda b, i: (row(b, i), U_GLA_K // HK)),
            pl.BlockSpec((GLA_TC, HV), lambda b, i: (row(b, i), U_GLA_V // HV)),
            pl.BlockSpec((GLA_TC, HV), lambda b, i: (row(b, i), U_GLA_G // HV)),
            pl.BlockSpec((GLA_TC, 128), lambda b, i: (row(b, i), S_GLA_A // 128)),
            pl.BlockSpec((128, HK), lambda b, i: (0, 0)),
            pl.BlockSpec((1, HK), lambda b, i: (0, 0)),
            pl.BlockSpec((1, GLA_DV), lambda b, i: (0, 0)),
        ],
        out_specs=pl.BlockSpec((GLA_TC, HV), lambda b, i: (row(b, i), 0)),
        scratch_shapes=[pltpu.VMEM((GLA_HEADS, GLA_DV, GLA_DK), F32), pltpu.VMEM((GLA_TC, HK), F32)],
        compiler_params=_cparams(("parallel", "arbitrary")),
        name="gla",
    )(U, U, U, U, Us, a2p, a_b.reshape(1, -1), norm_g.reshape(1, -1))


RWKV_TM = 256


def _seg_ones(n=256, seg=RWKV_N):
    i = np.arange(n)
    return jnp.asarray((i[:, None] // seg == i[None, :] // seg).astype(np.float32), BF16)


def _seg_sum(x, bd):
    outs = [_dot_split(x[:, s:s + 256], bd) for s in range(0, x.shape[1], 256)]
    return outs[0] if len(outs) == 1 else jnp.concatenate(outs, axis=1)


def _shift_lerp(u_ref, p_ref, mu, first):
    u = u_ref[...].astype(F32)
    prev_last = jnp.where(first, 0.0, p_ref[7:8, :].astype(F32))
    rolled = pltpu.roll(u, 1, 0)
    is_row0 = lax.broadcasted_iota(jnp.int32, u.shape, 0) == 0
    sh = jnp.where(is_row0, prev_last, rolled)
    return u + (sh - u) * mu


def _rwkv_prep_kernel(r_ref, k_ref, v_ref, wa_ref, gl_ref, rp_ref, kp_ref, vp_ref, wap_ref, glp_ref,
                      mur_ref, muk_ref, muv_ref, muwa_ref, mug_ref, w0_ref, w2_ref, a0_ref, a2_ref, g2_ref,
                      kk_ref, ka_ref, bd_ref,
                      ro_ref, ko_ref, vo_ref, kap_ref, bo_ref, lw_ref, go_ref, *, spb):
    first = (pl.program_id(0) % spb) == 0
    r = _shift_lerp(r_ref, rp_ref, mur_ref[...], first)
    k = _shift_lerp(k_ref, kp_ref, muk_ref[...], first)
    v = _shift_lerp(v_ref, vp_ref, muv_ref[...], first)
    wa = _shift_lerp(wa_ref, wap_ref, muwa_ref[...], first)
    gl = _shift_lerp(gl_ref, glp_ref, mug_ref[...], first)
    w_log = -_softplus(-(w0_ref[...] + _dot_x3(jnp.tanh(wa), w2_ref[...]))) - 0.5
    lw_ref[...] = -jnp.exp(w_log)
    a = _sigmoid(a0_ref[...] + _dot_x3(wa, a2_ref[...]))
    go_ref[...] = _dot(_sigmoid(gl).astype(BF16), g2_ref[...]).astype(go_ref.dtype)
    kk = k * kk_ref[...]
    nrm = jnp.sqrt(_seg_sum(kk * kk, bd_ref[...]))
    kk = kk / jnp.maximum(nrm, 1e-12)
    ro_ref[...] = r.astype(ro_ref.dtype)
    ko_ref[...] = (k * (1.0 + (a - 1.0) * ka_ref[...])).astype(ko_ref.dtype)
    vo_ref[...] = v.astype(vo_ref.dtype)
    kap_ref[...] = kk.astype(kap_ref.dtype)
    bo_ref[...] = (kk * a).astype(bo_ref.dtype)


def rwkv_prep(U, Us, mu, w0, w2, a0, a2, g2, k_k, k_a, *, seq):
    N = U.shape[0]
    tm = RWKV_TM
    spb = seq // tm
    R = RWKV_DIM
    cur = lambda cb: (lambda i: (i, cb))
    prv = lambda cb: (lambda i: (jnp.maximum(i * (tm // 8) - 1, 0), cb))
    mu_r, mu_k, mu_v = mu[:R], mu[R:2 * R], mu[2 * R:3 * R]
    mu_wa = mu[3 * R:3 * R + 128]
    mu_g = jnp.zeros((256,), F32).at[:RWKV_G_RANK].set(mu[3 * R + 128:])
    w2p = jnp.zeros((128, R), F32).at[:RWKV_W_RANK].set(w2)
    a2p = jnp.zeros((128, R), F32).at[RWKV_W_RANK:].set(a2)
    g2p = jnp.zeros((256, R), BF16).at[:RWKV_G_RANK].set(g2.astype(BF16))
    vec = lambda a: a.reshape(1, -1)
    full = lambda a: pl.BlockSpec(a.shape, lambda i: (0, 0))
    params = [vec(mu_r), vec(mu_k), vec(mu_v), vec(mu_wa), vec(mu_g), vec(w0), w2p, vec(a0), a2p, g2p,
              vec(k_k), vec(k_a), _seg_ones()]
    in_specs = [
        pl.BlockSpec((tm, R), cur(U_RWKV_R // R)), pl.BlockSpec((tm, R), cur(U_RWKV_K // R)),
        pl.BlockSpec((tm, R), cur(U_RWKV_V // R)),
        pl.BlockSpec((tm, 128), cur(S_RWKV_WA // 128)), pl.BlockSpec((tm, 256), cur(S_RWKV_G // 256)),
        pl.BlockSpec((8, R), prv(U_RWKV_R // R)), pl.BlockSpec((8, R), prv(U_RWKV_K // R)),
        pl.BlockSpec((8, R), prv(U_RWKV_V // R)),
        pl.BlockSpec((8, 128), prv(S_RWKV_WA // 128)), pl.BlockSpec((8, 256), prv(S_RWKV_G // 256)),
    ] + [full(p) for p in params]
    out = lambda dt: jax.ShapeDtypeStruct((N, R), dt)
    ospec = pl.BlockSpec((tm, R), lambda i: (i, 0))
    return pl.pallas_call(
        functools.partial(_rwkv_prep_kernel, spb=spb),
        out_shape=(out(BF16), out(BF16), out(BF16), out(BF16), out(BF16), out(F32), out(BF16)),
        grid=(N // tm,),
        in_specs=in_specs,
        out_specs=(ospec,) * 7,
        compiler_params=_cparams(("parallel",)),
        name="rwkv_prep",
    )(U, U, U, Us, Us, U, U, U, Us, Us, *params)


RWKV_TC = 512
RWKV_W = RWKV_HB * RWKV_N


def _rwkv_masks():
    W, C = RWKV_W, CHUNK
    i = np.arange(W)
    same = i[:, None] // C == i[None, :] // C
    m_bd = same.astype(np.float32)
    low_s = (same & (i[:, None] % C > i[None, :] % C)).astype(np.float32)
    low_i = (same & (i[:, None] % C >= i[None, :] % C)).astype(np.float32)
    tril = np.tril(np.ones((C, C), np.float32))
    return (jnp.asarray(m_bd), jnp.asarray(low_s), jnp.asarray(low_i), jnp.asarray(tril),
            jnp.asarray(np.eye(W, dtype=np.float32)))


def _tile4(x):
    return jnp.concatenate([x] * RWKV_HB, axis=0)


def _rwkv_chunk_kernel(r_ref, k_ref, v_ref, kap_ref, b_ref, lw_ref, g_ref,
                       mbd_ref, lows_ref, lowi_ref, tril_ref, eye_ref, bd_ref,
                       rk_ref, lnw_ref, lnb_ref, o_ref, st_ref):
    @pl.when(pl.program_id(1) == 0)
    def _():
        st_ref[...] = jnp.zeros_like(st_ref)

    C, W = CHUNK, RWKV_W
    n_batch = r_ref.shape[0]
    n_groups = r_ref.shape[2] // W
    m_bd = mbd_ref[...]
    low_s = lows_ref[...]
    low_i = lowi_ref[...]
    tril = tril_ref[...]
    eye = eye_ref[...]
    bd = bd_ref[...]

    def expand(x):
        return (_tile4(x) * m_bd).astype(BF16)

    def body(c, carry):
        sl = pl.ds(pl.multiple_of(c * C, C), C)
        lw_alls = [lw_ref[bi, sl, :] for bi in range(n_batch)]
        cum_alls = [_dot_sel(tril, lw) for lw in lw_alls]
        chains = [(bi, slice(gi * W, (gi + 1) * W)) for bi in range(n_batch) for gi in range(n_groups)]
        G = range(len(chains))
        lns = [ln for _, ln in chains]
        cums = [cum_alls[bi][:, ln] for bi, ln in chains]
        clasts = [cum[C - 1:C, :] for cum in cums]
        rs = [r_ref[bi, sl, ln].astype(F32) for bi, ln in chains]
        ks = [k_ref[bi, sl, ln].astype(F32) for bi, ln in chains]
        vs = [v_ref[bi, sl, ln].astype(F32) for bi, ln in chains]
        bs = [b_ref[bi, sl, ln].astype(F32) for bi, ln in chains]
        p_invs = [jnp.exp(-cum) for cum in cums]
        xes = [jnp.concatenate([expand(kap_ref[bi, sl, ln].astype(F32) * jnp.exp(cum - lw_alls[bi][:, ln])),
                                expand(r * jnp.exp(cum))], axis=0)
               for (bi, ln), cum, r in zip(chains, cums, rs)]
        hes = [jnp.concatenate([expand(b * pi), expand(k * pi)], axis=0) for b, k, pi in zip(bs, ks, p_invs)]
        scs = [_dot_nt(xe, he) for xe, he in zip(xes, hes)]
        ams = [sc[:W, :W] * low_s for sc in scs]
        tinvs = [eye - a_m for a_m in ams]
        ambs = [a_m.astype(BF16) for a_m in ams]
        pws = [_dot(ab, ab) for ab in ambs]
        n_lev = int(np.log2(C))
        for lev in range(1, n_lev):
            pwbs = [pw.astype(BF16) for pw in pws]
            if lev < n_lev - 1:
                outs = [_dot(jnp.concatenate([pwb, tinv.astype(BF16)], axis=0), pwb)
                        for pwb, tinv in zip(pwbs, tinvs)]
                pws = [o[:W] for o in outs]
                tinvs = [tinv + o[W:] for tinv, o in zip(tinvs, outs)]
            else:
                tinvs = [tinv + _dot(tinv.astype(BF16), pwb) for tinv, pwb in zip(tinvs, pwbs)]
        ves = [expand(v) for v in vs]
        bqv = [_dot(jnp.concatenate([(sc[:W, W:] * low_s).astype(BF16), (sc[W:, W:] * low_i).astype(BF16)], axis=0),
                    ve) for sc, ve in zip(scs, ves)]
        bmv = [o[:W] for o in bqv]
        qkv = [o[W:] for o in bqv]
        sts = [st_ref[gi] for gi in G]
        xss = [_dot_nt(xe, st.astype(BF16)) for xe, st in zip(xes, sts)]
        us = [_dot(tinv.astype(BF16), (xs[:W] + bv).astype(BF16)) for tinv, xs, bv in zip(tinvs, xss, bmv)]
        kbs = [jnp.concatenate([expand(k * jnp.exp(cl - cum)), expand(b * jnp.exp(cl - cum))], axis=0)
               for k, b, cl, cum in zip(ks, bs, clasts, cums)]
        for gi in G:
            vu = jnp.concatenate([ves[gi], (-us[gi]).astype(BF16)], axis=0)
            st_ref[gi] = sts[gi] * jnp.exp(clasts[gi]) + _dot_tn(vu, kbs[gi]) * m_bd
        for gi in G:
            ln = lns[gi]
            qb = (scs[gi][W:, :W] * low_i).astype(BF16)
            y_e = xss[gi][W:] + qkv[gi] - _dot(qb, us[gi].astype(BF16))
            y = y_e[0:C] + y_e[C:2 * C] + y_e[2 * C:3 * C] + y_e[3 * C:4 * C]
            mu = _seg_sum(y, bd) * (1.0 / RWKV_N)
            yc = y - mu
            var = _seg_sum(yc * yc, bd) * (1.0 / RWKV_N)
            yn = yc * lax.rsqrt(var + RWKV_LN_EPS) * lnw_ref[:, ln] + lnb_ref[:, ln]
            bonus = _seg_sum(rs[gi] * ks[gi] * rk_ref[:, ln], bd) * vs[gi]
            bi = chains[gi][0]
            o_ref[bi, sl, ln] = ((yn + bonus) * g_ref[bi, sl, ln].astype(F32)).astype(o_ref.dtype)
        return carry

    lax.fori_loop(0, RWKV_TC // C, body, 0)


RWKV_NB = 1


def rwkv_chunk(r, k, v, kap, b, lw, gate, r_k, ln_w, ln_b, *, batch, seq):
    N, R = r.shape
    nt = seq // RWKV_TC
    W = RWKV_W
    nb = RWKV_NB if batch % RWKV_NB == 0 else 1
    blk = pl.BlockSpec((nb, RWKV_TC, R), lambda bb, i: (bb, i, 0))
    masks = _rwkv_masks() + (_seg_ones(),)
    full = lambda a: pl.BlockSpec(a.shape, lambda bb, i: (0, 0))
    pvec = pl.BlockSpec((1, R), lambda bb, i: (0, 0))
    seqs = [a.reshape(batch, seq, R) for a in (r, k, v, kap, b, lw, gate)]
    out = pl.pallas_call(
        _rwkv_chunk_kernel,
        out_shape=jax.ShapeDtypeStruct((batch, seq, R), BF16),
        grid=(batch // nb, nt),
        in_specs=[blk] * 7 + [full(m) for m in masks] + [pvec] * 3,
        out_specs=blk,
        scratch_shapes=[pltpu.VMEM((nb * (R // W), W, W), F32)],
        compiler_params=_cparams(("parallel", "arbitrary")),
        name="rwkv_chunk",
    )(*seqs, *masks, r_k.reshape(1, R), ln_w.reshape(1, R), ln_b.reshape(1, R))
    return out.reshape(N, R)


def _gelu_tanh(x):
    return 0.5 * x * (1.0 + jnp.tanh(np.sqrt(2.0 / np.pi) * (x + 0.044715 * (x * x * x))))


def _nsa_compress_kernel(x_ref, pos_ref, w1_ref, w2_ref, o_ref):
    x = x_ref[0, 0]
    w1 = w1_ref[0]
    half = w1.shape[0] // 2
    nrow = x.shape[0]
    ha = _dot(x, w1[:half])
    hb = _dot(x, w1[half:])
    h = ha + pltpu.roll(hb, nrow - 1, 0)
    pb = _dot(pos_ref[0], w1)
    h = _gelu_tanh(h + pb[0:1, :])
    o_ref[0, 0, 0] = _dot(h.astype(BF16), w2_ref[0]).astype(o_ref.dtype)


def nsa_compress(x, pos, w1, w2):
    two = 2
    B, G, nr = x.shape[0] // two, x.shape[1], x.shape[2]
    dh = x.shape[3] // NSA_CMP_STRIDE
    posf = jnp.broadcast_to(pos.reshape(two, 1, NSA_CMP_LEN * dh), (two, 8, NSA_CMP_LEN * dh)).astype(BF16)
    return pl.pallas_call(
        _nsa_compress_kernel,
        out_shape=jax.ShapeDtypeStruct((two, B, G, nr, dh), BF16),
        grid=(two, B, G),
        in_specs=[
            pl.BlockSpec((1, 1, nr, NSA_CMP_STRIDE * dh), lambda s, b, g: (s * B + b, g, 0, 0)),
            pl.BlockSpec((1, 8, NSA_CMP_LEN * dh), lambda s, b, g: (s, 0, 0)),
            pl.BlockSpec((1, NSA_CMP_LEN * dh, NSA_CMP_HIDDEN), lambda s, b, g: (s, 0, 0)),
            pl.BlockSpec((1, NSA_CMP_HIDDEN, dh), lambda s, b, g: (s, 0, 0)),
        ],
        out_specs=pl.BlockSpec((1, 1, 1, nr, dh), lambda s, b, g: (s, b, g, 0, 0)),
        compiler_params=_cparams(("parallel", "parallel", "parallel")),
        name="nsa_compress",
    )(x, posf, w1.astype(BF16), w2.astype(BF16))


def _nsa_cmp_kernel(q_ref, kc_ref, vct_ref, ov_ref, cb_ref, oc_ref, sb_ref, qs_ref, *, n_slc, n_sel):
    qi = pl.program_id(2)
    QB = CMP_QB
    ncp = kc_ref.shape[2]
    for h in range(NSA_HPG):
        qs_ref[h * QB:(h + 1) * QB, :] = q_ref[:, h * NSA_DH:(h + 1) * NSA_DH] * (NSA_DH ** -0.5)
    s = _dot_nt(kc_ref[0, 0], qs_ref[...])
    vct = vct_ref[0, 0]
    psum = jnp.zeros((ncp, QB), F32)
    for h in range(NSA_HPG):
        sh = s[:, h * QB:(h + 1) * QB] + cb_ref[h]
        m = jnp.maximum(jnp.max(sh, axis=0, keepdims=True), -1e20)
        e = jnp.exp(sh - m)
        acc = _dot(vct, e.astype(BF16))
        inv_l = 1.0 / jnp.maximum(acc[NSA_DH:NSA_DH + 1], 1e-30)
        oc_ref[:, h * NSA_DH:(h + 1) * NSA_DH] = (acc[0:NSA_DH] * inv_l).T.astype(oc_ref.dtype)
        psum = psum + e * inv_l
    imp = _dot_sel(ov_ref[...], psum)
    j = lax.broadcasted_iota(jnp.int32, (n_slc, QB), 0)
    tt = qi * QB + lax.broadcasted_iota(jnp.int32, (n_slc, QB), 1)
    cur = jnp.right_shift(tt, 6)
    forced = (j == 0) | (j == cur) | (j == cur - 1)
    cand = (j >= 1) & (j <= cur - 2)
    n_free = n_sel - 3

    def emit(sel):
        sb_ref[0, 0] = jnp.where(sel, 0.0, -SEL_MASK_BIAS).T.astype(sb_ref.dtype)

    last_cur = (qi * QB + QB - 1) // NSA_SLC_LEN

    @pl.when(last_cur - 2 <= n_free)
    def _():
        emit(forced | cand)

    @pl.when(last_cur - 2 > n_free)
    def _():
        cur_row = cur[0:1, :]
        rank = jnp.zeros((n_slc, QB), jnp.int32)
        for jp in range(1, n_slc):
            row = imp[jp:jp + 1, :]
            ahead = (row > imp) | ((row == imp) & (j > jp))
            rank = rank + jnp.where(ahead & (jp <= cur_row - 2), 1, 0)
        emit(forced | (cand & (rank < n_free)))


def nsa_cmp(U, kc, vct, slopes, *, batch, seq):
    N = U.shape[0]
    QB = CMP_QB
    nq = seq // QB
    ncp = kc.shape[2]
    n_slc = seq // NSA_SLC_LEN
    n_sel = min(NSA_N_SEL, n_slc)
    nn = np.arange(ncp)
    jj = np.arange(n_slc)
    ov = ((nn[None, :] * NSA_CMP_STRIDE + NSA_CMP_LEN - 1 >= jj[:, None] * NSA_SLC_LEN)
          & (nn[None, :] * NSA_CMP_STRIDE <= jj[:, None] * NSA_SLC_LEN + NSA_SLC_LEN - 1)
          & (nn[None, :] < ncp - 1)).astype(np.float32)
    tt = np.arange(seq)
    ended = jnp.asarray(nn[:, None] * NSA_CMP_STRIDE + NSA_CMP_LEN - 1 <= tt[None, :])
    adist = np.abs(tt[None, :] - (nn[:, None] * NSA_CMP_STRIDE + (NSA_CMP_LEN - 1) / 2.0)).astype(np.float32)
    cbias = jnp.where(ended[None], -slopes[:, None, None] * jnp.asarray(adist)[None], NEG_BIG)
    G = NSA_GROUPS
    W = NSA_HPG * NSA_DH
    return pl.pallas_call(
        functools.partial(_nsa_cmp_kernel, n_slc=n_slc, n_sel=n_sel),
        out_shape=(jax.ShapeDtypeStruct((N, NSA_HEADS * NSA_DH), BF16),
                   jax.ShapeDtypeStruct((batch, G, seq, n_slc), BF16)),
        grid=(batch, G, nq),
        in_specs=[
            pl.BlockSpec((QB, W), lambda b, g, i: (b * nq + i, U_NSA_Q // W + g)),
            pl.BlockSpec((1, 1, ncp, NSA_DH), lambda b, g, i: (b, g, 0, 0)),
            pl.BlockSpec((1, 1, NSA_DH + 8, ncp), lambda b, g, i: (b, g, 0, 0)),
            pl.BlockSpec((n_slc, ncp), lambda b, g, i: (0, 0)),
            pl.BlockSpec((NSA_HPG, ncp, QB), lambda b, g, i: (g, 0, i)),
        ],
        out_specs=(
            pl.BlockSpec((QB, W), lambda b, g, i: (b * nq + i, g)),
            pl.BlockSpec((1, 1, QB, n_slc), lambda b, g, i: (b, g, i, 0)),
        ),
        scratch_shapes=[pltpu.VMEM((NSA_HPG * QB, NSA_DH), BF16)],
        compiler_params=_cparams(("parallel", "parallel", "arbitrary")),
        name="nsa_cmp",
    )(U, kc, vct, jnp.asarray(ov), cbias)


def _nsa_win_kernel(*refs):
    nb = NSA_WINDOW // NSA_QB + 1
    q_ref, k_refs, v_refs = refs[0], refs[1:1 + nb], refs[1 + nb:1 + 2 * nb]
    bias_ref, o_ref, qs_ref = refs[1 + 2 * nb:]
    qi = pl.program_id(2)
    QB = NSA_QB
    KW = NSA_WINDOW + QB
    for h in range(NSA_HPG):
        qs_ref[h * QB:(h + 1) * QB, :] = q_ref[:, h * NSA_DH:(h + 1) * NSA_DH] * (NSA_DH ** -0.5)
    k = jnp.concatenate([r[0, 0] for r in k_refs], axis=0)
    vt = jnp.concatenate([r[0, 0] for r in v_refs], axis=1)
    s = _dot_nt(k, qs_ref[...])

    def finish(before_start):
        for h in range(NSA_HPG):
            cols = slice(h * QB, (h + 1) * QB)
            sh = s[:, cols] + bias_ref[h]
            if before_start:
                r = lax.broadcasted_iota(jnp.int32, (KW, QB), 0)
                sh = jnp.where(r >= NSA_WINDOW - qi * QB, sh, NEG_BIG)
            m = jnp.max(sh, axis=0, keepdims=True)
            e = jnp.exp(sh - m)
            acc = _dot(vt, e.astype(BF16))
            o = acc[0:NSA_DH] / acc[NSA_DH:NSA_DH + 1]
            o_ref[:, h * NSA_DH:(h + 1) * NSA_DH] = o.T.astype(o_ref.dtype)

    @pl.when(qi * QB < NSA_WINDOW)
    def _():
        finish(True)

    @pl.when(qi * QB >= NSA_WINDOW)
    def _():
        finish(False)


def nsa_win(U, kw_pad, vtw_pad, slopes, *, batch, seq):
    N = U.shape[0]
    QB = NSA_QB
    KW = NSA_WINDOW + QB
    nq = seq // QB
    G = NSA_GROUPS
    W = NSA_HPG * NSA_DH
    dist = (np.arange(QB)[None, :] - np.arange(KW)[:, None] + NSA_WINDOW).astype(np.float32)
    inside = jnp.asarray((dist >= 0) & (dist < NSA_WINDOW))
    bias = jnp.where(inside[None], -slopes[:, None, None] * jnp.asarray(dist)[None], NEG_BIG)
    nb = NSA_WINDOW // QB + 1
    kb = lambda off: pl.BlockSpec((1, 1, QB, NSA_DH), lambda b, g, i: (b, g, i + off, 0))
    vb = lambda off: pl.BlockSpec((1, 1, NSA_DH + 8, QB), lambda b, g, i: (b, g, 0, i + off))
    return pl.pallas_call(
        _nsa_win_kernel,
        out_shape=jax.ShapeDtypeStruct((N, NSA_HEADS * NSA_DH), BF16),
        grid=(batch, G, nq),
        in_specs=[pl.BlockSpec((QB, W), lambda b, g, i: (b * nq + i, U_NSA_Q // W + g)),
                  *[kb(o) for o in range(nb)], *[vb(o) for o in range(nb)],
                  pl.BlockSpec((NSA_HPG, KW, QB), lambda b, g, i: (g, 0, 0))],
        out_specs=pl.BlockSpec((QB, W), lambda b, g, i: (b * nq + i, g)),
        scratch_shapes=[pltpu.VMEM((NSA_HPG * QB, NSA_DH), BF16)],
        compiler_params=_cparams(("parallel", "parallel", "arbitrary")),
        name="nsa_win",
    )(U, *([kw_pad] * nb), *([vtw_pad] * nb), bias)


SEL_QB = 512
SEL_KV = 512


def _sel_pairs(seq):
    qs, ks = [], []
    for qi in range(seq // SEL_QB):
        for kj in range((qi * SEL_QB) // SEL_KV + 1):
            qs.append(qi)
            ks.append(kj)
    return np.asarray(qs, np.int32), np.asarray(ks, np.int32)


def _nsa_sel_kernel(qi_ref, kj_ref, slope_ref, q_ref, sb_ref, qx_ref, ka_ref, vt_ref, o_ref,
                    qa_ref, m_ref, acc_ref):
    g = pl.program_id(1)
    p = pl.program_id(2)
    qi = qi_ref[p]
    kj = kj_ref[p]
    QB, KV = SEL_QB, SEL_KV
    nblk = sb_ref.shape[3]
    base = NSA_DH + nblk

    @pl.when(kj == 0)
    def _():
        for h in range(NSA_HPG):
            rows = slice(h * QB, (h + 1) * QB)
            qa_ref[rows, 0:NSA_DH] = q_ref[:, h * NSA_DH:(h + 1) * NSA_DH] * (NSA_DH ** -0.5)
            qa_ref[rows, NSA_DH:base] = sb_ref[0, 0]
            qa_ref[rows, base:] = jnp.broadcast_to(qx_ref[0, h:h + 1, :], (QB, qa_ref.shape[1] - base))
        m_ref[...] = jnp.full_like(m_ref, NEG_BIG)
        acc_ref[...] = jnp.zeros_like(acc_ref)

    off = qi * QB - kj * KV
    tile_start = (kj * KV).astype(F32)

    def step(masked):
        s = _dot_nt(ka_ref[0, 0], qa_ref[...])
        vt = vt_ref[0, 0]
        if masked:
            r = lax.broadcasted_iota(jnp.int32, (KV, QB), 0)
            c = lax.broadcasted_iota(jnp.int32, (KV, QB), 1)
            causal = (c - r + off) >= 0
        for h in range(NSA_HPG):
            cols = slice(h * QB, (h + 1) * QB)
            sh = s[:, cols]
            if masked:
                sh = jnp.where(causal, sh, NEG_BIG)
            delta = slope_ref[g * NSA_HPG + h] * tile_start
            m_old = m_ref[h:h + 1, :]
            m_new = jnp.maximum(m_old, jnp.max(sh, axis=0, keepdims=True) + delta)
            e = jnp.exp(sh - (m_new - delta))
            alpha = jnp.exp(m_old - m_new)
            acc_ref[:, cols] = alpha * acc_ref[:, cols] + _dot(vt, e.astype(BF16))
            m_ref[h:h + 1, :] = m_new

    last = (qi * QB) // KV

    @pl.when(kj < last)
    def _():
        step(False)

    @pl.when(kj == last)
    def _():
        step(True)
        for h in range(NSA_HPG):
            cols = slice(h * QB, (h + 1) * QB)
            o = acc_ref[0:NSA_DH, cols] / acc_ref[NSA_DH:NSA_DH + 1, cols]
            o_ref[:, h * NSA_DH:(h + 1) * NSA_DH] = o.T.astype(o_ref.dtype)


def sel_key_columns(seq):
    n_slc = seq // NSA_SLC_LEN
    pos = np.arange(seq)
    onehot = (pos[:, None] // NSA_SLC_LEN == np.arange(n_slc)[None, :]).astype(np.float32)
    r = pos % SEL_KV
    extra = np.zeros((seq, 128), np.float32)
    extra[:, 0] = extra[:, 2] = (r // 32) * 32
    extra[:, 1] = extra[:, 3] = r % 32
    return jnp.asarray(np.concatenate([onehot, extra], axis=1), BF16)


def nsa_sel(U, selbias, k_aug, vt_slc, slopes, *, batch, seq):
    N = U.shape[0]
    QB, KV = SEL_QB, SEL_KV
    nq = seq // QB
    G = NSA_GROUPS
    W = NSA_HPG * NSA_DH
    n_slc = selbias.shape[3]
    ka_w = k_aug.shape[3]
    qs, ks = _sel_pairs(seq)
    s_hi = slopes.astype(BF16)
    s_lo = (slopes - s_hi.astype(F32)).astype(BF16)
    qx = jnp.zeros((NSA_HEADS, 128), BF16).at[:, 0].set(s_hi).at[:, 1].set(s_hi).at[:, 2].set(s_lo).at[:, 3].set(s_lo)
    qx = jnp.pad(qx.reshape(G, NSA_HPG, 128), ((0, 0), (0, 8 - NSA_HPG), (0, 0)))
    grid_spec = pltpu.PrefetchScalarGridSpec(
        num_scalar_prefetch=3,
        grid=(batch, G, len(qs)),
        in_specs=[
            pl.BlockSpec((QB, W), lambda b, g, p, qi, kj, s: (b * nq + qi[p], U_NSA_Q // W + g)),
            pl.BlockSpec((1, 1, QB, n_slc), lambda b, g, p, qi, kj, s: (b, g, qi[p], 0)),
            pl.BlockSpec((1, 8, 128), lambda b, g, p, qi, kj, s: (g, 0, 0)),
            pl.BlockSpec((1, 1, KV, ka_w), lambda b, g, p, qi, kj, s: (b, g, kj[p], 0)),
            pl.BlockSpec((1, 1, NSA_DH + 8, KV), lambda b, g, p, qi, kj, s: (b, g, 0, kj[p])),
        ],
        out_specs=pl.BlockSpec((QB, W), lambda b, g, p, qi, kj, s: (b * nq + qi[p], g)),
        scratch_shapes=[
            pltpu.VMEM((NSA_HPG * QB, ka_w), BF16),
            pltpu.VMEM((NSA_HPG, QB), F32),
            pltpu.VMEM((NSA_DH + 8, NSA_HPG * QB), F32),
        ],
    )
    return pl.pallas_call(
        _nsa_sel_kernel,
        out_shape=jax.ShapeDtypeStruct((N, NSA_HEADS * NSA_DH), BF16),
        grid_spec=grid_spec,
        compiler_params=_cparams(("parallel", "parallel", "arbitrary")),
        name="nsa_sel",
    )(jnp.asarray(qs), jnp.asarray(ks), slopes, U, selbias, qx, k_aug, vt_slc)


def _nsa_combine_kernel(oc_ref, os_ref, ow_ref, gate_ref, e_ref, o_ref):
    ge = _dot_split(_sigmoid(gate_ref[...]), e_ref[...])
    Wd = NSA_HEADS * NSA_DH
    o = (ge[:, :Wd] * oc_ref[...].astype(F32) + ge[:, Wd:2 * Wd] * os_ref[...].astype(F32)
         + ge[:, 2 * Wd:] * ow_ref[...].astype(F32))
    o_ref[...] = o.astype(o_ref.dtype)


def nsa_combine(o_c, o_s, o_w, Us):
    N, Wd = o_c.shape
    tm = 512
    e = np.zeros((128, 3 * Wd), np.float32)
    for h in range(NSA_HEADS):
        for j in range(3):
            e[h * 3 + j, j * Wd + h * NSA_DH:j * Wd + (h + 1) * NSA_DH] = 1.0
    blk = pl.BlockSpec((tm, Wd), lambda i: (i, 0))
    return pl.pallas_call(
        _nsa_combine_kernel,
        out_shape=jax.ShapeDtypeStruct((N, Wd), BF16),
        grid=(N // tm,),
        in_specs=[blk, blk, blk, pl.BlockSpec((tm, 128), lambda i: (i, S_NSA_GATE // 128)),
                  pl.BlockSpec((128, 3 * Wd), lambda i: (0, 0))],
        out_specs=blk,
        compiler_params=_cparams(("parallel",)),
        name="nsa_combine",
    )(o_c, o_s, o_w, Us, jnp.asarray(e, BF16))


def _merge_kernel(ya_ref, yb_ref, yc_ref, ga_ref, gb_ref, gc_ref, p_ref, o_ref):
    m = (_sigmoid(ga_ref[...].astype(F32)) * _dot(ya_ref[...], p_ref[0])
         + _sigmoid(gb_ref[...].astype(F32)) * _dot(yb_ref[...], p_ref[1])
         + _sigmoid(gc_ref[...].astype(F32)) * _dot(yc_ref[...], p_ref[2]))
    o_ref[...] = m.astype(o_ref.dtype)


def merge(y_a, y_b, y_c, U, p_merge):
    N = y_a.shape[0]
    tm, tn = 1024, 512
    yb = pl.BlockSpec((tm, BRANCH_DIM), lambda i, j: (i, 0))
    gate = lambda br: pl.BlockSpec((tm, tn), lambda i, j: (i, (U_GATE + br * D_MODEL) // tn + j))
    return pl.pallas_call(
        _merge_kernel,
        out_shape=jax.ShapeDtypeStruct((N, D_MODEL), BF16),
        grid=(N // tm, D_MODEL // tn),
        in_specs=[yb, yb, yb, gate(0), gate(1), gate(2),
                  pl.BlockSpec((3, BRANCH_DIM, tn), lambda i, j: (0, 0, j))],
        out_specs=pl.BlockSpec((tm, tn), lambda i, j: (i, j)),
        compiler_params=_cparams(("parallel", "parallel")),
        name="merge",
    )(y_a, y_b, y_c, U, U, U, p_merge)


def _ffn_kernel(h_ref, w1_ref, w3_ref, w2_ref, o_ref, acc_ref):
    f = pl.program_id(1)

    @pl.when(f == 0)
    def _():
        acc_ref[...] = jnp.zeros_like(acc_ref)

    h = h_ref[...]
    a = _dot(h, w1_ref[...])
    z = (a * _sigmoid(a) * _dot(h, w3_ref[...])).astype(BF16)
    acc_ref[...] += _dot(z, w2_ref[...])

    @pl.when(f == pl.num_programs(1) - 1)
    def _():
        o_ref[...] = acc_ref[...].astype(o_ref.dtype)


def ffn(h, w1, w3, w2):
    N, D = h.shape
    F = w1.shape[1]
    tm, tf = 1024, 512
    return pl.pallas_call(
        _ffn_kernel,
        out_shape=jax.ShapeDtypeStruct((N, D), BF16),
        grid=(N // tm, F // tf),
        in_specs=[pl.BlockSpec((tm, D), lambda i, f: (i, 0)),
                  pl.BlockSpec((D, tf), lambda i, f: (0, f)),
                  pl.BlockSpec((D, tf), lambda i, f: (0, f)),
                  pl.BlockSpec((tf, D), lambda i, f: (f, 0))],
        out_specs=pl.BlockSpec((tm, D), lambda i, f: (i, 0)),
        scratch_shapes=[pltpu.VMEM((tm, D), F32)],
        compiler_params=_cparams(("parallel", "arbitrary")),
        name="ffn",
    )(h, w1, w3, w2)


def _route_kernel(lg_ref, cb_ref, sel_ref):
    lg = lg_ref[...]
    lane = lax.broadcasted_iota(jnp.int32, lg.shape, 1)
    x = jnp.where(lane < N_EXPERTS, lg, NEG_BIG)
    v1 = jnp.max(x, axis=-1, keepdims=True)
    i1 = jnp.min(jnp.where(x == v1, lane, 1024), axis=-1, keepdims=True)
    x2 = jnp.where(lane == i1, NEG_BIG, x)
    v2 = jnp.max(x2, axis=-1, keepdims=True)
    i2 = jnp.min(jnp.where(x2 == v2, lane, 1024), axis=-1, keepdims=True)
    e2 = jnp.exp(v2 - v1)
    w1 = 1.0 / (1.0 + e2)
    w2 = e2 / (1.0 + e2)
    cb_ref[...] = jnp.where(lane == i1, w1, 0.0) + jnp.where(lane == i2, w2, 0.0)
    sel_ref[...] = jnp.where((lane == i1) | (lane == i2), 1.0, 0.0)


def route(logits):
    N = logits.shape[0]
    tm = 1024
    blk = pl.BlockSpec((tm, 128), lambda i: (i, 0))
    return pl.pallas_call(
        _route_kernel,
        out_shape=(jax.ShapeDtypeStruct((N, 128), F32), jax.ShapeDtypeStruct((N, 128), F32)),
        grid=(N // tm,),
        in_specs=[blk],
        out_specs=(blk, blk),
        compiler_params=_cparams(("parallel",)),
        name="route",
    )(logits)


MOE_TM = 1024
MOE_TF = 256
MOE_WB = 512
MOE_GT = 256


def _moe_plan(sel):
    N, E = sel.shape
    tm, wb, gt = MOE_TM, MOE_WB, MOE_GT
    sub = tm // gt
    NT = 2 * N // tm + E
    NG = NT * sub
    R = NT * tm
    P = NG + E * (N // wb)
    i32 = jnp.int32
    cs = jnp.cumsum(sel.astype(i32), axis=0)
    cnt = cs[-1]
    tiles_e = (cnt + tm - 1) // tm
    tile_end = jnp.cumsum(tiles_e)
    tile_start = tile_end - tiles_e
    total_tiles = tile_end[-1]
    dest = jnp.where(sel, tile_start[None, :] * tm + cs - 1, R)
    ti = jnp.arange(NT, dtype=i32)
    tile_valid = ti < total_tiles
    count_le = lambda ends, q: jnp.sum((ends[None, :] <= q[:, None]).astype(i32), axis=1)
    tile_e = jnp.minimum(count_le(tile_end, ti), E - 1)
    gi = jnp.arange(NG, dtype=i32)
    g_valid = tile_valid[gi // sub]
    g_e = tile_e[gi // sub]
    k_lo = gi * gt - tile_start[g_e] * tm
    nonempty = g_valid & (k_lo < cnt[g_e])
    k_hi = jnp.minimum(k_lo + gt, cnt[g_e]) - 1
    cb = cs[wb - 1::wb, :].T[g_e]
    blk_of = lambda k: jnp.sum((cb <= k[:, None]).astype(i32), axis=1)
    blo = jnp.where(nonempty, blk_of(k_lo), 0)
    bhi = jnp.where(nonempty, blk_of(k_hi), jnp.where(g_valid, 0, -1))
    npair = bhi - blo + 1
    pend = jnp.cumsum(npair)
    pstart = pend - npair
    total_p = pend[-1]
    pi = jnp.arange(P, dtype=i32)
    pvalid = pi < total_p
    ptile = jnp.minimum(count_le(pend, pi), NG - 1)
    pblk = blo[ptile] + pi - pstart[ptile]
    pfirst = pvalid & (pi == pstart[ptile])
    ptile = jnp.where(pvalid, ptile, ptile[total_p - 1])
    pblk = jnp.where(pvalid, pblk, pblk[total_p - 1])
    pexp = g_e[ptile]
    order = jnp.argsort(jnp.where(pvalid, pblk * NG + ptile, jnp.iinfo(jnp.int32).max))
    s_valid = pvalid
    s_tile = jnp.where(s_valid, ptile[order], ptile[order][total_p - 1])
    s_blk = jnp.where(s_valid, pblk[order], pblk[order][total_p - 1])
    s_exp = g_e[s_tile]
    s_first = s_valid & ((pi == 0) | (s_blk != jnp.roll(s_blk, 1)))
    e_src = jnp.where(tile_valid, ti, 0)
    e_exp = jnp.where(tile_valid, tile_e, tile_e[jnp.maximum(total_tiles - 1, 0)])
    b2i = lambda x: x.astype(i32)
    return dict(dest=dest, g=(ptile, pblk, pexp, b2i(pvalid), b2i(pfirst)),
                e=(e_exp, b2i(tile_valid), e_src), s=(s_tile, s_blk, s_exp, b2i(s_valid), b2i(s_first)), NT=NT, P=P)


def _moe_group_kernel(pt_ref, pb_ref, pe_ref, pv_ref, pf_ref, dest_ref, h_ref, o_ref):
    p = pl.program_id(0)

    @pl.when(pv_ref[p] == 1)
    def _():
        rel = dest_ref[0] - pt_ref[p] * MOE_GT
        row = lax.broadcasted_iota(jnp.int32, (MOE_GT, MOE_WB), 0)
        onehot = jnp.where(rel == row, 1.0, 0.0).astype(BF16)
        rows = _dot(onehot, h_ref[...]).astype(o_ref.dtype)

        @pl.when(pf_ref[p] == 1)
        def _():
            o_ref[...] = rows

        @pl.when(pf_ref[p] == 0)
        def _():
            o_ref[...] += rows


def _moe_expert_kernel(te_ref, tv_ref, ts_ref, h_ref, w1_ref, w3_ref, w2_ref, o_ref, acc_ref):
    i = pl.program_id(0)
    f = pl.program_id(1)
    nf = pl.num_programs(1)

    @pl.when(tv_ref[i] == 1)
    def _():
        @pl.when(f == 0)
        def _():
            acc_ref[...] = jnp.zeros_like(acc_ref)

        h = h_ref[...]
        a = _dot(h, w1_ref[0, 0].astype(BF16))
        z = (a * _sigmoid(a) * _dot(h, w3_ref[0, 0].astype(BF16))).astype(BF16)
        acc_ref[...] += _dot(z, w2_ref[0, 0].astype(BF16))

        @pl.when(f == nf - 1)
        def _():
            o_ref[...] = acc_ref[...].astype(o_ref.dtype)

    @pl.when((tv_ref[i] == 0) & (f == nf - 1))
    def _():
        o_ref[...] = jnp.zeros_like(o_ref)


def _moe_ungroup_kernel(st_ref, sb_ref, se_ref, sv_ref, sf_ref, dest_ref, w_ref, y_ref, o_ref):
    p = pl.program_id(0)

    @pl.when(sv_ref[p] == 1)
    def _():
        rel = dest_ref[0] - st_ref[p] * MOE_GT
        col = lax.broadcasted_iota(jnp.int32, (MOE_WB, MOE_GT), 1)
        onehot_t = jnp.where(rel == col, 1.0, 0.0).astype(BF16)
        part = _dot(onehot_t, y_ref[...]) * w_ref[0]

        @pl.when(sf_ref[p] == 1)
        def _():
            o_ref[...] = part

        @pl.when(sf_ref[p] == 0)
        def _():
            o_ref[...] += part


def moe(h, combine, selm, w1, w3, w2, layer):
    N, D = h.shape
    _, E, _, F = w1.shape
    tm, wb, gt, tf = MOE_TM, MOE_WB, MOE_GT, MOE_TF
    plan = _moe_plan(selm[:, :E] > 0.5)
    NT, P = plan["NT"], plan["P"]
    R = NT * tm
    dest_t = plan["dest"].T
    h_sorted = pl.pallas_call(
        _moe_group_kernel,
        out_shape=jax.ShapeDtypeStruct((R, D), BF16),
        grid_spec=pltpu.PrefetchScalarGridSpec(
            num_scalar_prefetch=5,
            grid=(P,),
            in_specs=[pl.BlockSpec((1, 1, wb), lambda p, pt, pb, pe, pv, pf: (pe[p], 0, pb[p])),
                      pl.BlockSpec((wb, D), lambda p, pt, pb, pe, pv, pf: (pb[p], 0))],
            out_specs=pl.BlockSpec((gt, D), lambda p, pt, pb, pe, pv, pf: (pt[p], 0)),
        ),
        compiler_params=_cparams(("arbitrary",)),
        name="moe_group",
    )(*plan["g"], dest_t.reshape(E, 1, N), h)
    y_sorted = pl.pallas_call(
        _moe_expert_kernel,
        out_shape=jax.ShapeDtypeStruct((R, D), BF16),
        grid_spec=pltpu.PrefetchScalarGridSpec(
            num_scalar_prefetch=3,
            grid=(NT, F // tf),
            in_specs=[pl.BlockSpec((tm, D), lambda i, f, te, tv, ts: (ts[i], 0)),
                      pl.BlockSpec((1, 1, D, tf), lambda i, f, te, tv, ts: (layer, te[i], 0, f * tv[i])),
                      pl.BlockSpec((1, 1, D, tf), lambda i, f, te, tv, ts: (layer, te[i], 0, f * tv[i])),
                      pl.BlockSpec((1, 1, tf, D), lambda i, f, te, tv, ts: (layer, te[i], f * tv[i], 0))],
            out_specs=pl.BlockSpec((tm, D), lambda i, f, te, tv, ts: (i, 0)),
            scratch_shapes=[pltpu.VMEM((tm, D), F32)],
        ),
        compiler_params=_cparams(("arbitrary", "arbitrary")),
        name="moe_expert",
    )(*plan["e"], h_sorted, w1, w3, w2)
    return pl.pallas_call(
        _moe_ungroup_kernel,
        out_shape=jax.ShapeDtypeStruct((N, D), F32),
        grid_spec=pltpu.PrefetchScalarGridSpec(
            num_scalar_prefetch=5,
            grid=(P,),
            in_specs=[pl.BlockSpec((1, wb, 1), lambda p, st, sb, se, sv, sf: (se[p], sb[p], 0)),
                      pl.BlockSpec((1, wb, 1), lambda p, st, sb, se, sv, sf: (se[p], sb[p], 0)),
                      pl.BlockSpec((gt, D), lambda p, st, sb, se, sv, sf: (st[p], 0))],
            out_specs=pl.BlockSpec((wb, D), lambda p, st, sb, se, sv, sf: (sb[p], 0)),
        ),
        compiler_params=_cparams(("arbitrary",)),
        name="moe_ungroup",
    )(*plan["s"], dest_t.reshape(E, N, 1), combine[:, :E].T.reshape(E, N, 1), y_sorted)


def _project_weights(w_in):
    gla_w = 2 * GLA_HEADS * GLA_DK + 2 * GLA_HEADS * GLA_DV + GLA_RANK
    rw_w = 3 * RWKV_DIM + RWKV_W_RANK + RWKV_A_RANK + RWKV_G_RANK
    kvw = NSA_GROUPS * NSA_DH
    nsa_w = NSA_HEADS * NSA_DH + 6 * kvw + NSA_HEADS * 3
    o_rw = gla_w
    o_nsa = gla_w + rw_w
    o_gate = o_nsa + nsa_w
    big = jnp.concatenate([
        w_in[:, 0:gla_w - GLA_RANK],
        w_in[:, o_rw:o_rw + 3 * RWKV_DIM],
        w_in[:, o_nsa:o_nsa + NSA_HEADS * NSA_DH + 6 * kvw],
        w_in[:, o_gate:],
    ], axis=1).astype(BF16)
    D = w_in.shape[0]
    z = lambda n: jnp.zeros((D, n), w_in.dtype)
    small = jnp.concatenate([
        w_in[:, gla_w - GLA_RANK:gla_w], z(128 - GLA_RANK),
        w_in[:, o_rw + 3 * RWKV_DIM:o_rw + rw_w], z(256 - RWKV_G_RANK),
        w_in[:, o_gate - NSA_HEADS * 3:o_gate], z(128 - NSA_HEADS * 3),
    ], axis=1).astype(BF16)
    return big, small


def _mixer(h, batch, seq, w_in, gla_a2, gla_a_b, gla_norm, rwkv_mu, rwkv_w0, rwkv_w2, rwkv_a0, rwkv_a2, rwkv_g2,
           rwkv_k_k, rwkv_k_a, rwkv_r_k, rwkv_ln_w, rwkv_ln_b,
           nsa_pos_k, nsa_w1_k, nsa_w2_k, nsa_pos_v, nsa_w1_v, nsa_w2_v, p_merge, w_out, slopes):
    N = h.shape[0]
    w_big, w_small = _project_weights(w_in)
    tm_u = 2048 if N % 2048 == 0 else N
    U = matmul(h, w_big, BF16, tm_u, 512)
    Us = matmul(h, w_small, F32, tm_u, S_COLS)

    y_a = gla(U, Us, gla_a2, gla_a_b, gla_norm, batch=batch, seq=seq)

    r, k, v, kap, b, lw, gate = rwkv_prep(U, Us, rwkv_mu, rwkv_w0, rwkv_w2, rwkv_a0, rwkv_a2, rwkv_g2,
                                          rwkv_k_k, rwkv_k_a, seq=seq)
    y_b = rwkv_chunk(r, k, v, kap, b, lw, gate, rwkv_r_k.reshape(-1), rwkv_ln_w, rwkv_ln_b, batch=batch, seq=seq)

    kvw = NSA_GROUPS * NSA_DH

    def stream(i):
        return U[:, U_NSA_KV + i * kvw:U_NSA_KV + (i + 1) * kvw].reshape(batch, seq, NSA_GROUPS, NSA_DH)

    nr = seq // NSA_CMP_STRIDE
    x_cmp = U[:, U_NSA_KV:U_NSA_KV + 2 * kvw].reshape(batch, nr, NSA_CMP_STRIDE, 2, NSA_GROUPS, NSA_DH)
    x_cmp = x_cmp.transpose(3, 0, 4, 1, 2, 5).reshape(2 * batch, NSA_GROUPS, nr, NSA_CMP_STRIDE * NSA_DH)
    kv6 = {2: stream(2).transpose(0, 2, 1, 3), 4: stream(4).transpose(0, 2, 1, 3)}
    vt6 = {3: stream(3).transpose(0, 2, 3, 1), 5: stream(5).transpose(0, 2, 3, 1)}
    kvc = nsa_compress(x_cmp, jnp.stack([nsa_pos_k, nsa_pos_v]), jnp.stack([nsa_w1_k, nsa_w1_v]),
                       jnp.stack([nsa_w2_k, nsa_w2_v]))
    ncp = kvc.shape[3]
    ones_c = jnp.zeros((batch, NSA_GROUPS, 8, ncp), BF16).at[:, :, 0, :].set(1.0)
    vct = jnp.concatenate([kvc[1].transpose(0, 1, 3, 2), ones_c], axis=2)
    o_c, selbias = nsa_cmp(U, kvc[0], vct, slopes, batch=batch, seq=seq)
    n_slc = seq // NSA_SLC_LEN
    k_aug = jnp.concatenate([kv6[2], jnp.broadcast_to(sel_key_columns(seq), (batch, NSA_GROUPS, seq, n_slc + 128))],
                            axis=-1)
    ones_rows = jnp.zeros((batch, NSA_GROUPS, 8, seq), BF16).at[:, :, 0, :].set(1.0)
    vt_aug = jnp.concatenate([vt6[3], ones_rows], axis=2)
    o_s = nsa_sel(U, selbias, k_aug, vt_aug, slopes, batch=batch, seq=seq)
    kw_pad = jnp.pad(kv6[4], ((0, 0), (0, 0), (NSA_WINDOW, 0), (0, 0)))
    vtw = jnp.concatenate([vt6[5], ones_rows], axis=2)
    vtw_pad = jnp.pad(vtw, ((0, 0), (0, 0), (0, 0), (NSA_WINDOW, 0)))
    o_w = nsa_win(U, kw_pad, vtw_pad, slopes, batch=batch, seq=seq)
    y_c = nsa_combine(o_c, o_s, o_w, Us)

    merged = merge(y_a, y_b, y_c, U, p_merge.astype(BF16))
    return matmul(merged, w_out.astype(BF16), BF16, 1024 if N % 1024 == 0 else N, 1024)


def kernel(x, c, norm_mix, norm_ffn, ada_w, ada_b, w_in, gla_a2, gla_a_b, gla_norm, rwkv_mu, rwkv_w0, rwkv_w2, rwkv_a0, rwkv_a2, rwkv_g2, rwkv_k_k, rwkv_k_a, rwkv_r_k, rwkv_ln_w, rwkv_ln_b, nsa_pos_k, nsa_w1_k, nsa_w2_k, nsa_pos_v, nsa_w1_v, nsa_w2_v, p_merge, w_out, ffn_w1, ffn_w3, ffn_w2, moe_router, moe_w1, moe_w3, moe_w2, final_norm):
    B, T, D = x.shape
    depth = w_in.shape[0]
    N = B * T
    xs = x.reshape(N, D)
    ada = ada_all(c, ada_w, ada_b)
    slopes = jnp.exp2(-8.0 * jnp.arange(1, NSA_HEADS + 1, dtype=F32) / NSA_HEADS)
    y = None
    g_prev = None
    for l in range(depth):
        sh1, sc1, g1, sh2, sc2, g2 = jnp.split(ada[l], 6, axis=-1)
        if y is None:
            (h,) = resmod(xs, None, None, norm_mix[l], sh1, sc1, seq=T)
        else:
            xs, h = resmod(xs, y, g_prev, norm_mix[l], sh1, sc1, seq=T)
        y = _mixer(h, B, T, w_in[l], gla_a2[l], gla_a_b[l], gla_norm[l], rwkv_mu[l], rwkv_w0[l], rwkv_w2[l],
                   rwkv_a0[l], rwkv_a2[l], rwkv_g2[l], rwkv_k_k[l], rwkv_k_a[l], rwkv_r_k[l],
                   rwkv_ln_w[l], rwkv_ln_b[l], nsa_pos_k[l], nsa_w1_k[l], nsa_w2_k[l],
                   nsa_pos_v[l], nsa_w1_v[l], nsa_w2_v[l], p_merge[l], w_out[l], slopes)
        if l % 2 == 0:
            xs, h = resmod(xs, y, g1, norm_ffn[l], sh2, sc2, seq=T)
            y = ffn(h, ffn_w1[l // 2].astype(BF16), ffn_w3[l // 2].astype(BF16), ffn_w2[l // 2].astype(BF16))
        else:
            rt = jnp.zeros((D, 128), F32).at[:, :N_EXPERTS].set(moe_router[l // 2])
            xs, h, logits = resmod(xs, y, g1, norm_ffn[l], sh2, sc2, seq=T, router=rt)
            combine, selm = route(logits)
            y = moe(h, combine, selm, moe_w1, moe_w3, moe_w2, l // 2)
        g_prev = g2
    (out,) = resmod(xs, y, g_prev, final_norm, None, None, seq=T, final=True)
    return out.reshape(B, T, D)
```

```python
import functools

import numpy as np
import jax
import jax.numpy as jnp
from jax import lax
from jax.experimental import pallas as pl
from jax.experimental.pallas import tpu as pltpu

F32 = jnp.float32
BF16 = jnp.bfloat16
HI = lax.Precision.HIGHEST

V7X_VMEM_LIMIT_BYTES = 56 * 1024 * 1024

D_MODEL = 2048
NORM_EPS = 1e-6
NEG_BIG = -1e30

GLA_HEADS = 4
GLA_DK = 128
GLA_DV = 256
GLA_RANK = 16
GLA_NORMALIZER = 16.0
CHUNK = 64

RWKV_HEADS = 16
RWKV_N = 64
RWKV_DIM = RWKV_HEADS * RWKV_N
RWKV_W_RANK = 64
RWKV_A_RANK = 64
RWKV_G_RANK = 160
RWKV_LN_EPS = 64e-5
RWKV_HB = 4

NSA_HEADS = 16
NSA_GROUPS = 4
NSA_HPG = 4
NSA_DH = 64
NSA_CMP_LEN = 32
NSA_CMP_STRIDE = 16
NSA_CMP_HIDDEN = 128
NSA_SLC_LEN = 64
NSA_N_SEL = 16
NSA_WINDOW = 512
NSA_QB = 256
CMP_QB = 512
SEL_MASK_BIAS = 131072.0

BRANCH_DIM = 1024
D_FF = 5632
N_EXPERTS = 8

U_GLA_Q, U_GLA_K, U_GLA_V, U_GLA_G = 0, 512, 1024, 2048
U_RWKV_R, U_RWKV_K, U_RWKV_V = 3072, 4096, 5120
U_NSA_Q = 6144
U_NSA_KV = 7168
U_GATE = 8704
U_COLS = U_GATE + 3 * D_MODEL
S_GLA_A = 0
S_RWKV_WA = 128
S_RWKV_G = 256
S_NSA_GATE = 512
S_COLS = 640


def _cparams(sem, vmem=V7X_VMEM_LIMIT_BYTES):
    return pltpu.CompilerParams(dimension_semantics=sem, vmem_limit_bytes=vmem)


def _sigmoid(x):
    return 1.0 / (1.0 + jnp.exp(-x))


def _softplus(x):
    return jnp.maximum(x, 0.0) + jnp.log(1.0 + jnp.exp(-jnp.abs(x)))


def _dot(a, b):
    return jnp.dot(a, b, preferred_element_type=F32)


def _dot_nt(a, b):
    return lax.dot_general(a, b, (((1,), (1,)), ((), ())), preferred_element_type=F32)


def _dot_tn(a, b):
    return lax.dot_general(a, b, (((0,), (0,)), ((), ())), preferred_element_type=F32)


def _dot_hi(a, b):
    return jnp.dot(a, b, preferred_element_type=F32, precision=HI)


def _dot_x3(a, b):
    ah = a.astype(BF16)
    al = (a - ah.astype(F32)).astype(BF16)
    bh = b.astype(BF16)
    bl = (b - bh.astype(F32)).astype(BF16)
    return _dot(ah, bh) + _dot(ah, bl) + _dot(al, bh)


def _dot_sel(w, x):
    wb = w.astype(BF16)
    hi = x.astype(BF16)
    r1 = x - hi.astype(F32)
    mid = r1.astype(BF16)
    lo = (r1 - mid.astype(F32)).astype(BF16)
    return _dot(wb, hi) + _dot(wb, mid) + _dot(wb, lo)


def _dot_split(x, w):
    hi = x.astype(BF16)
    lo = (x - hi.astype(F32)).astype(BF16)
    return _dot(hi, w) + _dot(lo, w)


def _ada_kernel(c_ref, w_ref, b_ref, o_ref):
    c = c_ref[...]
    o_ref[0] = _dot_hi(c * _sigmoid(c), w_ref[0]) + b_ref[0]


def ada_all(c, ada_w, ada_b):
    L, D, N6 = ada_w.shape
    B = c.shape[0]
    tn = 1024
    return pl.pallas_call(
        _ada_kernel,
        out_shape=jax.ShapeDtypeStruct((L, B, N6), F32),
        grid=(L, N6 // tn),
        in_specs=[
            pl.BlockSpec((B, D), lambda l, j: (0, 0)),
            pl.BlockSpec((1, D, tn), lambda l, j: (l, 0, j)),
            pl.BlockSpec((1, 1, tn), lambda l, j: (l, 0, j)),
        ],
        out_specs=pl.BlockSpec((1, B, tn), lambda l, j: (l, 0, j)),
        compiler_params=_cparams(("parallel", "parallel")),
        name="ada",
    )(c, ada_w, ada_b.reshape(L, 1, N6))


def _resmod_kernel(*refs, has_res, final, router):
    it = iter(refs)
    x_ref = next(it)
    y_ref = next(it) if has_res else None
    g_ref = next(it) if has_res else None
    ng_ref = next(it)
    sh_ref = None if final else next(it)
    sc_ref = None if final else next(it)
    rt_ref = next(it) if router else None
    xo_ref = next(it) if (has_res and not final) else None
    h_ref = next(it)
    lg_ref = next(it) if router else None

    x = x_ref[...]
    if has_res:
        x = x + g_ref[0] * y_ref[...].astype(F32)
        if xo_ref is not None:
            xo_ref[...] = x
    ms = jnp.mean(x * x, axis=-1, keepdims=True)
    h = x * lax.rsqrt(ms + NORM_EPS) * ng_ref[...]
    if not final:
        h = h * (1.0 + sc_ref[0]) + sh_ref[0]
    h_ref[...] = h.astype(h_ref.dtype)
    if router:
        lg_ref[...] = _dot_hi(h, rt_ref[...])


def resmod(x, y, g, norm_g, shift, scale, *, seq, router=None, final=False):
    N, D = x.shape
    tm = 512
    spb = seq // tm
    has_res = y is not None
    row = lambda i: (i, 0)
    per_b = lambda i: (i // spb, 0, 0)
    ins, specs = [x], [pl.BlockSpec((tm, D), row)]
    if has_res:
        ins += [y, g.reshape(-1, 1, D)]
        specs += [pl.BlockSpec((tm, D), row), pl.BlockSpec((1, 1, D), per_b)]
    ins.append(norm_g.reshape(1, D))
    specs.append(pl.BlockSpec((1, D), lambda i: (0, 0)))
    if not final:
        ins += [shift.reshape(-1, 1, D), scale.reshape(-1, 1, D)]
        specs += [pl.BlockSpec((1, 1, D), per_b), pl.BlockSpec((1, 1, D), per_b)]
    if router is not None:
        ins.append(router)
        specs.append(pl.BlockSpec(router.shape, lambda i: (0, 0)))
    outs, ospecs = [], []
    if has_res and not final:
        outs.append(jax.ShapeDtypeStruct((N, D), F32))
        ospecs.append(pl.BlockSpec((tm, D), row))
    outs.append(jax.ShapeDtypeStruct((N, D), F32 if final else BF16))
    ospecs.append(pl.BlockSpec((tm, D), row))
    if router is not None:
        outs.append(jax.ShapeDtypeStruct((N, 128), F32))
        ospecs.append(pl.BlockSpec((tm, 128), row))
    res = pl.pallas_call(
        functools.partial(_resmod_kernel, has_res=has_res, final=final, router=router is not None),
        out_shape=tuple(outs),
        grid=(N // tm,),
        in_specs=specs,
        out_specs=tuple(ospecs),
        compiler_params=_cparams(("parallel",)),
        name="resmod",
    )(*ins)
    return res


def _mm_kernel(x_ref, w_ref, o_ref):
    o_ref[...] = _dot(x_ref[...], w_ref[...]).astype(o_ref.dtype)


def matmul(x, w, out_dtype, tm, tn):
    M, K = x.shape
    N = w.shape[1]
    return pl.pallas_call(
        _mm_kernel,
        out_shape=jax.ShapeDtypeStruct((M, N), out_dtype),
        grid=(M // tm, N // tn),
        in_specs=[pl.BlockSpec((tm, K), lambda i, j: (i, 0)), pl.BlockSpec((K, tn), lambda i, j: (0, j))],
        out_specs=pl.BlockSpec((tm, tn), lambda i, j: (i, j)),
        compiler_params=_cparams(("parallel", "parallel")),
        name="matmul",
    )(x, w)


GLA_TC = 512


def _gla_kernel(q_ref, k_ref, v_ref, g_ref, alr_ref, a2_ref, ab_ref, ng_ref, o_ref, st_ref, la_ref):
    @pl.when(pl.program_id(1) == 0)
    def _():
        st_ref[...] = jnp.zeros_like(st_ref)

    C, DK, DV = CHUNK, GLA_DK, GLA_DV
    la = _dot_x3(alr_ref[...], a2_ref[...]) + ab_ref[...]
    la_ref[...] = -_softplus(-la) / GLA_NORMALIZER
    ri = lax.broadcasted_iota(jnp.int32, (C, C), 0)
    ci = lax.broadcasted_iota(jnp.int32, (C, C), 1)
    causal = ri >= ci
    tril = causal.astype(F32)
    H = range(GLA_HEADS)

    def body(c, carry):
        sl = pl.ds(pl.multiple_of(c * C, C), C)
        bc_all = _dot_sel(tril, la_ref[sl, :])
        bcs = [bc_all[:, h * DK:(h + 1) * DK] for h in H]
        bls = [bc[C - 1:C, :] for bc in bcs]
        ks = [k_ref[sl, h * DK:(h + 1) * DK].astype(F32) for h in H]
        vs = [v_ref[sl, h * DV:(h + 1) * DV] for h in H]
        qds = [(q_ref[sl, h * DK:(h + 1) * DK].astype(F32) * (DK ** -0.5) * jnp.exp(bc)).astype(BF16)
               for h, bc in zip(H, bcs)]
        kds = [(k * jnp.exp(-bc)).astype(BF16) for k, bc in zip(ks, bcs)]
        kls = [(k * jnp.exp(bl - bc)).astype(BF16) for k, bl, bc in zip(ks, bls, bcs)]
        atts = [jnp.where(causal, _dot_nt(qd, kd), 0.0).astype(BF16) for qd, kd in zip(qds, kds)]
        sts = [st_ref[h] for h in H]
        os_ = [_dot(att, v) + _dot_nt(qd, st.astype(BF16)) for att, v, qd, st in zip(atts, vs, qds, sts)]
        for h in H:
            st_ref[h] = sts[h] * jnp.exp(bls[h]) + _dot_tn(vs[h], kls[h])
        for h in H:
            o = os_[h]
            o = o * lax.rsqrt(jnp.mean(o * o, axis=-1, keepdims=True) + NORM_EPS) * ng_ref[...]
            gg = g_ref[sl, h * DV:(h + 1) * DV].astype(F32)
            o_ref[sl, h * DV:(h + 1) * DV] = (o * (gg * _sigmoid(gg))).astype(o_ref.dtype)
        return carry

    lax.fori_loop(0, GLA_TC // C, body, 0)


def gla(U, Us, a2, a_b, norm_g, *, batch, seq):
    N = U.shape[0]
    nt = seq // GLA_TC
    HK, HV = GLA_HEADS * GLA_DK, GLA_HEADS * GLA_DV
    a2p = jnp.zeros((128, HK), F32).at[:GLA_RANK].set(a2)
    row = lambda b, i: b * nt + i
    return pl.pallas_call(
        _gla_kernel,
        out_shape=jax.ShapeDtypeStruct((N, HV), BF16),
        grid=(batch, nt),
        in_specs=[
            pl.BlockSpec((GLA_TC, HK), lambda b, i: (row(b, i), U_GLA_Q // HK)),
            pl.BlockSpec((GLA_TC, HK), lambda b, i: (row(b, i), U_GLA_K // HK)),
            pl.BlockSpec((GLA_TC, HV), lambda b, i: (row(b, i), U_GLA_V // HV)),
            pl.BlockSpec((GLA_TC, HV), lambda b, i: (row(b, i), U_GLA_G // HV)),
            pl.BlockSpec((GLA_TC, 128), lambda b, i: (row(b, i), S_GLA_A // 128)),
            pl.BlockSpec((128, HK), lambda b, i: (0, 0)),
            pl.BlockSpec((1, HK), lambda b, i: (0, 0)),
            pl.BlockSpec((1, GLA_DV), lambda b, i: (0, 0)),
        ],
        out_specs=pl.BlockSpec((GLA_TC, HV), lambda b, i: (row(b, i), 0)),
        scratch_shapes=[pltpu.VMEM((GLA_HEADS, GLA_DV, GLA_DK), F32), pltpu.VMEM((GLA_TC, HK), F32)],
        compiler_params=_cparams(("parallel", "arbitrary")),
        name="gla",
    )(U, U, U, U, Us, a2p, a_b.reshape(1, -1), norm_g.reshape(1, -1))


RWKV_TM = 256


def _seg_ones(n=256, seg=RWKV_N):
    i = np.arange(n)
    return jnp.asarray((i[:, None] // seg == i[None, :] // seg).astype(np.float32), BF16)


def _seg_sum(x, bd):
    outs = [_dot_split(x[:, s:s + 256], bd) for s in range(0, x.shape[1], 256)]
    return outs[0] if len(outs) == 1 else jnp.concatenate(outs, axis=1)


def _shift_lerp(u_ref, p_ref, mu, first):
    u = u_ref[...].astype(F32)
    prev_last = jnp.where(first, 0.0, p_ref[7:8, :].astype(F32))
    rolled = pltpu.roll(u, 1, 0)
    is_row0 = lax.broadcasted_iota(jnp.int32, u.shape, 0) == 0
    sh = jnp.where(is_row0, prev_last, rolled)
    return u + (sh - u) * mu


def _rwkv_prep_kernel(r_ref, k_ref, v_ref, wa_ref, gl_ref, rp_ref, kp_ref, vp_ref, wap_ref, glp_ref,
                      mur_ref, muk_ref, muv_ref, muwa_ref, mug_ref, w0_ref, w2_ref, a0_ref, a2_ref, g2_ref,
                      kk_ref, ka_ref, bd_ref,
                      ro_ref, ko_ref, vo_ref, kap_ref, bo_ref, lw_ref, go_ref, *, spb):
    first = (pl.program_id(0) % spb) == 0
    r = _shift_lerp(r_ref, rp_ref, mur_ref[...], first)
    k = _shift_lerp(k_ref, kp_ref, muk_ref[...], first)
    v = _shift_lerp(v_ref, vp_ref, muv_ref[...], first)
    wa = _shift_lerp(wa_ref, wap_ref, muwa_ref[...], first)
    gl = _shift_lerp(gl_ref, glp_ref, mug_ref[...], first)
    w_log = -_softplus(-(w0_ref[...] + _dot_x3(jnp.tanh(wa), w2_ref[...]))) - 0.5
    lw_ref[...] = -jnp.exp(w_log)
    a = _sigmoid(a0_ref[...] + _dot_x3(wa, a2_ref[...]))
    go_ref[...] = _dot(_sigmoid(gl).astype(BF16), g2_ref[...]).astype(go_ref.dtype)
    kk = k * kk_ref[...]
    nrm = jnp.sqrt(_seg_sum(kk * kk, bd_ref[...]))
    kk = kk / jnp.maximum(nrm, 1e-12)
    ro_ref[...] = r.astype(ro_ref.dtype)
    ko_ref[...] = (k * (1.0 + (a - 1.0) * ka_ref[...])).astype(ko_ref.dtype)
    vo_ref[...] = v.astype(vo_ref.dtype)
    kap_ref[...] = kk.astype(kap_ref.dtype)
    bo_ref[...] = (kk * a).astype(bo_ref.dtype)


def rwkv_prep(U, Us, mu, w0, w2, a0, a2, g2, k_k, k_a, *, seq):
    N = U.shape[0]
    tm = RWKV_TM
    spb = seq // tm
    R = RWKV_DIM
    cur = lambda cb: (lambda i: (i, cb))
    prv = lambda cb: (lambda i: (jnp.maximum(i * (tm // 8) - 1, 0), cb))
    mu_r, mu_k, mu_v = mu[:R], mu[R:2 * R], mu[2 * R:3 * R]
    mu_wa = mu[3 * R:3 * R + 128]
    mu_g = jnp.zeros((256,), F32).at[:RWKV_G_RANK].set(mu[3 * R + 128:])
    w2p = jnp.zeros((128, R), F32).at[:RWKV_W_RANK].set(w2)
    a2p = jnp.zeros((128, R), F32).at[RWKV_W_RANK:].set(a2)
    g2p = jnp.zeros((256, R), BF16).at[:RWKV_G_RANK].set(g2.astype(BF16))
    vec = lambda a: a.reshape(1, -1)
    full = lambda a: pl.BlockSpec(a.shape, lambda i: (0, 0))
    params = [vec(mu_r), vec(mu_k), vec(mu_v), vec(mu_wa), vec(mu_g), vec(w0), w2p, vec(a0), a2p, g2p,
              vec(k_k), vec(k_a), _seg_ones()]
    in_specs = [
        pl.BlockSpec((tm, R), cur(U_RWKV_R // R)), pl.BlockSpec((tm, R), cur(U_RWKV_K // R)),
        pl.BlockSpec((tm, R), cur(U_RWKV_V // R)),
        pl.BlockSpec((tm, 128), cur(S_RWKV_WA // 128)), pl.BlockSpec((tm, 256), cur(S_RWKV_G // 256)),
        pl.BlockSpec((8, R), prv(U_RWKV_R // R)), pl.BlockSpec((8, R), prv(U_RWKV_K // R)),
        pl.BlockSpec((8, R), prv(U_RWKV_V // R)),
        pl.BlockSpec((8, 128), prv(S_RWKV_WA // 128)), pl.BlockSpec((8, 256), prv(S_RWKV_G // 256)),
    ] + [full(p) for p in params]
    out = lambda dt: jax.ShapeDtypeStruct((N, R), dt)
    ospec = pl.BlockSpec((tm, R), lambda i: (i, 0))
    return pl.pallas_call(
        functools.partial(_rwkv_prep_kernel, spb=spb),
        out_shape=(out(BF16), out(BF16), out(BF16), out(BF16), out(BF16), out(F32), out(BF16)),
        grid=(N // tm,),
        in_specs=in_specs,
        out_specs=(ospec,) * 7,
        compiler_params=_cparams(("parallel",)),
        name="rwkv_prep",
    )(U, U, U, Us, Us, U, U, U, Us, Us, *params)


RWKV_TC = 512
RWKV_W = RWKV_HB * RWKV_N


def _rwkv_masks():
    W, C = RWKV_W, CHUNK
    i = np.arange(W)
    same = i[:, None] // C == i[None, :] // C
    m_bd = same.astype(np.float32)
    low_s = (same & (i[:, None] % C > i[None, :] % C)).astype(np.float32)
    low_i = (same & (i[:, None] % C >= i[None, :] % C)).astype(np.float32)
    tril = np.tril(np.ones((C, C), np.float32))
    return (jnp.asarray(m_bd), jnp.asarray(low_s), jnp.asarray(low_i), jnp.asarray(tril),
            jnp.asarray(np.eye(W, dtype=np.float32)))


def _tile4(x):
    return jnp.concatenate([x] * RWKV_HB, axis=0)


def _rwkv_chunk_kernel(r_ref, k_ref, v_ref, kap_ref, b_ref, lw_ref, g_ref,
                       mbd_ref, lows_ref, lowi_ref, tril_ref, eye_ref, bd_ref,
                       rk_ref, lnw_ref, lnb_ref, o_ref, st_ref):
    @pl.when(pl.program_id(1) == 0)
    def _():
        st_ref[...] = jnp.zeros_like(st_ref)

    C, W = CHUNK, RWKV_W
    n_batch = r_ref.shape[0]
    n_groups = r_ref.shape[2] // W
    m_bd = mbd_ref[...]
    low_s = lows_ref[...]
    low_i = lowi_ref[...]
    tril = tril_ref[...]
    eye = eye_ref[...]
    bd = bd_ref[...]

    def expand(x):
        return (_tile4(x) * m_bd).astype(BF16)

    def body(c, carry):
        sl = pl.ds(pl.multiple_of(c * C, C), C)
        lw_alls = [lw_ref[bi, sl, :] for bi in range(n_batch)]
        cum_alls = [_dot_sel(tril, lw) for lw in lw_alls]
        chains = [(bi, slice(gi * W, (gi + 1) * W)) for bi in range(n_batch) for gi in range(n_groups)]
        G = range(len(chains))
        lns = [ln for _, ln in chains]
        cums = [cum_alls[bi][:, ln] for bi, ln in chains]
        clasts = [cum[C - 1:C, :] for cum in cums]
        rs = [r_ref[bi, sl, ln].astype(F32) for bi, ln in chains]
        ks = [k_ref[bi, sl, ln].astype(F32) for bi, ln in chains]
        vs = [v_ref[bi, sl, ln].astype(F32) for bi, ln in chains]
        bs = [b_ref[bi, sl, ln].astype(F32) for bi, ln in chains]
        p_invs = [jnp.exp(-cum) for cum in cums]
        xes = [jnp.concatenate([expand(kap_ref[bi, sl, ln].astype(F32) * jnp.exp(cum - lw_alls[bi][:, ln])),
                                expand(r * jnp.exp(cum))], axis=0)
               for (bi, ln), cum, r in zip(chains, cums, rs)]
        hes = [jnp.concatenate([expand(b * pi), expand(k * pi)], axis=0) for b, k, pi in zip(bs, ks, p_invs)]
        scs = [_dot_nt(xe, he) for xe, he in zip(xes, hes)]
        ams = [sc[:W, :W] * low_s for sc in scs]
        tinvs = [eye - a_m for a_m in ams]
        ambs = [a_m.astype(BF16) for a_m in ams]
        pws = [_dot(ab, ab) for ab in ambs]
        n_lev = int(np.log2(C))
        for lev in range(1, n_lev):
            pwbs = [pw.astype(BF16) for pw in pws]
            if lev < n_lev - 1:
                outs = [_dot(jnp.concatenate([pwb, tinv.astype(BF16)], axis=0), pwb)
                        for pwb, tinv in zip(pwbs, tinvs)]
                pws = [o[:W] for o in outs]
                tinvs = [tinv + o[W:] for tinv, o in zip(tinvs, outs)]
            else:
                tinvs = [tinv + _dot(tinv.astype(BF16), pwb) for tinv, pwb in zip(tinvs, pwbs)]
        ves = [expand(v) for v in vs]
        bqv = [_dot(jnp.concatenate([(sc[:W, W:] * low_s).astype(BF16), (sc[W:, W:] * low_i).astype(BF16)], axis=0),
                    ve) for sc, ve in zip(scs, ves)]
        bmv = [o[:W] for o in bqv]
        qkv = [o[W:] for o in bqv]
        sts = [st_ref[gi] for gi in G]
        xss = [_dot_nt(xe, st.astype(BF16)) for xe, st in zip(xes, sts)]
        us = [_dot(tinv.astype(BF16), (xs[:W] + bv).astype(BF16)) for tinv, xs, bv in zip(tinvs, xss, bmv)]
        kbs = [jnp.concatenate([expand(k * jnp.exp(cl - cum)), expand(b * jnp.exp(cl - cum))], axis=0)
               for k, b, cl, cum in zip(ks, bs, clasts, cums)]
        for gi in G:
            vu = jnp.concatenate([ves[gi], (-us[gi]).astype(BF16)], axis=0)
            st_ref[gi] = sts[gi] * jnp.exp(clasts[gi]) + _dot_tn(vu, kbs[gi]) * m_bd
        for gi in G:
            ln = lns[gi]
            qb = (scs[gi][W:, :W] * low_i).astype(BF16)
            y_e = xss[gi][W:] + qkv[gi] - _dot(qb, us[gi].astype(BF16))
            y = y_e[0:C] + y_e[C:2 * C] + y_e[2 * C:3 * C] + y_e[3 * C:4 * C]
            mu = _seg_sum(y, bd) * (1.0 / RWKV_N)
            yc = y - mu
            var = _seg_sum(yc * yc, bd) * (1.0 / RWKV_N)
            yn = yc * lax.rsqrt(var + RWKV_LN_EPS) * lnw_ref[:, ln] + lnb_ref[:, ln]
            bonus = _seg_sum(rs[gi] * ks[gi] * rk_ref[:, ln], bd) * vs[gi]
            bi = chains[gi][0]
            o_ref[bi, sl, ln] = ((yn + bonus) * g_ref[bi, sl, ln].astype(F32)).astype(o_ref.dtype)
        return carry

    lax.fori_loop(0, RWKV_TC // C, body, 0)


RWKV_NB = 1


def rwkv_chunk(r, k, v, kap, b, lw, gate, r_k, ln_w, ln_b, *, batch, seq):
    N, R = r.shape
    nt = seq // RWKV_TC
    W = RWKV_W
    nb = RWKV_NB if batch % RWKV_NB == 0 else 1
    blk = pl.BlockSpec((nb, RWKV_TC, R), lambda bb, i: (bb, i, 0))
    masks = _rwkv_masks() + (_seg_ones(),)
    full = lambda a: pl.BlockSpec(a.shape, lambda bb, i: (0, 0))
    pvec = pl.BlockSpec((1, R), lambda bb, i: (0, 0))
    seqs = [a.reshape(batch, seq, R) for a in (r, k, v, kap, b, lw, gate)]
    out = pl.pallas_call(
        _rwkv_chunk_kernel,
        out_shape=jax.ShapeDtypeStruct((batch, seq, R), BF16),
        grid=(batch // nb, nt),
        in_specs=[blk] * 7 + [full(m) for m in masks] + [pvec] * 3,
        out_specs=blk,
        scratch_shapes=[pltpu.VMEM((nb * (R // W), W, W), F32)],
        compiler_params=_cparams(("parallel", "arbitrary")),
        name="rwkv_chunk",
    )(*seqs, *masks, r_k.reshape(1, R), ln_w.reshape(1, R), ln_b.reshape(1, R))
    return out.reshape(N, R)


def _gelu_tanh(x):
    return 0.5 * x * (1.0 + jnp.tanh(np.sqrt(2.0 / np.pi) * (x + 0.044715 * (x * x * x))))


def _nsa_compress_kernel(x_ref, pos_ref, w1_ref, w2_ref, o_ref):
    x = x_ref[0, 0]
    w1 = w1_ref[0]
    half = w1.shape[0] // 2
    nrow = x.shape[0]
    ha = _dot(x, w1[:half])
    hb = _dot(x, w1[half:])
    h = ha + pltpu.roll(hb, nrow - 1, 0)
    pb = _dot(pos_ref[0], w1)
    h = _gelu_tanh(h + pb[0:1, :])
    o_ref[0, 0, 0] = _dot(h.astype(BF16), w2_ref[0]).astype(o_ref.dtype)


def nsa_compress(x, pos, w1, w2):
    two = 2
    B, G, nr = x.shape[0] // two, x.shape[1], x.shape[2]
    dh = x.shape[3] // NSA_CMP_STRIDE
    posf = jnp.broadcast_to(pos.reshape(two, 1, NSA_CMP_LEN * dh), (two, 8, NSA_CMP_LEN * dh)).astype(BF16)
    return pl.pallas_call(
        _nsa_compress_kernel,
        out_shape=jax.ShapeDtypeStruct((two, B, G, nr, dh), BF16),
        grid=(two, B, G),
        in_specs=[
            pl.BlockSpec((1, 1, nr, NSA_CMP_STRIDE * dh), lambda s, b, g: (s * B + b, g, 0, 0)),
            pl.BlockSpec((1, 8, NSA_CMP_LEN * dh), lambda s, b, g: (s, 0, 0)),
            pl.BlockSpec((1, NSA_CMP_LEN * dh, NSA_CMP_HIDDEN), lambda s, b, g: (s, 0, 0)),
            pl.BlockSpec((1, NSA_CMP_HIDDEN, dh), lambda s, b, g: (s, 0, 0)),
        ],
        out_specs=pl.BlockSpec((1, 1, 1, nr, dh), lambda s, b, g: (s, b, g, 0, 0)),
        compiler_params=_cparams(("parallel", "parallel", "parallel")),
        name="nsa_compress",
    )(x, posf, w1.astype(BF16), w2.astype(BF16))


def _nsa_cmp_kernel(q_ref, kc_ref, vct_ref, ov_ref, cb_ref, oc_ref, sb_ref, qs_ref, *, n_slc, n_sel):
    qi = pl.program_id(2)
    QB = CMP_QB
    ncp = kc_ref.shape[2]
    for h in range(NSA_HPG):
        qs_ref[h * QB:(h + 1) * QB, :] = q_ref[:, h * NSA_DH:(h + 1) * NSA_DH] * (NSA_DH ** -0.5)
    s = _dot_nt(kc_ref[0, 0], qs_ref[...])
    vct = vct_ref[0, 0]
    psum = jnp.zeros((ncp, QB), F32)
    for h in range(NSA_HPG):
        sh = s[:, h * QB:(h + 1) * QB] + cb_ref[h]
        m = jnp.maximum(jnp.max(sh, axis=0, keepdims=True), -1e20)
        e = jnp.exp(sh - m)
        acc = _dot(vct, e.astype(BF16))
        inv_l = 1.0 / jnp.maximum(acc[NSA_DH:NSA_DH + 1], 1e-30)
        oc_ref[:, h * NSA_DH:(h + 1) * NSA_DH] = (acc[0:NSA_DH] * inv_l).T.astype(oc_ref.dtype)
        psum = psum + e * inv_l
    imp = _dot_sel(ov_ref[...], psum)
    j = lax.broadcasted_iota(jnp.int32, (n_slc, QB), 0)
    tt = qi * QB + lax.broadcasted_iota(jnp.int32, (n_slc, QB), 1)
    cur = jnp.right_shift(tt, 6)
    forced = (j == 0) | (j == cur) | (j == cur - 1)
    cand = (j >= 1) & (j <= cur - 2)
    n_free = n_sel - 3

    def emit(sel):
        sb_ref[0, 0] = jnp.where(sel, 0.0, -SEL_MASK_BIAS).T.astype(sb_ref.dtype)

    last_cur = (qi * QB + QB - 1) // NSA_SLC_LEN

    @pl.when(last_cur - 2 <= n_free)
    def _():
        emit(forced | cand)

    @pl.when(last_cur - 2 > n_free)
    def _():
        cur_row = cur[0:1, :]
        rank = jnp.zeros((n_slc, QB), jnp.int32)
        for jp in range(1, n_slc):
            row = imp[jp:jp + 1, :]
            ahead = (row > imp) | ((row == imp) & (j > jp))
            rank = rank + jnp.where(ahead & (jp <= cur_row - 2), 1, 0)
        emit(forced | (cand & (rank < n_free)))


def nsa_cmp(U, kc, vct, slopes, *, batch, seq):
    N = U.shape[0]
    QB = CMP_QB
    nq = seq // QB
    ncp = kc.shape[2]
    n_slc = seq // NSA_SLC_LEN
    n_sel = min(NSA_N_SEL, n_slc)
    nn = np.arange(ncp)
    jj = np.arange(n_slc)
    ov = ((nn[None, :] * NSA_CMP_STRIDE + NSA_CMP_LEN - 1 >= jj[:, None] * NSA_SLC_LEN)
          & (nn[None, :] * NSA_CMP_STRIDE <= jj[:, None] * NSA_SLC_LEN + NSA_SLC_LEN - 1)
          & (nn[None, :] < ncp - 1)).astype(np.float32)
    tt = np.arange(seq)
    ended = jnp.asarray(nn[:, None] * NSA_CMP_STRIDE + NSA_CMP_LEN - 1 <= tt[None, :])
    adist = np.abs(tt[None, :] - (nn[:, None] * NSA_CMP_STRIDE + (NSA_CMP_LEN - 1) / 2.0)).astype(np.float32)
    cbias = jnp.where(ended[None], -slopes[:, None, None] * jnp.asarray(adist)[None], NEG_BIG)
    G = NSA_GROUPS
    W = NSA_HPG * NSA_DH
    return pl.pallas_call(
        functools.partial(_nsa_cmp_kernel, n_slc=n_slc, n_sel=n_sel),
        out_shape=(jax.ShapeDtypeStruct((N, NSA_HEADS * NSA_DH), BF16),
                   jax.ShapeDtypeStruct((batch, G, seq, n_slc), BF16)),
        grid=(batch, G, nq),
        in_specs=[
            pl.BlockSpec((QB, W), lambda b, g, i: (b * nq + i, U_NSA_Q // W + g)),
            pl.BlockSpec((1, 1, ncp, NSA_DH), lambda b, g, i: (b, g, 0, 0)),
            pl.BlockSpec((1, 1, NSA_DH + 8, ncp), lambda b, g, i: (b, g, 0, 0)),
            pl.BlockSpec((n_slc, ncp), lambda b, g, i: (0, 0)),
            pl.BlockSpec((NSA_HPG, ncp, QB), lambda b, g, i: (g, 0, i)),
        ],
        out_specs=(
            pl.BlockSpec((QB, W), lambda b, g, i: (b * nq + i, g)),
            pl.BlockSpec((1, 1, QB, n_slc), lambda b, g, i: (b, g, i, 0)),
        ),
        scratch_shapes=[pltpu.VMEM((NSA_HPG * QB, NSA_DH), BF16)],
        compiler_params=_cparams(("parallel", "parallel", "arbitrary")),
        name="nsa_cmp",
    )(U, kc, vct, jnp.asarray(ov), cbias)


def _nsa_win_kernel(*refs):
    nb = NSA_WINDOW // NSA_QB + 1
    q_ref, k_refs, v_refs = refs[0], refs[1:1 + nb], refs[1 + nb:1 + 2 * nb]
    bias_ref, o_ref, qs_ref = refs[1 + 2 * nb:]
    qi = pl.program_id(2)
    QB = NSA_QB
    KW = NSA_WINDOW + QB
    for h in range(NSA_HPG):
        qs_ref[h * QB:(h + 1) * QB, :] = q_ref[:, h * NSA_DH:(h + 1) * NSA_DH] * (NSA_DH ** -0.5)
    k = jnp.concatenate([r[0, 0] for r in k_refs], axis=0)
    vt = jnp.concatenate([r[0, 0] for r in v_refs], axis=1)
    s = _dot_nt(k, qs_ref[...])

    def finish(before_start):
        for h in range(NSA_HPG):
            cols = slice(h * QB, (h + 1) * QB)
            sh = s[:, cols] + bias_ref[h]
            if before_start:
                r = lax.broadcasted_iota(jnp.int32, (KW, QB), 0)
                sh = jnp.where(r >= NSA_WINDOW - qi * QB, sh, NEG_BIG)
            m = jnp.max(sh, axis=0, keepdims=True)
            e = jnp.exp(sh - m)
            acc = _dot(vt, e.astype(BF16))
            o = acc[0:NSA_DH] / acc[NSA_DH:NSA_DH + 1]
            o_ref[:, h * NSA_DH:(h + 1) * NSA_DH] = o.T.astype(o_ref.dtype)

    @pl.when(qi * QB < NSA_WINDOW)
    def _():
        finish(True)

    @pl.when(qi * QB >= NSA_WINDOW)
    def _():
        finish(False)


def nsa_win(U, kw_pad, vtw_pad, slopes, *, batch, seq):
    N = U.shape[0]
    QB = NSA_QB
    KW = NSA_WINDOW + QB
    nq = seq // QB
    G = NSA_GROUPS
    W = NSA_HPG * NSA_DH
    dist = (np.arange(QB)[None, :] - np.arange(KW)[:, None] + NSA_WINDOW).astype(np.float32)
    inside = jnp.asarray((dist >= 0) & (dist < NSA_WINDOW))
    bias = jnp.where(inside[None], -slopes[:, None, None] * jnp.asarray(dist)[None], NEG_BIG)
    nb = NSA_WINDOW // QB + 1
    kb = lambda off: pl.BlockSpec((1, 1, QB, NSA_DH), lambda b, g, i: (b, g, i + off, 0))
    vb = lambda off: pl.BlockSpec((1, 1, NSA_DH + 8, QB), lambda b, g, i: (b, g, 0, i + off))
    return pl.pallas_call(
        _nsa_win_kernel,
        out_shape=jax.ShapeDtypeStruct((N, NSA_HEADS * NSA_DH), BF16),
        grid=(batch, G, nq),
        in_specs=[pl.BlockSpec((QB, W), lambda b, g, i: (b * nq + i, U_NSA_Q // W + g)),
                  *[kb(o) for o in range(nb)], *[vb(o) for o in range(nb)],
                  pl.BlockSpec((NSA_HPG, KW, QB), lambda b, g, i: (g, 0, 0))],
        out_specs=pl.BlockSpec((QB, W), lambda b, g, i: (b * nq + i, g)),
        scratch_shapes=[pltpu.VMEM((NSA_HPG * QB, NSA_DH), BF16)],
        compiler_params=_cparams(("parallel", "parallel", "arbitrary")),
        name="nsa_win",
    )(U, *([kw_pad] * nb), *([vtw_pad] * nb), bias)


SEL_QB = 512
SEL_KV = 512


def _sel_pairs(seq):
    qs, ks = [], []
    for qi in range(seq // SEL_QB):
        for kj in range((qi * SEL_QB) // SEL_KV + 1):
            qs.append(qi)
            ks.append(kj)
    return np.asarray(qs, np.int32), np.asarray(ks, np.int32)


def _nsa_sel_kernel(qi_ref, kj_ref, slope_ref, q_ref, sb_ref, qx_ref, ka_ref, vt_ref, o_ref,
                    qa_ref, m_ref, acc_ref):
    g = pl.program_id(1)
    p = pl.program_id(2)
    qi = qi_ref[p]
    kj = kj_ref[p]
    QB, KV = SEL_QB, SEL_KV
    nblk = sb_ref.shape[3]
    base = NSA_DH + nblk

    @pl.when(kj == 0)
    def _():
        for h in range(NSA_HPG):
            rows = slice(h * QB, (h + 1) * QB)
            qa_ref[rows, 0:NSA_DH] = q_ref[:, h * NSA_DH:(h + 1) * NSA_DH] * (NSA_DH ** -0.5)
            qa_ref[rows, NSA_DH:base] = sb_ref[0, 0]
            qa_ref[rows, base:] = jnp.broadcast_to(qx_ref[0, h:h + 1, :], (QB, qa_ref.shape[1] - base))
        m_ref[...] = jnp.full_like(m_ref, NEG_BIG)
        acc_ref[...] = jnp.zeros_like(acc_ref)

    off = qi * QB - kj * KV
    tile_start = (kj * KV).astype(F32)

    def step(masked):
        s = _dot_nt(ka_ref[0, 0], qa_ref[...])
        vt = vt_ref[0, 0]
        if masked:
            r = lax.broadcasted_iota(jnp.int32, (KV, QB), 0)
            c = lax.broadcasted_iota(jnp.int32, (KV, QB), 1)
            causal = (c - r + off) >= 0
        for h in range(NSA_HPG):
            cols = slice(h * QB, (h + 1) * QB)
            sh = s[:, cols]
            if masked:
                sh = jnp.where(causal, sh, NEG_BIG)
            delta = slope_ref[g * NSA_HPG + h] * tile_start
            m_old = m_ref[h:h + 1, :]
            m_new = jnp.maximum(m_old, jnp.max(sh, axis=0, keepdims=True) + delta)
            e = jnp.exp(sh - (m_new - delta))
            alpha = jnp.exp(m_old - m_new)
            acc_ref[:, cols] = alpha * acc_ref[:, cols] + _dot(vt, e.astype(BF16))
            m_ref[h:h + 1, :] = m_new

    last = (qi * QB) // KV

    @pl.when(kj < last)
    def _():
        step(False)

    @pl.when(kj == last)
    def _():
        step(True)
        for h in range(NSA_HPG):
            cols = slice(h * QB, (h + 1) * QB)
            o = acc_ref[0:NSA_DH, cols] / acc_ref[NSA_DH:NSA_DH + 1, cols]
            o_ref[:, h * NSA_DH:(h + 1) * NSA_DH] = o.T.astype(o_ref.dtype)


def sel_key_columns(seq):
    n_slc = seq // NSA_SLC_LEN
    pos = np.arange(seq)
    onehot = (pos[:, None] // NSA_SLC_LEN == np.arange(n_slc)[None, :]).astype(np.float32)
    r = pos % SEL_KV
    extra = np.zeros((seq, 128), np.float32)
    extra[:, 0] = extra[:, 2] = (r // 32) * 32
    extra[:, 1] = extra[:, 3] = r % 32
    return jnp.asarray(np.concatenate([onehot, extra], axis=1), BF16)


def nsa_sel(U, selbias, k_aug, vt_slc, slopes, *, batch, seq):
    N = U.shape[0]
    QB, KV = SEL_QB, SEL_KV
    nq = seq // QB
    G = NSA_GROUPS
    W = NSA_HPG * NSA_DH
    n_slc = selbias.shape[3]
    ka_w = k_aug.shape[3]
    qs, ks = _sel_pairs(seq)
    s_hi = slopes.astype(BF16)
    s_lo = (slopes - s_hi.astype(F32)).astype(BF16)
    qx = jnp.zeros((NSA_HEADS, 128), BF16).at[:, 0].set(s_hi).at[:, 1].set(s_hi).at[:, 2].set(s_lo).at[:, 3].set(s_lo)
    qx = jnp.pad(qx.reshape(G, NSA_HPG, 128), ((0, 0), (0, 8 - NSA_HPG), (0, 0)))
    grid_spec = pltpu.PrefetchScalarGridSpec(
        num_scalar_prefetch=3,
        grid=(batch, G, len(qs)),
        in_specs=[
            pl.BlockSpec((QB, W), lambda b, g, p, qi, kj, s: (b * nq + qi[p], U_NSA_Q // W + g)),
            pl.BlockSpec((1, 1, QB, n_slc), lambda b, g, p, qi, kj, s: (b, g, qi[p], 0)),
            pl.BlockSpec((1, 8, 128), lambda b, g, p, qi, kj, s: (g, 0, 0)),
            pl.BlockSpec((1, 1, KV, ka_w), lambda b, g, p, qi, kj, s: (b, g, kj[p], 0)),
            pl.BlockSpec((1, 1, NSA_DH + 8, KV), lambda b, g, p, qi, kj, s: (b, g, 0, kj[p])),
        ],
        out_specs=pl.BlockSpec((QB, W), lambda b, g, p, qi, kj, s: (b * nq + qi[p], g)),
        scratch_shapes=[
            pltpu.VMEM((NSA_HPG * QB, ka_w), BF16),
            pltpu.VMEM((NSA_HPG, QB), F32),
            pltpu.VMEM((NSA_DH + 8, NSA_HPG * QB), F32),
        ],
    )
    return pl.pallas_call(
        _nsa_sel_kernel,
        out_shape=jax.ShapeDtypeStruct((N, NSA_HEADS * NSA_DH), BF16),
        grid_spec=grid_spec,
        compiler_params=_cparams(("parallel", "parallel", "arbitrary")),
        name="nsa_sel",
    )(jnp.asarray(qs), jnp.asarray(ks), slopes, U, selbias, qx, k_aug, vt_slc)


def _nsa_combine_kernel(oc_ref, os_ref, ow_ref, gate_ref, e_ref, o_ref):
    ge = _dot_split(_sigmoid(gate_ref[...]), e_ref[...])
    Wd = NSA_HEADS * NSA_DH
    o = (ge[:, :Wd] * oc_ref[...].astype(F32) + ge[:, Wd:2 * Wd] * os_ref[...].astype(F32)
         + ge[:, 2 * Wd:] * ow_ref[...].astype(F32))
    o_ref[...] = o.astype(o_ref.dtype)


def nsa_combine(o_c, o_s, o_w, Us):
    N, Wd = o_c.shape
    tm = 512
    e = np.zeros((128, 3 * Wd), np.float32)
    for h in range(NSA_HEADS):
        for j in range(3):
            e[h * 3 + j, j * Wd + h * NSA_DH:j * Wd + (h + 1) * NSA_DH] = 1.0
    blk = pl.BlockSpec((tm, Wd), lambda i: (i, 0))
    return pl.pallas_call(
        _nsa_combine_kernel,
        out_shape=jax.ShapeDtypeStruct((N, Wd), BF16),
        grid=(N // tm,),
        in_specs=[blk, blk, blk, pl.BlockSpec((tm, 128), lambda i: (i, S_NSA_GATE // 128)),
                  pl.BlockSpec((128, 3 * Wd), lambda i: (0, 0))],
        out_specs=blk,
        compiler_params=_cparams(("parallel",)),
        name="nsa_combine",
    )(o_c, o_s, o_w, Us, jnp.asarray(e, BF16))


def _merge_kernel(ya_ref, yb_ref, yc_ref, ga_ref, gb_ref, gc_ref, p_ref, o_ref):
    m = (_sigmoid(ga_ref[...].astype(F32)) * _dot(ya_ref[...], p_ref[0])
         + _sigmoid(gb_ref[...].astype(F32)) * _dot(yb_ref[...], p_ref[1])
         + _sigmoid(gc_ref[...].astype(F32)) * _dot(yc_ref[...], p_ref[2]))
    o_ref[...] = m.astype(o_ref.dtype)


def merge(y_a, y_b, y_c, U, p_merge):
    N = y_a.shape[0]
    tm, tn = 1024, 512
    yb = pl.BlockSpec((tm, BRANCH_DIM), lambda i, j: (i, 0))
    gate = lambda br: pl.BlockSpec((tm, tn), lambda i, j: (i, (U_GATE + br * D_MODEL) // tn + j))
    return pl.pallas_call(
        _merge_kernel,
        out_shape=jax.ShapeDtypeStruct((N, D_MODEL), BF16),
        grid=(N // tm, D_MODEL // tn),
        in_specs=[yb, yb, yb, gate(0), gate(1), gate(2),
                  pl.BlockSpec((3, BRANCH_DIM, tn), lambda i, j: (0, 0, j))],
        out_specs=pl.BlockSpec((tm, tn), lambda i, j: (i, j)),
        compiler_params=_cparams(("parallel", "parallel")),
        name="merge",
    )(y_a, y_b, y_c, U, U, U, p_merge)


def _ffn_kernel(h_ref, w1_ref, w3_ref, w2_ref, o_ref, acc_ref):
    f = pl.program_id(1)

    @pl.when(f == 0)
    def _():
        acc_ref[...] = jnp.zeros_like(acc_ref)

    h = h_ref[...]
    a = _dot(h, w1_ref[...])
    z = (a * _sigmoid(a) * _dot(h, w3_ref[...])).astype(BF16)
    acc_ref[...] += _dot(z, w2_ref[...])

    @pl.when(f == pl.num_programs(1) - 1)
    def _():
        o_ref[...] = acc_ref[...].astype(o_ref.dtype)


def ffn(h, w1, w3, w2):
    N, D = h.shape
    F = w1.shape[1]
    tm, tf = 1024, 512
    return pl.pallas_call(
        _ffn_kernel,
        out_shape=jax.ShapeDtypeStruct((N, D), BF16),
        grid=(N // tm, F // tf),
        in_specs=[pl.BlockSpec((tm, D), lambda i, f: (i, 0)),
                  pl.BlockSpec((D, tf), lambda i, f: (0, f)),
                  pl.BlockSpec((D, tf), lambda i, f: (0, f)),
                  pl.BlockSpec((tf, D), lambda i, f: (f, 0))],
        out_specs=pl.BlockSpec((tm, D), lambda i, f: (i, 0)),
        scratch_shapes=[pltpu.VMEM((tm, D), F32)],
        compiler_params=_cparams(("parallel", "arbitrary")),
        name="ffn",
    )(h, w1, w3, w2)


def _route_kernel(lg_ref, cb_ref, sel_ref):
    lg = lg_ref[...]
    lane = lax.broadcasted_iota(jnp.int32, lg.shape, 1)
    x = jnp.where(lane < N_EXPERTS, lg, NEG_BIG)
    v1 = jnp.max(x, axis=-1, keepdims=True)
    i1 = jnp.min(jnp.where(x == v1, lane, 1024), axis=-1, keepdims=True)
    x2 = jnp.where(lane == i1, NEG_BIG, x)
    v2 = jnp.max(x2, axis=-1, keepdims=True)
    i2 = jnp.min(jnp.where(x2 == v2, lane, 1024), axis=-1, keepdims=True)
    e2 = jnp.exp(v2 - v1)
    w1 = 1.0 / (1.0 + e2)
    w2 = e2 / (1.0 + e2)
    cb_ref[...] = jnp.where(lane == i1, w1, 0.0) + jnp.where(lane == i2, w2, 0.0)
    sel_ref[...] = jnp.where((lane == i1) | (lane == i2), 1.0, 0.0)


def route(logits):
    N = logits.shape[0]
    tm = 1024
    blk = pl.BlockSpec((tm, 128), lambda i: (i, 0))
    return pl.pallas_call(
        _route_kernel,
        out_shape=(jax.ShapeDtypeStruct((N, 128), F32), jax.ShapeDtypeStruct((N, 128), F32)),
        grid=(N // tm,),
        in_specs=[blk],
        out_specs=(blk, blk),
        compiler_params=_cparams(("parallel",)),
        name="route",
    )(logits)


MOE_TM = 1024
MOE_TF = 512
MOE_WB = 512
MOE_GT = 256


def _moe_plan(sel):
    N, E = sel.shape
    tm, wb, gt = MOE_TM, MOE_WB, MOE_GT
    sub = tm // gt
    NT = 2 * N // tm + E
    NG = NT * sub
    R = NT * tm
    P = NG + E * (N // wb)
    i32 = jnp.int32
    cs = jnp.cumsum(sel.astype(i32), axis=0)
    cnt = cs[-1]
    tiles_e = (cnt + tm - 1) // tm
    tile_end = jnp.cumsum(tiles_e)
    tile_start = tile_end - tiles_e
    total_tiles = tile_end[-1]
    dest = jnp.where(sel, tile_start[None, :] * tm + cs - 1, R)
    ti = jnp.arange(NT, dtype=i32)
    tile_valid = ti < total_tiles
    count_le = lambda ends, q: jnp.sum((ends[None, :] <= q[:, None]).astype(i32), axis=1)
    tile_e = jnp.minimum(count_le(tile_end, ti), E - 1)
    gi = jnp.arange(NG, dtype=i32)
    g_valid = tile_valid[gi // sub]
    g_e = tile_e[gi // sub]
    k_lo = gi * gt - tile_start[g_e] * tm
    nonempty = g_valid & (k_lo < cnt[g_e])
    k_hi = jnp.minimum(k_lo + gt, cnt[g_e]) - 1
    cb = cs[wb - 1::wb, :].T[g_e]
    blk_of = lambda k: jnp.sum((cb <= k[:, None]).astype(i32), axis=1)
    blo = jnp.where(nonempty, blk_of(k_lo), 0)
    bhi = jnp.where(nonempty, blk_of(k_hi), jnp.where(g_valid, 0, -1))
    npair = bhi - blo + 1
    pend = jnp.cumsum(npair)
    pstart = pend - npair
    total_p = pend[-1]
    pi = jnp.arange(P, dtype=i32)
    pvalid = pi < total_p
    ptile = jnp.minimum(count_le(pend, pi), NG - 1)
    pblk = blo[ptile] + pi - pstart[ptile]
    pfirst = pvalid & (pi == pstart[ptile])
    ptile = jnp.where(pvalid, ptile, ptile[total_p - 1])
    pblk = jnp.where(pvalid, pblk, pblk[total_p - 1])
    pexp = g_e[ptile]
    order = jnp.argsort(jnp.where(pvalid, pblk * NG + ptile, jnp.iinfo(jnp.int32).max))
    s_valid = pvalid
    s_tile = jnp.where(s_valid, ptile[order], ptile[order][total_p - 1])
    s_blk = jnp.where(s_valid, pblk[order], pblk[order][total_p - 1])
    s_exp = g_e[s_tile]
    s_first = s_valid & ((pi == 0) | (s_blk != jnp.roll(s_blk, 1)))
    e_src = jnp.where(tile_valid, ti, 0)
    e_exp = jnp.where(tile_valid, tile_e, tile_e[jnp.maximum(total_tiles - 1, 0)])
    b2i = lambda x: x.astype(i32)
    return dict(dest=dest, g=(ptile, pblk, pexp, b2i(pvalid), b2i(pfirst)),
                e=(e_exp, b2i(tile_valid), e_src), s=(s_tile, s_blk, s_exp, b2i(s_valid), b2i(s_first)), NT=NT, P=P)


def _moe_group_kernel(pt_ref, pb_ref, pe_ref, pv_ref, pf_ref, dest_ref, h_ref, o_ref):
    p = pl.program_id(0)

    @pl.when(pv_ref[p] == 1)
    def _():
        rel = dest_ref[0] - pt_ref[p] * MOE_GT
        row = lax.broadcasted_iota(jnp.int32, (MOE_GT, MOE_WB), 0)
        onehot = jnp.where(rel == row, 1.0, 0.0).astype(BF16)
        rows = _dot(onehot, h_ref[...]).astype(o_ref.dtype)

        @pl.when(pf_ref[p] == 1)
        def _():
            o_ref[...] = rows

        @pl.when(pf_ref[p] == 0)
        def _():
            o_ref[...] += rows


def _moe_expert_kernel(te_ref, tv_ref, ts_ref, h_ref, w1_ref, w3_ref, w2_ref, o_ref, acc_ref):
    i = pl.program_id(0)
    f = pl.program_id(1)
    nf = pl.num_programs(1)

    @pl.when(tv_ref[i] == 1)
    def _():
        @pl.when(f == 0)
        def _():
            acc_ref[...] = jnp.zeros_like(acc_ref)

        h = h_ref[...]
        a = _dot(h, w1_ref[0, 0].astype(BF16))
        z = (a * _sigmoid(a) * _dot(h, w3_ref[0, 0].astype(BF16))).astype(BF16)
        acc_ref[...] += _dot(z, w2_ref[0, 0].astype(BF16))

        @pl.when(f == nf - 1)
        def _():
            o_ref[...] = acc_ref[...].astype(o_ref.dtype)

    @pl.when((tv_ref[i] == 0) & (f == nf - 1))
    def _():
        o_ref[...] = jnp.zeros_like(o_ref)


def _moe_ungroup_kernel(st_ref, sb_ref, se_ref, sv_ref, sf_ref, dest_ref, w_ref, y_ref, o_ref):
    p = pl.program_id(0)

    @pl.when(sv_ref[p] == 1)
    def _():
        rel = dest_ref[0] - st_ref[p] * MOE_GT
        col = lax.broadcasted_iota(jnp.int32, (MOE_WB, MOE_GT), 1)
        onehot_t = jnp.where(rel == col, 1.0, 0.0).astype(BF16)
        part = _dot(onehot_t, y_ref[...]) * w_ref[0]

        @pl.when(sf_ref[p] == 1)
        def _():
            o_ref[...] = part

        @pl.when(sf_ref[p] == 0)
        def _():
            o_ref[...] += part


def moe(h, combine, selm, w1, w3, w2, layer):
    N, D = h.shape
    _, E, _, F = w1.shape
    tm, wb, gt, tf = MOE_TM, MOE_WB, MOE_GT, MOE_TF
    plan = _moe_plan(selm[:, :E] > 0.5)
    NT, P = plan["NT"], plan["P"]
    R = NT * tm
    dest_t = plan["dest"].T
    h_sorted = pl.pallas_call(
        _moe_group_kernel,
        out_shape=jax.ShapeDtypeStruct((R, D), BF16),
        grid_spec=pltpu.PrefetchScalarGridSpec(
            num_scalar_prefetch=5,
            grid=(P,),
            in_specs=[pl.BlockSpec((1, 1, wb), lambda p, pt, pb, pe, pv, pf: (pe[p], 0, pb[p])),
                      pl.BlockSpec((wb, D), lambda p, pt, pb, pe, pv, pf: (pb[p], 0))],
            out_specs=pl.BlockSpec((gt, D), lambda p, pt, pb, pe, pv, pf: (pt[p], 0)),
        ),
        compiler_params=_cparams(("arbitrary",)),
        name="moe_group",
    )(*plan["g"], dest_t.reshape(E, 1, N), h)
    y_sorted = pl.pallas_call(
        _moe_expert_kernel,
        out_shape=jax.ShapeDtypeStruct((R, D), BF16),
        grid_spec=pltpu.PrefetchScalarGridSpec(
            num_scalar_prefetch=3,
            grid=(NT, F // tf),
            in_specs=[pl.BlockSpec((tm, D), lambda i, f, te, tv, ts: (ts[i], 0)),
                      pl.BlockSpec((1, 1, D, tf), lambda i, f, te, tv, ts: (layer, te[i], 0, f * tv[i])),
                      pl.BlockSpec((1, 1, D, tf), lambda i, f, te, tv, ts: (layer, te[i], 0, f * tv[i])),
                      pl.BlockSpec((1, 1, tf, D), lambda i, f, te, tv, ts: (layer, te[i], f * tv[i], 0))],
            out_specs=pl.BlockSpec((tm, D), lambda i, f, te, tv, ts: (i, 0)),
            scratch_shapes=[pltpu.VMEM((tm, D), F32)],
        ),
        compiler_params=_cparams(("arbitrary", "arbitrary")),
        name="moe_expert",
    )(*plan["e"], h_sorted, w1, w3, w2)
    return pl.pallas_call(
        _moe_ungroup_kernel,
        out_shape=jax.ShapeDtypeStruct((N, D), F32),
        grid_spec=pltpu.PrefetchScalarGridSpec(
            num_scalar_prefetch=5,
            grid=(P,),
            in_specs=[pl.BlockSpec((1, wb, 1), lambda p, st, sb, se, sv, sf: (se[p], sb[p], 0)),
                      pl.BlockSpec((1, wb, 1), lambda p, st, sb, se, sv, sf: (se[p], sb[p], 0)),
                      pl.BlockSpec((gt, D), lambda p, st, sb, se, sv, sf: (st[p], 0))],
            out_specs=pl.BlockSpec((wb, D), lambda p, st, sb, se, sv, sf: (sb[p], 0)),
        ),
        compiler_params=_cparams(("arbitrary",)),
        name="moe_ungroup",
    )(*plan["s"], dest_t.reshape(E, N, 1), combine[:, :E].T.reshape(E, N, 1), y_sorted)


def _project_weights(w_in):
    gla_w = 2 * GLA_HEADS * GLA_DK + 2 * GLA_HEADS * GLA_DV + GLA_RANK
    rw_w = 3 * RWKV_DIM + RWKV_W_RANK + RWKV_A_RANK + RWKV_G_RANK
    kvw = NSA_GROUPS * NSA_DH
    nsa_w = NSA_HEADS * NSA_DH + 6 * kvw + NSA_HEADS * 3
    o_rw = gla_w
    o_nsa = gla_w + rw_w
    o_gate = o_nsa + nsa_w
    big = jnp.concatenate([
        w_in[:, 0:gla_w - GLA_RANK],
        w_in[:, o_rw:o_rw + 3 * RWKV_DIM],
        w_in[:, o_nsa:o_nsa + NSA_HEADS * NSA_DH + 6 * kvw],
        w_in[:, o_gate:],
    ], axis=1).astype(BF16)
    D = w_in.shape[0]
    z = lambda n: jnp.zeros((D, n), w_in.dtype)
    small = jnp.concatenate([
        w_in[:, gla_w - GLA_RANK:gla_w], z(128 - GLA_RANK),
        w_in[:, o_rw + 3 * RWKV_DIM:o_rw + rw_w], z(256 - RWKV_G_RANK),
        w_in[:, o_gate - NSA_HEADS * 3:o_gate], z(128 - NSA_HEADS * 3),
    ], axis=1).astype(BF16)
    return big, small


def _mixer(h, batch, seq, w_in, gla_a2, gla_a_b, gla_norm, rwkv_mu, rwkv_w0, rwkv_w2, rwkv_a0, rwkv_a2, rwkv_g2,
           rwkv_k_k, rwkv_k_a, rwkv_r_k, rwkv_ln_w, rwkv_ln_b,
           nsa_pos_k, nsa_w1_k, nsa_w2_k, nsa_pos_v, nsa_w1_v, nsa_w2_v, p_merge, w_out, slopes):
    N = h.shape[0]
    w_big, w_small = _project_weights(w_in)
    tm_u = 2048 if N % 2048 == 0 else N
    U = matmul(h, w_big, BF16, tm_u, 512)
    Us = matmul(h, w_small, F32, tm_u, S_COLS)

    y_a = gla(U, Us, gla_a2, gla_a_b, gla_norm, batch=batch, seq=seq)

    r, k, v, kap, b, lw, gate = rwkv_prep(U, Us, rwkv_mu, rwkv_w0, rwkv_w2, rwkv_a0, rwkv_a2, rwkv_g2,
                                          rwkv_k_k, rwkv_k_a, seq=seq)
    y_b = rwkv_chunk(r, k, v, kap, b, lw, gate, rwkv_r_k.reshape(-1), rwkv_ln_w, rwkv_ln_b, batch=batch, seq=seq)

    kvw = NSA_GROUPS * NSA_DH

    def stream(i):
        return U[:, U_NSA_KV + i * kvw:U_NSA_KV + (i + 1) * kvw].reshape(batch, seq, NSA_GROUPS, NSA_DH)

    nr = seq // NSA_CMP_STRIDE
    x_cmp = U[:, U_NSA_KV:U_NSA_KV + 2 * kvw].reshape(batch, nr, NSA_CMP_STRIDE, 2, NSA_GROUPS, NSA_DH)
    x_cmp = x_cmp.transpose(3, 0, 4, 1, 2, 5).reshape(2 * batch, NSA_GROUPS, nr, NSA_CMP_STRIDE * NSA_DH)
    kv6 = {2: stream(2).transpose(0, 2, 1, 3), 4: stream(4).transpose(0, 2, 1, 3)}
    vt6 = {3: stream(3).transpose(0, 2, 3, 1), 5: stream(5).transpose(0, 2, 3, 1)}
    kvc = nsa_compress(x_cmp, jnp.stack([nsa_pos_k, nsa_pos_v]), jnp.stack([nsa_w1_k, nsa_w1_v]),
                       jnp.stack([nsa_w2_k, nsa_w2_v]))
    ncp = kvc.shape[3]
    ones_c = jnp.zeros((batch, NSA_GROUPS, 8, ncp), BF16).at[:, :, 0, :].set(1.0)
    vct = jnp.concatenate([kvc[1].transpose(0, 1, 3, 2), ones_c], axis=2)
    o_c, selbias = nsa_cmp(U, kvc[0], vct, slopes, batch=batch, seq=seq)
    n_slc = seq // NSA_SLC_LEN
    k_aug = jnp.concatenate([kv6[2], jnp.broadcast_to(sel_key_columns(seq), (batch, NSA_GROUPS, seq, n_slc + 128))],
                            axis=-1)
    ones_rows = jnp.zeros((batch, NSA_GROUPS, 8, seq), BF16).at[:, :, 0, :].set(1.0)
    vt_aug = jnp.concatenate([vt6[3], ones_rows], axis=2)
    o_s = nsa_sel(U, selbias, k_aug, vt_aug, slopes, batch=batch, seq=seq)
    kw_pad = jnp.pad(kv6[4], ((0, 0), (0, 0), (NSA_WINDOW, 0), (0, 0)))
    vtw = jnp.concatenate([vt6[5], ones_rows], axis=2)
    vtw_pad = jnp.pad(vtw, ((0, 0), (0, 0), (0, 0), (NSA_WINDOW, 0)))
    o_w = nsa_win(U, kw_pad, vtw_pad, slopes, batch=batch, seq=seq)
    y_c = nsa_combine(o_c, o_s, o_w, Us)

    merged = merge(y_a, y_b, y_c, U, p_merge.astype(BF16))
    return matmul(merged, w_out.astype(BF16), BF16, 1024 if N % 1024 == 0 else N, 1024)


def kernel(x, c, norm_mix, norm_ffn, ada_w, ada_b, w_in, gla_a2, gla_a_b, gla_norm, rwkv_mu, rwkv_w0, rwkv_w2, rwkv_a0, rwkv_a2, rwkv_g2, rwkv_k_k, rwkv_k_a, rwkv_r_k, rwkv_ln_w, rwkv_ln_b, nsa_pos_k, nsa_w1_k, nsa_w2_k, nsa_pos_v, nsa_w1_v, nsa_w2_v, p_merge, w_out, ffn_w1, ffn_w3, ffn_w2, moe_router, moe_w1, moe_w3, moe_w2, final_norm):
    B, T, D = x.shape
    depth = w_in.shape[0]
    N = B * T
    xs = x.reshape(N, D)
    ada = ada_all(c, ada_w, ada_b)
    slopes = jnp.exp2(-8.0 * jnp.arange(1, NSA_HEADS + 1, dtype=F32) / NSA_HEADS)
    y = None
    g_prev = None
    for l in range(depth):
        sh1, sc1, g1, sh2, sc2, g2 = jnp.split(ada[l], 6, axis=-1)
        if y is None:
            (h,) = resmod(xs, None, None, norm_mix[l], sh1, sc1, seq=T)
        else:
            xs, h = resmod(xs, y, g_prev, norm_mix[l], sh1, sc1, seq=T)
        y = _mixer(h, B, T, w_in[l], gla_a2[l], gla_a_b[l], gla_norm[l], rwkv_mu[l], rwkv_w0[l], rwkv_w2[l],
                   rwkv_a0[l], rwkv_a2[l], rwkv_g2[l], rwkv_k_k[l], rwkv_k_a[l], rwkv_r_k[l],
                   rwkv_ln_w[l], rwkv_ln_b[l], nsa_pos_k[l], nsa_w1_k[l], nsa_w2_k[l],
                   nsa_pos_v[l], nsa_w1_v[l], nsa_w2_v[l], p_merge[l], w_out[l], slopes)
        if l % 2 == 0:
            xs, h = resmod(xs, y, g1, norm_ffn[l], sh2, sc2, seq=T)
            y = ffn(h, ffn_w1[l // 2].astype(BF16), ffn_w3[l // 2].astype(BF16), ffn_w2[l // 2].astype(BF16))
        else:
            rt = jnp.zeros((D, 128), F32).at[:, :N_EXPERTS].set(moe_router[l // 2])
            xs, h, logits = resmod(xs, y, g1, norm_ffn[l], sh2, sc2, seq=T, router=rt)
            combine, selm = route(logits)
            y = moe(h, combine, selm, moe_w1, moe_w3, moe_w2, l // 2)
        g_prev = g2
    (out,) = resmod(xs, y, g_prev, final_norm, None, None, seq=T, final=True)
    return out.reshape(B, T, D)
```

```python
import functools

import numpy as np
import jax
import jax.numpy as jnp
from jax import lax
from jax.experimental import pallas as pl
from jax.experimental.pallas import tpu as pltpu

F32 = jnp.float32
BF16 = jnp.bfloat16
HI = lax.Precision.HIGHEST

V7X_VMEM_LIMIT_BYTES = 56 * 1024 * 1024

D_MODEL = 2048
NORM_EPS = 1e-6
NEG_BIG = -1e30

GLA_HEADS = 4
GLA_DK = 128
GLA_DV = 256
GLA_RANK = 16
GLA_NORMALIZER = 16.0
CHUNK = 64

RWKV_HEADS = 16
RWKV_N = 64
RWKV_DIM = RWKV_HEADS * RWKV_N
RWKV_W_RANK = 64
RWKV_A_RANK = 64
RWKV_G_RANK = 160
RWKV_LN_EPS = 64e-5
RWKV_HB = 4

NSA_HEADS = 16
NSA_GROUPS = 4
NSA_HPG = 4
NSA_DH = 64
NSA_CMP_LEN = 32
NSA_CMP_STRIDE = 16
NSA_CMP_HIDDEN = 128
NSA_SLC_LEN = 64
NSA_N_SEL = 16
NSA_WINDOW = 512
NSA_QB = 256
CMP_QB = 1024
SEL_MASK_BIAS = 131072.0

BRANCH_DIM = 1024
D_FF = 5632
N_EXPERTS = 8

U_GLA_Q, U_GLA_K, U_GLA_V, U_GLA_G = 0, 512, 1024, 2048
U_RWKV_R, U_RWKV_K, U_RWKV_V = 3072, 4096, 5120
U_NSA_Q = 6144
U_NSA_KV = 7168
U_GATE = 8704
U_COLS = U_GATE + 3 * D_MODEL
S_GLA_A = 0
S_RWKV_WA = 128
S_RWKV_G = 256
S_NSA_GATE = 512
S_COLS = 640


def _cparams(sem, vmem=V7X_VMEM_LIMIT_BYTES):
    return pltpu.CompilerParams(dimension_semantics=sem, vmem_limit_bytes=vmem)


def _sigmoid(x):
    return 1.0 / (1.0 + jnp.exp(-x))


def _softplus(x):
    return jnp.maximum(x, 0.0) + jnp.log(1.0 + jnp.exp(-jnp.abs(x)))


def _dot(a, b):
    return jnp.dot(a, b, preferred_element_type=F32)


def _dot_nt(a, b):
    return lax.dot_general(a, b, (((1,), (1,)), ((), ())), preferred_element_type=F32)


def _dot_tn(a, b):
    return lax.dot_general(a, b, (((0,), (0,)), ((), ())), preferred_element_type=F32)


def _dot_hi(a, b):
    return jnp.dot(a, b, preferred_element_type=F32, precision=HI)


def _dot_x3(a, b):
    ah = a.astype(BF16)
    al = (a - ah.astype(F32)).astype(BF16)
    bh = b.astype(BF16)
    bl = (b - bh.astype(F32)).astype(BF16)
    return _dot(ah, bh) + _dot(ah, bl) + _dot(al, bh)


def _dot_sel(w, x):
    wb = w.astype(BF16)
    hi = x.astype(BF16)
    r1 = x - hi.astype(F32)
    mid = r1.astype(BF16)
    lo = (r1 - mid.astype(F32)).astype(BF16)
    return _dot(wb, hi) + _dot(wb, mid) + _dot(wb, lo)


def _dot_split(x, w):
    hi = x.astype(BF16)
    lo = (x - hi.astype(F32)).astype(BF16)
    return _dot(hi, w) + _dot(lo, w)


def _ada_kernel(c_ref, w_ref, b_ref, o_ref):
    c = c_ref[...]
    o_ref[0] = _dot_hi(c * _sigmoid(c), w_ref[0]) + b_ref[0]


def ada_all(c, ada_w, ada_b):
    L, D, N6 = ada_w.shape
    B = c.shape[0]
    tn = 1024
    return pl.pallas_call(
        _ada_kernel,
        out_shape=jax.ShapeDtypeStruct((L, B, N6), F32),
        grid=(L, N6 // tn),
        in_specs=[
            pl.BlockSpec((B, D), lambda l, j: (0, 0)),
            pl.BlockSpec((1, D, tn), lambda l, j: (l, 0, j)),
            pl.BlockSpec((1, 1, tn), lambda l, j: (l, 0, j)),
        ],
        out_specs=pl.BlockSpec((1, B, tn), lambda l, j: (l, 0, j)),
        compiler_params=_cparams(("parallel", "parallel")),
        name="ada",
    )(c, ada_w, ada_b.reshape(L, 1, N6))


def _resmod_kernel(*refs, has_res, final, router):
    it = iter(refs)
    x_ref = next(it)
    y_ref = next(it) if has_res else None
    g_ref = next(it) if has_res else None
    ng_ref = next(it)
    sh_ref = None if final else next(it)
    sc_ref = None if final else next(it)
    rt_ref = next(it) if router else None
    xo_ref = next(it) if (has_res and not final) else None
    h_ref = next(it)
    lg_ref = next(it) if router else None

    x = x_ref[...]
    if has_res:
        x = x + g_ref[0] * y_ref[...].astype(F32)
        if xo_ref is not None:
            xo_ref[...] = x
    ms = jnp.mean(x * x, axis=-1, keepdims=True)
    h = x * lax.rsqrt(ms + NORM_EPS) * ng_ref[...]
    if not final:
        h = h * (1.0 + sc_ref[0]) + sh_ref[0]
    h_ref[...] = h.astype(h_ref.dtype)
    if router:
        lg_ref[...] = _dot_hi(h, rt_ref[...])


def resmod(x, y, g, norm_g, shift, scale, *, seq, router=None, final=False):
    N, D = x.shape
    tm = 512
    spb = seq // tm
    has_res = y is not None
    row = lambda i: (i, 0)
    per_b = lambda i: (i // spb, 0, 0)
    ins, specs = [x], [pl.BlockSpec((tm, D), row)]
    if has_res:
        ins += [y, g.reshape(-1, 1, D)]
        specs += [pl.BlockSpec((tm, D), row), pl.BlockSpec((1, 1, D), per_b)]
    ins.append(norm_g.reshape(1, D))
    specs.append(pl.BlockSpec((1, D), lambda i: (0, 0)))
    if not final:
        ins += [shift.reshape(-1, 1, D), scale.reshape(-1, 1, D)]
        specs += [pl.BlockSpec((1, 1, D), per_b), pl.BlockSpec((1, 1, D), per_b)]
    if router is not None:
        ins.append(router)
        specs.append(pl.BlockSpec(router.shape, lambda i: (0, 0)))
    outs, ospecs = [], []
    if has_res and not final:
        outs.append(jax.ShapeDtypeStruct((N, D), F32))
        ospecs.append(pl.BlockSpec((tm, D), row))
    outs.append(jax.ShapeDtypeStruct((N, D), F32 if final else BF16))
    ospecs.append(pl.BlockSpec((tm, D), row))
    if router is not None:
        outs.append(jax.ShapeDtypeStruct((N, 128), F32))
        ospecs.append(pl.BlockSpec((tm, 128), row))
    res = pl.pallas_call(
        functools.partial(_resmod_kernel, has_res=has_res, final=final, router=router is not None),
        out_shape=tuple(outs),
        grid=(N // tm,),
        in_specs=specs,
        out_specs=tuple(ospecs),
        compiler_params=_cparams(("parallel",)),
        name="resmod",
    )(*ins)
    return res


def _mm_kernel(x_ref, w_ref, o_ref):
    o_ref[...] = _dot(x_ref[...], w_ref[...]).astype(o_ref.dtype)


def matmul(x, w, out_dtype, tm, tn):
    M, K = x.shape
    N = w.shape[1]
    return pl.pallas_call(
        _mm_kernel,
        out_shape=jax.ShapeDtypeStruct((M, N), out_dtype),
        grid=(M // tm, N // tn),
        in_specs=[pl.BlockSpec((tm, K), lambda i, j: (i, 0)), pl.BlockSpec((K, tn), lambda i, j: (0, j))],
        out_specs=pl.BlockSpec((tm, tn), lambda i, j: (i, j)),
        compiler_params=_cparams(("parallel", "parallel")),
        name="matmul",
    )(x, w)


GLA_TC = 512


def _gla_kernel(q_ref, k_ref, v_ref, g_ref, alr_ref, a2_ref, ab_ref, ng_ref, o_ref, st_ref, la_ref):
    @pl.when(pl.program_id(1) == 0)
    def _():
        st_ref[...] = jnp.zeros_like(st_ref)

    C, DK, DV = CHUNK, GLA_DK, GLA_DV
    la = _dot_x3(alr_ref[...], a2_ref[...]) + ab_ref[...]
    la_ref[...] = -_softplus(-la) / GLA_NORMALIZER
    ri = lax.broadcasted_iota(jnp.int32, (C, C), 0)
    ci = lax.broadcasted_iota(jnp.int32, (C, C), 1)
    causal = ri >= ci
    tril = causal.astype(F32)
    H = range(GLA_HEADS)

    def body(c, carry):
        sl = pl.ds(pl.multiple_of(c * C, C), C)
        bc_all = _dot_sel(tril, la_ref[sl, :])
        bcs = [bc_all[:, h * DK:(h + 1) * DK] for h in H]
        bls = [bc[C - 1:C, :] for bc in bcs]
        ks = [k_ref[sl, h * DK:(h + 1) * DK].astype(F32) for h in H]
        vs = [v_ref[sl, h * DV:(h + 1) * DV] for h in H]
        qds = [(q_ref[sl, h * DK:(h + 1) * DK].astype(F32) * (DK ** -0.5) * jnp.exp(bc)).astype(BF16)
               for h, bc in zip(H, bcs)]
        kds = [(k * jnp.exp(-bc)).astype(BF16) for k, bc in zip(ks, bcs)]
        kls = [(k * jnp.exp(bl - bc)).astype(BF16) for k, bl, bc in zip(ks, bls, bcs)]
        atts = [jnp.where(causal, _dot_nt(qd, kd), 0.0).astype(BF16) for qd, kd in zip(qds, kds)]
        sts = [st_ref[h] for h in H]
        os_ = [_dot(att, v) + _dot_nt(qd, st.astype(BF16)) for att, v, qd, st in zip(atts, vs, qds, sts)]
        for h in H:
            st_ref[h] = sts[h] * jnp.exp(bls[h]) + _dot_tn(vs[h], kls[h])
        for h in H:
            o = os_[h]
            o = o * lax.rsqrt(jnp.mean(o * o, axis=-1, keepdims=True) + NORM_EPS) * ng_ref[...]
            gg = g_ref[sl, h * DV:(h + 1) * DV].astype(F32)
            o_ref[sl, h * DV:(h + 1) * DV] = (o * (gg * _sigmoid(gg))).astype(o_ref.dtype)
        return carry

    lax.fori_loop(0, GLA_TC // C, body, 0)


def gla(U, Us, a2, a_b, norm_g, *, batch, seq):
    N = U.shape[0]
    nt = seq // GLA_TC
    HK, HV = GLA_HEADS * GLA_DK, GLA_HEADS * GLA_DV
    a2p = jnp.zeros((128, HK), F32).at[:GLA_RANK].set(a2)
    row = lambda b, i: b * nt + i
    return pl.pallas_call(
        _gla_kernel,
        out_shape=jax.ShapeDtypeStruct((N, HV), BF16),
        grid=(batch, nt),
        in_specs=[
            pl.BlockSpec((GLA_TC, HK), lambda b, i: (row(b, i), U_GLA_Q // HK)),
            pl.BlockSpec((GLA_TC, HK), lambda b, i: (row(b, i), U_GLA_K // HK)),
            pl.BlockSpec((GLA_TC, HV), lambda b, i: (row(b, i), U_GLA_V // HV)),
            pl.BlockSpec((GLA_TC, HV), lambda b, i: (row(b, i), U_GLA_G // HV)),
            pl.BlockSpec((GLA_TC, 128), lambda b, i: (row(b, i), S_GLA_A // 128)),
            pl.BlockSpec((128, HK), lambda b, i: (0, 0)),
            pl.BlockSpec((1, HK), lambda b, i: (0, 0)),
            pl.BlockSpec((1, GLA_DV), lambda b, i: (0, 0)),
        ],
        out_specs=pl.BlockSpec((GLA_TC, HV), lambda b, i: (row(b, i), 0)),
        scratch_shapes=[pltpu.VMEM((GLA_HEADS, GLA_DV, GLA_DK), F32), pltpu.VMEM((GLA_TC, HK), F32)],
        compiler_params=_cparams(("parallel", "arbitrary")),
        name="gla",
    )(U, U, U, U, Us, a2p, a_b.reshape(1, -1), norm_g.reshape(1, -1))


RWKV_TM = 256


def _seg_ones(n=256, seg=RWKV_N):
    i = np.arange(n)
    return jnp.asarray((i[:, None] // seg == i[None, :] // seg).astype(np.float32), BF16)


def _seg_sum(x, bd):
    outs = [_dot_split(x[:, s:s + 256], bd) for s in range(0, x.shape[1], 256)]
    return outs[0] if len(outs) == 1 else jnp.concatenate(outs, axis=1)


def _shift_lerp(u_ref, p_ref, mu, first):
    u = u_ref[...].astype(F32)
    prev_last = jnp.where(first, 0.0, p_ref[7:8, :].astype(F32))
    rolled = pltpu.roll(u, 1, 0)
    is_row0 = lax.broadcasted_iota(jnp.int32, u.shape, 0) == 0
    sh = jnp.where(is_row0, prev_last, rolled)
    return u + (sh - u) * mu


def _rwkv_prep_kernel(r_ref, k_ref, v_ref, wa_ref, gl_ref, rp_ref, kp_ref, vp_ref, wap_ref, glp_ref,
                      mur_ref, muk_ref, muv_ref, muwa_ref, mug_ref, w0_ref, w2_ref, a0_ref, a2_ref, g2_ref,
                      kk_ref, ka_ref, bd_ref,
                      ro_ref, ko_ref, vo_ref, kap_ref, bo_ref, lw_ref, go_ref, *, spb):
    first = (pl.program_id(0) % spb) == 0
    r = _shift_lerp(r_ref, rp_ref, mur_ref[...], first)
    k = _shift_lerp(k_ref, kp_ref, muk_ref[...], first)
    v = _shift_lerp(v_ref, vp_ref, muv_ref[...], first)
    wa = _shift_lerp(wa_ref, wap_ref, muwa_ref[...], first)
    gl = _shift_lerp(gl_ref, glp_ref, mug_ref[...], first)
    w_log = -_softplus(-(w0_ref[...] + _dot_x3(jnp.tanh(wa), w2_ref[...]))) - 0.5
    lw_ref[...] = -jnp.exp(w_log)
    a = _sigmoid(a0_ref[...] + _dot_x3(wa, a2_ref[...]))
    go_ref[...] = _dot(_sigmoid(gl).astype(BF16), g2_ref[...]).astype(go_ref.dtype)
    kk = k * kk_ref[...]
    nrm = jnp.sqrt(_seg_sum(kk * kk, bd_ref[...]))
    kk = kk / jnp.maximum(nrm, 1e-12)
    ro_ref[...] = r.astype(ro_ref.dtype)
    ko_ref[...] = (k * (1.0 + (a - 1.0) * ka_ref[...])).astype(ko_ref.dtype)
    vo_ref[...] = v.astype(vo_ref.dtype)
    kap_ref[...] = kk.astype(kap_ref.dtype)
    bo_ref[...] = (kk * a).astype(bo_ref.dtype)


def rwkv_prep(U, Us, mu, w0, w2, a0, a2, g2, k_k, k_a, *, seq):
    N = U.shape[0]
    tm = RWKV_TM
    spb = seq // tm
    R = RWKV_DIM
    cur = lambda cb: (lambda i: (i, cb))
    prv = lambda cb: (lambda i: (jnp.maximum(i * (tm // 8) - 1, 0), cb))
    mu_r, mu_k, mu_v = mu[:R], mu[R:2 * R], mu[2 * R:3 * R]
    mu_wa = mu[3 * R:3 * R + 128]
    mu_g = jnp.zeros((256,), F32).at[:RWKV_G_RANK].set(mu[3 * R + 128:])
    w2p = jnp.zeros((128, R), F32).at[:RWKV_W_RANK].set(w2)
    a2p = jnp.zeros((128, R), F32).at[RWKV_W_RANK:].set(a2)
    g2p = jnp.zeros((256, R), BF16).at[:RWKV_G_RANK].set(g2.astype(BF16))
    vec = lambda a: a.reshape(1, -1)
    full = lambda a: pl.BlockSpec(a.shape, lambda i: (0, 0))
    params = [vec(mu_r), vec(mu_k), vec(mu_v), vec(mu_wa), vec(mu_g), vec(w0), w2p, vec(a0), a2p, g2p,
              vec(k_k), vec(k_a), _seg_ones()]
    in_specs = [
        pl.BlockSpec((tm, R), cur(U_RWKV_R // R)), pl.BlockSpec((tm, R), cur(U_RWKV_K // R)),
        pl.BlockSpec((tm, R), cur(U_RWKV_V // R)),
        pl.BlockSpec((tm, 128), cur(S_RWKV_WA // 128)), pl.BlockSpec((tm, 256), cur(S_RWKV_G // 256)),
        pl.BlockSpec((8, R), prv(U_RWKV_R // R)), pl.BlockSpec((8, R), prv(U_RWKV_K // R)),
        pl.BlockSpec((8, R), prv(U_RWKV_V // R)),
        pl.BlockSpec((8, 128), prv(S_RWKV_WA // 128)), pl.BlockSpec((8, 256), prv(S_RWKV_G // 256)),
    ] + [full(p) for p in params]
    out = lambda dt: jax.ShapeDtypeStruct((N, R), dt)
    ospec = pl.BlockSpec((tm, R), lambda i: (i, 0))
    return pl.pallas_call(
        functools.partial(_rwkv_prep_kernel, spb=spb),
        out_shape=(out(BF16), out(BF16), out(BF16), out(BF16), out(BF16), out(F32), out(BF16)),
        grid=(N // tm,),
        in_specs=in_specs,
        out_specs=(ospec,) * 7,
        compiler_params=_cparams(("parallel",)),
        name="rwkv_prep",
    )(U, U, U, Us, Us, U, U, U, Us, Us, *params)


RWKV_TC = 512
RWKV_W = RWKV_HB * RWKV_N


def _rwkv_masks():
    W, C = RWKV_W, CHUNK
    i = np.arange(W)
    same = i[:, None] // C == i[None, :] // C
    m_bd = same.astype(np.float32)
    low_s = (same & (i[:, None] % C > i[None, :] % C)).astype(np.float32)
    low_i = (same & (i[:, None] % C >= i[None, :] % C)).astype(np.float32)
    tril = np.tril(np.ones((C, C), np.float32))
    return (jnp.asarray(m_bd), jnp.asarray(low_s), jnp.asarray(low_i), jnp.asarray(tril),
            jnp.asarray(np.eye(W, dtype=np.float32)))


def _tile4(x):
    return jnp.concatenate([x] * RWKV_HB, axis=0)


def _rwkv_chunk_kernel(r_ref, k_ref, v_ref, kap_ref, b_ref, lw_ref, g_ref,
                       mbd_ref, lows_ref, lowi_ref, tril_ref, eye_ref, bd_ref,
                       rk_ref, lnw_ref, lnb_ref, o_ref, st_ref):
    @pl.when(pl.program_id(1) == 0)
    def _():
        st_ref[...] = jnp.zeros_like(st_ref)

    C, W = CHUNK, RWKV_W
    n_batch = r_ref.shape[0]
    n_groups = r_ref.shape[2] // W
    m_bd = mbd_ref[...]
    low_s = lows_ref[...]
    low_i = lowi_ref[...]
    tril = tril_ref[...]
    eye = eye_ref[...]
    bd = bd_ref[...]

    def expand(x):
        return (_tile4(x) * m_bd).astype(BF16)

    def body(c, carry):
        sl = pl.ds(pl.multiple_of(c * C, C), C)
        lw_alls = [lw_ref[bi, sl, :] for bi in range(n_batch)]
        cum_alls = [_dot_sel(tril, lw) for lw in lw_alls]
        chains = [(bi, slice(gi * W, (gi + 1) * W)) for bi in range(n_batch) for gi in range(n_groups)]
        G = range(len(chains))
        lns = [ln for _, ln in chains]
        cums = [cum_alls[bi][:, ln] for bi, ln in chains]
        clasts = [cum[C - 1:C, :] for cum in cums]
        rs = [r_ref[bi, sl, ln].astype(F32) for bi, ln in chains]
        ks = [k_ref[bi, sl, ln].astype(F32) for bi, ln in chains]
        vs = [v_ref[bi, sl, ln].astype(F32) for bi, ln in chains]
        bs = [b_ref[bi, sl, ln].astype(F32) for bi, ln in chains]
        p_invs = [jnp.exp(-cum) for cum in cums]
        xes = [jnp.concatenate([expand(kap_ref[bi, sl, ln].astype(F32) * jnp.exp(cum - lw_alls[bi][:, ln])),
                                expand(r * jnp.exp(cum))], axis=0)
               for (bi, ln), cum, r in zip(chains, cums, rs)]
        hes = [jnp.concatenate([expand(b * pi), expand(k * pi)], axis=0) for b, k, pi in zip(bs, ks, p_invs)]
        scs = [_dot_nt(xe, he) for xe, he in zip(xes, hes)]
        ams = [sc[:W, :W] * low_s for sc in scs]
        tinvs = [eye - a_m for a_m in ams]
        ambs = [a_m.astype(BF16) for a_m in ams]
        pws = [_dot(ab, ab) for ab in ambs]
        n_lev = int(np.log2(C))
        for lev in range(1, n_lev):
            pwbs = [pw.astype(BF16) for pw in pws]
            if lev < n_lev - 1:
                outs = [_dot(jnp.concatenate([pwb, tinv.astype(BF16)], axis=0), pwb)
                        for pwb, tinv in zip(pwbs, tinvs)]
                pws = [o[:W] for o in outs]
                tinvs = [tinv + o[W:] for tinv, o in zip(tinvs, outs)]
            else:
                tinvs = [tinv + _dot(tinv.astype(BF16), pwb) for tinv, pwb in zip(tinvs, pwbs)]
        ves = [expand(v) for v in vs]
        bqv = [_dot(jnp.concatenate([(sc[:W, W:] * low_s).astype(BF16), (sc[W:, W:] * low_i).astype(BF16)], axis=0),
                    ve) for sc, ve in zip(scs, ves)]
        bmv = [o[:W] for o in bqv]
        qkv = [o[W:] for o in bqv]
        sts = [st_ref[gi] for gi in G]
        xss = [_dot_nt(xe, st.astype(BF16)) for xe, st in zip(xes, sts)]
        us = [_dot(tinv.astype(BF16), (xs[:W] + bv).astype(BF16)) for tinv, xs, bv in zip(tinvs, xss, bmv)]
        kbs = [jnp.concatenate([expand(k * jnp.exp(cl - cum)), expand(b * jnp.exp(cl - cum))], axis=0)
               for k, b, cl, cum in zip(ks, bs, clasts, cums)]
        for gi in G:
            vu = jnp.concatenate([ves[gi], (-us[gi]).astype(BF16)], axis=0)
            st_ref[gi] = sts[gi] * jnp.exp(clasts[gi]) + _dot_tn(vu, kbs[gi]) * m_bd
        for gi in G:
            ln = lns[gi]
            qb = (scs[gi][W:, :W] * low_i).astype(BF16)
            y_e = xss[gi][W:] + qkv[gi] - _dot(qb, us[gi].astype(BF16))
            y = y_e[0:C] + y_e[C:2 * C] + y_e[2 * C:3 * C] + y_e[3 * C:4 * C]
            mu = _seg_sum(y, bd) * (1.0 / RWKV_N)
            yc = y - mu
            var = _seg_sum(yc * yc, bd) * (1.0 / RWKV_N)
            yn = yc * lax.rsqrt(var + RWKV_LN_EPS) * lnw_ref[:, ln] + lnb_ref[:, ln]
            bonus = _seg_sum(rs[gi] * ks[gi] * rk_ref[:, ln], bd) * vs[gi]
            bi = chains[gi][0]
            o_ref[bi, sl, ln] = ((yn + bonus) * g_ref[bi, sl, ln].astype(F32)).astype(o_ref.dtype)
        return carry

    lax.fori_loop(0, RWKV_TC // C, body, 0)


RWKV_NB = 1


def rwkv_chunk(r, k, v, kap, b, lw, gate, r_k, ln_w, ln_b, *, batch, seq):
    N, R = r.shape
    nt = seq // RWKV_TC
    W = RWKV_W
    nb = RWKV_NB if batch % RWKV_NB == 0 else 1
    blk = pl.BlockSpec((nb, RWKV_TC, R), lambda bb, i: (bb, i, 0))
    masks = _rwkv_masks() + (_seg_ones(),)
    full = lambda a: pl.BlockSpec(a.shape, lambda bb, i: (0, 0))
    pvec = pl.BlockSpec((1, R), lambda bb, i: (0, 0))
    seqs = [a.reshape(batch, seq, R) for a in (r, k, v, kap, b, lw, gate)]
    out = pl.pallas_call(
        _rwkv_chunk_kernel,
        out_shape=jax.ShapeDtypeStruct((batch, seq, R), BF16),
        grid=(batch // nb, nt),
        in_specs=[blk] * 7 + [full(m) for m in masks] + [pvec] * 3,
        out_specs=blk,
        scratch_shapes=[pltpu.VMEM((nb * (R // W), W, W), F32)],
        compiler_params=_cparams(("parallel", "arbitrary")),
        name="rwkv_chunk",
    )(*seqs, *masks, r_k.reshape(1, R), ln_w.reshape(1, R), ln_b.reshape(1, R))
    return out.reshape(N, R)


def _gelu_tanh(x):
    return 0.5 * x * (1.0 + jnp.tanh(np.sqrt(2.0 / np.pi) * (x + 0.044715 * (x * x * x))))


def _nsa_compress_kernel(x_ref, pos_ref, w1_ref, w2_ref, o_ref):
    x = x_ref[0, 0]
    w1 = w1_ref[0]
    half = w1.shape[0] // 2
    nrow = x.shape[0]
    ha = _dot(x, w1[:half])
    hb = _dot(x, w1[half:])
    h = ha + pltpu.roll(hb, nrow - 1, 0)
    pb = _dot(pos_ref[0], w1)
    h = _gelu_tanh(h + pb[0:1, :])
    o_ref[0, 0, 0] = _dot(h.astype(BF16), w2_ref[0]).astype(o_ref.dtype)


def nsa_compress(x, pos, w1, w2):
    two = 2
    B, G, nr = x.shape[0] // two, x.shape[1], x.shape[2]
    dh = x.shape[3] // NSA_CMP_STRIDE
    posf = jnp.broadcast_to(pos.reshape(two, 1, NSA_CMP_LEN * dh), (two, 8, NSA_CMP_LEN * dh)).astype(BF16)
    return pl.pallas_call(
        _nsa_compress_kernel,
        out_shape=jax.ShapeDtypeStruct((two, B, G, nr, dh), BF16),
        grid=(two, B, G),
        in_specs=[
            pl.BlockSpec((1, 1, nr, NSA_CMP_STRIDE * dh), lambda s, b, g: (s * B + b, g, 0, 0)),
            pl.BlockSpec((1, 8, NSA_CMP_LEN * dh), lambda s, b, g: (s, 0, 0)),
            pl.BlockSpec((1, NSA_CMP_LEN * dh, NSA_CMP_HIDDEN), lambda s, b, g: (s, 0, 0)),
            pl.BlockSpec((1, NSA_CMP_HIDDEN, dh), lambda s, b, g: (s, 0, 0)),
        ],
        out_specs=pl.BlockSpec((1, 1, 1, nr, dh), lambda s, b, g: (s, b, g, 0, 0)),
        compiler_params=_cparams(("parallel", "parallel", "parallel")),
        name="nsa_compress",
    )(x, posf, w1.astype(BF16), w2.astype(BF16))


def _nsa_cmp_kernel(q_ref, kc_ref, vct_ref, ov_ref, cb_ref, oc_ref, sb_ref, qs_ref, *, n_slc, n_sel):
    qi = pl.program_id(2)
    QB = CMP_QB
    ncp = kc_ref.shape[2]
    for h in range(NSA_HPG):
        qs_ref[h * QB:(h + 1) * QB, :] = q_ref[:, h * NSA_DH:(h + 1) * NSA_DH] * (NSA_DH ** -0.5)
    s = _dot_nt(kc_ref[0, 0], qs_ref[...])
    vct = vct_ref[0, 0]
    psum = jnp.zeros((ncp, QB), F32)
    for h in range(NSA_HPG):
        sh = s[:, h * QB:(h + 1) * QB] + cb_ref[h]
        m = jnp.maximum(jnp.max(sh, axis=0, keepdims=True), -1e20)
        e = jnp.exp(sh - m)
        acc = _dot(vct, e.astype(BF16))
        inv_l = 1.0 / jnp.maximum(acc[NSA_DH:NSA_DH + 1], 1e-30)
        oc_ref[:, h * NSA_DH:(h + 1) * NSA_DH] = (acc[0:NSA_DH] * inv_l).T.astype(oc_ref.dtype)
        psum = psum + e * inv_l
    imp = _dot_sel(ov_ref[...], psum)
    j = lax.broadcasted_iota(jnp.int32, (n_slc, QB), 0)
    tt = qi * QB + lax.broadcasted_iota(jnp.int32, (n_slc, QB), 1)
    cur = jnp.right_shift(tt, 6)
    forced = (j == 0) | (j == cur) | (j == cur - 1)
    cand = (j >= 1) & (j <= cur - 2)
    n_free = n_sel - 3

    def emit(sel):
        sb_ref[0, 0] = jnp.where(sel, 0.0, -SEL_MASK_BIAS).T.astype(sb_ref.dtype)

    last_cur = (qi * QB + QB - 1) // NSA_SLC_LEN

    @pl.when(last_cur - 2 <= n_free)
    def _():
        emit(forced | cand)

    @pl.when(last_cur - 2 > n_free)
    def _():
        cur_row = cur[0:1, :]
        rank = jnp.zeros((n_slc, QB), jnp.int32)
        for jp in range(1, n_slc):
            row = imp[jp:jp + 1, :]
            ahead = (row > imp) | ((row == imp) & (j > jp))
            rank = rank + jnp.where(ahead & (jp <= cur_row - 2), 1, 0)
        emit(forced | (cand & (rank < n_free)))


def nsa_cmp(U, kc, vct, slopes, *, batch, seq):
    N = U.shape[0]
    QB = CMP_QB
    nq = seq // QB
    ncp = kc.shape[2]
    n_slc = seq // NSA_SLC_LEN
    n_sel = min(NSA_N_SEL, n_slc)
    nn = np.arange(ncp)
    jj = np.arange(n_slc)
    ov = ((nn[None, :] * NSA_CMP_STRIDE + NSA_CMP_LEN - 1 >= jj[:, None] * NSA_SLC_LEN)
          & (nn[None, :] * NSA_CMP_STRIDE <= jj[:, None] * NSA_SLC_LEN + NSA_SLC_LEN - 1)
          & (nn[None, :] < ncp - 1)).astype(np.float32)
    tt = np.arange(seq)
    ended = jnp.asarray(nn[:, None] * NSA_CMP_STRIDE + NSA_CMP_LEN - 1 <= tt[None, :])
    adist = np.abs(tt[None, :] - (nn[:, None] * NSA_CMP_STRIDE + (NSA_CMP_LEN - 1) / 2.0)).astype(np.float32)
    cbias = jnp.where(ended[None], -slopes[:, None, None] * jnp.asarray(adist)[None], NEG_BIG)
    G = NSA_GROUPS
    W = NSA_HPG * NSA_DH
    return pl.pallas_call(
        functools.partial(_nsa_cmp_kernel, n_slc=n_slc, n_sel=n_sel),
        out_shape=(jax.ShapeDtypeStruct((N, NSA_HEADS * NSA_DH), BF16),
                   jax.ShapeDtypeStruct((batch, G, seq, n_slc), BF16)),
        grid=(batch, G, nq),
        in_specs=[
            pl.BlockSpec((QB, W), lambda b, g, i: (b * nq + i, U_NSA_Q // W + g)),
            pl.BlockSpec((1, 1, ncp, NSA_DH), lambda b, g, i: (b, g, 0, 0)),
            pl.BlockSpec((1, 1, NSA_DH + 8, ncp), lambda b, g, i: (b, g, 0, 0)),
            pl.BlockSpec((n_slc, ncp), lambda b, g, i: (0, 0)),
            pl.BlockSpec((NSA_HPG, ncp, QB), lambda b, g, i: (g, 0, i)),
        ],
        out_specs=(
            pl.BlockSpec((QB, W), lambda b, g, i: (b * nq + i, g)),
            pl.BlockSpec((1, 1, QB, n_slc), lambda b, g, i: (b, g, i, 0)),
        ),
        scratch_shapes=[pltpu.VMEM((NSA_HPG * QB, NSA_DH), BF16)],
        compiler_params=_cparams(("parallel", "parallel", "arbitrary")),
        name="nsa_cmp",
    )(U, kc, vct, jnp.asarray(ov), cbias)


def _nsa_win_kernel(*refs):
    nb = NSA_WINDOW // NSA_QB + 1
    q_ref, k_refs, v_refs = refs[0], refs[1:1 + nb], refs[1 + nb:1 + 2 * nb]
    bias_ref, o_ref, qs_ref = refs[1 + 2 * nb:]
    qi = pl.program_id(2)
    QB = NSA_QB
    KW = NSA_WINDOW + QB
    for h in range(NSA_HPG):
        qs_ref[h * QB:(h + 1) * QB, :] = q_ref[:, h * NSA_DH:(h + 1) * NSA_DH] * (NSA_DH ** -0.5)
    k = jnp.concatenate([r[0, 0] for r in k_refs], axis=0)
    vt = jnp.concatenate([r[0, 0] for r in v_refs], axis=1)
    s = _dot_nt(k, qs_ref[...])

    def finish(before_start):
        for h in range(NSA_HPG):
            cols = slice(h * QB, (h + 1) * QB)
            sh = s[:, cols] + bias_ref[h]
            if before_start:
                r = lax.broadcasted_iota(jnp.int32, (KW, QB), 0)
                sh = jnp.where(r >= NSA_WINDOW - qi * QB, sh, NEG_BIG)
            m = jnp.max(sh, axis=0, keepdims=True)
            e = jnp.exp(sh - m)
            acc = _dot(vt, e.astype(BF16))
            o = acc[0:NSA_DH] / acc[NSA_DH:NSA_DH + 1]
            o_ref[:, h * NSA_DH:(h + 1) * NSA_DH] = o.T.astype(o_ref.dtype)

    @pl.when(qi * QB < NSA_WINDOW)
    def _():
        finish(True)

    @pl.when(qi * QB >= NSA_WINDOW)
    def _():
        finish(False)


def nsa_win(U, kw_pad, vtw_pad, slopes, *, batch, seq):
    N = U.shape[0]
    QB = NSA_QB
    KW = NSA_WINDOW + QB
    nq = seq // QB
    G = NSA_GROUPS
    W = NSA_HPG * NSA_DH
    dist = (np.arange(QB)[None, :] - np.arange(KW)[:, None] + NSA_WINDOW).astype(np.float32)
    inside = jnp.asarray((dist >= 0) & (dist < NSA_WINDOW))
    bias = jnp.where(inside[None], -slopes[:, None, None] * jnp.asarray(dist)[None], NEG_BIG)
    nb = NSA_WINDOW // QB + 1
    kb = lambda off: pl.BlockSpec((1, 1, QB, NSA_DH), lambda b, g, i: (b, g, i + off, 0))
    vb = lambda off: pl.BlockSpec((1, 1, NSA_DH + 8, QB), lambda b, g, i: (b, g, 0, i + off))
    return pl.pallas_call(
        _nsa_win_kernel,
        out_shape=jax.ShapeDtypeStruct((N, NSA_HEADS * NSA_DH), BF16),
        grid=(batch, G, nq),
        in_specs=[pl.BlockSpec((QB, W), lambda b, g, i: (b * nq + i, U_NSA_Q // W + g)),
                  *[kb(o) for o in range(nb)], *[vb(o) for o in range(nb)],
                  pl.BlockSpec((NSA_HPG, KW, QB), lambda b, g, i: (g, 0, 0))],
        out_specs=pl.BlockSpec((QB, W), lambda b, g, i: (b * nq + i, g)),
        scratch_shapes=[pltpu.VMEM((NSA_HPG * QB, NSA_DH), BF16)],
        compiler_params=_cparams(("parallel", "parallel", "arbitrary")),
        name="nsa_win",
    )(U, *([kw_pad] * nb), *([vtw_pad] * nb), bias)


SEL_QB = 512
SEL_KV = 512


def _sel_pairs(seq):
    qs, ks = [], []
    for qi in range(seq // SEL_QB):
        for kj in range((qi * SEL_QB) // SEL_KV + 1):
            qs.append(qi)
            ks.append(kj)
    return np.asarray(qs, np.int32), np.asarray(ks, np.int32)


def _nsa_sel_kernel(qi_ref, kj_ref, slope_ref, q_ref, sb_ref, qx_ref, ka_ref, vt_ref, o_ref,
                    qa_ref, m_ref, acc_ref):
    g = pl.program_id(1)
    p = pl.program_id(2)
    qi = qi_ref[p]
    kj = kj_ref[p]
    QB, KV = SEL_QB, SEL_KV
    nblk = sb_ref.shape[3]
    base = NSA_DH + nblk

    @pl.when(kj == 0)
    def _():
        for h in range(NSA_HPG):
            rows = slice(h * QB, (h + 1) * QB)
            qa_ref[rows, 0:NSA_DH] = q_ref[:, h * NSA_DH:(h + 1) * NSA_DH] * (NSA_DH ** -0.5)
            qa_ref[rows, NSA_DH:base] = sb_ref[0, 0]
            qa_ref[rows, base:] = jnp.broadcast_to(qx_ref[0, h:h + 1, :], (QB, qa_ref.shape[1] - base))
        m_ref[...] = jnp.full_like(m_ref, NEG_BIG)
        acc_ref[...] = jnp.zeros_like(acc_ref)

    off = qi * QB - kj * KV
    tile_start = (kj * KV).astype(F32)

    def step(masked):
        s = _dot_nt(ka_ref[0, 0], qa_ref[...])
        vt = vt_ref[0, 0]
        if masked:
            r = lax.broadcasted_iota(jnp.int32, (KV, QB), 0)
            c = lax.broadcasted_iota(jnp.int32, (KV, QB), 1)
            causal = (c - r + off) >= 0
        for h in range(NSA_HPG):
            cols = slice(h * QB, (h + 1) * QB)
            sh = s[:, cols]
            if masked:
                sh = jnp.where(causal, sh, NEG_BIG)
            delta = slope_ref[g * NSA_HPG + h] * tile_start
            m_old = m_ref[h:h + 1, :]
            m_new = jnp.maximum(m_old, jnp.max(sh, axis=0, keepdims=True) + delta)
            e = jnp.exp(sh - (m_new - delta))
            alpha = jnp.exp(m_old - m_new)
            acc_ref[:, cols] = alpha * acc_ref[:, cols] + _dot(vt, e.astype(BF16))
            m_ref[h:h + 1, :] = m_new

    last = (qi * QB) // KV

    @pl.when(kj < last)
    def _():
        step(False)

    @pl.when(kj == last)
    def _():
        step(True)
        for h in range(NSA_HPG):
            cols = slice(h * QB, (h + 1) * QB)
            o = acc_ref[0:NSA_DH, cols] / acc_ref[NSA_DH:NSA_DH + 1, cols]
            o_ref[:, h * NSA_DH:(h + 1) * NSA_DH] = o.T.astype(o_ref.dtype)


def sel_key_columns(seq):
    n_slc = seq // NSA_SLC_LEN
    pos = np.arange(seq)
    onehot = (pos[:, None] // NSA_SLC_LEN == np.arange(n_slc)[None, :]).astype(np.float32)
    r = pos % SEL_KV
    extra = np.zeros((seq, 128), np.float32)
    extra[:, 0] = extra[:, 2] = (r // 32) * 32
    extra[:, 1] = extra[:, 3] = r % 32
    return jnp.asarray(np.concatenate([onehot, extra], axis=1), BF16)


def nsa_sel(U, selbias, k_aug, vt_slc, slopes, *, batch, seq):
    N = U.shape[0]
    QB, KV = SEL_QB, SEL_KV
    nq = seq // QB
    G = NSA_GROUPS
    W = NSA_HPG * NSA_DH
    n_slc = selbias.shape[3]
    ka_w = k_aug.shape[3]
    qs, ks = _sel_pairs(seq)
    s_hi = slopes.astype(BF16)
    s_lo = (slopes - s_hi.astype(F32)).astype(BF16)
    qx = jnp.zeros((NSA_HEADS, 128), BF16).at[:, 0].set(s_hi).at[:, 1].set(s_hi).at[:, 2].set(s_lo).at[:, 3].set(s_lo)
    qx = jnp.pad(qx.reshape(G, NSA_HPG, 128), ((0, 0), (0, 8 - NSA_HPG), (0, 0)))
    grid_spec = pltpu.PrefetchScalarGridSpec(
        num_scalar_prefetch=3,
        grid=(batch, G, len(qs)),
        in_specs=[
            pl.BlockSpec((QB, W), lambda b, g, p, qi, kj, s: (b * nq + qi[p], U_NSA_Q // W + g)),
            pl.BlockSpec((1, 1, QB, n_slc), lambda b, g, p, qi, kj, s: (b, g, qi[p], 0)),
            pl.BlockSpec((1, 8, 128), lambda b, g, p, qi, kj, s: (g, 0, 0)),
            pl.BlockSpec((1, 1, KV, ka_w), lambda b, g, p, qi, kj, s: (b, g, kj[p], 0)),
            pl.BlockSpec((1, 1, NSA_DH + 8, KV), lambda b, g, p, qi, kj, s: (b, g, 0, kj[p])),
        ],
        out_specs=pl.BlockSpec((QB, W), lambda b, g, p, qi, kj, s: (b * nq + qi[p], g)),
        scratch_shapes=[
            pltpu.VMEM((NSA_HPG * QB, ka_w), BF16),
            pltpu.VMEM((NSA_HPG, QB), F32),
            pltpu.VMEM((NSA_DH + 8, NSA_HPG * QB), F32),
        ],
    )
    return pl.pallas_call(
        _nsa_sel_kernel,
        out_shape=jax.ShapeDtypeStruct((N, NSA_HEADS * NSA_DH), BF16),
        grid_spec=grid_spec,
        compiler_params=_cparams(("parallel", "parallel", "arbitrary")),
        name="nsa_sel",
    )(jnp.asarray(qs), jnp.asarray(ks), slopes, U, selbias, qx, k_aug, vt_slc)


def _nsa_combine_kernel(oc_ref, os_ref, ow_ref, gate_ref, e_ref, o_ref):
    ge = _dot_split(_sigmoid(gate_ref[...]), e_ref[...])
    Wd = NSA_HEADS * NSA_DH
    o = (ge[:, :Wd] * oc_ref[...].astype(F32) + ge[:, Wd:2 * Wd] * os_ref[...].astype(F32)
         + ge[:, 2 * Wd:] * ow_ref[...].astype(F32))
    o_ref[...] = o.astype(o_ref.dtype)


def nsa_combine(o_c, o_s, o_w, Us):
    N, Wd = o_c.shape
    tm = 512
    e = np.zeros((128, 3 * Wd), np.float32)
    for h in range(NSA_HEADS):
        for j in range(3):
            e[h * 3 + j, j * Wd + h * NSA_DH:j * Wd + (h + 1) * NSA_DH] = 1.0
    blk = pl.BlockSpec((tm, Wd), lambda i: (i, 0))
    return pl.pallas_call(
        _nsa_combine_kernel,
        out_shape=jax.ShapeDtypeStruct((N, Wd), BF16),
        grid=(N // tm,),
        in_specs=[blk, blk, blk, pl.BlockSpec((tm, 128), lambda i: (i, S_NSA_GATE // 128)),
                  pl.BlockSpec((128, 3 * Wd), lambda i: (0, 0))],
        out_specs=blk,
        compiler_params=_cparams(("parallel",)),
        name="nsa_combine",
    )(o_c, o_s, o_w, Us, jnp.asarray(e, BF16))


def _merge_kernel(ya_ref, yb_ref, yc_ref, ga_ref, gb_ref, gc_ref, p_ref, o_ref):
    m = (_sigmoid(ga_ref[...].astype(F32)) * _dot(ya_ref[...], p_ref[0])
         + _sigmoid(gb_ref[...].astype(F32)) * _dot(yb_ref[...], p_ref[1])
         + _sigmoid(gc_ref[...].astype(F32)) * _dot(yc_ref[...], p_ref[2]))
    o_ref[...] = m.astype(o_ref.dtype)


def merge(y_a, y_b, y_c, U, p_merge):
    N = y_a.shape[0]
    tm, tn = 1024, 512
    yb = pl.BlockSpec((tm, BRANCH_DIM), lambda i, j: (i, 0))
    gate = lambda br: pl.BlockSpec((tm, tn), lambda i, j: (i, (U_GATE + br * D_MODEL) // tn + j))
    return pl.pallas_call(
        _merge_kernel,
        out_shape=jax.ShapeDtypeStruct((N, D_MODEL), BF16),
        grid=(N // tm, D_MODEL // tn),
        in_specs=[yb, yb, yb, gate(0), gate(1), gate(2),
                  pl.BlockSpec((3, BRANCH_DIM, tn), lambda i, j: (0, 0, j))],
        out_specs=pl.BlockSpec((tm, tn), lambda i, j: (i, j)),
        compiler_params=_cparams(("parallel", "parallel")),
        name="merge",
    )(y_a, y_b, y_c, U, U, U, p_merge)


def _ffn_kernel(h_ref, w1_ref, w3_ref, w2_ref, o_ref, acc_ref):
    f = pl.program_id(1)

    @pl.when(f == 0)
    def _():
        acc_ref[...] = jnp.zeros_like(acc_ref)

    h = h_ref[...]
    a = _dot(h, w1_ref[...])
    z = (a * _sigmoid(a) * _dot(h, w3_ref[...])).astype(BF16)
    acc_ref[...] += _dot(z, w2_ref[...])

    @pl.when(f == pl.num_programs(1) - 1)
    def _():
        o_ref[...] = acc_ref[...].astype(o_ref.dtype)


def ffn(h, w1, w3, w2):
    N, D = h.shape
    F = w1.shape[1]
    tm, tf = 1024, 512
    return pl.pallas_call(
        _ffn_kernel,
        out_shape=jax.ShapeDtypeStruct((N, D), BF16),
        grid=(N // tm, F // tf),
        in_specs=[pl.BlockSpec((tm, D), lambda i, f: (i, 0)),
                  pl.BlockSpec((D, tf), lambda i, f: (0, f)),
                  pl.BlockSpec((D, tf), lambda i, f: (0, f)),
                  pl.BlockSpec((tf, D), lambda i, f: (f, 0))],
        out_specs=pl.BlockSpec((tm, D), lambda i, f: (i, 0)),
        scratch_shapes=[pltpu.VMEM((tm, D), F32)],
        compiler_params=_cparams(("parallel", "arbitrary")),
        name="ffn",
    )(h, w1, w3, w2)


def _route_kernel(lg_ref, cb_ref, sel_ref):
    lg = lg_ref[...]
    lane = lax.broadcasted_iota(jnp.int32, lg.shape, 1)
    x = jnp.where(lane < N_EXPERTS, lg, NEG_BIG)
    v1 = jnp.max(x, axis=-1, keepdims=True)
    i1 = jnp.min(jnp.where(x == v1, lane, 1024), axis=-1, keepdims=True)
    x2 = jnp.where(lane == i1, NEG_BIG, x)
    v2 = jnp.max(x2, axis=-1, keepdims=True)
    i2 = jnp.min(jnp.where(x2 == v2, lane, 1024), axis=-1, keepdims=True)
    e2 = jnp.exp(v2 - v1)
    w1 = 1.0 / (1.0 + e2)
    w2 = e2 / (1.0 + e2)
    cb_ref[...] = jnp.where(lane == i1, w1, 0.0) + jnp.where(lane == i2, w2, 0.0)
    sel_ref[...] = jnp.where((lane == i1) | (lane == i2), 1.0, 0.0)


def route(logits):
    N = logits.shape[0]
    tm = 1024
    blk = pl.BlockSpec((tm, 128), lambda i: (i, 0))
    return pl.pallas_call(
        _route_kernel,
        out_shape=(jax.ShapeDtypeStruct((N, 128), F32), jax.ShapeDtypeStruct((N, 128), F32)),
        grid=(N // tm,),
        in_specs=[blk],
        out_specs=(blk, blk),
        compiler_params=_cparams(("parallel",)),
        name="route",
    )(logits)


MOE_TM = 1024
MOE_TF = 512
MOE_WB = 512
MOE_GT = 256


def _moe_plan(sel):
    N, E = sel.shape
    tm, wb, gt = MOE_TM, MOE_WB, MOE_GT
    sub = tm // gt
    NT = 2 * N // tm + E
    NG = NT * sub
    R = NT * tm
    P = NG + E * (N // wb)
    i32 = jnp.int32
    cs = jnp.cumsum(sel.astype(i32), axis=0)
    cnt = cs[-1]
    tiles_e = (cnt + tm - 1) // tm
    tile_end = jnp.cumsum(tiles_e)
    tile_start = tile_end - tiles_e
    total_tiles = tile_end[-1]
    dest = jnp.where(sel, tile_start[None, :] * tm + cs - 1, R)
    ti = jnp.arange(NT, dtype=i32)
    tile_valid = ti < total_tiles
    count_le = lambda ends, q: jnp.sum((ends[None, :] <= q[:, None]).astype(i32), axis=1)
    tile_e = jnp.minimum(count_le(tile_end, ti), E - 1)
    gi = jnp.arange(NG, dtype=i32)
    g_valid = tile_valid[gi // sub]
    g_e = tile_e[gi // sub]
    k_lo = gi * gt - tile_start[g_e] * tm
    nonempty = g_valid & (k_lo < cnt[g_e])
    k_hi = jnp.minimum(k_lo + gt, cnt[g_e]) - 1
    cb = cs[wb - 1::wb, :].T[g_e]
    blk_of = lambda k: jnp.sum((cb <= k[:, None]).astype(i32), axis=1)
    blo = jnp.where(nonempty, blk_of(k_lo), 0)
    bhi = jnp.where(nonempty, blk_of(k_hi), jnp.where(g_valid, 0, -1))
    npair = bhi - blo + 1
    pend = jnp.cumsum(npair)
    pstart = pend - npair
    total_p = pend[-1]
    pi = jnp.arange(P, dtype=i32)
    pvalid = pi < total_p
    ptile = jnp.minimum(count_le(pend, pi), NG - 1)
    pblk = blo[ptile] + pi - pstart[ptile]
    pfirst = pvalid & (pi == pstart[ptile])
    ptile = jnp.where(pvalid, ptile, ptile[total_p - 1])
    pblk = jnp.where(pvalid, pblk, pblk[total_p - 1])
    pexp = g_e[ptile]
    order = jnp.argsort(jnp.where(pvalid, pblk * NG + ptile, jnp.iinfo(jnp.int32).max))
    s_valid = pvalid
    s_tile = jnp.where(s_valid, ptile[order], ptile[order][total_p - 1])
    s_blk = jnp.where(s_valid, pblk[order], pblk[order][total_p - 1])
    s_exp = g_e[s_tile]
    s_first = s_valid & ((pi == 0) | (s_blk != jnp.roll(s_blk, 1)))
    e_src = jnp.where(tile_valid, ti, 0)
    e_exp = jnp.where(tile_valid, tile_e, tile_e[jnp.maximum(total_tiles - 1, 0)])
    b2i = lambda x: x.astype(i32)
    return dict(dest=dest, g=(ptile, pblk, pexp, b2i(pvalid), b2i(pfirst)),
                e=(e_exp, b2i(tile_valid), e_src), s=(s_tile, s_blk, s_exp, b2i(s_valid), b2i(s_first)), NT=NT, P=P)


def _moe_group_kernel(pt_ref, pb_ref, pe_ref, pv_ref, pf_ref, dest_ref, h_ref, o_ref):
    p = pl.program_id(0)

    @pl.when(pv_ref[p] == 1)
    def _():
        rel = dest_ref[0] - pt_ref[p] * MOE_GT
        row = lax.broadcasted_iota(jnp.int32, (MOE_GT, MOE_WB), 0)
        onehot = jnp.where(rel == row, 1.0, 0.0).astype(BF16)
        rows = _dot(onehot, h_ref[...]).astype(o_ref.dtype)

        @pl.when(pf_ref[p] == 1)
        def _():
            o_ref[...] = rows

        @pl.when(pf_ref[p] == 0)
        def _():
            o_ref[...] += rows


def _moe_expert_kernel(te_ref, tv_ref, ts_ref, h_ref, w1_ref, w3_ref, w2_ref, o_ref, acc_ref):
    i = pl.program_id(0)
    f = pl.program_id(1)
    nf = pl.num_programs(1)

    @pl.when(tv_ref[i] == 1)
    def _():
        @pl.when(f == 0)
        def _():
            acc_ref[...] = jnp.zeros_like(acc_ref)

        h = h_ref[...]
        a = _dot(h, w1_ref[0, 0].astype(BF16))
        z = (a * _sigmoid(a) * _dot(h, w3_ref[0, 0].astype(BF16))).astype(BF16)
        acc_ref[...] += _dot(z, w2_ref[0, 0].astype(BF16))

        @pl.when(f == nf - 1)
        def _():
            o_ref[...] = acc_ref[...].astype(o_ref.dtype)

    @pl.when((tv_ref[i] == 0) & (f == nf - 1))
    def _():
        o_ref[...] = jnp.zeros_like(o_ref)


def _moe_ungroup_kernel(st_ref, sb_ref, se_ref, sv_ref, sf_ref, dest_ref, w_ref, y_ref, o_ref):
    p = pl.program_id(0)

    @pl.when(sv_ref[p] == 1)
    def _():
        sub = lax.broadcasted_iota(jnp.int32, dest_ref.shape, 0)
        mine = sub == se_ref[p]
        both = jnp.concatenate([jnp.sum(jnp.where(mine, dest_ref[...], 0.0), axis=0, keepdims=True),
                                jnp.sum(jnp.where(mine, w_ref[...], 0.0), axis=0, keepdims=True),
                                jnp.zeros((126, MOE_WB), F32)], axis=0)
        cols = both.T
        rel = cols[:, 0:1].astype(jnp.int32) - st_ref[p] * MOE_GT
        col = lax.broadcasted_iota(jnp.int32, (MOE_WB, MOE_GT), 1)
        onehot_t = jnp.where(rel == col, 1.0, 0.0).astype(BF16)
        part = _dot(onehot_t, y_ref[...]) * cols[:, 1:2]

        @pl.when(sf_ref[p] == 1)
        def _():
            o_ref[...] = part

        @pl.when(sf_ref[p] == 0)
        def _():
            o_ref[...] += part


def moe(h, combine, selm, w1, w3, w2, layer):
    N, D = h.shape
    _, E, _, F = w1.shape
    tm, wb, gt, tf = MOE_TM, MOE_WB, MOE_GT, MOE_TF
    plan = _moe_plan(selm[:, :E] > 0.5)
    NT, P = plan["NT"], plan["P"]
    R = NT * tm
    dest_t = plan["dest"].T
    h_sorted = pl.pallas_call(
        _moe_group_kernel,
        out_shape=jax.ShapeDtypeStruct((R, D), BF16),
        grid_spec=pltpu.PrefetchScalarGridSpec(
            num_scalar_prefetch=5,
            grid=(P,),
            in_specs=[pl.BlockSpec((1, 1, wb), lambda p, pt, pb, pe, pv, pf: (pe[p], 0, pb[p])),
                      pl.BlockSpec((wb, D), lambda p, pt, pb, pe, pv, pf: (pb[p], 0))],
            out_specs=pl.BlockSpec((gt, D), lambda p, pt, pb, pe, pv, pf: (pt[p], 0)),
        ),
        compiler_params=_cparams(("arbitrary",)),
        name="moe_group",
    )(*plan["g"], dest_t.reshape(E, 1, N), h)
    y_sorted = pl.pallas_call(
        _moe_expert_kernel,
        out_shape=jax.ShapeDtypeStruct((R, D), BF16),
        grid_spec=pltpu.PrefetchScalarGridSpec(
            num_scalar_prefetch=3,
            grid=(NT, F // tf),
            in_specs=[pl.BlockSpec((tm, D), lambda i, f, te, tv, ts: (ts[i], 0)),
                      pl.BlockSpec((1, 1, D, tf), lambda i, f, te, tv, ts: (layer, te[i], 0, f * tv[i])),
                      pl.BlockSpec((1, 1, D, tf), lambda i, f, te, tv, ts: (layer, te[i], 0, f * tv[i])),
                      pl.BlockSpec((1, 1, tf, D), lambda i, f, te, tv, ts: (layer, te[i], f * tv[i], 0))],
            out_specs=pl.BlockSpec((tm, D), lambda i, f, te, tv, ts: (i, 0)),
            scratch_shapes=[pltpu.VMEM((tm, D), F32)],
        ),
        compiler_params=_cparams(("arbitrary", "arbitrary")),
        name="moe_expert",
    )(*plan["e"], h_sorted, w1, w3, w2)
    return pl.pallas_call(
        _moe_ungroup_kernel,
        out_shape=jax.ShapeDtypeStruct((N, D), F32),
        grid_spec=pltpu.PrefetchScalarGridSpec(
            num_scalar_prefetch=5,
            grid=(P,),
            in_specs=[pl.BlockSpec((E, wb), lambda p, st, sb, se, sv, sf: (0, sb[p])),
                      pl.BlockSpec((E, wb), lambda p, st, sb, se, sv, sf: (0, sb[p])),
                      pl.BlockSpec((gt, D), lambda p, st, sb, se, sv, sf: (st[p], 0))],
            out_specs=pl.BlockSpec((wb, D), lambda p, st, sb, se, sv, sf: (sb[p], 0)),
        ),
        compiler_params=_cparams(("arbitrary",)),
        name="moe_ungroup",
    )(*plan["s"], dest_t.astype(F32), combine[:, :E].T, y_sorted)


def _project_weights(w_in):
    gla_w = 2 * GLA_HEADS * GLA_DK + 2 * GLA_HEADS * GLA_DV + GLA_RANK
    rw_w = 3 * RWKV_DIM + RWKV_W_RANK + RWKV_A_RANK + RWKV_G_RANK
    kvw = NSA_GROUPS * NSA_DH
    nsa_w = NSA_HEADS * NSA_DH + 6 * kvw + NSA_HEADS * 3
    o_rw = gla_w
    o_nsa = gla_w + rw_w
    o_gate = o_nsa + nsa_w
    big = jnp.concatenate([
        w_in[:, 0:gla_w - GLA_RANK],
        w_in[:, o_rw:o_rw + 3 * RWKV_DIM],
        w_in[:, o_nsa:o_nsa + NSA_HEADS * NSA_DH + 6 * kvw],
        w_in[:, o_gate:],
    ], axis=1).astype(BF16)
    D = w_in.shape[0]
    z = lambda n: jnp.zeros((D, n), w_in.dtype)
    small = jnp.concatenate([
        w_in[:, gla_w - GLA_RANK:gla_w], z(128 - GLA_RANK),
        w_in[:, o_rw + 3 * RWKV_DIM:o_rw + rw_w], z(256 - RWKV_G_RANK),
        w_in[:, o_gate - NSA_HEADS * 3:o_gate], z(128 - NSA_HEADS * 3),
    ], axis=1).astype(BF16)
    return big, small


def _mixer(h, batch, seq, w_in, gla_a2, gla_a_b, gla_norm, rwkv_mu, rwkv_w0, rwkv_w2, rwkv_a0, rwkv_a2, rwkv_g2,
           rwkv_k_k, rwkv_k_a, rwkv_r_k, rwkv_ln_w, rwkv_ln_b,
           nsa_pos_k, nsa_w1_k, nsa_w2_k, nsa_pos_v, nsa_w1_v, nsa_w2_v, p_merge, w_out, slopes):
    N = h.shape[0]
    w_big, w_small = _project_weights(w_in)
    tm_u = 2048 if N % 2048 == 0 else N
    U = matmul(h, w_big, BF16, tm_u, 512)
    Us = matmul(h, w_small, F32, tm_u, S_COLS)

    y_a = gla(U, Us, gla_a2, gla_a_b, gla_norm, batch=batch, seq=seq)

    r, k, v, kap, b, lw, gate = rwkv_prep(U, Us, rwkv_mu, rwkv_w0, rwkv_w2, rwkv_a0, rwkv_a2, rwkv_g2,
                                          rwkv_k_k, rwkv_k_a, seq=seq)
    y_b = rwkv_chunk(r, k, v, kap, b, lw, gate, rwkv_r_k.reshape(-1), rwkv_ln_w, rwkv_ln_b, batch=batch, seq=seq)

    kvw = NSA_GROUPS * NSA_DH

    def stream(i):
        return U[:, U_NSA_KV + i * kvw:U_NSA_KV + (i + 1) * kvw].reshape(batch, seq, NSA_GROUPS, NSA_DH)

    nr = seq // NSA_CMP_STRIDE
    x_cmp = U[:, U_NSA_KV:U_NSA_KV + 2 * kvw].reshape(batch, nr, NSA_CMP_STRIDE, 2, NSA_GROUPS, NSA_DH)
    x_cmp = x_cmp.transpose(3, 0, 4, 1, 2, 5).reshape(2 * batch, NSA_GROUPS, nr, NSA_CMP_STRIDE * NSA_DH)
    kv6 = {2: stream(2).transpose(0, 2, 1, 3), 4: stream(4).transpose(0, 2, 1, 3)}
    vt6 = {3: stream(3).transpose(0, 2, 3, 1), 5: stream(5).transpose(0, 2, 3, 1)}
    kvc = nsa_compress(x_cmp, jnp.stack([nsa_pos_k, nsa_pos_v]), jnp.stack([nsa_w1_k, nsa_w1_v]),
                       jnp.stack([nsa_w2_k, nsa_w2_v]))
    ncp = kvc.shape[3]
    ones_c = jnp.zeros((batch, NSA_GROUPS, 8, ncp), BF16).at[:, :, 0, :].set(1.0)
    vct = jnp.concatenate([kvc[1].transpose(0, 1, 3, 2), ones_c], axis=2)
    o_c, selbias = nsa_cmp(U, kvc[0], vct, slopes, batch=batch, seq=seq)
    n_slc = seq // NSA_SLC_LEN
    k_aug = jnp.concatenate([kv6[2], jnp.broadcast_to(sel_key_columns(seq), (batch, NSA_GROUPS, seq, n_slc + 128))],
                            axis=-1)
    ones_rows = jnp.zeros((batch, NSA_GROUPS, 8, seq), BF16).at[:, :, 0, :].set(1.0)
    vt_aug = jnp.concatenate([vt6[3], ones_rows], axis=2)
    o_s = nsa_sel(U, selbias, k_aug, vt_aug, slopes, batch=batch, seq=seq)
    kw_pad = jnp.pad(kv6[4], ((0, 0), (0, 0), (NSA_WINDOW, 0), (0, 0)))
    vtw = jnp.concatenate([vt6[5], ones_rows], axis=2)
    vtw_pad = jnp.pad(vtw, ((0, 0), (0, 0), (0, 0), (NSA_WINDOW, 0)))
    o_w = nsa_win(U, kw_pad, vtw_pad, slopes, batch=batch, seq=seq)
    y_c = nsa_combine(o_c, o_s, o_w, Us)

    merged = merge(y_a, y_b, y_c, U, p_merge.astype(BF16))
    return matmul(merged, w_out.astype(BF16), BF16, 1024 if N % 1024 == 0 else N, 1024)


def kernel(x, c, norm_mix, norm_ffn, ada_w, ada_b, w_in, gla_a2, gla_a_b, gla_norm, rwkv_mu, rwkv_w0, rwkv_w2, rwkv_a0, rwkv_a2, rwkv_g2, rwkv_k_k, rwkv_k_a, rwkv_r_k, rwkv_ln_w, rwkv_ln_b, nsa_pos_k, nsa_w1_k, nsa_w2_k, nsa_pos_v, nsa_w1_v, nsa_w2_v, p_merge, w_out, ffn_w1, ffn_w3, ffn_w2, moe_router, moe_w1, moe_w3, moe_w2, final_norm):
    B, T, D = x.shape
    depth = w_in.shape[0]
    N = B * T
    xs = x.reshape(N, D)
    ada = ada_all(c, ada_w, ada_b)
    slopes = jnp.exp2(-8.0 * jnp.arange(1, NSA_HEADS + 1, dtype=F32) / NSA_HEADS)
    y = None
    g_prev = None
    for l in range(depth):
        sh1, sc1, g1, sh2, sc2, g2 = jnp.split(ada[l], 6, axis=-1)
        if y is None:
            (h,) = resmod(xs, None, None, norm_mix[l], sh1, sc1, seq=T)
        else:
            xs, h = resmod(xs, y, g_prev, norm_mix[l], sh1, sc1, seq=T)
        y = _mixer(h, B, T, w_in[l], gla_a2[l], gla_a_b[l], gla_norm[l], rwkv_mu[l], rwkv_w0[l], rwkv_w2[l],
                   rwkv_a0[l], rwkv_a2[l], rwkv_g2[l], rwkv_k_k[l], rwkv_k_a[l], rwkv_r_k[l],
                   rwkv_ln_w[l], rwkv_ln_b[l], nsa_pos_k[l], nsa_w1_k[l], nsa_w2_k[l],
                   nsa_pos_v[l], nsa_w1_v[l], nsa_w2_v[l], p_merge[l], w_out[l], slopes)
        if l % 2 == 0:
            xs, h = resmod(xs, y, g1, norm_ffn[l], sh2, sc2, seq=T)
            y = ffn(h, ffn_w1[l // 2].astype(BF16), ffn_w3[l // 2].astype(BF16), ffn_w2[l // 2].astype(BF16))
        else:
            rt = jnp.zeros((D, 128), F32).at[:, :N_EXPERTS].set(moe_router[l // 2])
            xs, h, logits = resmod(xs, y, g1, norm_ffn[l], sh2, sc2, seq=T, router=rt)
            combine, selm = route(logits)
            y = moe(h, combine, selm, moe_w1, moe_w3, moe_w2, l // 2)
        g_prev = g2
    (out,) = resmod(xs, y, g_prev, final_norm, None, None, seq=T, final=True)
    return out.reshape(B, T, D)
```
